```python
import math
import jax, jax.numpy as jnp
from jax import lax
import numpy as np

D_MODEL = 4096
BATCH = 4
SEQ = 2048
DEPTH = 2
DEC_BATCH = 8
DEC_SEQ = 1
PAST_LEN = 16384
PAGE_SIZE = 128

HEAD_DIM = 128
ATT_WIDTH = D_MODEL // 2
ATT_HEADS = ATT_WIDTH // HEAD_DIM
ATT_KV_HEADS = ATT_HEADS // 4
ATT_GROUP = ATT_HEADS // ATT_KV_HEADS
ATT_KV = ATT_KV_HEADS * HEAD_DIM
MOBA_BLOCK = 256
MOBA_TOPK = 3
ATT_Q_CHUNK = 8
DN_HEAD_DIM = 128
DN_WIDTH = D_MODEL // 4
DN_HEADS = DN_WIDTH // DN_HEAD_DIM
DN_CONV_CH = 3 * DN_WIDTH
SSD_INNER = D_MODEL // 4
SSD_HEAD_DIM = 64
SSD_HEADS = SSD_INNER // SSD_HEAD_DIM
SSD_GROUPS = 2
SSD_STATE = 128
SSD_BC = SSD_GROUPS * SSD_STATE
SSD_CONV_CH = SSD_INNER + 2 * SSD_BC
CONV_W = 4
SCAN_CHUNK = 64
FFN_DIM = ((8 * D_MODEL // 3 + 255) // 256) * 256
FFN_CONV_W = 3
EPS = 1e-6
IN_SIZES = (ATT_WIDTH, ATT_KV, ATT_KV, DN_CONV_CH, DN_WIDTH, DN_HEADS, DN_HEADS, SSD_CONV_CH, SSD_INNER, SSD_HEADS)
N_IN = sum(IN_SIZES)

kernel_name = 'hymba_moba_gdn_ssd_convffn_step'


def _rms_norm(x, w):
    xf = x.astype(jnp.float32)
    y = xf * lax.rsqrt(jnp.mean(xf * xf, axis=-1, keepdims=True) + EPS)
    return (y * w.astype(jnp.float32)).astype(x.dtype)


def _l2_norm(x):
    return x * lax.rsqrt(jnp.sum(x * x, axis=-1, keepdims=True) + EPS)


def _split_cols(z, sizes):
    cuts = [int(c) for c in np.cumsum(sizes)[:-1]]
    return jnp.split(z, cuts, axis=-1)


def _pad_time(z, length):
    extra = length - z.shape[1]
    if extra == 0:
        return z
    return jnp.pad(z, [(0, 0), (0, extra)] + [(0, 0)] * (z.ndim - 2))


def _chunking(t_len, c):
    c = c if t_len >= c else t_len
    return c, -(-t_len // c)


def _causal_conv(x, buf, w, b=None):
    width = w.shape[0]
    t_len = x.shape[1]
    xx = jnp.concatenate([buf.astype(x.dtype), x], axis=1)
    y = sum(xx[:, i:i + t_len] * w[i] for i in range(width))
    if b is not None:
        y = y + b
    return y, xx[:, t_len:]


def _moba_attention(q, k_all, v_all, q_pos):
    f32 = jnp.float32
    bsz, t_len, _, dh = q.shape
    n_blk = -(-k_all.shape[1] // MOBA_BLOCK)
    l_pad = n_blk * MOBA_BLOCK
    kb = _pad_time(k_all, l_pad).reshape(bsz, n_blk, MOBA_BLOCK, ATT_KV_HEADS, dh).transpose(0, 3, 1, 2, 4)
    vb = _pad_time(v_all, l_pad).reshape(bsz, n_blk, MOBA_BLOCK, ATT_KV_HEADS, dh).transpose(0, 3, 1, 2, 4)
    k_mean = jnp.mean(kb.astype(f32), axis=3)
    n_sel = min(MOBA_TOPK, n_blk)
    c, n_q = _chunking(t_len, ATT_Q_CHUNK)
    t_pad = c * n_q
    qg = _pad_time(q, t_pad).reshape(bsz, n_q, c, ATT_KV_HEADS, ATT_GROUP, dh).transpose(1, 0, 3, 4, 2, 5)
    pos = jnp.pad(q_pos, (0, t_pad - t_len), mode='edge').reshape(n_q, c)
    gather = jax.vmap(jax.vmap(lambda blocks, ix: blocks[ix]))
    scale = dh ** -0.5

    def one_chunk(args):
        qc, pc = args
        qf = qc.astype(f32)
        own = pc // MOBA_BLOCK
        past = jnp.arange(n_blk)[None, :] < own[:, None]
        gate = jnp.einsum('bkgtd,bknd->bkgtn', qf, k_mean)
        gate = jnp.where(past, gate, -jnp.inf)
        _, top = lax.top_k(gate, n_sel)
        top_ok = jnp.take_along_axis(jnp.broadcast_to(past, gate.shape), top, axis=-1)
        idx = jnp.concatenate([top, jnp.broadcast_to(own[:, None], top.shape[:-1] + (1,))], axis=-1).astype(jnp.int32)
        ok = jnp.concatenate([top_ok, jnp.ones(top.shape[:-1] + (1,), bool)], axis=-1)
        k_sel = gather(kb, idx).astype(f32)
        v_sel = gather(vb, idx).astype(f32)
        kpos = idx[..., None] * MOBA_BLOCK + jnp.arange(MOBA_BLOCK)
        mask = ok[..., None] & (kpos <= pc[:, None, None])
        logits = jnp.einsum('bkgtd,bkgtspd->bkgtsp', qf, k_sel) * scale
        logits = jnp.where(mask, logits, -jnp.inf).reshape(mask.shape[:4] + (-1,))
        p = jax.nn.softmax(logits, axis=-1).reshape(mask.shape)
        return jnp.einsum('bkgtsp,bkgtspd->bkgtd', p, v_sel)

    o = lax.map(one_chunk, (qg, pos))
    o = o.transpose(1, 0, 4, 2, 3, 5).reshape(bsz, t_pad, ATT_HEADS, dh)[:, :t_len]
    return o.astype(q.dtype)


def _gated_delta_rule(q, k, v, beta, g, s0):
    bsz, t_len, nh, _ = q.shape
    c, nc = _chunking(t_len, SCAN_CHUNK)
    tp = c * nc
    ch = lambda z: _pad_time(z, tp).reshape((bsz, nc, c) + z.shape[2:])
    q, k, v, beta, g = ch(q), ch(k), ch(v), ch(beta), ch(g)
    gc = jnp.cumsum(g, axis=2)
    tril = jnp.tril(jnp.ones((c, c), bool))[None, None, :, :, None]
    strict = tril & ~jnp.eye(c, dtype=bool)[None, None, :, :, None]
    gam = jnp.exp(jnp.where(tril, gc[:, :, :, None] - gc[:, :, None, :], -jnp.inf))
    kb = k * beta[..., None]
    a_mat = jnp.where(strict, jnp.einsum('bnihd,bnjhd->bnijh', kb, k) * gam, 0.0)
    a_mat = jnp.moveaxis(a_mat, -1, 2)
    eye = jnp.eye(c, dtype=a_mat.dtype)
    t_inv = lax.linalg.triangular_solve(eye + a_mat, jnp.broadcast_to(eye, a_mat.shape), left_side=True, lower=True)
    u = jnp.einsum('bnhij,bnjhv->bnihv', t_inv, v * beta[..., None])
    w = jnp.einsum('bnhij,bnjhd->bnihd', t_inv, kb * jnp.exp(gc)[..., None])
    attn = jnp.einsum('bnihd,bnjhd->bnijh', q, k) * gam
    g_last = gc[:, :, -1]
    q_dec = q * jnp.exp(gc)[..., None]
    k_dec = k * jnp.exp(g_last[:, :, None] - gc)[..., None]

    def step(s, inp):
        u_c, w_c, at_c, qd_c, kd_c, gl_c = inp
        v_new = u_c - jnp.einsum('bihd,bhdv->bihv', w_c, s)
        o_c = jnp.einsum('bihd,bhdv->bihv', qd_c, s) + jnp.einsum('bijh,bjhv->bihv', at_c, v_new)
        s = s * jnp.exp(gl_c)[:, :, None, None] + jnp.einsum('bjhd,bjhv->bhdv', kd_c, v_new)
        return s, o_c

    xs = tuple(z.swapaxes(0, 1) for z in (u, w, attn, q_dec, k_dec, g_last))
    s_fin, o = lax.scan(step, s0, xs)
    o = o.swapaxes(0, 1).reshape(bsz, tp, nh, v.shape[-1])[:, :t_len]
    return o, s_fin


def _ssd_scan(cq, bk, xv, a, s0):
    bsz, t_len, nh, _ = cq.shape
    c, nc = _chunking(t_len, SCAN_CHUNK)
    tp = c * nc
    ch = lambda z: _pad_time(z, tp).reshape((bsz, nc, c) + z.shape[2:])
    cq, bk, xv, a = ch(cq), ch(bk), ch(xv), ch(a)
    ac = jnp.cumsum(a, axis=2)
    tril = jnp.tril(jnp.ones((c, c), bool))[None, None, :, :, None]
    gam = jnp.exp(jnp.where(tril, ac[:, :, :, None] - ac[:, :, None, :], -jnp.inf))
    scores = jnp.einsum('bnihd,bnjhd->bnijh', cq, bk) * gam
    y_intra = jnp.einsum('bnijh,bnjhp->bnihp', scores, xv)
    a_last = ac[:, :, -1]
    chunk_states = jnp.einsum('bnjhd,bnjhp->bnhdp', bk * jnp.exp(a_last[:, :, None] - ac)[..., None], xv)

    def step(s, inp):
        cs, al = inp
        return s * jnp.exp(al)[:, :, None, None] + cs, s

    s_fin, s_prev = lax.scan(step, s0, (chunk_states.swapaxes(0, 1), a_last.swapaxes(0, 1)))
    y_inter = jnp.einsum('bnihd,bnhdp->bnihp', cq * jnp.exp(ac)[..., None], s_prev.swapaxes(0, 1))
    y = (y_intra + y_inter).reshape(bsz, tp, nh, xv.shape[-1])[:, :t_len]
    return y, s_fin


def _layer(x, lp, k_past, v_past, dn_conv_buf, dn_state, ssd_conv_buf, ssd_state, ffn_conv_buf, q_pos):
    f32 = jnp.float32
    bsz, t_len, _ = x.shape
    h = _rms_norm(x, lp['norm1_w'])
    (a_q, a_k, a_v, dn_qkv, dn_z, dn_b, dn_a, ssd_xbc, ssd_z, ssd_dt) = _split_cols(h @ lp['w_in'], IN_SIZES)
    q = _rms_norm(a_q.reshape(bsz, t_len, ATT_HEADS, HEAD_DIM), lp['attn_q_norm_w'])
    k = _rms_norm(a_k.reshape(bsz, t_len, ATT_KV_HEADS, HEAD_DIM), lp['attn_k_norm_w'])
    v = a_v.reshape(bsz, t_len, ATT_KV_HEADS, HEAD_DIM)
    k_all = jnp.concatenate([k_past.astype(k.dtype), k], axis=1)
    v_all = jnp.concatenate([v_past.astype(v.dtype), v], axis=1)
    o_att = _moba_attention(q, k_all, v_all, q_pos).reshape(bsz, t_len, ATT_WIDTH).astype(f32)
    qkv, dn_conv_new = _causal_conv(dn_qkv, dn_conv_buf, lp['dn_conv_w'])
    dq, dk, dv = jnp.split(jax.nn.silu(qkv.astype(f32)), 3, axis=-1)
    shp = (bsz, t_len, DN_HEADS, DN_HEAD_DIM)
    dq = _l2_norm(dq.reshape(shp)) * DN_HEAD_DIM ** -0.5
    dk = _l2_norm(dk.reshape(shp))
    beta = jax.nn.sigmoid(dn_b.astype(f32))
    g = -jnp.exp(lp['dn_A_log'].astype(f32)) * jax.nn.softplus(dn_a.astype(f32) + lp['dn_dt_bias'].astype(f32))
    o_dn, dn_state_new = _gated_delta_rule(dq, dk, dv.reshape(shp), beta, g, dn_state.astype(f32))
    o_dn = (_rms_norm(o_dn, lp['dn_norm_w']) * jax.nn.silu(dn_z.astype(f32).reshape(shp))).reshape(bsz, t_len, DN_WIDTH)
    xbc, ssd_conv_new = _causal_conv(ssd_xbc, ssd_conv_buf, lp['ssd_conv_w'], lp['ssd_conv_b'])
    sx, sb, sc = _split_cols(jax.nn.silu(xbc.astype(f32)), (SSD_INNER, SSD_BC, SSD_BC))
    dt = jax.nn.softplus(ssd_dt.astype(f32) + lp['ssd_dt_bias'].astype(f32))
    a_dec = dt * (-jnp.exp(lp['ssd_A_log'].astype(f32)))
    sx = sx.reshape(bsz, t_len, SSD_HEADS, SSD_HEAD_DIM)
    rep = SSD_HEADS // SSD_GROUPS
    sb = jnp.repeat(sb.reshape(bsz, t_len, SSD_GROUPS, SSD_STATE), rep, axis=2)
    sc = jnp.repeat(sc.reshape(bsz, t_len, SSD_GROUPS, SSD_STATE), rep, axis=2)
    y, ssd_state_new = _ssd_scan(sc, sb, sx * dt[..., None], a_dec, ssd_state.astype(f32))
    y = (y + lp['ssd_D'].astype(f32)[:, None] * sx).reshape(bsz, t_len, SSD_INNER) * jax.nn.silu(ssd_z.astype(f32))
    y = _rms_norm(y.reshape(bsz, t_len, SSD_GROUPS, SSD_INNER // SSD_GROUPS), lp['ssd_norm_w'].reshape(SSD_GROUPS, -1)).reshape(bsz, t_len, SSD_INNER)
    mix = jnp.concatenate([o_att, o_dn, y], axis=-1).astype(x.dtype)
    x = x + mix @ lp['w_out']
    u, ffn_conv_new = _causal_conv(_rms_norm(x, lp['norm2_w']) @ lp['ffn_w_up'], ffn_conv_buf, lp['ffn_conv_w'], lp['ffn_conv_b'])
    gate, up = jnp.split(u, 2, axis=-1)
    x = x + (jax.nn.silu(gate) * up) @ lp['ffn_w_down']
    new = (k, v, dn_conv_new, dn_state_new.astype(x.dtype), ssd_conv_new, ssd_state_new.astype(x.dtype), ffn_conv_new)
    return x, new


def _dt_bias(key, shape):
    dt = jnp.exp(jax.random.uniform(key, shape, minval=math.log(1e-3), maxval=math.log(1e-1)))
    return dt + jnp.log(-jnp.expm1(-dt))


def setup_inputs(seed: int = 0) -> dict:
    key = jax.random.key(seed)
    ks = iter(jax.random.split(key, 40))
    nrm = lambda shape, s=1.0: jax.random.normal(next(ks), shape, jnp.float32) * s
    gain = lambda shape: 1.0 + 0.02 * jax.random.normal(next(ks), shape, jnp.float32)
    n_pages = PAST_LEN // PAGE_SIZE
    used = DEC_BATCH * n_pages
    n_pool = used + max(1, used // 4)
    page_table = jax.random.permutation(next(ks), n_pool)[:used].reshape(DEC_BATCH, n_pages).astype(jnp.int32)
    return {
        'x_prompt': nrm((BATCH, SEQ, D_MODEL)),
        'x_sample': nrm((DEC_BATCH, DEC_SEQ, D_MODEL)),
        'cache_k': nrm((DEPTH, n_pool, PAGE_SIZE, ATT_KV_HEADS, HEAD_DIM)),
        'cache_v': nrm((DEPTH, n_pool, PAGE_SIZE, ATT_KV_HEADS, HEAD_DIM)),
        'page_table': page_table,
        'state_dn_conv': nrm((DEPTH, DEC_BATCH, CONV_W - 1, DN_CONV_CH)),
        'state_dn': nrm((DEPTH, DEC_BATCH, DN_HEADS, DN_HEAD_DIM, DN_HEAD_DIM), 0.05),
        'state_ssd_conv': nrm((DEPTH, DEC_BATCH, CONV_W - 1, SSD_CONV_CH)),
        'state_ssd': nrm((DEPTH, DEC_BATCH, SSD_HEADS, SSD_STATE, SSD_HEAD_DIM), 0.1),
        'state_ffn_conv': nrm((DEPTH, DEC_BATCH, FFN_CONV_W - 1, 2 * FFN_DIM)),
        'norm1_w': gain((DEPTH, D_MODEL)),
        'w_in': nrm((DEPTH, D_MODEL, N_IN), D_MODEL ** -0.5),
        'attn_q_norm_w': gain((DEPTH, HEAD_DIM)),
        'attn_k_norm_w': gain((DEPTH, HEAD_DIM)),
        'dn_conv_w': nrm((DEPTH, CONV_W, DN_CONV_CH), CONV_W ** -0.5),
        'dn_A_log': jnp.log(jax.random.uniform(next(ks), (DEPTH, DN_HEADS), minval=1.0, maxval=16.0)),
        'dn_dt_bias': _dt_bias(next(ks), (DEPTH, DN_HEADS)),
        'dn_norm_w': gain((DEPTH, DN_HEAD_DIM)),
        'ssd_conv_w': nrm((DEPTH, CONV_W, SSD_CONV_CH), CONV_W ** -0.5),
        'ssd_conv_b': nrm((DEPTH, SSD_CONV_CH), 0.01),
        'ssd_dt_bias': _dt_bias(next(ks), (DEPTH, SSD_HEADS)),
        'ssd_A_log': jnp.log(jax.random.uniform(next(ks), (DEPTH, SSD_HEADS), minval=1.0, maxval=16.0)),
        'ssd_D': gain((DEPTH, SSD_HEADS)),
        'ssd_norm_w': gain((DEPTH, SSD_INNER)),
        'w_out': nrm((DEPTH, D_MODEL, D_MODEL), D_MODEL ** -0.5),
        'norm2_w': gain((DEPTH, D_MODEL)),
        'ffn_w_up': nrm((DEPTH, D_MODEL, 2 * FFN_DIM), D_MODEL ** -0.5),
        'ffn_conv_w': nrm((DEPTH, FFN_CONV_W, 2 * FFN_DIM), FFN_CONV_W ** -0.5),
        'ffn_conv_b': nrm((DEPTH, 2 * FFN_DIM), 0.01),
        'ffn_w_down': nrm((DEPTH, FFN_DIM, D_MODEL), FFN_DIM ** -0.5),
    }


def reference(x_prompt, x_sample, cache_k, cache_v, page_table, state_dn_conv, state_dn, state_ssd_conv,
              state_ssd, state_ffn_conv, norm1_w, w_in, attn_q_norm_w, attn_k_norm_w, dn_conv_w, dn_A_log,
              dn_dt_bias, dn_norm_w, ssd_conv_w, ssd_conv_b, ssd_dt_bias, ssd_A_log, ssd_D, ssd_norm_w, w_out,
              norm2_w, ffn_w_up, ffn_conv_w, ffn_conv_b, ffn_w_down):
    dtype = x_prompt.dtype
    bsz, seq, _ = x_prompt.shape
    dec_b, dec_seq, _ = x_sample.shape
    past_len = page_table.shape[1] * cache_k.shape[2]
    pos_prompt = jnp.arange(seq, dtype=jnp.int32)
    pos_sample = past_len + jnp.arange(dec_seq, dtype=jnp.int32)
    yp, ys = x_prompt, x_sample
    outs_p, outs_s = [], []
    for l in range(DEPTH):
        lp = {'norm1_w': norm1_w[l], 'w_in': w_in[l], 'attn_q_norm_w': attn_q_norm_w[l],
              'attn_k_norm_w': attn_k_norm_w[l], 'dn_conv_w': dn_conv_w[l], 'dn_A_log': dn_A_log[l],
              'dn_dt_bias': dn_dt_bias[l], 'dn_norm_w': dn_norm_w[l], 'ssd_conv_w': ssd_conv_w[l],
              'ssd_conv_b': ssd_conv_b[l], 'ssd_dt_bias': ssd_dt_bias[l], 'ssd_A_log': ssd_A_log[l],
              'ssd_D': ssd_D[l], 'ssd_norm_w': ssd_norm_w[l], 'w_out': w_out[l], 'norm2_w': norm2_w[l],
              'ffn_w_up': ffn_w_up[l], 'ffn_conv_w': ffn_conv_w[l], 'ffn_conv_b': ffn_conv_b[l],
              'ffn_w_down': ffn_w_down[l]}
        empty_kv = jnp.zeros((bsz, 0, ATT_KV_HEADS, HEAD_DIM), dtype)
        yp, new_p = _layer(yp, lp, empty_kv, empty_kv,
                           jnp.zeros((bsz, CONV_W - 1, DN_CONV_CH), dtype),
                           jnp.zeros((bsz, DN_HEADS, DN_HEAD_DIM, DN_HEAD_DIM), dtype),
                           jnp.zeros((bsz, CONV_W - 1, SSD_CONV_CH), dtype),
                           jnp.zeros((bsz, SSD_HEADS, SSD_STATE, SSD_HEAD_DIM), dtype),
                           jnp.zeros((bsz, FFN_CONV_W - 1, 2 * FFN_DIM), dtype), pos_prompt)
        k_past = cache_k[l][page_table].reshape(dec_b, past_len, ATT_KV_HEADS, HEAD_DIM)
        v_past = cache_v[l][page_table].reshape(dec_b, past_len, ATT_KV_HEADS, HEAD_DIM)
        ys, new_s = _layer(ys, lp, k_past, v_past, state_dn_conv[l], state_dn[l], state_ssd_conv[l],
                           state_ssd[l], state_ffn_conv[l], pos_sample)
        outs_p.append(new_p)
        outs_s.append(new_s)
    st = lambda outs, i: jnp.stack([o[i] for o in outs])
    return (yp, ys,
            st(outs_p, 0), st(outs_p, 1), st(outs_s, 0), st(outs_s, 1),
            st(outs_p, 2), st(outs_s, 2), st(outs_p, 3), st(outs_s, 3),
            st(outs_p, 4), st(outs_s, 4), st(outs_p, 5), st(outs_s, 5),
            st(outs_p, 6), st(outs_s, 6))
```

```python
import functools
import math

import jax
import jax.numpy as jnp
from jax import lax
from jax.experimental import pallas as pl
from jax.experimental.pallas import tpu as pltpu

F32 = jnp.float32
BF16 = jnp.bfloat16
HIGHEST = lax.Precision.HIGHEST

D_MODEL = 4096
HEAD_DIM = 128
ATT_HEADS = 16
ATT_KV_HEADS = 4
ATT_GROUP = 4
MOBA_BLOCK = 256
MOBA_TOPK = 3
DN_HEADS = 8
DN_WIDTH = 1024
DN_CONV_CH = 3072
SSD_INNER = 1024
SSD_HEAD_DIM = 64
SSD_HEADS = 16
SSD_GROUPS = 2
SSD_STATE = 128
SSD_CONV_CH = 1536
CONV_W = 4
CHUNK = 64
FFN_DIM = 11008
FFN_CONV_W = 3
EPS = 1e-6

Q_OFF, K_OFF, V_OFF = 0, 2048, 2560
DNQKV_OFF, DNZ_OFF = 3072, 6144
SSDX_OFF, SSDZ_OFF = 7168, 8704
SMALL_OFF = 9728
N_IN_PACKED = 9984
LANE = 128
BETA_LANE, DNG_LANE, SSD_LANE = 0, 8, 16

VMEM_LIMIT_BYTES = 56 * 1024 * 1024
NEG_BIG = -1e30


def _cparams(*sem):
    return pltpu.CompilerParams(dimension_semantics=sem, vmem_limit_bytes=VMEM_LIMIT_BYTES)


def _silu(x):
    return x / (1.0 + jnp.exp(-x))


def _sigmoid(x):
    return 1.0 / (1.0 + jnp.exp(-x))


def _softplus(x):
    return jnp.maximum(x, 0.0) + jnp.log1p(jnp.exp(-jnp.abs(x)))


def _dot(a, b, precision=None):
    return jnp.dot(a, b, preferred_element_type=F32, precision=precision)


def _dot_nt(a, b, precision=None):
    return lax.dot_general(a, b, (((1,), (1,)), ((), ())), preferred_element_type=F32, precision=precision)


def _shift_rows(x, s):
    rows = lax.broadcasted_iota(jnp.int32, x.shape, 0)
    return jnp.where(rows < s, 0.0, pltpu.roll(x, s, axis=0))


def _causal_conv_rows(x, w_ref, width):
    y = x * w_ref[width - 1:width, :]
    for i in range(width - 1):
        y = y + _shift_rows(x, width - 1 - i) * w_ref[i:i + 1, :]
    return y


def _lane_col(x, lane):
    lanes = lax.broadcasted_iota(jnp.int32, x.shape, 1)
    return jnp.sum(jnp.where(lanes == lane, x, 0.0), axis=-1, keepdims=True)


def _decay_matrix(gc_col, c):
    ii = lax.broadcasted_iota(jnp.int32, (c, c), 0)
    jj = lax.broadcasted_iota(jnp.int32, (c, c), 1)
    gcb = jnp.broadcast_to(gc_col, (c, c))
    gc_row = jnp.sum(jnp.where(ii == jj, gcb, 0.0), axis=0, keepdims=True)
    low = ii >= jj
    gam = jnp.where(low, jnp.exp(jnp.where(low, gcb - gc_row, 0.0)), 0.0)
    return gam, ii, jj


def _rmsnorm_kernel(x_ref, w_ref, o_ref):
    x = x_ref[...]
    ms = jnp.mean(x * x, axis=-1, keepdims=True)
    o_ref[...] = (x * lax.rsqrt(ms + EPS) * w_ref[...]).astype(o_ref.dtype)


def _rmsnorm_cast(x, w, tm):
    m, d = x.shape
    return pl.pallas_call(
        _rmsnorm_kernel, out_shape=jax.ShapeDtypeStruct((m, d), BF16), grid=(m // tm,),
        in_specs=[pl.BlockSpec((tm, d), lambda i: (i, 0)), pl.BlockSpec((1, d), lambda i: (0, 0))],
        out_specs=pl.BlockSpec((tm, d), lambda i: (i, 0)),
        compiler_params=_cparams("arbitrary"), name="rmsnorm")(x, w.reshape(1, d))


def _matmul_kernel(*refs, n_parts, has_res):
    a_refs = refs[:n_parts]
    b_refs = refs[n_parts:2 * n_parts]
    o_ref = refs[-1]
    acc = None
    for a_ref, b_ref in zip(a_refs, b_refs):
        d = _dot(a_ref[...], b_ref[...].astype(BF16))
        acc = d if acc is None else acc + d
    if has_res:
        acc = acc + refs[2 * n_parts][...]
    o_ref[...] = acc.astype(o_ref.dtype)


def _matmul(a_parts, b, *, tm, tn, res=None, out_dtype=F32, name="matmul"):
    m = a_parts[0][0].shape[0]
    n = b.shape[1]
    in_specs, args = [], []
    for arr, kp, cb in a_parts:
        in_specs.append(pl.BlockSpec((tm, kp), functools.partial(lambda i, j, cb: (i, cb), cb=cb)))
        args.append(arr)
    row = 0
    for arr, kp, cb in a_parts:
        assert row % kp == 0
        in_specs.append(pl.BlockSpec((kp, tn), functools.partial(lambda i, j, rb: (rb, j), rb=row // kp)))
        args.append(b)
        row += kp
    assert row == b.shape[0]
    if res is not None:
        in_specs.append(pl.BlockSpec((tm, tn), lambda i, j: (i, j)))
        args.append(res)
    return pl.pallas_call(
        functools.partial(_matmul_kernel, n_parts=len(a_parts), has_res=res is not None),
        out_shape=jax.ShapeDtypeStruct((m, n), out_dtype), grid=(m // tm, n // tn),
        in_specs=in_specs, out_specs=pl.BlockSpec((tm, tn), lambda i, j: (i, j)),
        compiler_params=_cparams("arbitrary", "arbitrary"), name=name)(*args)


def _aux_kernel(s_ref, bias_ref, alog_ref, p1_ref, p2_ref, *, t_len, chunk):
    x = s_ref[...]
    lanes = lax.broadcasted_iota(jnp.int32, x.shape, 1)
    sp = _softplus(x + bias_ref[...])
    p1_ref[...] = jnp.where(lanes < DNG_LANE, _sigmoid(x), sp)
    g = -jnp.exp(alog_ref[...]) * sp
    if chunk == 1:
        p2_ref[...] = g
    else:
        ii = lax.broadcasted_iota(jnp.int32, (chunk, chunk), 0)
        jj = lax.broadcasted_iota(jnp.int32, (chunk, chunk), 1)
        tril = jnp.where(ii >= jj, 1.0, 0.0).astype(F32)
        for c in range(t_len // chunk):
            p2_ref[c * chunk:(c + 1) * chunk, :] = _dot(tril, g[c * chunk:(c + 1) * chunk, :], HIGHEST)


def _aux(z, bias_vec, alog_vec, n_seq, t_len, chunk):
    m = n_seq * t_len
    blk = pl.BlockSpec((t_len, LANE), lambda b: (b, SMALL_OFF // LANE))
    vec = pl.BlockSpec((1, LANE), lambda b: (0, 0))
    out = pl.BlockSpec((t_len, LANE), lambda b: (b, 0))
    return pl.pallas_call(
        functools.partial(_aux_kernel, t_len=t_len, chunk=chunk),
        out_shape=(jax.ShapeDtypeStruct((m, LANE), F32), jax.ShapeDtypeStruct((m, LANE), F32)),
        grid=(n_seq,), in_specs=[blk, vec, vec], out_specs=(out, out),
        compiler_params=_cparams("arbitrary"), name="aux")(z, bias_vec, alog_vec)


def _attn_prompt_kernel(q_ref, k_ref, v_ref, wq_ref, wk_ref, o_ref, kn_ref, kb_ref, vt_ref, kmean_ref,
                        *, t_len):
    nb = t_len // MOBA_BLOCK
    g = pl.program_id(2)

    @pl.when(g == 0)
    def _():
        k = k_ref[...]
        kn = k * lax.rsqrt(jnp.mean(k * k, axis=-1, keepdims=True) + EPS) * wk_ref[...]
        kn_ref[...] = kn
        kb_ref[...] = kn.astype(BF16)
        for n in range(nb):
            kmean_ref[n:n + 1, :] = jnp.mean(kn[n * MOBA_BLOCK:(n + 1) * MOBA_BLOCK, :], axis=0, keepdims=True)
        vt_ref[...] = v_ref[...].T.astype(BF16)

    q = q_ref[...]
    qn = q * lax.rsqrt(jnp.mean(q * q, axis=-1, keepdims=True) + EPS) * wq_ref[...]
    gate = _dot_nt(kmean_ref[...], qn, HIGHEST)
    blk = lax.broadcasted_iota(jnp.int32, (nb, t_len), 0)
    own = lax.broadcasted_iota(jnp.int32, (nb, t_len), 1) // MOBA_BLOCK
    valid = blk < own
    gm = jnp.where(valid, gate, -jnp.inf)
    cnt = jnp.zeros((nb, t_len), F32)
    for m in range(nb):
        row = gm[m:m + 1, :]
        beats = jnp.where(row > gm, 1.0, jnp.where(row == gm, jnp.where(blk > m, 1.0, 0.0), 0.0))
        cnt = cnt + beats
    bias = jnp.where(valid, jnp.where(cnt < MOBA_TOPK, 0.0, NEG_BIG), NEG_BIG)
    qs = (qn * (HEAD_DIM ** -0.5)).astype(BF16)
    kk = lax.broadcasted_iota(jnp.int32, (MOBA_BLOCK, MOBA_BLOCK), 0)
    qq = lax.broadcasted_iota(jnp.int32, (MOBA_BLOCK, MOBA_BLOCK), 1)
    causal = jnp.where(kk <= qq, 0.0, NEG_BIG)
    for qi in range(nb):
        n_keys = (qi + 1) * MOBA_BLOCK
        qt = qs[qi * MOBA_BLOCK:(qi + 1) * MOBA_BLOCK, :]
        st = _dot_nt(kb_ref[0:n_keys, :], qt)
        pieces = [jnp.broadcast_to(bias[n:n + 1, qi * MOBA_BLOCK:(qi + 1) * MOBA_BLOCK], (MOBA_BLOCK, MOBA_BLOCK))
                  for n in range(qi)]
        pieces.append(causal)
        st = st + (jnp.concatenate(pieces, axis=0) if qi else causal)
        mx = jnp.max(st, axis=0, keepdims=True)
        p = jnp.exp(st - mx)
        den = jnp.sum(p, axis=0, keepdims=True)
        ot = _dot(vt_ref[:, 0:n_keys], p.astype(BF16)) / den
        o_ref[qi * MOBA_BLOCK:(qi + 1) * MOBA_BLOCK, :] = ot.T.astype(o_ref.dtype)


def _attn_prompt(z, wq, wk, n_seq, t_len):
    m = n_seq * t_len
    qblk = pl.BlockSpec((t_len, HEAD_DIM), lambda b, k, g: (b, Q_OFF // HEAD_DIM + k * ATT_GROUP + g))
    kblk = pl.BlockSpec((t_len, HEAD_DIM), lambda b, k, g: (b, K_OFF // HEAD_DIM + k))
    vblk = pl.BlockSpec((t_len, HEAD_DIM), lambda b, k, g: (b, V_OFF // HEAD_DIM + k))
    wspec = pl.BlockSpec((1, HEAD_DIM), lambda b, k, g: (0, 0))
    return pl.pallas_call(
        functools.partial(_attn_prompt_kernel, t_len=t_len),
        out_shape=(jax.ShapeDtypeStruct((m, ATT_HEADS * HEAD_DIM), BF16),
                   jax.ShapeDtypeStruct((m, ATT_KV_HEADS * HEAD_DIM), F32)),
        grid=(n_seq, ATT_KV_HEADS, ATT_GROUP),
        in_specs=[qblk, kblk, vblk, wspec, wspec],
        out_specs=(pl.BlockSpec((t_len, HEAD_DIM), lambda b, k, g: (b, k * ATT_GROUP + g)),
                   pl.BlockSpec((t_len, HEAD_DIM), lambda b, k, g: (b, k))),
        scratch_shapes=[pltpu.VMEM((t_len, HEAD_DIM), BF16), pltpu.VMEM((HEAD_DIM, t_len), BF16),
                        pltpu.VMEM((t_len // MOBA_BLOCK, HEAD_DIM), F32)],
        compiler_params=_cparams("arbitrary", "arbitrary", "arbitrary"), name="attn_prompt")(
            z, z, z, wq.reshape(1, HEAD_DIM), wk.reshape(1, HEAD_DIM))


def _tri_inverse(a, c):
    ii = lax.broadcasted_iota(jnp.int32, (c, c), 0)
    jj = lax.broadcasted_iota(jnp.int32, (c, c), 1)
    eye = jnp.where(ii == jj, 1.0, 0.0).astype(F32)
    p = -a
    inv = eye + p
    steps = int(math.log2(c)) - 1
    for _ in range(steps):
        p = _dot(p, p, HIGHEST)
        inv = inv + _dot(inv, p, HIGHEST)
    return inv


def _gdn_prompt_kernel(zq_ref, zk_ref, zv_ref, wq_ref, wk_ref, wv_ref, p1_ref, p2_ref, zg_ref, nw_ref,
                       o_ref, s_ref, q_s, k_s, v_s, beta_s, gc_s, o_s, *, t_len):
    h = pl.program_id(1)
    c = CHUNK

    def l2n(x):
        return x * lax.rsqrt(jnp.sum(x * x, axis=-1, keepdims=True) + EPS)

    q_s[...] = l2n(_silu(_causal_conv_rows(zq_ref[...], wq_ref, CONV_W))) * (HEAD_DIM ** -0.5)
    k_s[...] = l2n(_silu(_causal_conv_rows(zk_ref[...], wk_ref, CONV_W)))
    v_s[...] = _silu(_causal_conv_rows(zv_ref[...], wv_ref, CONV_W))
    beta_s[...] = jnp.broadcast_to(_lane_col(p1_ref[...], BETA_LANE + h), (t_len, HEAD_DIM))
    gc_s[...] = jnp.broadcast_to(_lane_col(p2_ref[...], DNG_LANE + h), (t_len, HEAD_DIM))

    def chunk_step(ci, s):
        r = pl.ds(pl.multiple_of(ci * c, c), c)
        q, k, v = q_s[r, :], k_s[r, :], v_s[r, :]
        beta, gcb = beta_s[r, :], gc_s[r, :]
        gam, ii, jj = _decay_matrix(gcb[:, 0:1], c)
        kb = k * beta
        a = jnp.where(ii > jj, _dot_nt(kb, k) * gam, 0.0)
        t_inv = _tri_inverse(a, c)
        eg = jnp.exp(gcb)
        u = _dot(t_inv, v * beta)
        w = _dot(t_inv, kb * eg)
        attn = _dot_nt(q, k) * gam
        v_new = u - _dot(w, s)
        o_s[r, :] = eg * _dot(q, s) + _dot(attn, v_new)
        g_last = gcb[c - 1:c, :]
        k_dec = k * jnp.exp(g_last - gcb)
        return s * jnp.exp(g_last[:, 0:1]) + _dot(k_dec.T, v_new)

    s_fin = lax.fori_loop(0, t_len // c, chunk_step, jnp.zeros((HEAD_DIM, HEAD_DIM), F32))
    s_ref[...] = s_fin
    o = o_s[...]
    on = o * lax.rsqrt(jnp.mean(o * o, axis=-1, keepdims=True) + EPS) * nw_ref[...]
    o_ref[...] = (on * _silu(zg_ref[...])).astype(o_ref.dtype)


def _gdn_prompt(z, p1, p2, conv_w, norm_w, n_seq, t_len):
    m = n_seq * t_len
    base = DNQKV_OFF // HEAD_DIM

    def zcol(off):
        return pl.BlockSpec((t_len, HEAD_DIM), functools.partial(lambda b, h, off: (b, off + h), off=off))

    def wcol(off):
        return pl.BlockSpec((CONV_W, HEAD_DIM), functools.partial(lambda b, h, off: (0, off + h), off=off))

    aux = pl.BlockSpec((t_len, LANE), lambda b, h: (b, 0))
    tbuf = pltpu.VMEM((t_len, HEAD_DIM), F32)
    return pl.pallas_call(
        functools.partial(_gdn_prompt_kernel, t_len=t_len),
        out_shape=(jax.ShapeDtypeStruct((m, DN_WIDTH), BF16),
                   jax.ShapeDtypeStruct((n_seq, DN_HEADS, HEAD_DIM, HEAD_DIM), F32)),
        grid=(n_seq, DN_HEADS),
        in_specs=[zcol(base), zcol(base + DN_HEADS), zcol(base + 2 * DN_HEADS),
                  wcol(0), wcol(DN_HEADS), wcol(2 * DN_HEADS), aux, aux,
                  zcol(DNZ_OFF // HEAD_DIM), pl.BlockSpec((1, HEAD_DIM), lambda b, h: (0, 0))],
        out_specs=(pl.BlockSpec((t_len, HEAD_DIM), lambda b, h: (b, h)),
                   pl.BlockSpec((None, None, HEAD_DIM, HEAD_DIM), lambda b, h: (b, h, 0, 0))),
        scratch_shapes=[tbuf] * 6,
        compiler_params=_cparams("arbitrary", "arbitrary"), name="gdn_prompt")(
            z, z, z, conv_w, conv_w, conv_w, p1, p2, z, norm_w.reshape(1, HEAD_DIM))


_GH = SSD_HEADS // SSD_GROUPS
_GW = _GH * SSD_HEAD_DIM


def _ssd_prompt_kernel(zx_ref, zb_ref, zc_ref, wx_ref, wb_ref, wc_ref, bx_ref, bb_ref, bc_ref, p1_ref, p2_ref,
                       zg_ref, d_ref, nw_ref, o_ref, st_ref, x_s, b_s, c_s, y_s, state_s, *, t_len):
    grp = pl.program_id(1)
    c = CHUNK
    x_s[...] = _silu(_causal_conv_rows(zx_ref[...], wx_ref, CONV_W) + bx_ref[...])
    b_s[...] = _silu(_causal_conv_rows(zb_ref[...], wb_ref, CONV_W) + bb_ref[...])
    c_s[...] = _silu(_causal_conv_rows(zc_ref[...], wc_ref, CONV_W) + bc_ref[...])
    state_s[...] = jnp.zeros_like(state_s)

    def chunk_step(ci, carry):
        r = pl.ds(pl.multiple_of(ci * c, c), c)
        x, bm, cm = x_s[r, :], b_s[r, :], c_s[r, :]
        p1, p2 = p1_ref[r, :], p2_ref[r, :]
        cb = _dot_nt(cm, bm)
        s_prev = state_s[...]
        cs_prev = _dot(cm, s_prev)
        y_parts, xs_parts, dec_parts = [], [], []
        for hh in range(_GH):
            lane = SSD_LANE + grp * _GH + hh
            ac = _lane_col(p2, lane)
            dt = _lane_col(p1, lane)
            gam, _, _ = _decay_matrix(ac, c)
            xv = x[:, hh * SSD_HEAD_DIM:(hh + 1) * SSD_HEAD_DIM] * dt
            a_last = ac[c - 1:c, :]
            y_h = _dot(cb * gam, xv) + jnp.exp(ac) * cs_prev[:, hh * SSD_HEAD_DIM:(hh + 1) * SSD_HEAD_DIM]
            y_parts.append(y_h)
            xs_parts.append(xv * jnp.exp(a_last - ac))
            dec_parts.append(jnp.broadcast_to(jnp.exp(a_last), (1, SSD_HEAD_DIM)))
        y_s[r, :] = jnp.concatenate(y_parts, axis=1)
        new_states = _dot(bm.T, jnp.concatenate(xs_parts, axis=1))
        state_s[...] = s_prev * jnp.concatenate(dec_parts, axis=1) + new_states
        return carry

    lax.fori_loop(0, t_len // c, chunk_step, 0)
    for hh in range(_GH):
        st_ref[hh] = state_s[:, hh * SSD_HEAD_DIM:(hh + 1) * SSD_HEAD_DIM]
    y = (y_s[...] + d_ref[...] * x_s[...]) * _silu(zg_ref[...])
    o_ref[...] = (y * lax.rsqrt(jnp.mean(y * y, axis=-1, keepdims=True) + EPS) * nw_ref[...]).astype(o_ref.dtype)


def _ssd_prompt(z, p1, p2, conv_w, conv_b, d_vec, norm_w, n_seq, t_len):
    m = n_seq * t_len
    xb, bb, cbk = SSDX_OFF // _GW, (SSDX_OFF + SSD_INNER) // LANE, (SSDX_OFF + SSD_INNER + 2 * SSD_STATE) // LANE
    wb0, wc0 = SSD_INNER // LANE, (SSD_INNER + 2 * SSD_STATE) // LANE

    def spec(rows, width, off):
        return pl.BlockSpec((rows, width), functools.partial(lambda b, g, off: (0, off + g), off=off))

    def zspec(width, off):
        return pl.BlockSpec((t_len, width), functools.partial(lambda b, g, off: (b, off + g), off=off))

    aux = pl.BlockSpec((t_len, LANE), lambda b, g: (b, 0))
    return pl.pallas_call(
        functools.partial(_ssd_prompt_kernel, t_len=t_len),
        out_shape=(jax.ShapeDtypeStruct((m, SSD_INNER), BF16),
                   jax.ShapeDtypeStruct((n_seq, SSD_HEADS, SSD_STATE, SSD_HEAD_DIM), F32)),
        grid=(n_seq, SSD_GROUPS),
        in_specs=[zspec(_GW, xb), zspec(LANE, bb), zspec(LANE, cbk),
                  spec(CONV_W, _GW, 0), spec(CONV_W, LANE, wb0), spec(CONV_W, LANE, wc0),
                  spec(1, _GW, 0), spec(1, LANE, wb0), spec(1, LANE, wc0),
                  aux, aux, zspec(_GW, SSDZ_OFF // _GW), spec(1, _GW, 0), spec(1, _GW, 0)],
        out_specs=(pl.BlockSpec((t_len, _GW), lambda b, g: (b, g)),
                   pl.BlockSpec((None, _GH, SSD_STATE, SSD_HEAD_DIM), lambda b, g: (b, g, 0, 0))),
        scratch_shapes=[pltpu.VMEM((t_len, _GW), F32), pltpu.VMEM((t_len, LANE), F32), pltpu.VMEM((t_len, LANE), F32),
                        pltpu.VMEM((t_len, _GW), F32), pltpu.VMEM((SSD_STATE, _GW), F32)],
        compiler_params=_cparams("arbitrary", "arbitrary"), name="ssd_prompt")(
            z, z, z, conv_w, conv_w, conv_w, conv_b, conv_b, conv_b, p1, p2, z, d_vec, norm_w.reshape(1, SSD_INNER))


_FFN_TN = 256
_FFN_NJ = FFN_DIM // _FFN_TN
_TAIL = 8


def _ffn_up_prompt_kernel(a_ref, bg_ref, bu_ref, wg_ref, wu_ref, cg_ref, cu_ref, g_ref, sg_ref, su_ref, carry_ref,
                          *, tiles_per_seq):
    i = pl.program_id(0)
    j = pl.program_id(1)
    a = a_ref[...]
    tm = a.shape[0]
    first = (i % tiles_per_seq) == 0
    rows = lax.broadcasted_iota(jnp.int32, (tm, _FFN_TN), 0)
    halves = []
    for idx, (b_ref, w_ref, c_ref, s_ref) in enumerate(((bg_ref, wg_ref, cg_ref, sg_ref), (bu_ref, wu_ref, cu_ref, su_ref))):
        y = _dot(a, b_ref[...].astype(BF16))

        @pl.when(first)
        def _(idx=idx):
            carry_ref[idx, j] = jnp.zeros((_TAIL, _FFN_TN), F32)

        prev = carry_ref[idx, j]
        p1 = prev[_TAIL - 1:_TAIL, :]
        p2 = prev[_TAIL - 2:_TAIL - 1, :]
        y1 = jnp.where(rows == 0, p1, pltpu.roll(y, 1, axis=0))
        y2 = jnp.where(rows == 0, p2, jnp.where(rows == 1, p1, pltpu.roll(y, 2, axis=0)))
        halves.append(w_ref[0:1, :] * y2 + w_ref[1:2, :] * y1 + w_ref[2:3, :] * y + c_ref[...])
        tail = y[tm - _TAIL:tm, :]
        carry_ref[idx, j] = tail
        s_ref[...] = tail
    g_ref[...] = (_silu(halves[0]) * halves[1]).astype(g_ref.dtype)


def _ffn_up_prompt(h2, w_up, conv_w, conv_b, n_seq, t_len, tm):
    m = n_seq * t_len
    k = h2.shape[1]
    nj = _FFN_NJ
    tiles_per_seq = t_len // tm
    tail = jax.ShapeDtypeStruct((m // tm, _TAIL, FFN_DIM), F32)
    tail_spec = pl.BlockSpec((None, _TAIL, _FFN_TN), lambda i, j: (i, 0, j))
    return pl.pallas_call(
        functools.partial(_ffn_up_prompt_kernel, tiles_per_seq=tiles_per_seq),
        out_shape=(jax.ShapeDtypeStruct((m, FFN_DIM), BF16), tail, tail),
        grid=(m // tm, nj),
        in_specs=[pl.BlockSpec((tm, k), lambda i, j: (i, 0)),
                  pl.BlockSpec((k, _FFN_TN), lambda i, j: (0, j)), pl.BlockSpec((k, _FFN_TN), lambda i, j: (0, j + nj)),
                  pl.BlockSpec((FFN_CONV_W, _FFN_TN), lambda i, j: (0, j)),
                  pl.BlockSpec((FFN_CONV_W, _FFN_TN), lambda i, j: (0, j + nj)),
                  pl.BlockSpec((1, _FFN_TN), lambda i, j: (0, j)), pl.BlockSpec((1, _FFN_TN), lambda i, j: (0, j + nj))],
        out_specs=(pl.BlockSpec((tm, _FFN_TN), lambda i, j: (i, j)), tail_spec, tail_spec),
        scratch_shapes=[pltpu.VMEM((2, nj, _TAIL, _FFN_TN), F32)],
        compiler_params=_cparams("arbitrary", "arbitrary"), name="ffn_up_prompt")(
            h2, w_up, w_up, conv_w, conv_w, conv_b, conv_b)


def _ffn_up_sample_kernel(a_ref, bg_ref, bu_ref, wg_ref, wu_ref, cg_ref, cu_ref, s0g_ref, s1g_ref, s0u_ref, s1u_ref,
                          g_ref, yg_ref, yu_ref):
    a = a_ref[...]
    yg = _dot(a, bg_ref[...].astype(BF16))
    yu = _dot(a, bu_ref[...].astype(BF16))
    ug = wg_ref[0:1, :] * s0g_ref[...] + wg_ref[1:2, :] * s1g_ref[...] + wg_ref[2:3, :] * yg + cg_ref[...]
    uu = wu_ref[0:1, :] * s0u_ref[...] + wu_ref[1:2, :] * s1u_ref[...] + wu_ref[2:3, :] * yu + cu_ref[...]
    g_ref[...] = (_silu(ug) * uu).astype(g_ref.dtype)
    yg_ref[...] = yg
    yu_ref[...] = yu


def _ffn_up_sample(h2, w_up, conv_w, conv_b, s0, s1):
    m, k = h2.shape
    nj = _FFN_NJ
    lo = lambda j: (0, j)
    hi = lambda j: (0, j + nj)
    row = lambda f: pl.BlockSpec((m, _FFN_TN), f)
    ysd = jax.ShapeDtypeStruct((m, FFN_DIM), F32)
    return pl.pallas_call(
        _ffn_up_sample_kernel, out_shape=(jax.ShapeDtypeStruct((m, FFN_DIM), BF16), ysd, ysd), grid=(nj,),
        in_specs=[pl.BlockSpec((m, k), lambda j: (0, 0)), pl.BlockSpec((k, _FFN_TN), lo), pl.BlockSpec((k, _FFN_TN), hi),
                  pl.BlockSpec((FFN_CONV_W, _FFN_TN), lo), pl.BlockSpec((FFN_CONV_W, _FFN_TN), hi),
                  pl.BlockSpec((1, _FFN_TN), lo), pl.BlockSpec((1, _FFN_TN), hi),
                  row(lo), row(lo), row(hi), row(hi)],
        out_specs=(row(lo), row(lo), row(lo)),
        compiler_params=_cparams("arbitrary"), name="ffn_up_sample")(
            h2, w_up, w_up, conv_w, conv_w, conv_b, conv_b, s0, s1, s0, s1)


_PAGES_PER_STEP = 16


def _kmean_kernel(pt_ref, *refs):
    o_ref = refs[-1]
    page = refs[0].shape[0]
    per_block = MOBA_BLOCK // page
    for n in range(_PAGES_PER_STEP // per_block):
        acc = jnp.sum(refs[n * per_block][...], axis=0, keepdims=True)
        for r in range(1, per_block):
            acc = acc + jnp.sum(refs[n * per_block + r][...], axis=0, keepdims=True)
        o_ref[n:n + 1, :] = acc * (1.0 / MOBA_BLOCK)


def _kmean(cache_k4, page_table, layer):
    n_b, n_pages = page_table.shape
    page, width = cache_k4.shape[2], cache_k4.shape[3]
    per_block = MOBA_BLOCK // page
    n_blocks = n_pages // per_block
    steps = n_pages // _PAGES_PER_STEP
    in_specs = [pl.BlockSpec((None, None, page, width),
                             functools.partial(lambda b, s, pt, r: (layer, pt[b, s * _PAGES_PER_STEP + r], 0, 0), r=r))
                for r in range(_PAGES_PER_STEP)]
    grid_spec = pltpu.PrefetchScalarGridSpec(
        num_scalar_prefetch=1, grid=(n_b, steps), in_specs=in_specs,
        out_specs=pl.BlockSpec((None, _PAGES_PER_STEP // per_block, width), lambda b, s, pt: (b, s, 0)))
    return pl.pallas_call(
        _kmean_kernel, out_shape=jax.ShapeDtypeStruct((n_b, n_blocks, width), F32), grid_spec=grid_spec,
        compiler_params=_cparams("arbitrary", "arbitrary"), name="kmean")(page_table, *([cache_k4] * _PAGES_PER_STEP))


def _select_kernel(q_ref, k_ref, kmean_ref, wq_ref, wk_ref, qn_ref, kn_ref, idx_ref):
    n_b, n_h, _ = q_ref.shape
    n_blocks = kmean_ref.shape[1]
    q = q_ref[...]
    qn = q * lax.rsqrt(jnp.mean(q * q, axis=-1, keepdims=True) + EPS) * wq_ref[...]
    qn_ref[...] = qn
    k = k_ref[...]
    kn_ref[...] = k * lax.rsqrt(jnp.mean(k * k, axis=-1, keepdims=True) + EPS) * wk_ref[...]
    head = lax.broadcasted_iota(jnp.int32, (n_h, n_blocks), 0)
    lane = lax.broadcasted_iota(jnp.int32, (n_h, n_blocks), 1).astype(F32)
    lane_out = lax.broadcasted_iota(jnp.int32, (n_h, LANE), 1)
    for b in range(n_b):
        gate = jnp.zeros((n_h, n_blocks), F32)
        for kv in range(ATT_KV_HEADS):
            gk = _dot_nt(qn[b], kmean_ref[b, :, kv * HEAD_DIM:(kv + 1) * HEAD_DIM], HIGHEST)
            gate = jnp.where(head // ATT_GROUP == kv, gk, gate)
        out = jnp.zeros((n_h, LANE), F32)
        for s in range(MOBA_TOPK):
            mx = jnp.max(gate, axis=-1, keepdims=True)
            pick = jnp.min(jnp.where(gate == mx, lane, float(n_blocks)), axis=-1, keepdims=True)
            out = jnp.where(lane_out == s, pick, out)
            gate = jnp.where(lane == pick, -jnp.inf, gate)
        idx_ref[b] = out.astype(jnp.int32)


def _select(q3, k3, kmean, wq, wk):
    n_b, n_h, d = q3.shape
    full = lambda shape: pl.BlockSpec(shape, lambda i: (0,) * len(shape))
    return pl.pallas_call(
        _select_kernel,
        out_shape=(jax.ShapeDtypeStruct(q3.shape, F32), jax.ShapeDtypeStruct(k3.shape, F32),
                   jax.ShapeDtypeStruct((n_b, n_h, LANE), jnp.int32)),
        grid=(1,),
        in_specs=[full(q3.shape), full(k3.shape), full(kmean.shape), full((1, 1, d)), full((1, 1, d))],
        out_specs=(full(q3.shape), full(k3.shape), full((n_b, n_h, LANE))),
        compiler_params=_cparams("arbitrary"), name="select")(q3, k3, kmean, wq.reshape(1, 1, d), wk.reshape(1, 1, d))


def _attn_sample_kernel(pt_ref, idx_ref, qn_ref, kn_ref, vn_ref, *refs, n_sel, per_block):
    o_ref = refs[-1]
    pages = refs[:-1]
    n_pg = n_sel * per_block
    k_refs, v_refs = pages[:n_pg], pages[n_pg:]
    h = pl.program_id(1)
    kv = h // ATT_GROUP
    q = qn_ref[pl.ds(h, 1), :] * (HEAD_DIM ** -0.5)
    q8 = jnp.broadcast_to(q, (8, HEAD_DIM))
    k_new = kn_ref[pl.ds(kv, 1), :]
    v_new = vn_ref[pl.ds(kv, 1), :]
    s_own = jnp.sum(q * k_new, axis=-1, keepdims=True)
    logits = [_dot_nt(q8, r[...], HIGHEST)[0:1, :] for r in k_refs]
    mx = s_own
    for s in logits:
        mx = jnp.maximum(mx, jnp.max(s, axis=-1, keepdims=True))
    p_own = jnp.exp(s_own - mx)
    den = p_own
    acc = p_own * v_new
    for s, v_ref in zip(logits, v_refs):
        p = jnp.exp(s - mx)
        den = den + jnp.sum(p, axis=-1, keepdims=True)
        acc = acc + _dot(jnp.broadcast_to(p, (8, p.shape[1])), v_ref[...], HIGHEST)[0:1, :]
    o_ref[pl.ds(h, 1), :] = acc / den


def _attn_sample(cache_k4, cache_v4, page_table, idx, qn, kn, vn, layer):
    n_b, n_h, d = qn.shape
    page = cache_k4.shape[2]
    per_block = MOBA_BLOCK // page
    n_sel = idx.shape[2]

    def page_spec(s, r):
        def imap(b, h, pt, ix):
            return (layer, pt[b, ix[b, h * n_sel + s] * per_block + r], 0, h // ATT_GROUP)
        return pl.BlockSpec((None, None, page, d), imap)

    pspecs = [page_spec(s, r) for s in range(n_sel) for r in range(per_block)]
    slab = lambda n: pl.BlockSpec((None, n, d), lambda b, h, pt, ix: (b, 0, 0))
    grid_spec = pltpu.PrefetchScalarGridSpec(
        num_scalar_prefetch=2, grid=(n_b, n_h),
        in_specs=[slab(n_h), slab(ATT_KV_HEADS), slab(ATT_KV_HEADS)] + pspecs + pspecs,
        out_specs=slab(n_h))
    n_pg = len(pspecs)
    return pl.pallas_call(
        functools.partial(_attn_sample_kernel, n_sel=n_sel, per_block=per_block),
        out_shape=jax.ShapeDtypeStruct((n_b, n_h, d), F32), grid_spec=grid_spec,
        compiler_params=_cparams("arbitrary", "arbitrary"), name="attn_sample")(
            page_table, idx.reshape(n_b, n_h * n_sel), qn, kn, vn, *([cache_k4] * n_pg), *([cache_v4] * n_pg))


def _rows_to_cols(x):
    r, n = x.shape
    return jnp.concatenate([x, jnp.zeros((n - r, n), x.dtype)], axis=0).T


def _mix_sample_kernel(dx_ref, dbuf_ref, dw_ref, dbeta_ref, dg_ref, dz_ref, dnw_ref, dst_ref,
                       sx_ref, sxbuf_ref, sxw_ref, sxb_ref, bc_ref, bcbuf_ref, bcw_ref, bcb_ref,
                       sdt_ref, sa_ref, sz_ref, sd_ref, snw_ref, sst_ref,
                       dno_ref, dnst_ref, so_ref, sso_ref, y_s):
    x = dx_ref[...]
    conv = x * dw_ref[CONV_W - 1]
    for i in range(CONV_W - 1):
        conv = conv + dbuf_ref[i] * dw_ref[i]
    act = _silu(conv)
    nh = DN_HEADS
    q, k, v = act[0:nh], act[nh:2 * nh], act[2 * nh:3 * nh]
    q = q * lax.rsqrt(jnp.sum(q * q, axis=-1, keepdims=True) + EPS) * (HEAD_DIM ** -0.5)
    k = k * lax.rsqrt(jnp.sum(k * k, axis=-1, keepdims=True) + EPS)
    beta = dbeta_ref[...]
    eg = jnp.exp(dg_ref[...])
    qk = jnp.sum(q * k, axis=-1, keepdims=True)
    qt, kt = _rows_to_cols(q), _rows_to_cols(k)
    o_rows = []
    for h in range(nh):
        s0 = dst_ref[h]
        kcol, qcol = kt[:, h:h + 1], qt[:, h:h + 1]
        ks = jnp.sum(kcol * s0, axis=0, keepdims=True)
        qs = jnp.sum(qcol * s0, axis=0, keepdims=True)
        b_h, e_h = beta[h:h + 1, :], eg[h:h + 1, :]
        v_new = v[h:h + 1, :] * b_h - (b_h * e_h) * ks
        o_rows.append(e_h * qs + qk[h:h + 1, :] * v_new)
        dnst_ref[h] = s0 * e_h + kcol * v_new
    o = jnp.concatenate(o_rows, axis=0)
    on = o * lax.rsqrt(jnp.mean(o * o, axis=-1, keepdims=True) + EPS) * dnw_ref[...]
    dno_ref[...] = on * _silu(dz_ref[...])
    xs = sx_ref[...] * sxw_ref[CONV_W - 1] + sxb_ref[...]
    bc = bc_ref[...] * bcw_ref[CONV_W - 1] + bcb_ref[...]
    for i in range(CONV_W - 1):
        xs = xs + sxbuf_ref[i] * sxw_ref[i]
        bc = bc + bcbuf_ref[i] * bcw_ref[i]
    xs = _silu(xs)
    bc = _silu(bc)
    bct = _rows_to_cols(bc)
    cb = jnp.sum(bc[0:SSD_GROUPS] * bc[SSD_GROUPS:2 * SSD_GROUPS], axis=-1, keepdims=True)
    dt = sdt_ref[...]
    ea = jnp.exp(sa_ref[...])
    for h in range(SSD_HEADS):
        grp = h // _GH
        s0 = sst_ref[h]
        bcol, ccol = bct[:, grp:grp + 1], bct[:, SSD_GROUPS + grp:SSD_GROUPS + grp + 1]
        xv = xs[h:h + 1, :] * dt[h:h + 1, :]
        e_h = ea[h:h + 1, :]
        y_s[h:h + 1, :] = cb[grp:grp + 1, :] * xv + e_h * jnp.sum(ccol * s0, axis=0, keepdims=True)
        sso_ref[h] = s0 * e_h + bcol * xv
    y = (y_s[...] + sd_ref[...] * xs) * _silu(sz_ref[...])
    norm_rows = []
    for grp in range(SSD_GROUPS):
        yg = y[grp * _GH:(grp + 1) * _GH]
        ms = jnp.sum(jnp.sum(yg * yg, axis=-1, keepdims=True), axis=0, keepdims=True) * (1.0 / _GW)
        norm_rows.append(yg * lax.rsqrt(ms + EPS))
    so_ref[...] = jnp.concatenate(norm_rows, axis=0) * snw_ref[...]


def _mix_sample(args, n_b):
    def per_b(shape):
        nd = len(shape) - 1
        return pl.BlockSpec((None,) + tuple(shape[1:]), lambda b: (b,) + (0,) * nd)

    def shared(shape):
        nd = len(shape)
        return pl.BlockSpec(tuple(shape), lambda b: (0,) * nd)

    names_per_b = {"dx", "dbuf", "dbeta", "dg", "dz", "dst", "sx", "sxbuf", "bc", "bcbuf", "sdt", "sa", "sz", "sst"}
    order = ["dx", "dbuf", "dw", "dbeta", "dg", "dz", "dnw", "dst", "sx", "sxbuf", "sxw", "sxb", "bc", "bcbuf", "bcw",
             "bcb", "sdt", "sa", "sz", "sd", "snw", "sst"]
    in_specs = [per_b(args[n].shape) if n in names_per_b else shared(args[n].shape) for n in order]
    outs = (jax.ShapeDtypeStruct((n_b, DN_HEADS, HEAD_DIM), F32),
            jax.ShapeDtypeStruct((n_b, DN_HEADS, HEAD_DIM, HEAD_DIM), F32),
            jax.ShapeDtypeStruct((n_b, SSD_HEADS, SSD_HEAD_DIM), F32),
            jax.ShapeDtypeStruct((n_b, SSD_HEADS, SSD_STATE, SSD_HEAD_DIM), F32))
    return pl.pallas_call(
        _mix_sample_kernel, out_shape=outs, grid=(n_b,), in_specs=in_specs,
        out_specs=tuple(per_b(o.shape) for o in outs),
        scratch_shapes=[pltpu.VMEM((SSD_HEADS, SSD_HEAD_DIM), F32)],
        compiler_params=_cparams("arbitrary"), name="mix_sample")(*[args[n] for n in order])


def _pack_w_in(w_in):
    k = w_in.shape[0]
    small = jnp.concatenate([w_in[:, 7168:7184], w_in[:, 9744:9760],
                             jnp.zeros((k, N_IN_PACKED - SMALL_OFF - 32), w_in.dtype)], axis=1)
    return jnp.concatenate([w_in[:, :7168], w_in[:, 7184:9744], small], axis=1).astype(BF16)


def _lane_vec(dn_vals, ssd_vals):
    v = jnp.zeros((LANE,), F32)
    v = v.at[DNG_LANE:DNG_LANE + DN_HEADS].set(dn_vals.astype(F32))
    v = v.at[SSD_LANE:SSD_LANE + SSD_HEADS].set(ssd_vals.astype(F32))
    return v.reshape(1, LANE)


def _ffn_down_part(g, w_down, res, part, tm):
    half = FFN_DIM // 2
    m = g.shape[0]
    n = w_down.shape[1]
    tn = 256
    return pl.pallas_call(
        functools.partial(_matmul_kernel, n_parts=1, has_res=True),
        out_shape=jax.ShapeDtypeStruct((m, n), F32), grid=(m // tm, n // tn),
        in_specs=[pl.BlockSpec((tm, half), lambda i, j: (i, part)),
                  pl.BlockSpec((half, tn), lambda i, j: (part, j)),
                  pl.BlockSpec((tm, tn), lambda i, j: (i, j))],
        out_specs=pl.BlockSpec((tm, tn), lambda i, j: (i, j)),
        compiler_params=_cparams("arbitrary", "arbitrary"), name="ffn_down")(g, w_down, res)


def _prompt_layer(x, lw, n_seq, t_len):
    tm = 1024
    h = _rmsnorm_cast(x, lw["norm1_w"], 512)
    z = _matmul([(h, D_MODEL, 0)], lw["w_in_packed"], tm=tm, tn=768, name="in_proj")
    p1, p2 = _aux(z, lw["bias_vec"], lw["alog_vec"], n_seq, t_len, CHUNK)
    o_att, k_norm = _attn_prompt(z, lw["attn_q_norm_w"], lw["attn_k_norm_w"], n_seq, t_len)
    o_dn, dn_state = _gdn_prompt(z, p1, p2, lw["dn_conv_w"], lw["dn_norm_w"], n_seq, t_len)
    o_ssd, ssd_state = _ssd_prompt(z, p1, p2, lw["ssd_conv_w"], lw["ssd_conv_b"].reshape(1, -1), lw["ssd_d_vec"],
                                   lw["ssd_norm_w"], n_seq, t_len)
    x = _matmul([(o_att, 2048, 0), (o_dn, 1024, 0), (o_ssd, 1024, 0)], lw["w_out"], tm=tm, tn=512, res=x,
                name="out_proj")
    h2 = _rmsnorm_cast(x, lw["norm2_w"], 512)
    g, tail_g, tail_u = _ffn_up_prompt(h2, lw["ffn_w_up"], lw["ffn_conv_w"], lw["ffn_conv_b"].reshape(1, -1),
                                       n_seq, t_len, tm)
    for part in range(2):
        x = _ffn_down_part(g, lw["ffn_w_down"], x, part, tm)
    z3 = z.reshape(n_seq, t_len, N_IN_PACKED)
    tps = t_len // tm
    new = (k_norm.reshape(n_seq, t_len, ATT_KV_HEADS, HEAD_DIM),
           z3[:, :, V_OFF:V_OFF + 512].reshape(n_seq, t_len, ATT_KV_HEADS, HEAD_DIM),
           z3[:, t_len - (CONV_W - 1):, DNQKV_OFF:DNQKV_OFF + DN_CONV_CH],
           dn_state,
           z3[:, t_len - (CONV_W - 1):, SSDX_OFF:SSDX_OFF + SSD_CONV_CH],
           ssd_state,
           jnp.concatenate([tail_g[tps - 1::tps, _TAIL - 2:], tail_u[tps - 1::tps, _TAIL - 2:]], axis=-1))
    return x, new


_SAMPLE_ROWS = 16


def _sample_layer(x, lw, layer, cache_k4, cache_v4, page_table, dn_conv_buf, dn_state, ssd_conv_buf, ssd_state,
                  ffn_conv_buf):
    n_b = page_table.shape[0]
    rows = x.shape[0]
    h = _rmsnorm_cast(x, lw["norm1_w"], rows)
    z = _matmul([(h, D_MODEL, 0)], lw["w_in_packed"], tm=rows, tn=768, name="in_proj_s")
    p1, p2 = _aux(z, lw["bias_vec"], lw["alog_vec"], 1, rows, 1)
    zb = z[:n_b]
    kmean = _kmean(cache_k4, page_table, layer)
    q3 = zb[:, Q_OFF:Q_OFF + 2048].reshape(n_b, ATT_HEADS, HEAD_DIM)
    k3 = zb[:, K_OFF:K_OFF + 512].reshape(n_b, ATT_KV_HEADS, HEAD_DIM)
    v3 = zb[:, V_OFF:V_OFF + 512].reshape(n_b, ATT_KV_HEADS, HEAD_DIM)
    qn, kn, idx = _select(q3, k3, kmean, lw["attn_q_norm_w"], lw["attn_k_norm_w"])
    o_att = _attn_sample(cache_k4, cache_v4, page_table, idx[:, :, :MOBA_TOPK], qn, kn, v3, layer)
    dn_x = zb[:, DNQKV_OFF:DNQKV_OFF + DN_CONV_CH]
    ssd_x = zb[:, SSDX_OFF:SSDX_OFF + SSD_CONV_CH]
    nh3 = 3 * DN_HEADS
    ng2 = 2 * SSD_GROUPS
    args = {
        "dx": dn_x.reshape(n_b, nh3, HEAD_DIM),
        "dbuf": dn_conv_buf.reshape(n_b, CONV_W - 1, nh3, HEAD_DIM),
        "dw": lw["dn_conv_w"].reshape(CONV_W, nh3, HEAD_DIM),
        "dbeta": p1[:n_b, BETA_LANE:BETA_LANE + DN_HEADS].reshape(n_b, DN_HEADS, 1),
        "dg": p2[:n_b, DNG_LANE:DNG_LANE + DN_HEADS].reshape(n_b, DN_HEADS, 1),
        "dz": zb[:, DNZ_OFF:DNZ_OFF + DN_WIDTH].reshape(n_b, DN_HEADS, HEAD_DIM),
        "dnw": lw["dn_norm_w"].reshape(1, HEAD_DIM),
        "dst": dn_state,
        "sx": ssd_x[:, :SSD_INNER].reshape(n_b, SSD_HEADS, SSD_HEAD_DIM),
        "sxbuf": ssd_conv_buf[:, :, :SSD_INNER].reshape(n_b, CONV_W - 1, SSD_HEADS, SSD_HEAD_DIM),
        "sxw": lw["ssd_conv_w"][:, :SSD_INNER].reshape(CONV_W, SSD_HEADS, SSD_HEAD_DIM),
        "sxb": lw["ssd_conv_b"][:SSD_INNER].reshape(SSD_HEADS, SSD_HEAD_DIM),
        "bc": ssd_x[:, SSD_INNER:].reshape(n_b, ng2, SSD_STATE),
        "bcbuf": ssd_conv_buf[:, :, SSD_INNER:].reshape(n_b, CONV_W - 1, ng2, SSD_STATE),
        "bcw": lw["ssd_conv_w"][:, SSD_INNER:].reshape(CONV_W, ng2, SSD_STATE),
        "bcb": lw["ssd_conv_b"][SSD_INNER:].reshape(ng2, SSD_STATE),
        "sdt": p1[:n_b, SSD_LANE:SSD_LANE + SSD_HEADS].reshape(n_b, SSD_HEADS, 1),
        "sa": p2[:n_b, SSD_LANE:SSD_LANE + SSD_HEADS].reshape(n_b, SSD_HEADS, 1),
        "sz": zb[:, SSDZ_OFF:SSDZ_OFF + SSD_INNER].reshape(n_b, SSD_HEADS, SSD_HEAD_DIM),
        "sd": lw["ssd_D"].astype(F32).reshape(SSD_HEADS, 1),
        "snw": lw["ssd_norm_w"].reshape(SSD_HEADS, SSD_HEAD_DIM),
        "sst": ssd_state,
    }
    o_dn, dn_state_new, o_ssd, ssd_state_new = _mix_sample(args, n_b)
    pad = lambda a: jnp.pad(a.reshape(n_b, -1), ((0, rows - n_b), (0, 0))).astype(BF16)
    x = _matmul([(pad(o_att), 2048, 0), (pad(o_dn), 1024, 0), (pad(o_ssd), 1024, 0)], lw["w_out"], tm=rows, tn=512,
                res=x, name="out_proj_s")
    h2 = _rmsnorm_cast(x, lw["norm2_w"], rows)
    padf = lambda a: jnp.pad(a, ((0, rows - n_b), (0, 0)))
    g, y_g, y_u = _ffn_up_sample(h2, lw["ffn_w_up"], lw["ffn_conv_w"], lw["ffn_conv_b"].reshape(1, -1),
                                 padf(ffn_conv_buf[:, 0]), padf(ffn_conv_buf[:, 1]))
    for part in range(2):
        x = _ffn_down_part(g, lw["ffn_w_down"], x, part, rows)
    y_new = jnp.concatenate([y_g[:n_b], y_u[:n_b]], axis=-1)
    new = (kn.reshape(n_b, 1, ATT_KV_HEADS, HEAD_DIM), v3.reshape(n_b, 1, ATT_KV_HEADS, HEAD_DIM),
           jnp.concatenate([dn_conv_buf[:, 1:], dn_x[:, None]], axis=1), dn_state_new,
           jnp.concatenate([ssd_conv_buf[:, 1:], ssd_x[:, None]], axis=1), ssd_state_new,
           jnp.stack([ffn_conv_buf[:, 1], y_new], axis=1))
    return x, new


def kernel(x_prompt, x_sample, cache_k, cache_v, page_table, state_dn_conv, state_dn, state_ssd_conv, state_ssd, state_ffn_conv, norm1_w, w_in, attn_q_norm_w, attn_k_norm_w, dn_conv_w, dn_A_log, dn_dt_bias, dn_norm_w, ssd_conv_w, ssd_conv_b, ssd_dt_bias, ssd_A_log, ssd_D, ssd_norm_w, w_out, norm2_w, ffn_w_up, ffn_conv_w, ffn_conv_b, ffn_w_down):
    depth = w_in.shape[0]
    n_seq, t_len, d_model = x_prompt.shape
    n_b = x_sample.shape[0]
    n_pool, page = cache_k.shape[1], cache_k.shape[2]
    cache_k4 = cache_k.reshape(depth, n_pool, page, ATT_KV_HEADS * HEAD_DIM)
    cache_v4 = cache_v.reshape(depth, n_pool, page, ATT_KV_HEADS * HEAD_DIM)
    yp = x_prompt.reshape(n_seq * t_len, d_model)
    ys = jnp.pad(x_sample.reshape(n_b, d_model), ((0, _SAMPLE_ROWS - n_b), (0, 0)))
    outs_p, outs_s = [], []
    for l in range(depth):
        lw = {"norm1_w": norm1_w[l], "w_in_packed": _pack_w_in(w_in[l]), "attn_q_norm_w": attn_q_norm_w[l],
              "attn_k_norm_w": attn_k_norm_w[l], "dn_conv_w": dn_conv_w[l], "dn_norm_w": dn_norm_w[l],
              "ssd_conv_w": ssd_conv_w[l], "ssd_conv_b": ssd_conv_b[l], "ssd_D": ssd_D[l],
              "ssd_norm_w": ssd_norm_w[l], "w_out": w_out[l], "norm2_w": norm2_w[l], "ffn_w_up": ffn_w_up[l],
              "ffn_conv_w": ffn_conv_w[l], "ffn_conv_b": ffn_conv_b[l], "ffn_w_down": ffn_w_down[l],
              "bias_vec": _lane_vec(dn_dt_bias[l], ssd_dt_bias[l]),
              "alog_vec": _lane_vec(dn_A_log[l], ssd_A_log[l]),
              "ssd_d_vec": jnp.repeat(ssd_D[l].astype(F32), SSD_HEAD_DIM).reshape(1, SSD_INNER)}
        yp, new_p = _prompt_layer(yp, lw, n_seq, t_len)
        ys, new_s = _sample_layer(ys, lw, l, cache_k4, cache_v4, page_table, state_dn_conv[l], state_dn[l],
                                  state_ssd_conv[l], state_ssd[l], state_ffn_conv[l])
        outs_p.append(new_p)
        outs_s.append(new_s)
    st = lambda outs, i: jnp.stack([o[i] for o in outs])
    return (yp.reshape(n_seq, t_len, d_model), ys[:n_b].reshape(n_b, 1, d_model),
            st(outs_p, 0), st(outs_p, 1), st(outs_s, 0), st(outs_s, 1),
            st(outs_p, 2), st(outs_s, 2), st(outs_p, 3), st(outs_s, 3),
            st(outs_p, 4), st(outs_s, 4), st(outs_p, 5), st(outs_s, 5),
            st(outs_p, 6), st(outs_s, 6))
```

```python
import functools
import math

import jax
import jax.numpy as jnp
from jax import lax
from jax.experimental import pallas as pl
from jax.experimental.pallas import tpu as pltpu

F32 = jnp.float32
BF16 = jnp.bfloat16
HIGHEST = lax.Precision.HIGHEST

D_MODEL = 4096
HEAD_DIM = 128
ATT_HEADS = 16
ATT_KV_HEADS = 4
ATT_GROUP = 4
MOBA_BLOCK = 256
MOBA_TOPK = 3
DN_HEADS = 8
DN_WIDTH = 1024
DN_CONV_CH = 3072
SSD_INNER = 1024
SSD_HEAD_DIM = 64
SSD_HEADS = 16
SSD_GROUPS = 2
SSD_STATE = 128
SSD_CONV_CH = 1536
CONV_W = 4
CHUNK = 64
FFN_DIM = 11008
FFN_CONV_W = 3
EPS = 1e-6

Q_OFF, K_OFF, V_OFF = 0, 2048, 2560
DNQKV_OFF, DNZ_OFF = 3072, 6144
SSDX_OFF, SSDZ_OFF = 7168, 8704
SMALL_OFF = 9728
N_IN_PACKED = 9984
LANE = 128
BETA_LANE, DNG_LANE, SSD_LANE = 0, 8, 16

VMEM_LIMIT_BYTES = 56 * 1024 * 1024
NEG_BIG = -1e30


def _cparams(*sem):
    return pltpu.CompilerParams(dimension_semantics=sem, vmem_limit_bytes=VMEM_LIMIT_BYTES)


def _silu(x):
    return x / (1.0 + jnp.exp(-x))


def _sigmoid(x):
    return 1.0 / (1.0 + jnp.exp(-x))


def _softplus(x):
    return jnp.maximum(x, 0.0) + jnp.log1p(jnp.exp(-jnp.abs(x)))


def _dot(a, b, precision=None):
    return jnp.dot(a, b, preferred_element_type=F32, precision=precision)


def _dot_nt(a, b, precision=None):
    return lax.dot_general(a, b, (((1,), (1,)), ((), ())), preferred_element_type=F32, precision=precision)


def _shift_rows(x, s):
    rows = lax.broadcasted_iota(jnp.int32, x.shape, 0)
    return jnp.where(rows < s, 0.0, pltpu.roll(x, s, axis=0))


def _causal_conv_rows(x, w_ref, width):
    y = x * w_ref[width - 1:width, :]
    for i in range(width - 1):
        y = y + _shift_rows(x, width - 1 - i) * w_ref[i:i + 1, :]
    return y


def _lane_col(x, lane):
    lanes = lax.broadcasted_iota(jnp.int32, x.shape, 1)
    return jnp.sum(jnp.where(lanes == lane, x, 0.0), axis=-1, keepdims=True)


def _decay_matrix(gc_col, c):
    ii = lax.broadcasted_iota(jnp.int32, (c, c), 0)
    jj = lax.broadcasted_iota(jnp.int32, (c, c), 1)
    gcb = jnp.broadcast_to(gc_col, (c, c))
    gc_row = jnp.sum(jnp.where(ii == jj, gcb, 0.0), axis=0, keepdims=True)
    low = ii >= jj
    gam = jnp.where(low, jnp.exp(jnp.where(low, gcb - gc_row, 0.0)), 0.0)
    return gam, ii, jj


def _rmsnorm_kernel(x_ref, w_ref, o_ref):
    x = x_ref[...]
    ms = jnp.mean(x * x, axis=-1, keepdims=True)
    o_ref[...] = (x * lax.rsqrt(ms + EPS) * w_ref[...]).astype(o_ref.dtype)


def _rmsnorm_cast(x, w, tm):
    m, d = x.shape
    return pl.pallas_call(
        _rmsnorm_kernel, out_shape=jax.ShapeDtypeStruct((m, d), BF16), grid=(m // tm,),
        in_specs=[pl.BlockSpec((tm, d), lambda i: (i, 0)), pl.BlockSpec((1, d), lambda i: (0, 0))],
        out_specs=pl.BlockSpec((tm, d), lambda i: (i, 0)),
        compiler_params=_cparams("arbitrary"), name="rmsnorm")(x, w.reshape(1, d))


def _matmul_kernel(*refs, n_parts, has_res):
    a_refs = refs[:n_parts]
    b_refs = refs[n_parts:2 * n_parts]
    o_ref = refs[-1]
    acc = None
    for a_ref, b_ref in zip(a_refs, b_refs):
        d = _dot(a_ref[...], b_ref[...].astype(BF16))
        acc = d if acc is None else acc + d
    if has_res:
        acc = acc + refs[2 * n_parts][...]
    o_ref[...] = acc.astype(o_ref.dtype)


def _matmul(a_parts, b, layer, *, tm, tn, res=None, out_dtype=F32, name="matmul"):
    m = a_parts[0][0].shape[0]
    n = b.shape[2]
    in_specs, args = [], []
    for arr, kp, cb, rb in a_parts:
        in_specs.append(pl.BlockSpec((tm, kp), functools.partial(lambda i, j, cb: (i, cb), cb=cb)))
        args.append(arr)
    for arr, kp, cb, rb in a_parts:
        in_specs.append(pl.BlockSpec((None, kp, tn), functools.partial(lambda i, j, rb: (layer, rb, j), rb=rb)))
        args.append(b)
    if res is not None:
        in_specs.append(pl.BlockSpec((tm, tn), lambda i, j: (i, j)))
        args.append(res)
    return pl.pallas_call(
        functools.partial(_matmul_kernel, n_parts=len(a_parts), has_res=res is not None),
        out_shape=jax.ShapeDtypeStruct((m, n), out_dtype), grid=(m // tm, n // tn),
        in_specs=in_specs, out_specs=pl.BlockSpec((tm, tn), lambda i, j: (i, j)),
        compiler_params=_cparams("arbitrary", "arbitrary"), name=name)(*args)


def _aux_kernel(s_ref, bias_ref, alog_ref, p1_ref, p2_ref, *, t_len, chunk):
    x = s_ref[...]
    lanes = lax.broadcasted_iota(jnp.int32, x.shape, 1)
    sp = _softplus(x + bias_ref[...])
    p1_ref[...] = jnp.where(lanes < DNG_LANE, _sigmoid(x), sp)
    g = -jnp.exp(alog_ref[...]) * sp
    if chunk == 1:
        p2_ref[...] = g
    else:
        ii = lax.broadcasted_iota(jnp.int32, (chunk, chunk), 0)
        jj = lax.broadcasted_iota(jnp.int32, (chunk, chunk), 1)
        tril = jnp.where(ii >= jj, 1.0, 0.0).astype(F32)
        for c in range(t_len // chunk):
            p2_ref[c * chunk:(c + 1) * chunk, :] = _dot(tril, g[c * chunk:(c + 1) * chunk, :], HIGHEST)


def _aux(z, bias_vec, alog_vec, n_seq, t_len, chunk):
    m = n_seq * t_len
    blk = pl.BlockSpec((t_len, LANE), lambda b: (b, SMALL_OFF // LANE))
    vec = pl.BlockSpec((1, LANE), lambda b: (0, 0))
    out = pl.BlockSpec((t_len, LANE), lambda b: (b, 0))
    return pl.pallas_call(
        functools.partial(_aux_kernel, t_len=t_len, chunk=chunk),
        out_shape=(jax.ShapeDtypeStruct((m, LANE), F32), jax.ShapeDtypeStruct((m, LANE), F32)),
        grid=(n_seq,), in_specs=[blk, vec, vec], out_specs=(out, out),
        compiler_params=_cparams("arbitrary"), name="aux")(z, bias_vec, alog_vec)


def _attn_prompt_kernel(q_ref, k_ref, v_ref, wq_ref, wk_ref, o_ref, kn_ref, kb_ref, vt_ref, kmean_ref,
                        *, t_len):
    nb = t_len // MOBA_BLOCK
    g = pl.program_id(2)

    @pl.when(g == 0)
    def _():
        k = k_ref[...]
        kn = k * lax.rsqrt(jnp.mean(k * k, axis=-1, keepdims=True) + EPS) * wk_ref[...]
        kn_ref[...] = kn
        kb_ref[...] = kn.astype(BF16)
        for n in range(nb):
            kmean_ref[n:n + 1, :] = jnp.mean(kn[n * MOBA_BLOCK:(n + 1) * MOBA_BLOCK, :], axis=0, keepdims=True)
        vt_ref[...] = v_ref[...].T.astype(BF16)

    q = q_ref[...]
    qn = q * lax.rsqrt(jnp.mean(q * q, axis=-1, keepdims=True) + EPS) * wq_ref[...]
    gate = _dot_nt(kmean_ref[...], qn, HIGHEST)
    blk = lax.broadcasted_iota(jnp.int32, (nb, t_len), 0)
    own = lax.broadcasted_iota(jnp.int32, (nb, t_len), 1) // MOBA_BLOCK
    valid = blk < own
    gm = jnp.where(valid, gate, -jnp.inf)
    cnt = jnp.zeros((nb, t_len), F32)
    for m in range(nb):
        row = gm[m:m + 1, :]
        beats = jnp.where(row > gm, 1.0, jnp.where(row == gm, jnp.where(blk > m, 1.0, 0.0), 0.0))
        cnt = cnt + beats
    bias = jnp.where(valid, jnp.where(cnt < MOBA_TOPK, 0.0, NEG_BIG), NEG_BIG)
    qs = (qn * (HEAD_DIM ** -0.5)).astype(BF16)
    kk = lax.broadcasted_iota(jnp.int32, (MOBA_BLOCK, MOBA_BLOCK), 0)
    qq = lax.broadcasted_iota(jnp.int32, (MOBA_BLOCK, MOBA_BLOCK), 1)
    causal = jnp.where(kk <= qq, 0.0, NEG_BIG)
    for qi in range(nb):
        n_keys = (qi + 1) * MOBA_BLOCK
        qt = qs[qi * MOBA_BLOCK:(qi + 1) * MOBA_BLOCK, :]
        st = _dot_nt(kb_ref[0:n_keys, :], qt)
        pieces = [jnp.broadcast_to(bias[n:n + 1, qi * MOBA_BLOCK:(qi + 1) * MOBA_BLOCK], (MOBA_BLOCK, MOBA_BLOCK))
                  for n in range(qi)]
        pieces.append(causal)
        st = st + (jnp.concatenate(pieces, axis=0) if qi else causal)
        mx = jnp.max(st, axis=0, keepdims=True)
        p = jnp.exp(st - mx)
        den = jnp.sum(p, axis=0, keepdims=True)
        ot = _dot(vt_ref[:, 0:n_keys], p.astype(BF16)) / den
        o_ref[qi * MOBA_BLOCK:(qi + 1) * MOBA_BLOCK, :] = ot.T.astype(o_ref.dtype)


def _attn_prompt(z, wq, wk, n_seq, t_len):
    m = n_seq * t_len
    qblk = pl.BlockSpec((t_len, HEAD_DIM), lambda b, k, g: (b, Q_OFF // HEAD_DIM + k * ATT_GROUP + g))
    kblk = pl.BlockSpec((t_len, HEAD_DIM), lambda b, k, g: (b, K_OFF // HEAD_DIM + k))
    vblk = pl.BlockSpec((t_len, HEAD_DIM), lambda b, k, g: (b, V_OFF // HEAD_DIM + k))
    wspec = pl.BlockSpec((1, HEAD_DIM), lambda b, k, g: (0, 0))
    return pl.pallas_call(
        functools.partial(_attn_prompt_kernel, t_len=t_len),
        out_shape=(jax.ShapeDtypeStruct((m, ATT_HEADS * HEAD_DIM), BF16),
                   jax.ShapeDtypeStruct((m, ATT_KV_HEADS * HEAD_DIM), F32)),
        grid=(n_seq, ATT_KV_HEADS, ATT_GROUP),
        in_specs=[qblk, kblk, vblk, wspec, wspec],
        out_specs=(pl.BlockSpec((t_len, HEAD_DIM), lambda b, k, g: (b, k * ATT_GROUP + g)),
                   pl.BlockSpec((t_len, HEAD_DIM), lambda b, k, g: (b, k))),
        scratch_shapes=[pltpu.VMEM((t_len, HEAD_DIM), BF16), pltpu.VMEM((HEAD_DIM, t_len), BF16),
                        pltpu.VMEM((t_len // MOBA_BLOCK, HEAD_DIM), F32)],
        compiler_params=_cparams("arbitrary", "arbitrary", "arbitrary"), name="attn_prompt")(
            z, z, z, wq.reshape(1, HEAD_DIM), wk.reshape(1, HEAD_DIM))


def _tri_inverse(a, c):
    ii = lax.broadcasted_iota(jnp.int32, (c, c), 0)
    jj = lax.broadcasted_iota(jnp.int32, (c, c), 1)
    eye = jnp.where(ii == jj, 1.0, 0.0).astype(F32)
    p = -a
    inv = eye + p
    steps = int(math.log2(c)) - 1
    for _ in range(steps):
        p = _dot(p, p, HIGHEST)
        inv = inv + _dot(inv, p, HIGHEST)
    return inv


def _gdn_prompt_kernel(zq_ref, zk_ref, zv_ref, wq_ref, wk_ref, wv_ref, p1_ref, p2_ref, zg_ref, nw_ref,
                       o_ref, s_ref, q_s, k_s, v_s, beta_s, gc_s, o_s, *, t_len):
    h = pl.program_id(1)
    c = CHUNK

    def l2n(x):
        return x * lax.rsqrt(jnp.sum(x * x, axis=-1, keepdims=True) + EPS)

    q_s[...] = l2n(_silu(_causal_conv_rows(zq_ref[...], wq_ref, CONV_W))) * (HEAD_DIM ** -0.5)
    k_s[...] = l2n(_silu(_causal_conv_rows(zk_ref[...], wk_ref, CONV_W)))
    v_s[...] = _silu(_causal_conv_rows(zv_ref[...], wv_ref, CONV_W))
    beta_s[...] = jnp.broadcast_to(_lane_col(p1_ref[...], BETA_LANE + h), (t_len, HEAD_DIM))
    gc_s[...] = jnp.broadcast_to(_lane_col(p2_ref[...], DNG_LANE + h), (t_len, HEAD_DIM))

    def chunk_step(ci, s):
        r = pl.ds(pl.multiple_of(ci * c, c), c)
        q, k, v = q_s[r, :], k_s[r, :], v_s[r, :]
        beta, gcb = beta_s[r, :], gc_s[r, :]
        gam, ii, jj = _decay_matrix(gcb[:, 0:1], c)
        kb = k * beta
        a = jnp.where(ii > jj, _dot_nt(kb, k) * gam, 0.0)
        t_inv = _tri_inverse(a, c)
        eg = jnp.exp(gcb)
        u = _dot(t_inv, v * beta)
        w = _dot(t_inv, kb * eg)
        attn = _dot_nt(q, k) * gam
        v_new = u - _dot(w, s)
        o_s[r, :] = eg * _dot(q, s) + _dot(attn, v_new)
        g_last = gcb[c - 1:c, :]
        k_dec = k * jnp.exp(g_last - gcb)
        return s * jnp.exp(g_last[:, 0:1]) + _dot(k_dec.T, v_new)

    s_fin = lax.fori_loop(0, t_len // c, chunk_step, jnp.zeros((HEAD_DIM, HEAD_DIM), F32))
    s_ref[...] = s_fin
    o = o_s[...]
    on = o * lax.rsqrt(jnp.mean(o * o, axis=-1, keepdims=True) + EPS) * nw_ref[...]
    o_ref[...] = (on * _silu(zg_ref[...])).astype(o_ref.dtype)


def _gdn_prompt(z, p1, p2, conv_w, norm_w, n_seq, t_len):
    m = n_seq * t_len
    base = DNQKV_OFF // HEAD_DIM

    def zcol(off):
        return pl.BlockSpec((t_len, HEAD_DIM), functools.partial(lambda b, h, off: (b, off + h), off=off))

    def wcol(off):
        return pl.BlockSpec((CONV_W, HEAD_DIM), functools.partial(lambda b, h, off: (0, off + h), off=off))

    aux = pl.BlockSpec((t_len, LANE), lambda b, h: (b, 0))
    tbuf = pltpu.VMEM((t_len, HEAD_DIM), F32)
    return pl.pallas_call(
        functools.partial(_gdn_prompt_kernel, t_len=t_len),
        out_shape=(jax.ShapeDtypeStruct((m, DN_WIDTH), BF16),
                   jax.ShapeDtypeStruct((n_seq, DN_HEADS, HEAD_DIM, HEAD_DIM), F32)),
        grid=(n_seq, DN_HEADS),
        in_specs=[zcol(base), zcol(base + DN_HEADS), zcol(base + 2 * DN_HEADS),
                  wcol(0), wcol(DN_HEADS), wcol(2 * DN_HEADS), aux, aux,
                  zcol(DNZ_OFF // HEAD_DIM), pl.BlockSpec((1, HEAD_DIM), lambda b, h: (0, 0))],
        out_specs=(pl.BlockSpec((t_len, HEAD_DIM), lambda b, h: (b, h)),
                   pl.BlockSpec((None, None, HEAD_DIM, HEAD_DIM), lambda b, h: (b, h, 0, 0))),
        scratch_shapes=[tbuf] * 6,
        compiler_params=_cparams("arbitrary", "arbitrary"), name="gdn_prompt")(
            z, z, z, conv_w, conv_w, conv_w, p1, p2, z, norm_w.reshape(1, HEAD_DIM))


_GH = SSD_HEADS // SSD_GROUPS
_GW = _GH * SSD_HEAD_DIM


def _ssd_prompt_kernel(zx_ref, zb_ref, zc_ref, wx_ref, wb_ref, wc_ref, bx_ref, bb_ref, bc_ref, p1_ref, p2_ref,
                       zg_ref, d_ref, nw_ref, o_ref, st_ref, x_s, b_s, c_s, y_s, state_s, *, t_len):
    grp = pl.program_id(1)
    c = CHUNK
    x_s[...] = _silu(_causal_conv_rows(zx_ref[...], wx_ref, CONV_W) + bx_ref[...])
    b_s[...] = _silu(_causal_conv_rows(zb_ref[...], wb_ref, CONV_W) + bb_ref[...])
    c_s[...] = _silu(_causal_conv_rows(zc_ref[...], wc_ref, CONV_W) + bc_ref[...])
    state_s[...] = jnp.zeros_like(state_s)

    def chunk_step(ci, carry):
        r = pl.ds(pl.multiple_of(ci * c, c), c)
        x, bm, cm = x_s[r, :], b_s[r, :], c_s[r, :]
        p1, p2 = p1_ref[r, :], p2_ref[r, :]
        cb = _dot_nt(cm, bm)
        s_prev = state_s[...]
        cs_prev = _dot(cm, s_prev)
        y_parts, xs_parts, dec_parts = [], [], []
        for hh in range(_GH):
            lane = SSD_LANE + grp * _GH + hh
            ac = _lane_col(p2, lane)
            dt = _lane_col(p1, lane)
            gam, _, _ = _decay_matrix(ac, c)
            xv = x[:, hh * SSD_HEAD_DIM:(hh + 1) * SSD_HEAD_DIM] * dt
            a_last = ac[c - 1:c, :]
            y_h = _dot(cb * gam, xv) + jnp.exp(ac) * cs_prev[:, hh * SSD_HEAD_DIM:(hh + 1) * SSD_HEAD_DIM]
            y_parts.append(y_h)
            xs_parts.append(xv * jnp.exp(a_last - ac))
            dec_parts.append(jnp.broadcast_to(jnp.exp(a_last), (1, SSD_HEAD_DIM)))
        y_s[r, :] = jnp.concatenate(y_parts, axis=1)
        new_states = _dot(bm.T, jnp.concatenate(xs_parts, axis=1))
        state_s[...] = s_prev * jnp.concatenate(dec_parts, axis=1) + new_states
        return carry

    lax.fori_loop(0, t_len // c, chunk_step, 0)
    for hh in range(_GH):
        st_ref[hh] = state_s[:, hh * SSD_HEAD_DIM:(hh + 1) * SSD_HEAD_DIM]
    y = (y_s[...] + d_ref[...] * x_s[...]) * _silu(zg_ref[...])
    o_ref[...] = (y * lax.rsqrt(jnp.mean(y * y, axis=-1, keepdims=True) + EPS) * nw_ref[...]).astype(o_ref.dtype)


def _ssd_prompt(z, p1, p2, conv_w, conv_b, d_vec, norm_w, n_seq, t_len):
    m = n_seq * t_len
    xb, bb, cbk = SSDX_OFF // _GW, (SSDX_OFF + SSD_INNER) // LANE, (SSDX_OFF + SSD_INNER + 2 * SSD_STATE) // LANE
    wb0, wc0 = SSD_INNER // LANE, (SSD_INNER + 2 * SSD_STATE) // LANE

    def spec(rows, width, off):
        return pl.BlockSpec((rows, width), functools.partial(lambda b, g, off: (0, off + g), off=off))

    def zspec(width, off):
        return pl.BlockSpec((t_len, width), functools.partial(lambda b, g, off: (b, off + g), off=off))

    aux = pl.BlockSpec((t_len, LANE), lambda b, g: (b, 0))
    return pl.pallas_call(
        functools.partial(_ssd_prompt_kernel, t_len=t_len),
        out_shape=(jax.ShapeDtypeStruct((m, SSD_INNER), BF16),
                   jax.ShapeDtypeStruct((n_seq, SSD_HEADS, SSD_STATE, SSD_HEAD_DIM), F32)),
        grid=(n_seq, SSD_GROUPS),
        in_specs=[zspec(_GW, xb), zspec(LANE, bb), zspec(LANE, cbk),
                  spec(CONV_W, _GW, 0), spec(CONV_W, LANE, wb0), spec(CONV_W, LANE, wc0),
                  spec(1, _GW, 0), spec(1, LANE, wb0), spec(1, LANE, wc0),
                  aux, aux, zspec(_GW, SSDZ_OFF // _GW), spec(1, _GW, 0), spec(1, _GW, 0)],
        out_specs=(pl.BlockSpec((t_len, _GW), lambda b, g: (b, g)),
                   pl.BlockSpec((None, _GH, SSD_STATE, SSD_HEAD_DIM), lambda b, g: (b, g, 0, 0))),
        scratch_shapes=[pltpu.VMEM((t_len, _GW), F32), pltpu.VMEM((t_len, LANE), F32), pltpu.VMEM((t_len, LANE), F32),
                        pltpu.VMEM((t_len, _GW), F32), pltpu.VMEM((SSD_STATE, _GW), F32)],
        compiler_params=_cparams("arbitrary", "arbitrary"), name="ssd_prompt")(
            z, z, z, conv_w, conv_w, conv_w, conv_b, conv_b, conv_b, p1, p2, z, d_vec, norm_w.reshape(1, SSD_INNER))


_FFN_TN = 256
_FFN_NJ = FFN_DIM // _FFN_TN
_TAIL = 8


def _ffn_up_prompt_kernel(a_ref, bg_ref, bu_ref, wg_ref, wu_ref, cg_ref, cu_ref, g_ref, sg_ref, su_ref, b_s, carry_ref,
                          *, tiles_per_seq, n_sub):
    i = pl.program_id(1)
    tn = _FFN_TN

    @pl.when(i == 0)
    def _():
        b_s[:, 0:tn] = bg_ref[...].astype(BF16)
        b_s[:, tn:2 * tn] = bu_ref[...].astype(BF16)

    @pl.when(i % tiles_per_seq == 0)
    def _():
        carry_ref[...] = jnp.zeros_like(carry_ref)

    prev = carry_ref[...]
    w = jnp.concatenate([wg_ref[...], wu_ref[...]], axis=1)
    c = jnp.concatenate([cg_ref[...], cu_ref[...]], axis=1)
    ts = a_ref.shape[0] // n_sub
    rows = lax.broadcasted_iota(jnp.int32, (ts, 2 * tn), 0)
    for s in range(n_sub):
        y = _dot(a_ref[s * ts:(s + 1) * ts, :], b_s[...])
        p1 = prev[_TAIL - 1:_TAIL, :]
        p2 = prev[_TAIL - 2:_TAIL - 1, :]
        y1 = jnp.where(rows == 0, p1, pltpu.roll(y, 1, axis=0))
        y2 = jnp.where(rows == 0, p2, jnp.where(rows == 1, p1, pltpu.roll(y, 2, axis=0)))
        u = w[0:1, :] * y2 + w[1:2, :] * y1 + w[2:3, :] * y + c
        g_ref[s * ts:(s + 1) * ts, :] = (_silu(u[:, 0:tn]) * u[:, tn:2 * tn]).astype(g_ref.dtype)
        prev = y[ts - _TAIL:ts, :]
    carry_ref[...] = prev
    sg_ref[...] = prev[:, 0:tn]
    su_ref[...] = prev[:, tn:2 * tn]


def _ffn_up_prompt(h2, w_up, layer, conv_w, conv_b, n_seq, t_len, tm):
    m = n_seq * t_len
    k = h2.shape[1]
    nj = _FFN_NJ
    tiles_per_seq = t_len // tm
    tail = jax.ShapeDtypeStruct((m // tm, _TAIL, FFN_DIM), F32)
    tail_spec = pl.BlockSpec((None, _TAIL, _FFN_TN), lambda j, i: (i, 0, j))
    return pl.pallas_call(
        functools.partial(_ffn_up_prompt_kernel, tiles_per_seq=tiles_per_seq, n_sub=2),
        out_shape=(jax.ShapeDtypeStruct((m, FFN_DIM), BF16), tail, tail),
        grid=(nj, m // tm),
        in_specs=[pl.BlockSpec((tm, k), lambda j, i: (i, 0)),
                  pl.BlockSpec((None, k, _FFN_TN), lambda j, i: (layer, 0, j)),
                  pl.BlockSpec((None, k, _FFN_TN), lambda j, i: (layer, 0, j + nj)),
                  pl.BlockSpec((FFN_CONV_W, _FFN_TN), lambda j, i: (0, j)),
                  pl.BlockSpec((FFN_CONV_W, _FFN_TN), lambda j, i: (0, j + nj)),
                  pl.BlockSpec((1, _FFN_TN), lambda j, i: (0, j)), pl.BlockSpec((1, _FFN_TN), lambda j, i: (0, j + nj))],
        out_specs=(pl.BlockSpec((tm, _FFN_TN), lambda j, i: (i, j)), tail_spec, tail_spec),
        scratch_shapes=[pltpu.VMEM((k, 2 * _FFN_TN), BF16), pltpu.VMEM((_TAIL, 2 * _FFN_TN), F32)],
        compiler_params=_cparams("arbitrary", "arbitrary"), name="ffn_up_prompt")(
            h2, w_up, w_up, conv_w, conv_w, conv_b, conv_b)


def _ffn_up_sample_kernel(a_ref, bg_ref, bu_ref, wg_ref, wu_ref, cg_ref, cu_ref, s0g_ref, s1g_ref, s0u_ref, s1u_ref,
                          g_ref, yg_ref, yu_ref):
    a = a_ref[...]
    yg = _dot(a, bg_ref[...].astype(BF16))
    yu = _dot(a, bu_ref[...].astype(BF16))
    ug = wg_ref[0:1, :] * s0g_ref[...] + wg_ref[1:2, :] * s1g_ref[...] + wg_ref[2:3, :] * yg + cg_ref[...]
    uu = wu_ref[0:1, :] * s0u_ref[...] + wu_ref[1:2, :] * s1u_ref[...] + wu_ref[2:3, :] * yu + cu_ref[...]
    g_ref[...] = (_silu(ug) * uu).astype(g_ref.dtype)
    yg_ref[...] = yg
    yu_ref[...] = yu


def _ffn_up_sample(h2, w_up, layer, conv_w, conv_b, s0, s1):
    m, k = h2.shape
    nj = _FFN_NJ
    lo = lambda j: (0, j)
    hi = lambda j: (0, j + nj)
    row = lambda f: pl.BlockSpec((m, _FFN_TN), f)
    ysd = jax.ShapeDtypeStruct((m, FFN_DIM), F32)
    return pl.pallas_call(
        _ffn_up_sample_kernel, out_shape=(jax.ShapeDtypeStruct((m, FFN_DIM), BF16), ysd, ysd), grid=(nj,),
        in_specs=[pl.BlockSpec((m, k), lambda j: (0, 0)),
                  pl.BlockSpec((None, k, _FFN_TN), lambda j: (layer, 0, j)),
                  pl.BlockSpec((None, k, _FFN_TN), lambda j: (layer, 0, j + nj)),
                  pl.BlockSpec((FFN_CONV_W, _FFN_TN), lo), pl.BlockSpec((FFN_CONV_W, _FFN_TN), hi),
                  pl.BlockSpec((1, _FFN_TN), lo), pl.BlockSpec((1, _FFN_TN), hi),
                  row(lo), row(lo), row(hi), row(hi)],
        out_specs=(row(lo), row(lo), row(lo)),
        compiler_params=_cparams("arbitrary"), name="ffn_up_sample")(
            h2, w_up, w_up, conv_w, conv_w, conv_b, conv_b, s0, s1, s0, s1)


_PAGES_PER_STEP = 16


def _kmean_kernel(pt_ref, *refs):
    o_ref = refs[-1]
    page = refs[0].shape[0]
    per_block = MOBA_BLOCK // page
    for n in range(_PAGES_PER_STEP // per_block):
        acc = jnp.sum(refs[n * per_block][...], axis=0)
        for r in range(1, per_block):
            acc = acc + jnp.sum(refs[n * per_block + r][...], axis=0)
        o_ref[n] = acc * (1.0 / MOBA_BLOCK)


def _kmean(cache_k, page_table, layer):
    n_b, n_pages = page_table.shape
    page, n_kv, d = cache_k.shape[2:]
    per_block = MOBA_BLOCK // page
    n_blocks = n_pages // per_block
    steps = n_pages // _PAGES_PER_STEP
    in_specs = [pl.BlockSpec((None, None, page, n_kv, d),
                             functools.partial(lambda b, s, pt, r: (layer, pt[b, s * _PAGES_PER_STEP + r], 0, 0, 0), r=r))
                for r in range(_PAGES_PER_STEP)]
    grid_spec = pltpu.PrefetchScalarGridSpec(
        num_scalar_prefetch=1, grid=(n_b, steps), in_specs=in_specs,
        out_specs=pl.BlockSpec((None, _PAGES_PER_STEP // per_block, n_kv, d), lambda b, s, pt: (b, s, 0, 0)))
    return pl.pallas_call(
        _kmean_kernel, out_shape=jax.ShapeDtypeStruct((n_b, n_blocks, n_kv, d), F32), grid_spec=grid_spec,
        compiler_params=_cparams("arbitrary", "arbitrary"), name="kmean")(page_table, *([cache_k] * _PAGES_PER_STEP))


def _select_kernel(q_ref, k_ref, kmean_ref, wq_ref, wk_ref, qn_ref, kn_ref, idx_ref):
    n_b, n_h, _ = q_ref.shape
    n_blocks = kmean_ref.shape[1]
    q = q_ref[...]
    qn = q * lax.rsqrt(jnp.mean(q * q, axis=-1, keepdims=True) + EPS) * wq_ref[...]
    qn_ref[...] = qn
    k = k_ref[...]
    kn_ref[...] = k * lax.rsqrt(jnp.mean(k * k, axis=-1, keepdims=True) + EPS) * wk_ref[...]
    head = lax.broadcasted_iota(jnp.int32, (n_h, n_blocks), 0)
    lane = lax.broadcasted_iota(jnp.int32, (n_h, n_blocks), 1).astype(F32)
    lane_out = lax.broadcasted_iota(jnp.int32, (n_h, LANE), 1)
    for b in range(n_b):
        gate = jnp.zeros((n_h, n_blocks), F32)
        for kv in range(ATT_KV_HEADS):
            gk = _dot_nt(qn[b], kmean_ref[b, :, kv * HEAD_DIM:(kv + 1) * HEAD_DIM], HIGHEST)
            gate = jnp.where(head // ATT_GROUP == kv, gk, gate)
        out = jnp.zeros((n_h, LANE), F32)
        for s in range(MOBA_TOPK):
            mx = jnp.max(gate, axis=-1, keepdims=True)
            pick = jnp.min(jnp.where(gate == mx, lane, float(n_blocks)), axis=-1, keepdims=True)
            out = jnp.where(lane_out == s, pick, out)
            gate = jnp.where(lane == pick, -jnp.inf, gate)
        idx_ref[b] = out.astype(jnp.int32)


def _select(q3, k3, kmean, wq, wk):
    n_b, n_h, d = q3.shape
    full = lambda shape: pl.BlockSpec(shape, lambda i: (0,) * len(shape))
    return pl.pallas_call(
        _select_kernel,
        out_shape=(jax.ShapeDtypeStruct(q3.shape, F32), jax.ShapeDtypeStruct(k3.shape, F32),
                   jax.ShapeDtypeStruct((n_b, n_h, LANE), jnp.int32)),
        grid=(1,),
        in_specs=[full(q3.shape), full(k3.shape), full(kmean.shape), full((1, 1, d)), full((1, 1, d))],
        out_specs=(full(q3.shape), full(k3.shape), full((n_b, n_h, LANE))),
        compiler_params=_cparams("arbitrary"), name="select")(q3, k3, kmean, wq.reshape(1, 1, d), wk.reshape(1, 1, d))


def _attn_sample_kernel(pt_ref, idx_ref, qn_ref, kn_ref, vn_ref, *refs, n_sel, per_block):
    o_ref = refs[-1]
    pages = refs[:-1]
    n_pg = n_sel * per_block
    k_refs, v_refs = pages[:n_pg], pages[n_pg:]
    h = pl.program_id(1)
    kv = h // ATT_GROUP
    q = qn_ref[pl.ds(h, 1), :] * (HEAD_DIM ** -0.5)
    k_new = kn_ref[...]
    v_new = vn_ref[...]
    s_own = jnp.sum(q * k_new, axis=-1, keepdims=True)
    logits = [jnp.sum(r[...] * q, axis=-1, keepdims=True) for r in k_refs]
    mx = s_own
    for s in logits:
        mx = jnp.maximum(mx, jnp.max(s, axis=0))
    p_own = jnp.exp(s_own - mx)
    den = p_own
    acc = p_own * v_new
    for s, v_ref in zip(logits, v_refs):
        p = jnp.exp(s - mx)
        den = den + jnp.sum(p, axis=0)
        acc = acc + jnp.sum(p * v_ref[...], axis=0)
    out = acc / den
    rows = lax.broadcasted_iota(jnp.int32, out.shape, 0)
    o_ref[pl.ds(h, 1), :] = jnp.sum(jnp.where(rows == kv, out, 0.0), axis=0, keepdims=True)


def _attn_sample(cache_k, cache_v, page_table, idx, qn, kn, vn, layer):
    n_b, n_h, d = qn.shape
    page, n_kv = cache_k.shape[2], cache_k.shape[3]
    per_block = MOBA_BLOCK // page
    n_sel = idx.shape[2]

    def page_spec(s, r):
        def imap(b, h, pt, ix):
            return (layer, pt[b, ix[b, h * n_sel + s] * per_block + r], 0, 0, 0)
        return pl.BlockSpec((None, None, page, n_kv, d), imap)

    pspecs = [page_spec(s, r) for s in range(n_sel) for r in range(per_block)]
    slab = lambda n: pl.BlockSpec((None, n, d), lambda b, h, pt, ix: (b, 0, 0))
    grid_spec = pltpu.PrefetchScalarGridSpec(
        num_scalar_prefetch=2, grid=(n_b, n_h),
        in_specs=[slab(n_h), slab(ATT_KV_HEADS), slab(ATT_KV_HEADS)] + pspecs + pspecs,
        out_specs=slab(n_h))
    n_pg = len(pspecs)
    return pl.pallas_call(
        functools.partial(_attn_sample_kernel, n_sel=n_sel, per_block=per_block),
        out_shape=jax.ShapeDtypeStruct((n_b, n_h, d), F32), grid_spec=grid_spec,
        compiler_params=_cparams("arbitrary", "arbitrary"), name="attn_sample")(
            page_table, idx.reshape(n_b, n_h * n_sel), qn, kn, vn, *([cache_k] * n_pg), *([cache_v] * n_pg))


def _rows_to_cols(x):
    r, n = x.shape
    return jnp.concatenate([x, jnp.zeros((n - r, n), x.dtype)], axis=0).T


def _mix_sample_kernel(dx_ref, dbuf_ref, dw_ref, dbeta_ref, dg_ref, dz_ref, dnw_ref, dst_ref,
                       sx_ref, sxbuf_ref, sxw_ref, sxb_ref, bc_ref, bcbuf_ref, bcw_ref, bcb_ref,
                       sdt_ref, sa_ref, sz_ref, sd_ref, snw_ref, sst_ref,
                       dno_ref, dnst_ref, so_ref, sso_ref, y_s):
    x = dx_ref[...]
    conv = x * dw_ref[CONV_W - 1]
    for i in range(CONV_W - 1):
        conv = conv + dbuf_ref[i] * dw_ref[i]
    act = _silu(conv)
    nh = DN_HEADS
    q, k, v = act[0:nh], act[nh:2 * nh], act[2 * nh:3 * nh]
    q = q * lax.rsqrt(jnp.sum(q * q, axis=-1, keepdims=True) + EPS) * (HEAD_DIM ** -0.5)
    k = k * lax.rsqrt(jnp.sum(k * k, axis=-1, keepdims=True) + EPS)
    beta = dbeta_ref[...]
    eg = jnp.exp(dg_ref[...])
    qk = jnp.sum(q * k, axis=-1, keepdims=True)
    qt, kt = _rows_to_cols(q), _rows_to_cols(k)
    o_rows = []
    for h in range(nh):
        s0 = dst_ref[h]
        kcol, qcol = kt[:, h:h + 1], qt[:, h:h + 1]
        ks = jnp.sum(kcol * s0, axis=0, keepdims=True)
        qs = jnp.sum(qcol * s0, axis=0, keepdims=True)
        b_h, e_h = beta[h:h + 1, :], eg[h:h + 1, :]
        v_new = v[h:h + 1, :] * b_h - (b_h * e_h) * ks
        o_rows.append(e_h * qs + qk[h:h + 1, :] * v_new)
        dnst_ref[h] = s0 * e_h + kcol * v_new
    o = jnp.concatenate(o_rows, axis=0)
    on = o * lax.rsqrt(jnp.mean(o * o, axis=-1, keepdims=True) + EPS) * dnw_ref[...]
    dno_ref[...] = on * _silu(dz_ref[...])
    xs = sx_ref[...] * sxw_ref[CONV_W - 1] + sxb_ref[...]
    bc = bc_ref[...] * bcw_ref[CONV_W - 1] + bcb_ref[...]
    for i in range(CONV_W - 1):
        xs = xs + sxbuf_ref[i] * sxw_ref[i]
        bc = bc + bcbuf_ref[i] * bcw_ref[i]
    xs = _silu(xs)
    bc = _silu(bc)
    bct = _rows_to_cols(bc)
    cb = jnp.sum(bc[0:SSD_GROUPS] * bc[SSD_GROUPS:2 * SSD_GROUPS], axis=-1, keepdims=True)
    dt = sdt_ref[...]
    ea = jnp.exp(sa_ref[...])
    for h in range(SSD_HEADS):
        grp = h // _GH
        s0 = sst_ref[h]
        bcol, ccol = bct[:, grp:grp + 1], bct[:, SSD_GROUPS + grp:SSD_GROUPS + grp + 1]
        xv = xs[h:h + 1, :] * dt[h:h + 1, :]
        e_h = ea[h:h + 1, :]
        y_s[h:h + 1, :] = cb[grp:grp + 1, :] * xv + e_h * jnp.sum(ccol * s0, axis=0, keepdims=True)
        sso_ref[h] = s0 * e_h + bcol * xv
    y = (y_s[...] + sd_ref[...] * xs) * _silu(sz_ref[...])
    norm_rows = []
    for grp in range(SSD_GROUPS):
        yg = y[grp * _GH:(grp + 1) * _GH]
        ms = jnp.sum(jnp.sum(yg * yg, axis=-1, keepdims=True), axis=0, keepdims=True) * (1.0 / _GW)
        norm_rows.append(yg * lax.rsqrt(ms + EPS))
    so_ref[...] = jnp.concatenate(norm_rows, axis=0) * snw_ref[...]


def _mix_sample(args, n_b):
    def per_b(shape):
        nd = len(shape) - 1
        return pl.BlockSpec((None,) + tuple(shape[1:]), lambda b: (b,) + (0,) * nd)

    def shared(shape):
        nd = len(shape)
        return pl.BlockSpec(tuple(shape), lambda b: (0,) * nd)

    names_per_b = {"dx", "dbuf", "dbeta", "dg", "dz", "dst", "sx", "sxbuf", "bc", "bcbuf", "sdt", "sa", "sz", "sst"}
    order = ["dx", "dbuf", "dw", "dbeta", "dg", "dz", "dnw", "dst", "sx", "sxbuf", "sxw", "sxb", "bc", "bcbuf", "bcw",
             "bcb", "sdt", "sa", "sz", "sd", "snw", "sst"]
    in_specs = [per_b(args[n].shape) if n in names_per_b else shared(args[n].shape) for n in order]
    outs = (jax.ShapeDtypeStruct((n_b, DN_HEADS, HEAD_DIM), F32),
            jax.ShapeDtypeStruct((n_b, DN_HEADS, HEAD_DIM, HEAD_DIM), F32),
            jax.ShapeDtypeStruct((n_b, SSD_HEADS, SSD_HEAD_DIM), F32),
            jax.ShapeDtypeStruct((n_b, SSD_HEADS, SSD_STATE, SSD_HEAD_DIM), F32))
    return pl.pallas_call(
        _mix_sample_kernel, out_shape=outs, grid=(n_b,), in_specs=in_specs,
        out_specs=tuple(per_b(o.shape) for o in outs),
        scratch_shapes=[pltpu.VMEM((SSD_HEADS, SSD_HEAD_DIM), F32)],
        compiler_params=_cparams("arbitrary"), name="mix_sample")(*[args[n] for n in order])


def _pack_w_in(w_in):
    depth, k, _ = w_in.shape
    small = jnp.concatenate([w_in[:, :, 7168:7184], w_in[:, :, 9744:9760],
                             jnp.zeros((depth, k, N_IN_PACKED - SMALL_OFF - 32), w_in.dtype)], axis=2)
    return jnp.concatenate([w_in[:, :, :7168], w_in[:, :, 7184:9744], small], axis=2).astype(BF16)


def _lane_vec(dn_vals, ssd_vals):
    v = jnp.zeros((LANE,), F32)
    v = v.at[DNG_LANE:DNG_LANE + DN_HEADS].set(dn_vals.astype(F32))
    v = v.at[SSD_LANE:SSD_LANE + SSD_HEADS].set(ssd_vals.astype(F32))
    return v.reshape(1, LANE)


def _ffn_down(g, w_down, layer, x, tm):
    half = FFN_DIM // 2
    for part in range(2):
        x = _matmul([(g, half, part, part)], w_down, layer, tm=tm, tn=256, res=x, name="ffn_down")
    return x


def _prompt_layer(x, lw, layer, n_seq, t_len):
    tm = 1024
    h = _rmsnorm_cast(x, lw["norm1_w"], 512)
    z = _matmul([(h, D_MODEL, 0, 0)], lw["w_in_packed"], layer, tm=tm, tn=768, name="in_proj")
    p1, p2 = _aux(z, lw["bias_vec"], lw["alog_vec"], n_seq, t_len, CHUNK)
    o_att, k_norm = _attn_prompt(z, lw["attn_q_norm_w"], lw["attn_k_norm_w"], n_seq, t_len)
    o_dn, dn_state = _gdn_prompt(z, p1, p2, lw["dn_conv_w"], lw["dn_norm_w"], n_seq, t_len)
    o_ssd, ssd_state = _ssd_prompt(z, p1, p2, lw["ssd_conv_w"], lw["ssd_conv_b"].reshape(1, -1), lw["ssd_d_vec"],
                                   lw["ssd_norm_w"], n_seq, t_len)
    x = _matmul([(o_att, 2048, 0, 0), (o_dn, 1024, 0, 2), (o_ssd, 1024, 0, 3)], lw["w_out"], layer, tm=tm, tn=512,
                res=x, name="out_proj")
    h2 = _rmsnorm_cast(x, lw["norm2_w"], 512)
    g, tail_g, tail_u = _ffn_up_prompt(h2, lw["ffn_w_up"], layer, lw["ffn_conv_w"], lw["ffn_conv_b"].reshape(1, -1),
                                       n_seq, t_len, tm)
    x = _ffn_down(g, lw["ffn_w_down"], layer, x, tm)
    z3 = z.reshape(n_seq, t_len, N_IN_PACKED)
    tps = t_len // tm
    new = (k_norm.reshape(n_seq, t_len, ATT_KV_HEADS, HEAD_DIM),
           z3[:, :, V_OFF:V_OFF + 512].reshape(n_seq, t_len, ATT_KV_HEADS, HEAD_DIM),
           z3[:, t_len - (CONV_W - 1):, DNQKV_OFF:DNQKV_OFF + DN_CONV_CH],
           dn_state,
           z3[:, t_len - (CONV_W - 1):, SSDX_OFF:SSDX_OFF + SSD_CONV_CH],
           ssd_state,
           jnp.concatenate([tail_g[tps - 1::tps, _TAIL - 2:], tail_u[tps - 1::tps, _TAIL - 2:]], axis=-1))
    return x, new


_SAMPLE_ROWS = 16


def _sample_layer(x, lw, layer, cache_k, cache_v, page_table, dn_conv_buf, dn_state, ssd_conv_buf, ssd_state,
                  ffn_conv_buf):
    n_b = page_table.shape[0]
    rows = x.shape[0]
    h = _rmsnorm_cast(x, lw["norm1_w"], rows)
    z = _matmul([(h, D_MODEL, 0, 0)], lw["w_in_packed"], layer, tm=rows, tn=768, name="in_proj_s")
    p1, p2 = _aux(z, lw["bias_vec"], lw["alog_vec"], 1, rows, 1)
    zb = z[:n_b]
    kmean = _kmean(cache_k, page_table, layer)
    kmean = kmean.reshape(n_b, kmean.shape[1], ATT_KV_HEADS * HEAD_DIM)
    q3 = zb[:, Q_OFF:Q_OFF + 2048].reshape(n_b, ATT_HEADS, HEAD_DIM)
    k3 = zb[:, K_OFF:K_OFF + 512].reshape(n_b, ATT_KV_HEADS, HEAD_DIM)
    v3 = zb[:, V_OFF:V_OFF + 512].reshape(n_b, ATT_KV_HEADS, HEAD_DIM)
    qn, kn, idx = _select(q3, k3, kmean, lw["attn_q_norm_w"], lw["attn_k_norm_w"])
    o_att = _attn_sample(cache_k, cache_v, page_table, idx[:, :, :MOBA_TOPK], qn, kn, v3, layer)
    dn_x = zb[:, DNQKV_OFF:DNQKV_OFF + DN_CONV_CH]
    ssd_x = zb[:, SSDX_OFF:SSDX_OFF + SSD_CONV_CH]
    nh3 = 3 * DN_HEADS
    ng2 = 2 * SSD_GROUPS
    args = {
        "dx": dn_x.reshape(n_b, nh3, HEAD_DIM),
        "dbuf": dn_conv_buf.reshape(n_b, CONV_W - 1, nh3, HEAD_DIM),
        "dw": lw["dn_conv_w"].reshape(CONV_W, nh3, HEAD_DIM),
        "dbeta": p1[:n_b, BETA_LANE:BETA_LANE + DN_HEADS].reshape(n_b, DN_HEADS, 1),
        "dg": p2[:n_b, DNG_LANE:DNG_LANE + DN_HEADS].reshape(n_b, DN_HEADS, 1),
        "dz": zb[:, DNZ_OFF:DNZ_OFF + DN_WIDTH].reshape(n_b, DN_HEADS, HEAD_DIM),
        "dnw": lw["dn_norm_w"].reshape(1, HEAD_DIM),
        "dst": dn_state,
        "sx": ssd_x[:, :SSD_INNER].reshape(n_b, SSD_HEADS, SSD_HEAD_DIM),
        "sxbuf": ssd_conv_buf[:, :, :SSD_INNER].reshape(n_b, CONV_W - 1, SSD_HEADS, SSD_HEAD_DIM),
        "sxw": lw["ssd_conv_w"][:, :SSD_INNER].reshape(CONV_W, SSD_HEADS, SSD_HEAD_DIM),
        "sxb": lw["ssd_conv_b"][:SSD_INNER].reshape(SSD_HEADS, SSD_HEAD_DIM),
        "bc": ssd_x[:, SSD_INNER:].reshape(n_b, ng2, SSD_STATE),
        "bcbuf": ssd_conv_buf[:, :, SSD_INNER:].reshape(n_b, CONV_W - 1, ng2, SSD_STATE),
        "bcw": lw["ssd_conv_w"][:, SSD_INNER:].reshape(CONV_W, ng2, SSD_STATE),
        "bcb": lw["ssd_conv_b"][SSD_INNER:].reshape(ng2, SSD_STATE),
        "sdt": p1[:n_b, SSD_LANE:SSD_LANE + SSD_HEADS].reshape(n_b, SSD_HEADS, 1),
        "sa": p2[:n_b, SSD_LANE:SSD_LANE + SSD_HEADS].reshape(n_b, SSD_HEADS, 1),
        "sz": zb[:, SSDZ_OFF:SSDZ_OFF + SSD_INNER].reshape(n_b, SSD_HEADS, SSD_HEAD_DIM),
        "sd": lw["ssd_D"].astype(F32).reshape(SSD_HEADS, 1),
        "snw": lw["ssd_norm_w"].reshape(SSD_HEADS, SSD_HEAD_DIM),
        "sst": ssd_state,
    }
    o_dn, dn_state_new, o_ssd, ssd_state_new = _mix_sample(args, n_b)
    pad = lambda a: jnp.pad(a.reshape(n_b, -1), ((0, rows - n_b), (0, 0))).astype(BF16)
    x = _matmul([(pad(o_att), 2048, 0, 0), (pad(o_dn), 1024, 0, 2), (pad(o_ssd), 1024, 0, 3)], lw["w_out"], layer,
                tm=rows, tn=512, res=x, name="out_proj_s")
    h2 = _rmsnorm_cast(x, lw["norm2_w"], rows)
    padf = lambda a: jnp.pad(a, ((0, rows - n_b), (0, 0)))
    g, y_g, y_u = _ffn_up_sample(h2, lw["ffn_w_up"], layer, lw["ffn_conv_w"], lw["ffn_conv_b"].reshape(1, -1),
                                 padf(ffn_conv_buf[:, 0]), padf(ffn_conv_buf[:, 1]))
    x = _ffn_down(g, lw["ffn_w_down"], layer, x, rows)
    y_new = jnp.concatenate([y_g[:n_b], y_u[:n_b]], axis=-1)
    new = (kn.reshape(n_b, 1, ATT_KV_HEADS, HEAD_DIM), v3.reshape(n_b, 1, ATT_KV_HEADS, HEAD_DIM),
           jnp.concatenate([dn_conv_buf[:, 1:], dn_x[:, None]], axis=1), dn_state_new,
           jnp.concatenate([ssd_conv_buf[:, 1:], ssd_x[:, None]], axis=1), ssd_state_new,
           jnp.stack([ffn_conv_buf[:, 1], y_new], axis=1))
    return x, new


def kernel(x_prompt, x_sample, cache_k, cache_v, page_table, state_dn_conv, state_dn, state_ssd_conv, state_ssd, state_ffn_conv, norm1_w, w_in, attn_q_norm_w, attn_k_norm_w, dn_conv_w, dn_A_log, dn_dt_bias, dn_norm_w, ssd_conv_w, ssd_conv_b, ssd_dt_bias, ssd_A_log, ssd_D, ssd_norm_w, w_out, norm2_w, ffn_w_up, ffn_conv_w, ffn_conv_b, ffn_w_down):
    depth = w_in.shape[0]
    n_seq, t_len, d_model = x_prompt.shape
    n_b = x_sample.shape[0]
    yp = x_prompt.reshape(n_seq * t_len, d_model)
    ys = jnp.pad(x_sample.reshape(n_b, d_model), ((0, _SAMPLE_ROWS - n_b), (0, 0)))
    w_in_packed = _pack_w_in(w_in)
    outs_p, outs_s = [], []
    for l in range(depth):
        lw = {"norm1_w": norm1_w[l], "w_in_packed": w_in_packed, "attn_q_norm_w": attn_q_norm_w[l],
              "attn_k_norm_w": attn_k_norm_w[l], "dn_conv_w": dn_conv_w[l], "dn_norm_w": dn_norm_w[l],
              "ssd_conv_w": ssd_conv_w[l], "ssd_conv_b": ssd_conv_b[l], "ssd_D": ssd_D[l],
              "ssd_norm_w": ssd_norm_w[l], "w_out": w_out, "norm2_w": norm2_w[l], "ffn_w_up": ffn_w_up,
              "ffn_conv_w": ffn_conv_w[l], "ffn_conv_b": ffn_conv_b[l], "ffn_w_down": ffn_w_down,
              "bias_vec": _lane_vec(dn_dt_bias[l], ssd_dt_bias[l]),
              "alog_vec": _lane_vec(dn_A_log[l], ssd_A_log[l]),
              "ssd_d_vec": jnp.repeat(ssd_D[l].astype(F32), SSD_HEAD_DIM).reshape(1, SSD_INNER)}
        yp, new_p = _prompt_layer(yp, lw, l, n_seq, t_len)
        ys, new_s = _sample_layer(ys, lw, l, cache_k, cache_v, page_table, state_dn_conv[l], state_dn[l],
                                  state_ssd_conv[l], state_ssd[l], state_ffn_conv[l])
        outs_p.append(new_p)
        outs_s.append(new_s)
    st = lambda outs, i: jnp.stack([o[i] for o in outs])
    return (yp.reshape(n_seq, t_len, d_model), ys[:n_b].reshape(n_b, 1, d_model),
            st(outs_p, 0), st(outs_p, 1), st(outs_s, 0), st(outs_s, 1),
            st(outs_p, 2), st(outs_s, 2), st(outs_p, 3), st(outs_s, 3),
            st(outs_p, 4), st(outs_s, 4), st(outs_p, 5), st(outs_s, 5),
            st(outs_p, 6), st(outs_s, 6))
```

```python
import functools
import math

import jax
import jax.numpy as jnp
from jax import lax
from jax.experimental import pallas as pl
from jax.experimental.pallas import tpu as pltpu

F32 = jnp.float32
BF16 = jnp.bfloat16
HIGHEST = lax.Precision.HIGHEST

D_MODEL = 4096
HEAD_DIM = 128
ATT_HEADS = 16
ATT_KV_HEADS = 4
ATT_GROUP = 4
MOBA_BLOCK = 256
MOBA_TOPK = 3
DN_HEADS = 8
DN_WIDTH = 1024
DN_CONV_CH = 3072
SSD_INNER = 1024
SSD_HEAD_DIM = 64
SSD_HEADS = 16
SSD_GROUPS = 2
SSD_STATE = 128
SSD_CONV_CH = 1536
CONV_W = 4
CHUNK = 64
FFN_DIM = 11008
FFN_CONV_W = 3
EPS = 1e-6

Q_OFF, K_OFF, V_OFF = 0, 2048, 2560
DNQKV_OFF, DNZ_OFF = 3072, 6144
SSDX_OFF, SSDZ_OFF = 7168, 8704
SMALL_OFF = 9728
N_IN_PACKED = 9984
LANE = 128
BETA_LANE, DNG_LANE, SSD_LANE = 0, 8, 16

VMEM_LIMIT_BYTES = 56 * 1024 * 1024
NEG_BIG = -1e30


def _cparams(*sem):
    return pltpu.CompilerParams(dimension_semantics=sem, vmem_limit_bytes=VMEM_LIMIT_BYTES)


def _silu(x):
    return x / (1.0 + jnp.exp(-x))


def _sigmoid(x):
    return 1.0 / (1.0 + jnp.exp(-x))


def _softplus(x):
    return jnp.maximum(x, 0.0) + jnp.log1p(jnp.exp(-jnp.abs(x)))


def _dot(a, b, precision=None):
    return jnp.dot(a, b, preferred_element_type=F32, precision=precision)


def _dot_nt(a, b, precision=None):
    return lax.dot_general(a, b, (((1,), (1,)), ((), ())), preferred_element_type=F32, precision=precision)


def _shift_rows(x, s):
    rows = lax.broadcasted_iota(jnp.int32, x.shape, 0)
    return jnp.where(rows < s, 0.0, pltpu.roll(x, s, axis=0))


def _causal_conv_rows(x, w_ref, width):
    y = x * w_ref[width - 1:width, :]
    for i in range(width - 1):
        y = y + _shift_rows(x, width - 1 - i) * w_ref[i:i + 1, :]
    return y


def _lane_col(x, lane):
    lanes = lax.broadcasted_iota(jnp.int32, x.shape, 1)
    return jnp.sum(jnp.where(lanes == lane, x, 0.0), axis=-1, keepdims=True)


def _decay_matrix(gc_col, c):
    ii = lax.broadcasted_iota(jnp.int32, (c, c), 0)
    jj = lax.broadcasted_iota(jnp.int32, (c, c), 1)
    gcb = jnp.broadcast_to(gc_col, (c, c))
    gc_row = jnp.sum(jnp.where(ii == jj, gcb, 0.0), axis=0, keepdims=True)
    low = ii >= jj
    gam = jnp.where(low, jnp.exp(jnp.where(low, gcb - gc_row, 0.0)), 0.0)
    return gam, ii, jj


def _rmsnorm_kernel(x_ref, w_ref, o_ref):
    x = x_ref[...]
    ms = jnp.mean(x * x, axis=-1, keepdims=True)
    o_ref[...] = (x * lax.rsqrt(ms + EPS) * w_ref[...]).astype(o_ref.dtype)


def _rmsnorm_cast(x, w, tm):
    m, d = x.shape
    return pl.pallas_call(
        _rmsnorm_kernel, out_shape=jax.ShapeDtypeStruct((m, d), BF16), grid=(m // tm,),
        in_specs=[pl.BlockSpec((tm, d), lambda i: (i, 0)), pl.BlockSpec((1, d), lambda i: (0, 0))],
        out_specs=pl.BlockSpec((tm, d), lambda i: (i, 0)),
        compiler_params=_cparams("arbitrary"), name="rmsnorm")(x, w.reshape(1, d))


def _matmul_kernel(*refs, n_parts, has_res):
    a_refs = refs[:n_parts]
    b_refs = refs[n_parts:2 * n_parts]
    o_ref = refs[-1]
    acc = None
    for a_ref, b_ref in zip(a_refs, b_refs):
        d = _dot(a_ref[...], b_ref[...].astype(BF16))
        acc = d if acc is None else acc + d
    if has_res:
        acc = acc + refs[2 * n_parts][...]
    o_ref[...] = acc.astype(o_ref.dtype)


def _matmul(a_parts, b, layer, *, tm, tn, res=None, out_dtype=F32, name="matmul"):
    m = a_parts[0][0].shape[0]
    n = b.shape[2]
    in_specs, args = [], []
    for arr, kp, cb, rb in a_parts:
        in_specs.append(pl.BlockSpec((tm, kp), functools.partial(lambda i, j, cb: (i, cb), cb=cb)))
        args.append(arr)
    for arr, kp, cb, rb in a_parts:
        in_specs.append(pl.BlockSpec((None, kp, tn), functools.partial(lambda i, j, rb: (layer, rb, j), rb=rb)))
        args.append(b)
    if res is not None:
        in_specs.append(pl.BlockSpec((tm, tn), lambda i, j: (i, j)))
        args.append(res)
    return pl.pallas_call(
        functools.partial(_matmul_kernel, n_parts=len(a_parts), has_res=res is not None),
        out_shape=jax.ShapeDtypeStruct((m, n), out_dtype), grid=(m // tm, n // tn),
        in_specs=in_specs, out_specs=pl.BlockSpec((tm, tn), lambda i, j: (i, j)),
        compiler_params=_cparams("arbitrary", "arbitrary"), name=name)(*args)


def _aux_kernel(s_ref, bias_ref, alog_ref, p1_ref, p2_ref, *, t_len, chunk):
    x = s_ref[...]
    lanes = lax.broadcasted_iota(jnp.int32, x.shape, 1)
    sp = _softplus(x + bias_ref[...])
    p1_ref[...] = jnp.where(lanes < DNG_LANE, _sigmoid(x), sp)
    g = -jnp.exp(alog_ref[...]) * sp
    if chunk == 1:
        p2_ref[...] = g
    else:
        ii = lax.broadcasted_iota(jnp.int32, (chunk, chunk), 0)
        jj = lax.broadcasted_iota(jnp.int32, (chunk, chunk), 1)
        tril = jnp.where(ii >= jj, 1.0, 0.0).astype(F32)
        for c in range(t_len // chunk):
            p2_ref[c * chunk:(c + 1) * chunk, :] = _dot(tril, g[c * chunk:(c + 1) * chunk, :], HIGHEST)


def _aux(z, bias_vec, alog_vec, n_seq, t_len, chunk):
    m = n_seq * t_len
    blk = pl.BlockSpec((t_len, LANE), lambda b: (b, SMALL_OFF // LANE))
    vec = pl.BlockSpec((1, LANE), lambda b: (0, 0))
    out = pl.BlockSpec((t_len, LANE), lambda b: (b, 0))
    return pl.pallas_call(
        functools.partial(_aux_kernel, t_len=t_len, chunk=chunk),
        out_shape=(jax.ShapeDtypeStruct((m, LANE), F32), jax.ShapeDtypeStruct((m, LANE), F32)),
        grid=(n_seq,), in_specs=[blk, vec, vec], out_specs=(out, out),
        compiler_params=_cparams("arbitrary"), name="aux")(z, bias_vec, alog_vec)


def _attn_prompt_kernel(q_ref, k_ref, v_ref, wq_ref, wk_ref, o_ref, kn_ref, kb_ref, vt_ref, kmean_ref,
                        *, t_len):
    nb = t_len // MOBA_BLOCK
    g = pl.program_id(2)

    @pl.when(g == 0)
    def _():
        k = k_ref[...]
        kn = k * lax.rsqrt(jnp.mean(k * k, axis=-1, keepdims=True) + EPS) * wk_ref[...]
        kn_ref[...] = kn
        kb_ref[...] = kn.astype(BF16)
        for n in range(nb):
            kmean_ref[n:n + 1, :] = jnp.mean(kn[n * MOBA_BLOCK:(n + 1) * MOBA_BLOCK, :], axis=0, keepdims=True)
        vt_ref[...] = v_ref[...].T.astype(BF16)

    q = q_ref[...]
    qn = q * lax.rsqrt(jnp.mean(q * q, axis=-1, keepdims=True) + EPS) * wq_ref[...]
    gate = _dot_nt(kmean_ref[...], qn, HIGHEST)
    blk = lax.broadcasted_iota(jnp.int32, (nb, t_len), 0)
    own = lax.broadcasted_iota(jnp.int32, (nb, t_len), 1) // MOBA_BLOCK
    valid = blk < own
    gm = jnp.where(valid, gate, -jnp.inf)
    cnt = jnp.zeros((nb, t_len), F32)
    for m in range(nb):
        row = gm[m:m + 1, :]
        beats = jnp.where(row > gm, 1.0, jnp.where(row == gm, jnp.where(blk > m, 1.0, 0.0), 0.0))
        cnt = cnt + beats
    bias = jnp.where(valid, jnp.where(cnt < MOBA_TOPK, 0.0, NEG_BIG), NEG_BIG)
    qs = (qn * (HEAD_DIM ** -0.5)).astype(BF16)
    kk = lax.broadcasted_iota(jnp.int32, (MOBA_BLOCK, MOBA_BLOCK), 0)
    qq = lax.broadcasted_iota(jnp.int32, (MOBA_BLOCK, MOBA_BLOCK), 1)
    causal = jnp.where(kk <= qq, 0.0, NEG_BIG)
    for qi in range(nb):
        n_keys = (qi + 1) * MOBA_BLOCK
        qt = qs[qi * MOBA_BLOCK:(qi + 1) * MOBA_BLOCK, :]
        st = _dot_nt(kb_ref[0:n_keys, :], qt)
        pieces = [jnp.broadcast_to(bias[n:n + 1, qi * MOBA_BLOCK:(qi + 1) * MOBA_BLOCK], (MOBA_BLOCK, MOBA_BLOCK))
                  for n in range(qi)]
        pieces.append(causal)
        st = st + (jnp.concatenate(pieces, axis=0) if qi else causal)
        mx = jnp.max(st, axis=0, keepdims=True)
        p = jnp.exp(st - mx)
        den = jnp.sum(p, axis=0, keepdims=True)
        ot = _dot(vt_ref[:, 0:n_keys], p.astype(BF16)) / den
        o_ref[qi * MOBA_BLOCK:(qi + 1) * MOBA_BLOCK, :] = ot.T.astype(o_ref.dtype)


def _attn_prompt(z, wq, wk, n_seq, t_len):
    m = n_seq * t_len
    qblk = pl.BlockSpec((t_len, HEAD_DIM), lambda b, k, g: (b, Q_OFF // HEAD_DIM + k * ATT_GROUP + g))
    kblk = pl.BlockSpec((t_len, HEAD_DIM), lambda b, k, g: (b, K_OFF // HEAD_DIM + k))
    vblk = pl.BlockSpec((t_len, HEAD_DIM), lambda b, k, g: (b, V_OFF // HEAD_DIM + k))
    wspec = pl.BlockSpec((1, HEAD_DIM), lambda b, k, g: (0, 0))
    return pl.pallas_call(
        functools.partial(_attn_prompt_kernel, t_len=t_len),
        out_shape=(jax.ShapeDtypeStruct((m, ATT_HEADS * HEAD_DIM), BF16),
                   jax.ShapeDtypeStruct((m, ATT_KV_HEADS * HEAD_DIM), F32)),
        grid=(n_seq, ATT_KV_HEADS, ATT_GROUP),
        in_specs=[qblk, kblk, vblk, wspec, wspec],
        out_specs=(pl.BlockSpec((t_len, HEAD_DIM), lambda b, k, g: (b, k * ATT_GROUP + g)),
                   pl.BlockSpec((t_len, HEAD_DIM), lambda b, k, g: (b, k))),
        scratch_shapes=[pltpu.VMEM((t_len, HEAD_DIM), BF16), pltpu.VMEM((HEAD_DIM, t_len), BF16),
                        pltpu.VMEM((t_len // MOBA_BLOCK, HEAD_DIM), F32)],
        compiler_params=_cparams("arbitrary", "arbitrary", "arbitrary"), name="attn_prompt")(
            z, z, z, wq.reshape(1, HEAD_DIM), wk.reshape(1, HEAD_DIM))


def _tri_inverse(a, c):
    ii = lax.broadcasted_iota(jnp.int32, (c, c), 0)
    jj = lax.broadcasted_iota(jnp.int32, (c, c), 1)
    eye = jnp.where(ii == jj, 1.0, 0.0).astype(F32)
    p = -a
    inv = eye + p
    steps = int(math.log2(c)) - 1
    for _ in range(steps):
        p = _dot(p, p)
        inv = inv + _dot(inv, p)
    return inv


_GDN_UNROLL = 8


def _gdn_prompt_kernel(zq_ref, zk_ref, zv_ref, wq_ref, wk_ref, wv_ref, p1_ref, p2_ref, zg_ref, nw_ref,
                       o_ref, s_ref, q_s, k_s, v_s, beta_s, gc_s, qp_s, op_s, mn_s, nn_s, *, t_len):
    h = pl.program_id(1)
    c = CHUNK
    d = HEAD_DIM

    def l2n(x):
        return x * lax.rsqrt(jnp.sum(x * x, axis=-1, keepdims=True) + EPS)

    q_s[...] = l2n(_silu(_causal_conv_rows(zq_ref[...], wq_ref, CONV_W))) * (HEAD_DIM ** -0.5)
    k_s[...] = l2n(_silu(_causal_conv_rows(zk_ref[...], wk_ref, CONV_W)))
    v_s[...] = _silu(_causal_conv_rows(zv_ref[...], wv_ref, CONV_W))
    beta_s[...] = jnp.broadcast_to(_lane_col(p1_ref[...], BETA_LANE + h), (t_len, HEAD_DIM))
    gc_s[...] = jnp.broadcast_to(_lane_col(p2_ref[...], DNG_LANE + h), (t_len, HEAD_DIM))

    ii = lax.broadcasted_iota(jnp.int32, (c, c), 0)
    jj = lax.broadcasted_iota(jnp.int32, (c, c), 1)
    eye = jnp.where(ii == jj, 1.0, 0.0).astype(F32)

    def prepare_group(gi, carry):
        n = _GDN_UNROLL
        chunks = [gi * n + u for u in range(n)]
        rows = [pl.ds(pl.multiple_of(ci * c, c), c) for ci in chunks]
        q = [q_s[r, :] for r in rows]
        k = [k_s[r, :] for r in rows]
        gcb = [gc_s[r, :] for r in rows]
        kb = [k[u] * beta_s[rows[u], :] for u in range(n)]
        gam = [_decay_matrix(gcb[u][:, 0:1], c)[0] for u in range(n)]
        kq = [_dot_nt(jnp.concatenate([kb[u], q[u]], axis=0), k[u]) for u in range(n)]
        attn = [kq[u][c:2 * c] * gam[u] for u in range(n)]
        p = [-jnp.where(ii > jj, kq[u][0:c] * gam[u], 0.0) for u in range(n)]
        inv = [eye + p[u] for u in range(n)]
        for _ in range(int(math.log2(c)) - 1):
            p = [_dot(p[u], p[u]) for u in range(n)]
            inv = [inv[u] + _dot(inv[u], p[u]) for u in range(n)]
        eg = [jnp.exp(gcb[u]) for u in range(n)]
        rhs = [jnp.concatenate([v_s[rows[u], :] * beta_s[rows[u], :], kb[u] * eg[u]], axis=1) for u in range(n)]
        uw = [_dot(inv[u], rhs[u]) for u in range(n)]
        auw = [_dot(attn[u], uw[u]) for u in range(n)]
        k_dec = [k[u] * jnp.exp(gcb[u][c - 1:c, :] - gcb[u]) for u in range(n)]
        kuw = [_dot(k_dec[u].T, uw[u]) for u in range(n)]
        for u in range(n):
            rd = pl.ds(pl.multiple_of(chunks[u] * d, d), d)
            op_s[rows[u], :] = auw[u][:, 0:d]
            qp_s[rows[u], :] = eg[u] * q[u] - auw[u][:, d:2 * d]
            nn_s[rd, :] = kuw[u][:, 0:d]
            mn_s[rd, :] = -kuw[u][:, d:2 * d]
        return carry

    lax.fori_loop(0, t_len // (c * _GDN_UNROLL), prepare_group, 0)

    def chunk_step(ci, s):
        r = pl.ds(pl.multiple_of(ci * c, c), c)
        rd = pl.ds(pl.multiple_of(ci * d, d), d)
        ms = _dot(jnp.concatenate([mn_s[rd, :], qp_s[r, :]], axis=0), s)
        op_s[r, :] = ms[d:d + c] + op_s[r, :]
        e = jnp.exp(gc_s[pl.ds(ci * c + c - 1, 1), :])
        return s * e[:, 0:1] + ms[0:d] + nn_s[rd, :]

    s_fin = lax.fori_loop(0, t_len // c, chunk_step, jnp.zeros((HEAD_DIM, HEAD_DIM), F32))
    s_ref[...] = s_fin
    o = op_s[...]
    on = o * lax.rsqrt(jnp.mean(o * o, axis=-1, keepdims=True) + EPS) * nw_ref[...]
    o_ref[...] = (on * _silu(zg_ref[...])).astype(o_ref.dtype)


def _gdn_prompt(z, p1, p2, conv_w, norm_w, n_seq, t_len):
    m = n_seq * t_len
    base = DNQKV_OFF // HEAD_DIM

    def zcol(off):
        return pl.BlockSpec((t_len, HEAD_DIM), functools.partial(lambda b, h, off: (b, off + h), off=off))

    def wcol(off):
        return pl.BlockSpec((CONV_W, HEAD_DIM), functools.partial(lambda b, h, off: (0, off + h), off=off))

    aux = pl.BlockSpec((t_len, LANE), lambda b, h: (b, 0))
    tbuf = pltpu.VMEM((t_len, HEAD_DIM), F32)
    return pl.pallas_call(
        functools.partial(_gdn_prompt_kernel, t_len=t_len),
        out_shape=(jax.ShapeDtypeStruct((m, DN_WIDTH), BF16),
                   jax.ShapeDtypeStruct((n_seq, DN_HEADS, HEAD_DIM, HEAD_DIM), F32)),
        grid=(n_seq, DN_HEADS),
        in_specs=[zcol(base), zcol(base + DN_HEADS), zcol(base + 2 * DN_HEADS),
                  wcol(0), wcol(DN_HEADS), wcol(2 * DN_HEADS), aux, aux,
                  zcol(DNZ_OFF // HEAD_DIM), pl.BlockSpec((1, HEAD_DIM), lambda b, h: (0, 0))],
        out_specs=(pl.BlockSpec((t_len, HEAD_DIM), lambda b, h: (b, h)),
                   pl.BlockSpec((None, None, HEAD_DIM, HEAD_DIM), lambda b, h: (b, h, 0, 0))),
        scratch_shapes=[tbuf] * 7 + [pltpu.VMEM((t_len // CHUNK * HEAD_DIM, HEAD_DIM), F32)] * 2,
        compiler_params=_cparams("arbitrary", "arbitrary"), name="gdn_prompt")(
            z, z, z, conv_w, conv_w, conv_w, p1, p2, z, norm_w.reshape(1, HEAD_DIM))


_GH = SSD_HEADS // SSD_GROUPS
_GW = _GH * SSD_HEAD_DIM


def _ssd_prompt_kernel(zx_ref, zb_ref, zc_ref, wx_ref, wb_ref, wc_ref, bx_ref, bb_ref, bc_ref, p1_ref, p2_ref,
                       zg_ref, d_ref, nw_ref, o_ref, st_ref, x_s, b_s, c_s, y_s, state_s, *, t_len):
    grp = pl.program_id(1)
    c = CHUNK
    x_s[...] = _silu(_causal_conv_rows(zx_ref[...], wx_ref, CONV_W) + bx_ref[...])
    b_s[...] = _silu(_causal_conv_rows(zb_ref[...], wb_ref, CONV_W) + bb_ref[...])
    c_s[...] = _silu(_causal_conv_rows(zc_ref[...], wc_ref, CONV_W) + bc_ref[...])
    state_s[...] = jnp.zeros_like(state_s)

    def chunk_step(ci, carry):
        r = pl.ds(pl.multiple_of(ci * c, c), c)
        x, bm, cm = x_s[r, :], b_s[r, :], c_s[r, :]
        p1, p2 = p1_ref[r, :], p2_ref[r, :]
        cb = _dot_nt(cm, bm)
        s_prev = state_s[...]
        cs_prev = _dot(cm, s_prev)
        y_parts, xs_parts, dec_parts = [], [], []
        for hh in range(_GH):
            lane = SSD_LANE + grp * _GH + hh
            ac = _lane_col(p2, lane)
            dt = _lane_col(p1, lane)
            gam, _, _ = _decay_matrix(ac, c)
            xv = x[:, hh * SSD_HEAD_DIM:(hh + 1) * SSD_HEAD_DIM] * dt
            a_last = ac[c - 1:c, :]
            y_h = _dot(cb * gam, xv) + jnp.exp(ac) * cs_prev[:, hh * SSD_HEAD_DIM:(hh + 1) * SSD_HEAD_DIM]
            y_parts.append(y_h)
            xs_parts.append(xv * jnp.exp(a_last - ac))
            dec_parts.append(jnp.broadcast_to(jnp.exp(a_last), (1, SSD_HEAD_DIM)))
        y_s[r, :] = jnp.concatenate(y_parts, axis=1)
        new_states = _dot(bm.T, jnp.concatenate(xs_parts, axis=1))
        state_s[...] = s_prev * jnp.concatenate(dec_parts, axis=1) + new_states
        return carry

    lax.fori_loop(0, t_len // c, chunk_step, 0)
    for hh in range(_GH):
        st_ref[hh] = state_s[:, hh * SSD_HEAD_DIM:(hh + 1) * SSD_HEAD_DIM]
    y = (y_s[...] + d_ref[...] * x_s[...]) * _silu(zg_ref[...])
    o_ref[...] = (y * lax.rsqrt(jnp.mean(y * y, axis=-1, keepdims=True) + EPS) * nw_ref[...]).astype(o_ref.dtype)


def _ssd_prompt(z, p1, p2, conv_w, conv_b, d_vec, norm_w, n_seq, t_len):
    m = n_seq * t_len
    xb, bb, cbk = SSDX_OFF // _GW, (SSDX_OFF + SSD_INNER) // LANE, (SSDX_OFF + SSD_INNER + 2 * SSD_STATE) // LANE
    wb0, wc0 = SSD_INNER // LANE, (SSD_INNER + 2 * SSD_STATE) // LANE

    def spec(rows, width, off):
        return pl.BlockSpec((rows, width), functools.partial(lambda b, g, off: (0, off + g), off=off))

    def zspec(width, off):
        return pl.BlockSpec((t_len, width), functools.partial(lambda b, g, off: (b, off + g), off=off))

    aux = pl.BlockSpec((t_len, LANE), lambda b, g: (b, 0))
    return pl.pallas_call(
        functools.partial(_ssd_prompt_kernel, t_len=t_len),
        out_shape=(jax.ShapeDtypeStruct((m, SSD_INNER), BF16),
                   jax.ShapeDtypeStruct((n_seq, SSD_HEADS, SSD_STATE, SSD_HEAD_DIM), F32)),
        grid=(n_seq, SSD_GROUPS),
        in_specs=[zspec(_GW, xb), zspec(LANE, bb), zspec(LANE, cbk),
                  spec(CONV_W, _GW, 0), spec(CONV_W, LANE, wb0), spec(CONV_W, LANE, wc0),
                  spec(1, _GW, 0), spec(1, LANE, wb0), spec(1, LANE, wc0),
                  aux, aux, zspec(_GW, SSDZ_OFF // _GW), spec(1, _GW, 0), spec(1, _GW, 0)],
        out_specs=(pl.BlockSpec((t_len, _GW), lambda b, g: (b, g)),
                   pl.BlockSpec((None, _GH, SSD_STATE, SSD_HEAD_DIM), lambda b, g: (b, g, 0, 0))),
        scratch_shapes=[pltpu.VMEM((t_len, _GW), F32), pltpu.VMEM((t_len, LANE), F32), pltpu.VMEM((t_len, LANE), F32),
                        pltpu.VMEM((t_len, _GW), F32), pltpu.VMEM((SSD_STATE, _GW), F32)],
        compiler_params=_cparams("arbitrary", "arbitrary"), name="ssd_prompt")(
            z, z, z, conv_w, conv_w, conv_w, conv_b, conv_b, conv_b, p1, p2, z, d_vec, norm_w.reshape(1, SSD_INNER))


_FFN_TN = 256
_FFN_NJ = FFN_DIM // _FFN_TN
_TAIL = 8


def _ffn_up_prompt_kernel(a_ref, bg_ref, bu_ref, wg_ref, wu_ref, cg_ref, cu_ref, g_ref, sg_ref, su_ref, b_s, carry_ref,
                          *, tiles_per_seq, n_sub):
    i = pl.program_id(1)
    tn = _FFN_TN

    @pl.when(i == 0)
    def _():
        b_s[:, 0:tn] = bg_ref[...].astype(BF16)
        b_s[:, tn:2 * tn] = bu_ref[...].astype(BF16)

    @pl.when(i % tiles_per_seq == 0)
    def _():
        carry_ref[...] = jnp.zeros_like(carry_ref)

    prev = carry_ref[...]
    w = jnp.concatenate([wg_ref[...], wu_ref[...]], axis=1)
    c = jnp.concatenate([cg_ref[...], cu_ref[...]], axis=1)
    ts = a_ref.shape[0] // n_sub
    head_rows = lax.broadcasted_iota(jnp.int32, (_TAIL, 2 * tn), 0)
    for s in range(n_sub):
        y = _dot(a_ref[s * ts:(s + 1) * ts, :], b_s[...])
        y1 = pltpu.roll(y, 1, axis=0)
        y2 = pltpu.roll(y, 2, axis=0)
        u = w[0:1, :] * y2 + w[1:2, :] * y1 + w[2:3, :] * y + c
        p1 = prev[_TAIL - 1:_TAIL, :]
        p2 = prev[_TAIL - 2:_TAIL - 1, :]
        h1 = jnp.where(head_rows == 0, p1, y1[0:_TAIL])
        h2 = jnp.where(head_rows == 0, p2, jnp.where(head_rows == 1, p1, y2[0:_TAIL]))
        u_head = w[0:1, :] * h2 + w[1:2, :] * h1 + w[2:3, :] * y[0:_TAIL] + c
        u = jnp.concatenate([u_head, u[_TAIL:]], axis=0)
        g_ref[s * ts:(s + 1) * ts, :] = (_silu(u[:, 0:tn]) * u[:, tn:2 * tn]).astype(g_ref.dtype)
        prev = y[ts - _TAIL:ts, :]
    carry_ref[...] = prev
    sg_ref[...] = prev[:, 0:tn]
    su_ref[...] = prev[:, tn:2 * tn]


def _ffn_up_prompt(h2, w_up, layer, conv_w, conv_b, n_seq, t_len, tm):
    m = n_seq * t_len
    k = h2.shape[1]
    nj = _FFN_NJ
    tiles_per_seq = t_len // tm
    tail = jax.ShapeDtypeStruct((m // tm, _TAIL, FFN_DIM), F32)
    tail_spec = pl.BlockSpec((None, _TAIL, _FFN_TN), lambda j, i: (i, 0, j))
    return pl.pallas_call(
        functools.partial(_ffn_up_prompt_kernel, tiles_per_seq=tiles_per_seq, n_sub=2),
        out_shape=(jax.ShapeDtypeStruct((m, FFN_DIM), BF16), tail, tail),
        grid=(nj, m // tm),
        in_specs=[pl.BlockSpec((tm, k), lambda j, i: (i, 0)),
                  pl.BlockSpec((None, k, _FFN_TN), lambda j, i: (layer, 0, j)),
                  pl.BlockSpec((None, k, _FFN_TN), lambda j, i: (layer, 0, j + nj)),
                  pl.BlockSpec((FFN_CONV_W, _FFN_TN), lambda j, i: (0, j)),
                  pl.BlockSpec((FFN_CONV_W, _FFN_TN), lambda j, i: (0, j + nj)),
                  pl.BlockSpec((1, _FFN_TN), lambda j, i: (0, j)), pl.BlockSpec((1, _FFN_TN), lambda j, i: (0, j + nj))],
        out_specs=(pl.BlockSpec((tm, _FFN_TN), lambda j, i: (i, j)), tail_spec, tail_spec),
        scratch_shapes=[pltpu.VMEM((k, 2 * _FFN_TN), BF16), pltpu.VMEM((_TAIL, 2 * _FFN_TN), F32)],
        compiler_params=_cparams("arbitrary", "arbitrary"), name="ffn_up_prompt")(
            h2, w_up, w_up, conv_w, conv_w, conv_b, conv_b)


def _ffn_up_sample_kernel(a_ref, bg_ref, bu_ref, wg_ref, wu_ref, cg_ref, cu_ref, s0g_ref, s1g_ref, s0u_ref, s1u_ref,
                          g_ref, yg_ref, yu_ref):
    a = a_ref[...]
    yg = _dot(a, bg_ref[...].astype(BF16))
    yu = _dot(a, bu_ref[...].astype(BF16))
    ug = wg_ref[0:1, :] * s0g_ref[...] + wg_ref[1:2, :] * s1g_ref[...] + wg_ref[2:3, :] * yg + cg_ref[...]
    uu = wu_ref[0:1, :] * s0u_ref[...] + wu_ref[1:2, :] * s1u_ref[...] + wu_ref[2:3, :] * yu + cu_ref[...]
    g_ref[...] = (_silu(ug) * uu).astype(g_ref.dtype)
    yg_ref[...] = yg
    yu_ref[...] = yu


def _ffn_up_sample(h2, w_up, layer, conv_w, conv_b, s0, s1):
    m, k = h2.shape
    nj = _FFN_NJ
    lo = lambda j: (0, j)
    hi = lambda j: (0, j + nj)
    row = lambda f: pl.BlockSpec((m, _FFN_TN), f)
    ysd = jax.ShapeDtypeStruct((m, FFN_DIM), F32)
    return pl.pallas_call(
        _ffn_up_sample_kernel, out_shape=(jax.ShapeDtypeStruct((m, FFN_DIM), BF16), ysd, ysd), grid=(nj,),
        in_specs=[pl.BlockSpec((m, k), lambda j: (0, 0)),
                  pl.BlockSpec((None, k, _FFN_TN), lambda j: (layer, 0, j)),
                  pl.BlockSpec((None, k, _FFN_TN), lambda j: (layer, 0, j + nj)),
                  pl.BlockSpec((FFN_CONV_W, _FFN_TN), lo), pl.BlockSpec((FFN_CONV_W, _FFN_TN), hi),
                  pl.BlockSpec((1, _FFN_TN), lo), pl.BlockSpec((1, _FFN_TN), hi),
                  row(lo), row(lo), row(hi), row(hi)],
        out_specs=(row(lo), row(lo), row(lo)),
        compiler_params=_cparams("arbitrary"), name="ffn_up_sample")(
            h2, w_up, w_up, conv_w, conv_w, conv_b, conv_b, s0, s1, s0, s1)


_PAGES_PER_STEP = 16


def _kmean_kernel(pt_ref, *refs):
    o_ref = refs[-1]
    page = refs[0].shape[0]
    per_block = MOBA_BLOCK // page
    for n in range(_PAGES_PER_STEP // per_block):
        acc = jnp.sum(refs[n * per_block][...], axis=0)
        for r in range(1, per_block):
            acc = acc + jnp.sum(refs[n * per_block + r][...], axis=0)
        o_ref[n] = acc * (1.0 / MOBA_BLOCK)


def _kmean(cache_k, page_table, layer):
    n_b, n_pages = page_table.shape
    page, n_kv, d = cache_k.shape[2:]
    per_block = MOBA_BLOCK // page
    n_blocks = n_pages // per_block
    steps = n_pages // _PAGES_PER_STEP
    in_specs = [pl.BlockSpec((None, None, page, n_kv, d),
                             functools.partial(lambda b, s, pt, r: (layer, pt[b, s * _PAGES_PER_STEP + r], 0, 0, 0), r=r))
                for r in range(_PAGES_PER_STEP)]
    grid_spec = pltpu.PrefetchScalarGridSpec(
        num_scalar_prefetch=1, grid=(n_b, steps), in_specs=in_specs,
        out_specs=pl.BlockSpec((None, _PAGES_PER_STEP // per_block, n_kv, d), lambda b, s, pt: (b, s, 0, 0)))
    return pl.pallas_call(
        _kmean_kernel, out_shape=jax.ShapeDtypeStruct((n_b, n_blocks, n_kv, d), F32), grid_spec=grid_spec,
        compiler_params=_cparams("arbitrary", "arbitrary"), name="kmean")(page_table, *([cache_k] * _PAGES_PER_STEP))


def _select_kernel(q_ref, k_ref, kmean_ref, wq_ref, wk_ref, qn_ref, kn_ref, idx_ref):
    n_b, n_h, _ = q_ref.shape
    n_blocks = kmean_ref.shape[1]
    q = q_ref[...]
    qn = q * lax.rsqrt(jnp.mean(q * q, axis=-1, keepdims=True) + EPS) * wq_ref[...]
    qn_ref[...] = qn
    k = k_ref[...]
    kn_ref[...] = k * lax.rsqrt(jnp.mean(k * k, axis=-1, keepdims=True) + EPS) * wk_ref[...]
    head = lax.broadcasted_iota(jnp.int32, (n_h, n_blocks), 0)
    lane = lax.broadcasted_iota(jnp.int32, (n_h, n_blocks), 1).astype(F32)
    lane_out = lax.broadcasted_iota(jnp.int32, (n_h, LANE), 1)
    for b in range(n_b):
        gate = jnp.zeros((n_h, n_blocks), F32)
        for kv in range(ATT_KV_HEADS):
            gk = _dot_nt(qn[b], kmean_ref[b, :, kv * HEAD_DIM:(kv + 1) * HEAD_DIM], HIGHEST)
            gate = jnp.where(head // ATT_GROUP == kv, gk, gate)
        out = jnp.zeros((n_h, LANE), F32)
        for s in range(MOBA_TOPK):
            mx = jnp.max(gate, axis=-1, keepdims=True)
            pick = jnp.min(jnp.where(gate == mx, lane, float(n_blocks)), axis=-1, keepdims=True)
            out = jnp.where(lane_out == s, pick, out)
            gate = jnp.where(lane == pick, -jnp.inf, gate)
        idx_ref[b] = out.astype(jnp.int32)


def _select(q3, k3, kmean, wq, wk):
    n_b, n_h, d = q3.shape
    full = lambda shape: pl.BlockSpec(shape, lambda i: (0,) * len(shape))
    return pl.pallas_call(
        _select_kernel,
        out_shape=(jax.ShapeDtypeStruct(q3.shape, F32), jax.ShapeDtypeStruct(k3.shape, F32),
                   jax.ShapeDtypeStruct((n_b, n_h, LANE), jnp.int32)),
        grid=(1,),
        in_specs=[full(q3.shape), full(k3.shape), full(kmean.shape), full((1, 1, d)), full((1, 1, d))],
        out_specs=(full(q3.shape), full(k3.shape), full((n_b, n_h, LANE))),
        compiler_params=_cparams("arbitrary"), name="select")(q3, k3, kmean, wq.reshape(1, 1, d), wk.reshape(1, 1, d))


def _attn_sample_kernel(pt_ref, idx_ref, qn_ref, kn_ref, vn_ref, *refs, n_sel, per_block):
    o_ref = refs[-1]
    pages = refs[:-1]
    n_pg = n_sel * per_block
    k_refs, v_refs = pages[:n_pg], pages[n_pg:]
    h = pl.program_id(1)
    kv = h // ATT_GROUP
    q = qn_ref[pl.ds(h, 1), :] * (HEAD_DIM ** -0.5)
    k_new = kn_ref[pl.ds(kv, 1), :]
    v_new = vn_ref[pl.ds(kv, 1), :]
    s_own = jnp.sum(q * k_new, axis=-1, keepdims=True)
    n_rows = k_refs[0].shape[0]
    mine = lax.broadcasted_iota(jnp.int32, (n_rows, 1), 0) % ATT_KV_HEADS == kv
    logits = [jnp.where(mine, jnp.sum(r[...] * q, axis=-1, keepdims=True), NEG_BIG) for r in k_refs]
    mx = s_own
    for s in logits:
        mx = jnp.maximum(mx, jnp.max(s, axis=0, keepdims=True))
    p_own = jnp.exp(s_own - mx)
    den = p_own
    acc = p_own * v_new
    for s, v_ref in zip(logits, v_refs):
        p = jnp.exp(s - mx)
        den = den + jnp.sum(p, axis=0, keepdims=True)
        acc = acc + jnp.sum(p * v_ref[...], axis=0, keepdims=True)
    o_ref[pl.ds(h, 1), :] = acc / den


def _attn_sample(cache_k, cache_v, page_table, idx, qn, kn, vn, layer):
    n_b, n_h, d = qn.shape
    depth, n_pool, page, n_kv = cache_k.shape[:4]
    per_block = MOBA_BLOCK // page
    n_sel = idx.shape[2]
    cache_k = cache_k.reshape(depth, n_pool, page * n_kv, d)
    cache_v = cache_v.reshape(depth, n_pool, page * n_kv, d)

    def page_spec(s, r):
        def imap(b, h, pt, ix):
            return (layer, pt[b, ix[b, h * n_sel + s] * per_block + r], 0, 0)
        return pl.BlockSpec((None, None, page * n_kv, d), imap)

    pspecs = [page_spec(s, r) for s in range(n_sel) for r in range(per_block)]
    slab = lambda n: pl.BlockSpec((None, n, d), lambda b, h, pt, ix: (b, 0, 0))
    grid_spec = pltpu.PrefetchScalarGridSpec(
        num_scalar_prefetch=2, grid=(n_b, n_h),
        in_specs=[slab(n_h), slab(ATT_KV_HEADS), slab(ATT_KV_HEADS)] + pspecs + pspecs,
        out_specs=slab(n_h))
    n_pg = len(pspecs)
    return pl.pallas_call(
        functools.partial(_attn_sample_kernel, n_sel=n_sel, per_block=per_block),
        out_shape=jax.ShapeDtypeStruct((n_b, n_h, d), F32), grid_spec=grid_spec,
        compiler_params=_cparams("arbitrary", "arbitrary"), name="attn_sample")(
            page_table, idx.reshape(n_b, n_h * n_sel), qn, kn, vn, *([cache_k] * n_pg), *([cache_v] * n_pg))


def _rows_to_cols(x):
    r, n = x.shape
    return jnp.concatenate([x, jnp.zeros((n - r, n), x.dtype)], axis=0).T


def _mix_sample_kernel(dx_ref, dbuf_ref, dw_ref, dbeta_ref, dg_ref, dz_ref, dnw_ref, dst_ref,
                       sx_ref, sxbuf_ref, sxw_ref, sxb_ref, bc_ref, bcbuf_ref, bcw_ref, bcb_ref,
                       sdt_ref, sa_ref, sz_ref, sd_ref, snw_ref, sst_ref,
                       dno_ref, dnst_ref, so_ref, sso_ref, y_s):
    x = dx_ref[...]
    conv = x * dw_ref[CONV_W - 1]
    for i in range(CONV_W - 1):
        conv = conv + dbuf_ref[i] * dw_ref[i]
    act = _silu(conv)
    nh = DN_HEADS
    q, k, v = act[0:nh], act[nh:2 * nh], act[2 * nh:3 * nh]
    q = q * lax.rsqrt(jnp.sum(q * q, axis=-1, keepdims=True) + EPS) * (HEAD_DIM ** -0.5)
    k = k * lax.rsqrt(jnp.sum(k * k, axis=-1, keepdims=True) + EPS)
    beta = dbeta_ref[...]
    eg = jnp.exp(dg_ref[...])
    qk = jnp.sum(q * k, axis=-1, keepdims=True)
    qt, kt = _rows_to_cols(q), _rows_to_cols(k)
    o_rows = []
    for h in range(nh):
        s0 = dst_ref[h]
        kcol, qcol = kt[:, h:h + 1], qt[:, h:h + 1]
        ks = jnp.sum(kcol * s0, axis=0, keepdims=True)
        qs = jnp.sum(qcol * s0, axis=0, keepdims=True)
        b_h, e_h = beta[h:h + 1, :], eg[h:h + 1, :]
        v_new = v[h:h + 1, :] * b_h - (b_h * e_h) * ks
        o_rows.append(e_h * qs + qk[h:h + 1, :] * v_new)
        dnst_ref[h] = s0 * e_h + kcol * v_new
    o = jnp.concatenate(o_rows, axis=0)
    on = o * lax.rsqrt(jnp.mean(o * o, axis=-1, keepdims=True) + EPS) * dnw_ref[...]
    dno_ref[...] = on * _silu(dz_ref[...])
    xs = sx_ref[...] * sxw_ref[CONV_W - 1] + sxb_ref[...]
    bc = bc_ref[...] * bcw_ref[CONV_W - 1] + bcb_ref[...]
    for i in range(CONV_W - 1):
        xs = xs + sxbuf_ref[i] * sxw_ref[i]
        bc = bc + bcbuf_ref[i] * bcw_ref[i]
    xs = _silu(xs)
    bc = _silu(bc)
    bct = _rows_to_cols(bc)
    cb = jnp.sum(bc[0:SSD_GROUPS] * bc[SSD_GROUPS:2 * SSD_GROUPS], axis=-1, keepdims=True)
    dt = sdt_ref[...]
    ea = jnp.exp(sa_ref[...])
    for h in range(SSD_HEADS):
        grp = h // _GH
        s0 = sst_ref[h]
        bcol, ccol = bct[:, grp:grp + 1], bct[:, SSD_GROUPS + grp:SSD_GROUPS + grp + 1]
        xv = xs[h:h + 1, :] * dt[h:h + 1, :]
        e_h = ea[h:h + 1, :]
        y_s[h:h + 1, :] = cb[grp:grp + 1, :] * xv + e_h * jnp.sum(ccol * s0, axis=0, keepdims=True)
        sso_ref[h] = s0 * e_h + bcol * xv
    y = (y_s[...] + sd_ref[...] * xs) * _silu(sz_ref[...])
    norm_rows = []
    for grp in range(SSD_GROUPS):
        yg = y[grp * _GH:(grp + 1) * _GH]
        ms = jnp.sum(jnp.sum(yg * yg, axis=-1, keepdims=True), axis=0, keepdims=True) * (1.0 / _GW)
        norm_rows.append(yg * lax.rsqrt(ms + EPS))
    so_ref[...] = jnp.concatenate(norm_rows, axis=0) * snw_ref[...]


def _mix_sample(args, n_b):
    def per_b(shape):
        nd = len(shape) - 1
        return pl.BlockSpec((None,) + tuple(shape[1:]), lambda b: (b,) + (0,) * nd)

    def shared(shape):
        nd = len(shape)
        return pl.BlockSpec(tuple(shape), lambda b: (0,) * nd)

    names_per_b = {"dx", "dbuf", "dbeta", "dg", "dz", "dst", "sx", "sxbuf", "bc", "bcbuf", "sdt", "sa", "sz", "sst"}
    order = ["dx", "dbuf", "dw", "dbeta", "dg", "dz", "dnw", "dst", "sx", "sxbuf", "sxw", "sxb", "bc", "bcbuf", "bcw",
             "bcb", "sdt", "sa", "sz", "sd", "snw", "sst"]
    in_specs = [per_b(args[n].shape) if n in names_per_b else shared(args[n].shape) for n in order]
    outs = (jax.ShapeDtypeStruct((n_b, DN_HEADS, HEAD_DIM), F32),
            jax.ShapeDtypeStruct((n_b, DN_HEADS, HEAD_DIM, HEAD_DIM), F32),
            jax.ShapeDtypeStruct((n_b, SSD_HEADS, SSD_HEAD_DIM), F32),
            jax.ShapeDtypeStruct((n_b, SSD_HEADS, SSD_STATE, SSD_HEAD_DIM), F32))
    return pl.pallas_call(
        _mix_sample_kernel, out_shape=outs, grid=(n_b,), in_specs=in_specs,
        out_specs=tuple(per_b(o.shape) for o in outs),
        scratch_shapes=[pltpu.VMEM((SSD_HEADS, SSD_HEAD_DIM), F32)],
        compiler_params=_cparams("arbitrary"), name="mix_sample")(*[args[n] for n in order])


def _pack_w_in(w_in):
    depth, k, _ = w_in.shape
    small = jnp.concatenate([w_in[:, :, 7168:7184], w_in[:, :, 9744:9760],
                             jnp.zeros((depth, k, N_IN_PACKED - SMALL_OFF - 32), w_in.dtype)], axis=2)
    return jnp.concatenate([w_in[:, :, :7168], w_in[:, :, 7184:9744], small], axis=2).astype(BF16)


def _lane_vec(dn_vals, ssd_vals):
    v = jnp.zeros((LANE,), F32)
    v = v.at[DNG_LANE:DNG_LANE + DN_HEADS].set(dn_vals.astype(F32))
    v = v.at[SSD_LANE:SSD_LANE + SSD_HEADS].set(ssd_vals.astype(F32))
    return v.reshape(1, LANE)


def _ffn_down(g, w_down, layer, x, tm):
    half = FFN_DIM // 2
    for part in range(2):
        x = _matmul([(g, half, part, part)], w_down, layer, tm=tm, tn=256, res=x, name="ffn_down")
    return x


def _prompt_layer(x, lw, layer, n_seq, t_len):
    tm = 1024
    h = _rmsnorm_cast(x, lw["norm1_w"], 512)
    z = _matmul([(h, D_MODEL, 0, 0)], lw["w_in_packed"], layer, tm=tm, tn=768, name="in_proj")
    p1, p2 = _aux(z, lw["bias_vec"], lw["alog_vec"], n_seq, t_len, CHUNK)
    o_att, k_norm = _attn_prompt(z, lw["attn_q_norm_w"], lw["attn_k_norm_w"], n_seq, t_len)
    o_dn, dn_state = _gdn_prompt(z, p1, p2, lw["dn_conv_w"], lw["dn_norm_w"], n_seq, t_len)
    o_ssd, ssd_state = _ssd_prompt(z, p1, p2, lw["ssd_conv_w"], lw["ssd_conv_b"].reshape(1, -1), lw["ssd_d_vec"],
                                   lw["ssd_norm_w"], n_seq, t_len)
    x = _matmul([(o_att, 2048, 0, 0), (o_dn, 1024, 0, 2), (o_ssd, 1024, 0, 3)], lw["w_out"], layer, tm=tm, tn=512,
                res=x, name="out_proj")
    h2 = _rmsnorm_cast(x, lw["norm2_w"], 512)
    g, tail_g, tail_u = _ffn_up_prompt(h2, lw["ffn_w_up"], layer, lw["ffn_conv_w"], lw["ffn_conv_b"].reshape(1, -1),
                                       n_seq, t_len, tm)
    x = _ffn_down(g, lw["ffn_w_down"], layer, x, tm)
    z3 = z.reshape(n_seq, t_len, N_IN_PACKED)
    tps = t_len // tm
    new = (k_norm.reshape(n_seq, t_len, ATT_KV_HEADS, HEAD_DIM),
           z3[:, :, V_OFF:V_OFF + 512].reshape(n_seq, t_len, ATT_KV_HEADS, HEAD_DIM),
           z3[:, t_len - (CONV_W - 1):, DNQKV_OFF:DNQKV_OFF + DN_CONV_CH],
           dn_state,
           z3[:, t_len - (CONV_W - 1):, SSDX_OFF:SSDX_OFF + SSD_CONV_CH],
           ssd_state,
           jnp.concatenate([tail_g[tps - 1::tps, _TAIL - 2:], tail_u[tps - 1::tps, _TAIL - 2:]], axis=-1))
    return x, new


_SAMPLE_ROWS = 16


def _sample_layer(x, lw, layer, cache_k, cache_v, page_table, dn_conv_buf, dn_state, ssd_conv_buf, ssd_state,
                  ffn_conv_buf):
    n_b = page_table.shape[0]
    rows = x.shape[0]
    h = _rmsnorm_cast(x, lw["norm1_w"], rows)
    z = _matmul([(h, D_MODEL, 0, 0)], lw["w_in_packed"], layer, tm=rows, tn=768, name="in_proj_s")
    p1, p2 = _aux(z, lw["bias_vec"], lw["alog_vec"], 1, rows, 1)
    zb = z[:n_b]
    kmean = _kmean(cache_k, page_table, layer)
    kmean = kmean.reshape(n_b, kmean.shape[1], ATT_KV_HEADS * HEAD_DIM)
    q3 = zb[:, Q_OFF:Q_OFF + 2048].reshape(n_b, ATT_HEADS, HEAD_DIM)
    k3 = zb[:, K_OFF:K_OFF + 512].reshape(n_b, ATT_KV_HEADS, HEAD_DIM)
    v3 = zb[:, V_OFF:V_OFF + 512].reshape(n_b, ATT_KV_HEADS, HEAD_DIM)
    qn, kn, idx = _select(q3, k3, kmean, lw["attn_q_norm_w"], lw["attn_k_norm_w"])
    o_att = _attn_sample(cache_k, cache_v, page_table, idx[:, :, :MOBA_TOPK], qn, kn, v3, layer)
    dn_x = zb[:, DNQKV_OFF:DNQKV_OFF + DN_CONV_CH]
    ssd_x = zb[:, SSDX_OFF:SSDX_OFF + SSD_CONV_CH]
    nh3 = 3 * DN_HEADS
    ng2 = 2 * SSD_GROUPS
    args = {
        "dx": dn_x.reshape(n_b, nh3, HEAD_DIM),
        "dbuf": dn_conv_buf.reshape(n_b, CONV_W - 1, nh3, HEAD_DIM),
        "dw": lw["dn_conv_w"].reshape(CONV_W, nh3, HEAD_DIM),
        "dbeta": p1[:n_b, BETA_LANE:BETA_LANE + DN_HEADS].reshape(n_b, DN_HEADS, 1),
        "dg": p2[:n_b, DNG_LANE:DNG_LANE + DN_HEADS].reshape(n_b, DN_HEADS, 1),
        "dz": zb[:, DNZ_OFF:DNZ_OFF + DN_WIDTH].reshape(n_b, DN_HEADS, HEAD_DIM),
        "dnw": lw["dn_norm_w"].reshape(1, HEAD_DIM),
        "dst": dn_state,
        "sx": ssd_x[:, :SSD_INNER].reshape(n_b, SSD_HEADS, SSD_HEAD_DIM),
        "sxbuf": ssd_conv_buf[:, :, :SSD_INNER].reshape(n_b, CONV_W - 1, SSD_HEADS, SSD_HEAD_DIM),
        "sxw": lw["ssd_conv_w"][:, :SSD_INNER].reshape(CONV_W, SSD_HEADS, SSD_HEAD_DIM),
        "sxb": lw["ssd_conv_b"][:SSD_INNER].reshape(SSD_HEADS, SSD_HEAD_DIM),
        "bc": ssd_x[:, SSD_INNER:].reshape(n_b, ng2, SSD_STATE),
        "bcbuf": ssd_conv_buf[:, :, SSD_INNER:].reshape(n_b, CONV_W - 1, ng2, SSD_STATE),
        "bcw": lw["ssd_conv_w"][:, SSD_INNER:].reshape(CONV_W, ng2, SSD_STATE),
        "bcb": lw["ssd_conv_b"][SSD_INNER:].reshape(ng2, SSD_STATE),
        "sdt": p1[:n_b, SSD_LANE:SSD_LANE + SSD_HEADS].reshape(n_b, SSD_HEADS, 1),
        "sa": p2[:n_b, SSD_LANE:SSD_LANE + SSD_HEADS].reshape(n_b, SSD_HEADS, 1),
        "sz": zb[:, SSDZ_OFF:SSDZ_OFF + SSD_INNER].reshape(n_b, SSD_HEADS, SSD_HEAD_DIM),
        "sd": lw["ssd_D"].astype(F32).reshape(SSD_HEADS, 1),
        "snw": lw["ssd_norm_w"].reshape(SSD_HEADS, SSD_HEAD_DIM),
        "sst": ssd_state,
    }
    o_dn, dn_state_new, o_ssd, ssd_state_new = _mix_sample(args, n_b)
    pad = lambda a: jnp.pad(a.reshape(n_b, -1), ((0, rows - n_b), (0, 0))).astype(BF16)
    x = _matmul([(pad(o_att), 2048, 0, 0), (pad(o_dn), 1024, 0, 2), (pad(o_ssd), 1024, 0, 3)], lw["w_out"], layer,
                tm=rows, tn=512, res=x, name="out_proj_s")
    h2 = _rmsnorm_cast(x, lw["norm2_w"], rows)
    padf = lambda a: jnp.pad(a, ((0, rows - n_b), (0, 0)))
    g, y_g, y_u = _ffn_up_sample(h2, lw["ffn_w_up"], layer, lw["ffn_conv_w"], lw["ffn_conv_b"].reshape(1, -1),
                                 padf(ffn_conv_buf[:, 0]), padf(ffn_conv_buf[:, 1]))
    x = _ffn_down(g, lw["ffn_w_down"], layer, x, rows)
    y_new = jnp.concatenate([y_g[:n_b], y_u[:n_b]], axis=-1)
    new = (kn.reshape(n_b, 1, ATT_KV_HEADS, HEAD_DIM), v3.reshape(n_b, 1, ATT_KV_HEADS, HEAD_DIM),
           jnp.concatenate([dn_conv_buf[:, 1:], dn_x[:, None]], axis=1), dn_state_new,
           jnp.concatenate([ssd_conv_buf[:, 1:], ssd_x[:, None]], axis=1), ssd_state_new,
           jnp.stack([ffn_conv_buf[:, 1], y_new], axis=1))
    return x, new


def kernel(x_prompt, x_sample, cache_k, cache_v, page_table, state_dn_conv, state_dn, state_ssd_conv, state_ssd, state_ffn_conv, norm1_w, w_in, attn_q_norm_w, attn_k_norm_w, dn_conv_w, dn_A_log, dn_dt_bias, dn_norm_w, ssd_conv_w, ssd_conv_b, ssd_dt_bias, ssd_A_log, ssd_D, ssd_norm_w, w_out, norm2_w, ffn_w_up, ffn_conv_w, ffn_conv_b, ffn_w_down):
    depth = w_in.shape[0]
    n_seq, t_len, d_model = x_prompt.shape
    n_b = x_sample.shape[0]
    yp = x_prompt.reshape(n_seq * t_len, d_model)
    ys = jnp.pad(x_sample.reshape(n_b, d_model), ((0, _SAMPLE_ROWS - n_b), (0, 0)))
    w_in_packed = _pack_w_in(w_in)
    outs_p, outs_s = [], []
    for l in range(depth):
        lw = {"norm1_w": norm1_w[l], "w_in_packed": w_in_packed, "attn_q_norm_w": attn_q_norm_w[l],
              "attn_k_norm_w": attn_k_norm_w[l], "dn_conv_w": dn_conv_w[l], "dn_norm_w": dn_norm_w[l],
              "ssd_conv_w": ssd_conv_w[l], "ssd_conv_b": ssd_conv_b[l], "ssd_D": ssd_D[l],
              "ssd_norm_w": ssd_norm_w[l], "w_out": w_out, "norm2_w": norm2_w[l], "ffn_w_up": ffn_w_up,
              "ffn_conv_w": ffn_conv_w[l], "ffn_conv_b": ffn_conv_b[l], "ffn_w_down": ffn_w_down,
              "bias_vec": _lane_vec(dn_dt_bias[l], ssd_dt_bias[l]),
              "alog_vec": _lane_vec(dn_A_log[l], ssd_A_log[l]),
              "ssd_d_vec": jnp.repeat(ssd_D[l].astype(F32), SSD_HEAD_DIM).reshape(1, SSD_INNER)}
        yp, new_p = _prompt_layer(yp, lw, l, n_seq, t_len)
        ys, new_s = _sample_layer(ys, lw, l, cache_k, cache_v, page_table, state_dn_conv[l], state_dn[l],
                                  state_ssd_conv[l], state_ssd[l], state_ffn_conv[l])
        outs_p.append(new_p)
        outs_s.append(new_s)
    st = lambda outs, i: jnp.stack([o[i] for o in outs])
    return (yp.reshape(n_seq, t_len, d_model), ys[:n_b].reshape(n_b, 1, d_model),
            st(outs_p, 0), st(outs_p, 1), st(outs_s, 0), st(outs_s, 1),
            st(outs_p, 2), st(outs_s, 2), st(outs_p, 3), st(outs_s, 3),
            st(outs_p, 4), st(outs_s, 4), st(outs_p, 5), st(outs_s, 5),
            st(outs_p, 6), st(outs_s, 6))
```

```python
import functools
import math

import jax
import jax.numpy as jnp
from jax import lax
from jax.experimental import pallas as pl
from jax.experimental.pallas import tpu as pltpu

F32 = jnp.float32
BF16 = jnp.bfloat16
HIGHEST = lax.Precision.HIGHEST

D_MODEL = 4096
HEAD_DIM = 128
ATT_HEADS = 16
ATT_KV_HEADS = 4
ATT_GROUP = 4
MOBA_BLOCK = 256
MOBA_TOPK = 3
DN_HEADS = 8
DN_WIDTH = 1024
DN_CONV_CH = 3072
SSD_INNER = 1024
SSD_HEAD_DIM = 64
SSD_HEADS = 16
SSD_GROUPS = 2
SSD_STATE = 128
SSD_CONV_CH = 1536
CONV_W = 4
CHUNK = 64
FFN_DIM = 11008
FFN_CONV_W = 3
EPS = 1e-6

Q_OFF, K_OFF, V_OFF = 0, 2048, 2560
DNQKV_OFF, DNZ_OFF = 3072, 6144
ZA_COLS = 7168
SSDX_OFF, SSDZ_OFF = 0, 1536
ZB_SRC, ZB_COLS = 7184, 2560
ZS_SRC = (7168, 9744)
LANE = 128
BETA_LANE, DNG_LANE, SSD_LANE = 0, 8, 16

VMEM_LIMIT_BYTES = 56 * 1024 * 1024
NEG_BIG = -1e30


def _cparams(*sem):
    return pltpu.CompilerParams(dimension_semantics=sem, vmem_limit_bytes=VMEM_LIMIT_BYTES)


def _silu(x):
    return x / (1.0 + jnp.exp(-x))


def _sigmoid(x):
    return 1.0 / (1.0 + jnp.exp(-x))


def _softplus(x):
    return jnp.maximum(x, 0.0) + jnp.log1p(jnp.exp(-jnp.abs(x)))


def _dot(a, b, precision=None):
    return jnp.dot(a, b, preferred_element_type=F32, precision=precision)


def _dot_nt(a, b, precision=None):
    return lax.dot_general(a, b, (((1,), (1,)), ((), ())), preferred_element_type=F32, precision=precision)


def _shift_rows(x, s):
    rows = lax.broadcasted_iota(jnp.int32, x.shape, 0)
    return jnp.where(rows < s, 0.0, pltpu.roll(x, s, axis=0))


def _causal_conv_rows(x, w_ref, width):
    y = x * w_ref[width - 1:width, :]
    for i in range(width - 1):
        y = y + _shift_rows(x, width - 1 - i) * w_ref[i:i + 1, :]
    return y


def _lane_col(x, lane):
    lanes = lax.broadcasted_iota(jnp.int32, x.shape, 1)
    return jnp.sum(jnp.where(lanes == lane, x, 0.0), axis=-1, keepdims=True)


def _decay_matrix(gc_col, c):
    ii = lax.broadcasted_iota(jnp.int32, (c, c), 0)
    jj = lax.broadcasted_iota(jnp.int32, (c, c), 1)
    gcb = jnp.broadcast_to(gc_col, (c, c))
    gc_row = jnp.sum(jnp.where(ii == jj, gcb, 0.0), axis=0, keepdims=True)
    low = ii >= jj
    gam = jnp.where(low, jnp.exp(jnp.where(low, gcb - gc_row, 0.0)), 0.0)
    return gam, ii, jj


def _rmsnorm_kernel(x_ref, w_ref, o_ref):
    x = x_ref[...]
    ms = jnp.mean(x * x, axis=-1, keepdims=True)
    o_ref[...] = (x * lax.rsqrt(ms + EPS) * w_ref[...]).astype(o_ref.dtype)


def _rmsnorm_cast(x, w, tm):
    m, d = x.shape
    return pl.pallas_call(
        _rmsnorm_kernel, out_shape=jax.ShapeDtypeStruct((m, d), BF16), grid=(m // tm,),
        in_specs=[pl.BlockSpec((tm, d), lambda i: (i, 0)), pl.BlockSpec((1, d), lambda i: (0, 0))],
        out_specs=pl.BlockSpec((tm, d), lambda i: (i, 0)),
        compiler_params=_cparams("arbitrary"), name="rmsnorm")(x, w.reshape(1, d))


def _matmul_kernel(*refs, n_parts, has_res):
    a_refs = refs[:n_parts]
    b_refs = refs[n_parts:2 * n_parts]
    o_ref = refs[-1]
    acc = None
    for a_ref, b_ref in zip(a_refs, b_refs):
        d = _dot(a_ref[...], b_ref[...].astype(BF16))
        acc = d if acc is None else acc + d
    if has_res:
        acc = acc + refs[2 * n_parts][...]
    o_ref[...] = acc.astype(o_ref.dtype)


def _matmul(a_parts, b, layer, *, tm, tn, res=None, out_dtype=F32, name="matmul"):
    m = a_parts[0][0].shape[0]
    n = b.shape[2]
    in_specs, args = [], []
    for arr, kp, cb, rb in a_parts:
        in_specs.append(pl.BlockSpec((tm, kp), functools.partial(lambda i, j, cb: (i, cb), cb=cb)))
        args.append(arr)
    for arr, kp, cb, rb in a_parts:
        in_specs.append(pl.BlockSpec((None, kp, tn), functools.partial(lambda i, j, rb: (layer, rb, j), rb=rb)))
        args.append(b)
    if res is not None:
        in_specs.append(pl.BlockSpec((tm, tn), lambda i, j: (i, j)))
        args.append(res)
    return pl.pallas_call(
        functools.partial(_matmul_kernel, n_parts=len(a_parts), has_res=res is not None),
        out_shape=jax.ShapeDtypeStruct((m, n), out_dtype), grid=(m // tm, n // tn),
        in_specs=in_specs, out_specs=pl.BlockSpec((tm, tn), lambda i, j: (i, j)),
        compiler_params=_cparams("arbitrary", "arbitrary"), name=name)(*args)


def _in_proj_kernel(a_ref, w_ref, o_ref, b_s):
    @pl.when(pl.program_id(1) == 0)
    def _():
        b_s[...] = w_ref[...].astype(BF16)

    o_ref[...] = _dot_nt(a_ref[...], b_s[...])


def _in_proj(h, wt, layer, n, *, tm, tn, name):
    m, k = h.shape
    return pl.pallas_call(
        _in_proj_kernel, out_shape=jax.ShapeDtypeStruct((m, n), F32), grid=(n // tn, m // tm),
        in_specs=[pl.BlockSpec((tm, k), lambda j, i: (i, 0)), pl.BlockSpec((None, tn, k), lambda j, i: (layer, j, 0))],
        out_specs=pl.BlockSpec((tm, tn), lambda j, i: (i, j)),
        scratch_shapes=[pltpu.VMEM((tn, k), BF16)],
        compiler_params=_cparams("arbitrary", "arbitrary"), name=name)(h, wt)


def _aux_kernel(s_ref, bias_ref, alog_ref, p1_ref, p2_ref, *, t_len, chunk):
    x = s_ref[...]
    lanes = lax.broadcasted_iota(jnp.int32, x.shape, 1)
    sp = _softplus(x + bias_ref[...])
    p1_ref[...] = jnp.where(lanes < DNG_LANE, _sigmoid(x), sp)
    g = -jnp.exp(alog_ref[...]) * sp
    if chunk == 1:
        p2_ref[...] = g
    else:
        ii = lax.broadcasted_iota(jnp.int32, (chunk, chunk), 0)
        jj = lax.broadcasted_iota(jnp.int32, (chunk, chunk), 1)
        tril = jnp.where(ii >= jj, 1.0, 0.0).astype(F32)
        for c in range(t_len // chunk):
            p2_ref[c * chunk:(c + 1) * chunk, :] = _dot(tril, g[c * chunk:(c + 1) * chunk, :], HIGHEST)


def _aux(z, bias_vec, alog_vec, n_seq, t_len, chunk):
    m = n_seq * t_len
    blk = pl.BlockSpec((t_len, LANE), lambda b: (b, 0))
    vec = pl.BlockSpec((1, LANE), lambda b: (0, 0))
    out = pl.BlockSpec((t_len, LANE), lambda b: (b, 0))
    return pl.pallas_call(
        functools.partial(_aux_kernel, t_len=t_len, chunk=chunk),
        out_shape=(jax.ShapeDtypeStruct((m, LANE), F32), jax.ShapeDtypeStruct((m, LANE), F32)),
        grid=(n_seq,), in_specs=[blk, vec, vec], out_specs=(out, out),
        compiler_params=_cparams("arbitrary"), name="aux")(z, bias_vec, alog_vec)


_ONES_ROWS = 16


def _attn_prompt_kernel(q_ref, k_ref, v_ref, wq_ref, wk_ref, o_ref, kn_ref, kb_ref, vt_ref, kmean_ref,
                        *, t_len):
    nb = t_len // MOBA_BLOCK
    g = pl.program_id(2)

    @pl.when(g == 0)
    def _():
        k = k_ref[...]
        kn = k * lax.rsqrt(jnp.mean(k * k, axis=-1, keepdims=True) + EPS) * wk_ref[...]
        kn_ref[...] = kn
        kb_ref[...] = kn.astype(BF16)
        for n in range(nb):
            kmean_ref[n:n + 1, :] = jnp.mean(kn[n * MOBA_BLOCK:(n + 1) * MOBA_BLOCK, :], axis=0, keepdims=True)
        vt_ref[0:HEAD_DIM, :] = v_ref[...].T.astype(BF16)
        vt_ref[HEAD_DIM:HEAD_DIM + _ONES_ROWS, :] = jnp.ones((_ONES_ROWS, t_len), BF16)

    q = q_ref[...]
    qn = q * lax.rsqrt(jnp.mean(q * q, axis=-1, keepdims=True) + EPS) * wq_ref[...]
    gate = _dot_nt(kmean_ref[...], qn, HIGHEST)
    blk = lax.broadcasted_iota(jnp.int32, (nb, t_len), 0)
    own = lax.broadcasted_iota(jnp.int32, (nb, t_len), 1) // MOBA_BLOCK
    valid = blk < own
    gm = jnp.where(valid, gate, -jnp.inf)
    cnt = jnp.zeros((nb, t_len), F32)
    for m in range(nb):
        row = gm[m:m + 1, :]
        beats = jnp.where(row > gm, 1.0, jnp.where(row == gm, jnp.where(blk > m, 1.0, 0.0), 0.0))
        cnt = cnt + beats
    bias = jnp.where(valid, jnp.where(cnt < MOBA_TOPK, 0.0, NEG_BIG), NEG_BIG)
    qs = (qn * (HEAD_DIM ** -0.5 * math.log2(math.e))).astype(BF16)
    kk = lax.broadcasted_iota(jnp.int32, (MOBA_BLOCK, MOBA_BLOCK), 0)
    qq = lax.broadcasted_iota(jnp.int32, (MOBA_BLOCK, MOBA_BLOCK), 1)
    causal = jnp.where(kk <= qq, 0.0, NEG_BIG)
    for qi in range(nb):
        n_keys = (qi + 1) * MOBA_BLOCK
        cols = slice(qi * MOBA_BLOCK, (qi + 1) * MOBA_BLOCK)
        st = _dot_nt(kb_ref[0:n_keys, :], qs[cols, :])
        blocks = [st[n * MOBA_BLOCK:(n + 1) * MOBA_BLOCK, :] + bias[n:n + 1, cols] for n in range(qi)]
        blocks.append(st[qi * MOBA_BLOCK:n_keys, :] + causal)
        mx = jnp.max(blocks[0], axis=0, keepdims=True)
        for blk_s in blocks[1:]:
            mx = jnp.maximum(mx, jnp.max(blk_s, axis=0, keepdims=True))
        p = [jnp.exp2(blk_s - mx).astype(BF16) for blk_s in blocks]
        p = jnp.concatenate(p, axis=0) if qi else p[0]
        ot = _dot(vt_ref[:, 0:n_keys], p)
        o = ot[0:HEAD_DIM, :] / ot[HEAD_DIM:HEAD_DIM + 1, :]
        o_ref[cols, :] = o.T.astype(o_ref.dtype)


def _attn_prompt(z, wq, wk, n_seq, t_len):
    m = n_seq * t_len
    qblk = pl.BlockSpec((t_len, HEAD_DIM), lambda b, k, g: (b, Q_OFF // HEAD_DIM + k * ATT_GROUP + g))
    kblk = pl.BlockSpec((t_len, HEAD_DIM), lambda b, k, g: (b, K_OFF // HEAD_DIM + k))
    vblk = pl.BlockSpec((t_len, HEAD_DIM), lambda b, k, g: (b, V_OFF // HEAD_DIM + k))
    wspec = pl.BlockSpec((1, HEAD_DIM), lambda b, k, g: (0, 0))
    return pl.pallas_call(
        functools.partial(_attn_prompt_kernel, t_len=t_len),
        out_shape=(jax.ShapeDtypeStruct((m, ATT_HEADS * HEAD_DIM), BF16),
                   jax.ShapeDtypeStruct((m, ATT_KV_HEADS * HEAD_DIM), F32)),
        grid=(n_seq, ATT_KV_HEADS, ATT_GROUP),
        in_specs=[qblk, kblk, vblk, wspec, wspec],
        out_specs=(pl.BlockSpec((t_len, HEAD_DIM), lambda b, k, g: (b, k * ATT_GROUP + g)),
                   pl.BlockSpec((t_len, HEAD_DIM), lambda b, k, g: (b, k))),
        scratch_shapes=[pltpu.VMEM((t_len, HEAD_DIM), BF16), pltpu.VMEM((HEAD_DIM + _ONES_ROWS, t_len), BF16),
                        pltpu.VMEM((t_len // MOBA_BLOCK, HEAD_DIM), F32)],
        compiler_params=_cparams("arbitrary", "arbitrary", "arbitrary"), name="attn_prompt")(
            z, z, z, wq.reshape(1, HEAD_DIM), wk.reshape(1, HEAD_DIM))


def _tri_inverse(a, c):
    ii = lax.broadcasted_iota(jnp.int32, (c, c), 0)
    jj = lax.broadcasted_iota(jnp.int32, (c, c), 1)
    eye = jnp.where(ii == jj, 1.0, 0.0).astype(F32)
    p = -a
    inv = eye + p
    steps = int(math.log2(c)) - 1
    for _ in range(steps):
        p = _dot(p, p)
        inv = inv + _dot(inv, p)
    return inv


_GDN_UNROLL = 8


def _gdn_prompt_kernel(zq_ref, zk_ref, zv_ref, wq_ref, wk_ref, wv_ref, p1_ref, p2_ref, zg_ref, nw_ref,
                       o_ref, s_ref, q_s, k_s, v_s, beta_s, gc_s, qp_s, op_s, mn_s, nn_s, *, t_len):
    h = pl.program_id(1)
    c = CHUNK
    d = HEAD_DIM

    def l2n(x):
        return x * lax.rsqrt(jnp.sum(x * x, axis=-1, keepdims=True) + EPS)

    q_s[...] = l2n(_silu(_causal_conv_rows(zq_ref[...], wq_ref, CONV_W))) * (HEAD_DIM ** -0.5)
    k_s[...] = l2n(_silu(_causal_conv_rows(zk_ref[...], wk_ref, CONV_W)))
    v_s[...] = _silu(_causal_conv_rows(zv_ref[...], wv_ref, CONV_W))
    beta_s[...] = jnp.broadcast_to(_lane_col(p1_ref[...], BETA_LANE + h), (t_len, HEAD_DIM))
    gc_s[...] = jnp.broadcast_to(_lane_col(p2_ref[...], DNG_LANE + h), (t_len, HEAD_DIM))

    ii = lax.broadcasted_iota(jnp.int32, (c, c), 0)
    jj = lax.broadcasted_iota(jnp.int32, (c, c), 1)
    eye = jnp.where(ii == jj, 1.0, 0.0).astype(F32)

    def prepare_group(gi, carry):
        n = _GDN_UNROLL
        chunks = [gi * n + u for u in range(n)]
        rows = [pl.ds(pl.multiple_of(ci * c, c), c) for ci in chunks]
        q = [q_s[r, :] for r in rows]
        k = [k_s[r, :] for r in rows]
        gcb = [gc_s[r, :] for r in rows]
        kb = [k[u] * beta_s[rows[u], :] for u in range(n)]
        gam = [_decay_matrix(gcb[u][:, 0:1], c)[0] for u in range(n)]
        kq = [_dot_nt(jnp.concatenate([kb[u], q[u]], axis=0), k[u]) for u in range(n)]
        attn = [kq[u][c:2 * c] * gam[u] for u in range(n)]
        p = [-jnp.where(ii > jj, kq[u][0:c] * gam[u], 0.0) for u in range(n)]
        inv = [eye + p[u] for u in range(n)]
        for _ in range(int(math.log2(c)) - 1):
            p = [_dot(p[u], p[u]) for u in range(n)]
            inv = [inv[u] + _dot(inv[u], p[u]) for u in range(n)]
        eg = [jnp.exp(gcb[u]) for u in range(n)]
        rhs = [jnp.concatenate([v_s[rows[u], :] * beta_s[rows[u], :], kb[u] * eg[u]], axis=1) for u in range(n)]
        uw = [_dot(inv[u], rhs[u]) for u in range(n)]
        auw = [_dot(attn[u], uw[u]) for u in range(n)]
        k_dec = [k[u] * jnp.exp(gcb[u][c - 1:c, :] - gcb[u]) for u in range(n)]
        kuw = [_dot(k_dec[u].T, uw[u]) for u in range(n)]
        for u in range(n):
            rd = pl.ds(pl.multiple_of(chunks[u] * d, d), d)
            op_s[rows[u], :] = auw[u][:, 0:d]
            qp_s[rows[u], :] = eg[u] * q[u] - auw[u][:, d:2 * d]
            nn_s[rd, :] = kuw[u][:, 0:d]
            mn_s[rd, :] = -kuw[u][:, d:2 * d]
        return carry

    lax.fori_loop(0, t_len // (c * _GDN_UNROLL), prepare_group, 0)

    def chunk_step(ci, s):
        r = pl.ds(pl.multiple_of(ci * c, c), c)
        rd = pl.ds(pl.multiple_of(ci * d, d), d)
        ms = _dot(jnp.concatenate([mn_s[rd, :], qp_s[r, :]], axis=0), s)
        op_s[r, :] = ms[d:d + c] + op_s[r, :]
        e = jnp.exp(gc_s[pl.ds(ci * c + c - 1, 1), :])
        return s * e[:, 0:1] + ms[0:d] + nn_s[rd, :]

    s_fin = lax.fori_loop(0, t_len // c, chunk_step, jnp.zeros((HEAD_DIM, HEAD_DIM), F32))
    s_ref[...] = s_fin
    o = op_s[...]
    on = o * lax.rsqrt(jnp.mean(o * o, axis=-1, keepdims=True) + EPS) * nw_ref[...]
    o_ref[...] = (on * _silu(zg_ref[...])).astype(o_ref.dtype)


def _gdn_prompt(z, p1, p2, conv_w, norm_w, n_seq, t_len):
    m = n_seq * t_len
    base = DNQKV_OFF // HEAD_DIM

    def zcol(off):
        return pl.BlockSpec((t_len, HEAD_DIM), functools.partial(lambda b, h, off: (b, off + h), off=off))

    def wcol(off):
        return pl.BlockSpec((CONV_W, HEAD_DIM), functools.partial(lambda b, h, off: (0, off + h), off=off))

    aux = pl.BlockSpec((t_len, LANE), lambda b, h: (b, 0))
    tbuf = pltpu.VMEM((t_len, HEAD_DIM), F32)
    return pl.pallas_call(
        functools.partial(_gdn_prompt_kernel, t_len=t_len),
        out_shape=(jax.ShapeDtypeStruct((m, DN_WIDTH), BF16),
                   jax.ShapeDtypeStruct((n_seq, DN_HEADS, HEAD_DIM, HEAD_DIM), F32)),
        grid=(n_seq, DN_HEADS),
        in_specs=[zcol(base), zcol(base + DN_HEADS), zcol(base + 2 * DN_HEADS),
                  wcol(0), wcol(DN_HEADS), wcol(2 * DN_HEADS), aux, aux,
                  zcol(DNZ_OFF // HEAD_DIM), pl.BlockSpec((1, HEAD_DIM), lambda b, h: (0, 0))],
        out_specs=(pl.BlockSpec((t_len, HEAD_DIM), lambda b, h: (b, h)),
                   pl.BlockSpec((None, None, HEAD_DIM, HEAD_DIM), lambda b, h: (b, h, 0, 0))),
        scratch_shapes=[tbuf] * 7 + [pltpu.VMEM((t_len // CHUNK * HEAD_DIM, HEAD_DIM), F32)] * 2,
        compiler_params=_cparams("arbitrary", "arbitrary"), name="gdn_prompt")(
            z, z, z, conv_w, conv_w, conv_w, p1, p2, z, norm_w.reshape(1, HEAD_DIM))


_GH = SSD_HEADS // SSD_GROUPS
_GW = _GH * SSD_HEAD_DIM


def _ssd_prompt_kernel(zx_ref, zb_ref, zc_ref, wx_ref, wb_ref, wc_ref, bx_ref, bb_ref, bc_ref, p1_ref, p2_ref,
                       zg_ref, d_ref, nw_ref, o_ref, st_ref, x_s, b_s, c_s, y_s, state_s, *, t_len):
    grp = pl.program_id(1)
    c = CHUNK
    x_s[...] = _silu(_causal_conv_rows(zx_ref[...], wx_ref, CONV_W) + bx_ref[...])
    b_s[...] = _silu(_causal_conv_rows(zb_ref[...], wb_ref, CONV_W) + bb_ref[...])
    c_s[...] = _silu(_causal_conv_rows(zc_ref[...], wc_ref, CONV_W) + bc_ref[...])
    state_s[...] = jnp.zeros_like(state_s)

    def chunk_step(ci, carry):
        r = pl.ds(pl.multiple_of(ci * c, c), c)
        x, bm, cm = x_s[r, :], b_s[r, :], c_s[r, :]
        p1, p2 = p1_ref[r, :], p2_ref[r, :]
        cb = _dot_nt(cm, bm)
        s_prev = state_s[...]
        cs_prev = _dot(cm, s_prev)
        y_parts, xs_parts, dec_parts = [], [], []
        for hh in range(_GH):
            lane = SSD_LANE + grp * _GH + hh
            ac = _lane_col(p2, lane)
            dt = _lane_col(p1, lane)
            gam, _, _ = _decay_matrix(ac, c)
            xv = x[:, hh * SSD_HEAD_DIM:(hh + 1) * SSD_HEAD_DIM] * dt
            a_last = ac[c - 1:c, :]
            y_h = _dot(cb * gam, xv) + jnp.exp(ac) * cs_prev[:, hh * SSD_HEAD_DIM:(hh + 1) * SSD_HEAD_DIM]
            y_parts.append(y_h)
            xs_parts.append(xv * jnp.exp(a_last - ac))
            dec_parts.append(jnp.broadcast_to(jnp.exp(a_last), (1, SSD_HEAD_DIM)))
        y_s[r, :] = jnp.concatenate(y_parts, axis=1)
        new_states = _dot(bm.T, jnp.concatenate(xs_parts, axis=1))
        state_s[...] = s_prev * jnp.concatenate(dec_parts, axis=1) + new_states
        return carry

    lax.fori_loop(0, t_len // c, chunk_step, 0)
    for hh in range(_GH):
        st_ref[hh] = state_s[:, hh * SSD_HEAD_DIM:(hh + 1) * SSD_HEAD_DIM]
    y = (y_s[...] + d_ref[...] * x_s[...]) * _silu(zg_ref[...])
    o_ref[...] = (y * lax.rsqrt(jnp.mean(y * y, axis=-1, keepdims=True) + EPS) * nw_ref[...]).astype(o_ref.dtype)


def _ssd_prompt(z, p1, p2, conv_w, conv_b, d_vec, norm_w, n_seq, t_len):
    m = n_seq * t_len
    xb, bb, cbk = SSDX_OFF // _GW, (SSDX_OFF + SSD_INNER) // LANE, (SSDX_OFF + SSD_INNER + 2 * SSD_STATE) // LANE
    wb0, wc0 = SSD_INNER // LANE, (SSD_INNER + 2 * SSD_STATE) // LANE

    def spec(rows, width, off):
        return pl.BlockSpec((rows, width), functools.partial(lambda b, g, off: (0, off + g), off=off))

    def zspec(width, off):
        return pl.BlockSpec((t_len, width), functools.partial(lambda b, g, off: (b, off + g), off=off))

    aux = pl.BlockSpec((t_len, LANE), lambda b, g: (b, 0))
    return pl.pallas_call(
        functools.partial(_ssd_prompt_kernel, t_len=t_len),
        out_shape=(jax.ShapeDtypeStruct((m, SSD_INNER), BF16),
                   jax.ShapeDtypeStruct((n_seq, SSD_HEADS, SSD_STATE, SSD_HEAD_DIM), F32)),
        grid=(n_seq, SSD_GROUPS),
        in_specs=[zspec(_GW, xb), zspec(LANE, bb), zspec(LANE, cbk),
                  spec(CONV_W, _GW, 0), spec(CONV_W, LANE, wb0), spec(CONV_W, LANE, wc0),
                  spec(1, _GW, 0), spec(1, LANE, wb0), spec(1, LANE, wc0),
                  aux, aux, zspec(_GW, SSDZ_OFF // _GW), spec(1, _GW, 0), spec(1, _GW, 0)],
        out_specs=(pl.BlockSpec((t_len, _GW), lambda b, g: (b, g)),
                   pl.BlockSpec((None, _GH, SSD_STATE, SSD_HEAD_DIM), lambda b, g: (b, g, 0, 0))),
        scratch_shapes=[pltpu.VMEM((t_len, _GW), F32), pltpu.VMEM((t_len, LANE), F32), pltpu.VMEM((t_len, LANE), F32),
                        pltpu.VMEM((t_len, _GW), F32), pltpu.VMEM((SSD_STATE, _GW), F32)],
        compiler_params=_cparams("arbitrary", "arbitrary"), name="ssd_prompt")(
            z, z, z, conv_w, conv_w, conv_w, conv_b, conv_b, conv_b, p1, p2, z, d_vec, norm_w.reshape(1, SSD_INNER))


_FFN_TN = 256
_FFN_NJ = FFN_DIM // _FFN_TN
_TAIL = 8


def _ffn_up_prompt_kernel(a_ref, bg_ref, bu_ref, wg_ref, wu_ref, cg_ref, cu_ref, g_ref, sg_ref, su_ref, b_s, y_s,
                          *, tiles_per_seq, n_sub):
    i = pl.program_id(1)
    tn = _FFN_TN
    tm = a_ref.shape[0]

    @pl.when(i == 0)
    def _():
        b_s[:, 0:tn] = bg_ref[...].astype(BF16)
        b_s[:, tn:2 * tn] = bu_ref[...].astype(BF16)

    @pl.when(i % tiles_per_seq == 0)
    def _():
        y_s[0:_TAIL, :] = jnp.zeros((_TAIL, 2 * tn), F32)

    @pl.when(i % tiles_per_seq != 0)
    def _():
        y_s[0:_TAIL, :] = y_s[tm:tm + _TAIL, :]

    w = jnp.concatenate([wg_ref[...], wu_ref[...]], axis=1)
    c = jnp.concatenate([cg_ref[...], cu_ref[...]], axis=1)
    ts = tm // n_sub
    for s in range(n_sub):
        r0 = _TAIL + s * ts
        y = _dot(a_ref[s * ts:(s + 1) * ts, :], b_s[...])
        y_s[r0:r0 + ts, :] = y
        u = w[0:1, :] * y_s[r0 - 2:r0 - 2 + ts, :] + w[1:2, :] * y_s[r0 - 1:r0 - 1 + ts, :] + w[2:3, :] * y + c
        g_ref[s * ts:(s + 1) * ts, :] = (_silu(u[:, 0:tn]) * u[:, tn:2 * tn]).astype(g_ref.dtype)
    sg_ref[...] = y_s[tm:tm + _TAIL, 0:tn]
    su_ref[...] = y_s[tm:tm + _TAIL, tn:2 * tn]


def _ffn_up_prompt(h2, w_up, layer, conv_w, conv_b, n_seq, t_len, tm):
    m = n_seq * t_len
    k = h2.shape[1]
    nj = _FFN_NJ
    tiles_per_seq = t_len // tm
    tail = jax.ShapeDtypeStruct((m // tm, _TAIL, FFN_DIM), F32)
    tail_spec = pl.BlockSpec((None, _TAIL, _FFN_TN), lambda j, i: (i, 0, j))
    return pl.pallas_call(
        functools.partial(_ffn_up_prompt_kernel, tiles_per_seq=tiles_per_seq, n_sub=2),
        out_shape=(jax.ShapeDtypeStruct((m, FFN_DIM), BF16), tail, tail),
        grid=(nj, m // tm),
        in_specs=[pl.BlockSpec((tm, k), lambda j, i: (i, 0)),
                  pl.BlockSpec((None, k, _FFN_TN), lambda j, i: (layer, 0, j)),
                  pl.BlockSpec((None, k, _FFN_TN), lambda j, i: (layer, 0, j + nj)),
                  pl.BlockSpec((FFN_CONV_W, _FFN_TN), lambda j, i: (0, j)),
                  pl.BlockSpec((FFN_CONV_W, _FFN_TN), lambda j, i: (0, j + nj)),
                  pl.BlockSpec((1, _FFN_TN), lambda j, i: (0, j)), pl.BlockSpec((1, _FFN_TN), lambda j, i: (0, j + nj))],
        out_specs=(pl.BlockSpec((tm, _FFN_TN), lambda j, i: (i, j)), tail_spec, tail_spec),
        scratch_shapes=[pltpu.VMEM((k, 2 * _FFN_TN), BF16), pltpu.VMEM((_TAIL + tm, 2 * _FFN_TN), F32)],
        compiler_params=_cparams("arbitrary", "arbitrary"), name="ffn_up_prompt")(
            h2, w_up, w_up, conv_w, conv_w, conv_b, conv_b)


def _ffn_up_sample_kernel(a_ref, bg_ref, bu_ref, wg_ref, wu_ref, cg_ref, cu_ref, s0g_ref, s1g_ref, s0u_ref, s1u_ref,
                          g_ref, yg_ref, yu_ref):
    a = a_ref[...]
    yg = _dot(a, bg_ref[...].astype(BF16))
    yu = _dot(a, bu_ref[...].astype(BF16))
    ug = wg_ref[0:1, :] * s0g_ref[...] + wg_ref[1:2, :] * s1g_ref[...] + wg_ref[2:3, :] * yg + cg_ref[...]
    uu = wu_ref[0:1, :] * s0u_ref[...] + wu_ref[1:2, :] * s1u_ref[...] + wu_ref[2:3, :] * yu + cu_ref[...]
    g_ref[...] = (_silu(ug) * uu).astype(g_ref.dtype)
    yg_ref[...] = yg
    yu_ref[...] = yu


def _ffn_up_sample(h2, w_up, layer, conv_w, conv_b, s0, s1):
    m, k = h2.shape
    nj = _FFN_NJ
    lo = lambda j: (0, j)
    hi = lambda j: (0, j + nj)
    row = lambda f: pl.BlockSpec((m, _FFN_TN), f)
    ysd = jax.ShapeDtypeStruct((m, FFN_DIM), F32)
    return pl.pallas_call(
        _ffn_up_sample_kernel, out_shape=(jax.ShapeDtypeStruct((m, FFN_DIM), BF16), ysd, ysd), grid=(nj,),
        in_specs=[pl.BlockSpec((m, k), lambda j: (0, 0)),
                  pl.BlockSpec((None, k, _FFN_TN), lambda j: (layer, 0, j)),
                  pl.BlockSpec((None, k, _FFN_TN), lambda j: (layer, 0, j + nj)),
                  pl.BlockSpec((FFN_CONV_W, _FFN_TN), lo), pl.BlockSpec((FFN_CONV_W, _FFN_TN), hi),
                  pl.BlockSpec((1, _FFN_TN), lo), pl.BlockSpec((1, _FFN_TN), hi),
                  row(lo), row(lo), row(hi), row(hi)],
        out_specs=(row(lo), row(lo), row(lo)),
        compiler_params=_cparams("arbitrary"), name="ffn_up_sample")(
            h2, w_up, w_up, conv_w, conv_w, conv_b, conv_b, s0, s1, s0, s1)


_PAGES_PER_STEP = 16


def _kmean_kernel(pt_ref, *refs):
    o_ref = refs[-1]
    page = refs[0].shape[0]
    per_block = MOBA_BLOCK // page
    for n in range(_PAGES_PER_STEP // per_block):
        acc = jnp.sum(refs[n * per_block][...], axis=0)
        for r in range(1, per_block):
            acc = acc + jnp.sum(refs[n * per_block + r][...], axis=0)
        o_ref[n] = acc * (1.0 / MOBA_BLOCK)


def _kmean(cache_k, page_table, layer):
    n_b, n_pages = page_table.shape
    page, n_kv, d = cache_k.shape[2:]
    per_block = MOBA_BLOCK // page
    n_blocks = n_pages // per_block
    steps = n_pages // _PAGES_PER_STEP
    in_specs = [pl.BlockSpec((None, None, page, n_kv, d),
                             functools.partial(lambda b, s, pt, r: (layer, pt[b, s * _PAGES_PER_STEP + r], 0, 0, 0), r=r))
                for r in range(_PAGES_PER_STEP)]
    grid_spec = pltpu.PrefetchScalarGridSpec(
        num_scalar_prefetch=1, grid=(n_b, steps), in_specs=in_specs,
        out_specs=pl.BlockSpec((None, _PAGES_PER_STEP // per_block, n_kv, d), lambda b, s, pt: (b, s, 0, 0)))
    return pl.pallas_call(
        _kmean_kernel, out_shape=jax.ShapeDtypeStruct((n_b, n_blocks, n_kv, d), F32), grid_spec=grid_spec,
        compiler_params=_cparams("arbitrary", "arbitrary"), name="kmean")(page_table, *([cache_k] * _PAGES_PER_STEP))


def _select_kernel(q_ref, k_ref, kmean_ref, wq_ref, wk_ref, qn_ref, kn_ref, idx_ref):
    n_b, n_h, _ = q_ref.shape
    n_blocks = kmean_ref.shape[1]
    q = q_ref[...]
    qn = q * lax.rsqrt(jnp.mean(q * q, axis=-1, keepdims=True) + EPS) * wq_ref[...]
    qn_ref[...] = qn
    k = k_ref[...]
    kn_ref[...] = k * lax.rsqrt(jnp.mean(k * k, axis=-1, keepdims=True) + EPS) * wk_ref[...]
    head = lax.broadcasted_iota(jnp.int32, (n_h, n_blocks), 0)
    lane = lax.broadcasted_iota(jnp.int32, (n_h, n_blocks), 1).astype(F32)
    lane_out = lax.broadcasted_iota(jnp.int32, (n_h, LANE), 1)
    for b in range(n_b):
        gate = jnp.zeros((n_h, n_blocks), F32)
        for kv in range(ATT_KV_HEADS):
            gk = _dot_nt(qn[b], kmean_ref[b, :, kv * HEAD_DIM:(kv + 1) * HEAD_DIM], HIGHEST)
            gate = jnp.where(head // ATT_GROUP == kv, gk, gate)
        out = jnp.zeros((n_h, LANE), F32)
        for s in range(MOBA_TOPK):
            mx = jnp.max(gate, axis=-1, keepdims=True)
            pick = jnp.min(jnp.where(gate == mx, lane, float(n_blocks)), axis=-1, keepdims=True)
            out = jnp.where(lane_out == s, pick, out)
            gate = jnp.where(lane == pick, -jnp.inf, gate)
        idx_ref[b] = out.astype(jnp.int32)


def _select(q3, k3, kmean, wq, wk):
    n_b, n_h, d = q3.shape
    full = lambda shape: pl.BlockSpec(shape, lambda i: (0,) * len(shape))
    return pl.pallas_call(
        _select_kernel,
        out_shape=(jax.ShapeDtypeStruct(q3.shape, F32), jax.ShapeDtypeStruct(k3.shape, F32),
                   jax.ShapeDtypeStruct((n_b, n_h, LANE), jnp.int32)),
        grid=(1,),
        in_specs=[full(q3.shape), full(k3.shape), full(kmean.shape), full((1, 1, d)), full((1, 1, d))],
        out_specs=(full(q3.shape), full(k3.shape), full((n_b, n_h, LANE))),
        compiler_params=_cparams("arbitrary"), name="select")(q3, k3, kmean, wq.reshape(1, 1, d), wk.reshape(1, 1, d))


def _attn_sample_kernel(pt_ref, idx_ref, qn_ref, kn_ref, vn_ref, *refs, n_sel, per_block):
    o_ref = refs[-1]
    pages = refs[:-1]
    n_pg = n_sel * per_block
    k_refs, v_refs = pages[:n_pg], pages[n_pg:]
    h = pl.program_id(1)
    kv = h // ATT_GROUP
    q = qn_ref[pl.ds(h, 1), :] * (HEAD_DIM ** -0.5)
    k_new = kn_ref[pl.ds(kv, 1), :]
    v_new = vn_ref[pl.ds(kv, 1), :]
    s_own = jnp.sum(q * k_new, axis=-1, keepdims=True)
    n_rows = k_refs[0].shape[0]
    mine = lax.broadcasted_iota(jnp.int32, (n_rows, 1), 0) % ATT_KV_HEADS == kv
    logits = [jnp.where(mine, jnp.sum(r[...] * q, axis=-1, keepdims=True), NEG_BIG) for r in k_refs]
    mx = s_own
    for s in logits:
        mx = jnp.maximum(mx, jnp.max(s, axis=0, keepdims=True))
    p_own = jnp.exp(s_own - mx)
    den = p_own
    acc = p_own * v_new
    for s, v_ref in zip(logits, v_refs):
        p = jnp.exp(s - mx)
        den = den + jnp.sum(p, axis=0, keepdims=True)
        acc = acc + jnp.sum(p * v_ref[...], axis=0, keepdims=True)
    o_ref[pl.ds(h, 1), :] = acc / den


def _attn_sample(cache_k, cache_v, page_table, idx, qn, kn, vn, layer):
    n_b, n_h, d = qn.shape
    depth, n_pool, page, n_kv = cache_k.shape[:4]
    per_block = MOBA_BLOCK // page
    n_sel = idx.shape[2]
    cache_k = cache_k.reshape(depth, n_pool, page * n_kv, d)
    cache_v = cache_v.reshape(depth, n_pool, page * n_kv, d)

    def page_spec(s, r):
        def imap(b, h, pt, ix):
            return (layer, pt[b, ix[b, h * n_sel + s] * per_block + r], 0, 0)
        return pl.BlockSpec((None, None, page * n_kv, d), imap)

    pspecs = [page_spec(s, r) for s in range(n_sel) for r in range(per_block)]
    slab = lambda n: pl.BlockSpec((None, n, d), lambda b, h, pt, ix: (b, 0, 0))
    grid_spec = pltpu.PrefetchScalarGridSpec(
        num_scalar_prefetch=2, grid=(n_b, n_h),
        in_specs=[slab(n_h), slab(ATT_KV_HEADS), slab(ATT_KV_HEADS)] + pspecs + pspecs,
        out_specs=slab(n_h))
    n_pg = len(pspecs)
    return pl.pallas_call(
        functools.partial(_attn_sample_kernel, n_sel=n_sel, per_block=per_block),
        out_shape=jax.ShapeDtypeStruct((n_b, n_h, d), F32), grid_spec=grid_spec,
        compiler_params=_cparams("arbitrary", "arbitrary"), name="attn_sample")(
            page_table, idx.reshape(n_b, n_h * n_sel), qn, kn, vn, *([cache_k] * n_pg), *([cache_v] * n_pg))


def _rows_to_cols(x):
    r, n = x.shape
    return jnp.concatenate([x, jnp.zeros((n - r, n), x.dtype)], axis=0).T


def _mix_sample_kernel(dx_ref, dbuf_ref, dw_ref, dbeta_ref, dg_ref, dz_ref, dnw_ref, dst_ref,
                       sx_ref, sxbuf_ref, sxw_ref, sxb_ref, bc_ref, bcbuf_ref, bcw_ref, bcb_ref,
                       sdt_ref, sa_ref, sz_ref, sd_ref, snw_ref, sst_ref,
                       dno_ref, dnst_ref, so_ref, sso_ref, y_s):
    x = dx_ref[...]
    conv = x * dw_ref[CONV_W - 1]
    for i in range(CONV_W - 1):
        conv = conv + dbuf_ref[i] * dw_ref[i]
    act = _silu(conv)
    nh = DN_HEADS
    q, k, v = act[0:nh], act[nh:2 * nh], act[2 * nh:3 * nh]
    q = q * lax.rsqrt(jnp.sum(q * q, axis=-1, keepdims=True) + EPS) * (HEAD_DIM ** -0.5)
    k = k * lax.rsqrt(jnp.sum(k * k, axis=-1, keepdims=True) + EPS)
    beta = dbeta_ref[...]
    eg = jnp.exp(dg_ref[...])
    qk = jnp.sum(q * k, axis=-1, keepdims=True)
    qt, kt = _rows_to_cols(q), _rows_to_cols(k)
    o_rows = []
    for h in range(nh):
        s0 = dst_ref[h]
        kcol, qcol = kt[:, h:h + 1], qt[:, h:h + 1]
        ks = jnp.sum(kcol * s0, axis=0, keepdims=True)
        qs = jnp.sum(qcol * s0, axis=0, keepdims=True)
        b_h, e_h = beta[h:h + 1, :], eg[h:h + 1, :]
        v_new = v[h:h + 1, :] * b_h - (b_h * e_h) * ks
        o_rows.append(e_h * qs + qk[h:h + 1, :] * v_new)
        dnst_ref[h] = s0 * e_h + kcol * v_new
    o = jnp.concatenate(o_rows, axis=0)
    on = o * lax.rsqrt(jnp.mean(o * o, axis=-1, keepdims=True) + EPS) * dnw_ref[...]
    dno_ref[...] = on * _silu(dz_ref[...])
    xs = sx_ref[...] * sxw_ref[CONV_W - 1] + sxb_ref[...]
    bc = bc_ref[...] * bcw_ref[CONV_W - 1] + bcb_ref[...]
    for i in range(CONV_W - 1):
        xs = xs + sxbuf_ref[i] * sxw_ref[i]
        bc = bc + bcbuf_ref[i] * bcw_ref[i]
    xs = _silu(xs)
    bc = _silu(bc)
    bct = _rows_to_cols(bc)
    cb = jnp.sum(bc[0:SSD_GROUPS] * bc[SSD_GROUPS:2 * SSD_GROUPS], axis=-1, keepdims=True)
    dt = sdt_ref[...]
    ea = jnp.exp(sa_ref[...])
    for h in range(SSD_HEADS):
        grp = h // _GH
        s0 = sst_ref[h]
        bcol, ccol = bct[:, grp:grp + 1], bct[:, SSD_GROUPS + grp:SSD_GROUPS + grp + 1]
        xv = xs[h:h + 1, :] * dt[h:h + 1, :]
        e_h = ea[h:h + 1, :]
        y_s[h:h + 1, :] = cb[grp:grp + 1, :] * xv + e_h * jnp.sum(ccol * s0, axis=0, keepdims=True)
        sso_ref[h] = s0 * e_h + bcol * xv
    y = (y_s[...] + sd_ref[...] * xs) * _silu(sz_ref[...])
    norm_rows = []
    for grp in range(SSD_GROUPS):
        yg = y[grp * _GH:(grp + 1) * _GH]
        ms = jnp.sum(jnp.sum(yg * yg, axis=-1, keepdims=True), axis=0, keepdims=True) * (1.0 / _GW)
        norm_rows.append(yg * lax.rsqrt(ms + EPS))
    so_ref[...] = jnp.concatenate(norm_rows, axis=0) * snw_ref[...]


def _mix_sample(args, n_b):
    def per_b(shape):
        nd = len(shape) - 1
        return pl.BlockSpec((None,) + tuple(shape[1:]), lambda b: (b,) + (0,) * nd)

    def shared(shape):
        nd = len(shape)
        return pl.BlockSpec(tuple(shape), lambda b: (0,) * nd)

    names_per_b = {"dx", "dbuf", "dbeta", "dg", "dz", "dst", "sx", "sxbuf", "bc", "bcbuf", "sdt", "sa", "sz", "sst"}
    order = ["dx", "dbuf", "dw", "dbeta", "dg", "dz", "dnw", "dst", "sx", "sxbuf", "sxw", "sxb", "bc", "bcbuf", "bcw",
             "bcb", "sdt", "sa", "sz", "sd", "snw", "sst"]
    in_specs = [per_b(args[n].shape) if n in names_per_b else shared(args[n].shape) for n in order]
    outs = (jax.ShapeDtypeStruct((n_b, DN_HEADS, HEAD_DIM), F32),
            jax.ShapeDtypeStruct((n_b, DN_HEADS, HEAD_DIM, HEAD_DIM), F32),
            jax.ShapeDtypeStruct((n_b, SSD_HEADS, SSD_HEAD_DIM), F32),
            jax.ShapeDtypeStruct((n_b, SSD_HEADS, SSD_STATE, SSD_HEAD_DIM), F32))
    return pl.pallas_call(
        _mix_sample_kernel, out_shape=outs, grid=(n_b,), in_specs=in_specs,
        out_specs=tuple(per_b(o.shape) for o in outs),
        scratch_shapes=[pltpu.VMEM((SSD_HEADS, SSD_HEAD_DIM), F32)],
        compiler_params=_cparams("arbitrary"), name="mix_sample")(*[args[n] for n in order])


def _split_w_in(w_in):
    depth, k, _ = w_in.shape
    wt = jnp.swapaxes(w_in, 1, 2)
    wt_b = wt[:, ZB_SRC:ZB_SRC + ZB_COLS]
    wt_s = jnp.concatenate([wt[:, ZS_SRC[0]:ZS_SRC[0] + 16], wt[:, ZS_SRC[1]:ZS_SRC[1] + 16],
                            jnp.zeros((depth, LANE - 32, k), w_in.dtype)], axis=1)
    return wt, wt_b, wt_s


def _in_proj_all(h, lw, layer, tm, suffix=""):
    za = _in_proj(h, lw["wt_a"], layer, ZA_COLS, tm=tm, tn=512, name="in_proj_a" + suffix)
    zb = _in_proj(h, lw["wt_b"], layer, ZB_COLS, tm=tm, tn=512, name="in_proj_b" + suffix)
    zs = _in_proj(h, lw["wt_s"], layer, LANE, tm=tm, tn=LANE, name="in_proj_s" + suffix)
    return za, zb, zs


def _lane_vec(dn_vals, ssd_vals):
    v = jnp.zeros((LANE,), F32)
    v = v.at[DNG_LANE:DNG_LANE + DN_HEADS].set(dn_vals.astype(F32))
    v = v.at[SSD_LANE:SSD_LANE + SSD_HEADS].set(ssd_vals.astype(F32))
    return v.reshape(1, LANE)


def _ffn_down(g, w_down, layer, x, tm):
    half = FFN_DIM // 2
    for part in range(2):
        x = _matmul([(g, half, part, part)], w_down, layer, tm=tm, tn=256, res=x, name="ffn_down")
    return x


def _prompt_layer(x, lw, layer, n_seq, t_len):
    tm = 1024
    h = _rmsnorm_cast(x, lw["norm1_w"], 512)
    za, zb, zs = _in_proj_all(h, lw, layer, tm)
    p1, p2 = _aux(zs, lw["bias_vec"], lw["alog_vec"], n_seq, t_len, CHUNK)
    o_att, k_norm = _attn_prompt(za, lw["attn_q_norm_w"], lw["attn_k_norm_w"], n_seq, t_len)
    o_dn, dn_state = _gdn_prompt(za, p1, p2, lw["dn_conv_w"], lw["dn_norm_w"], n_seq, t_len)
    o_ssd, ssd_state = _ssd_prompt(zb, p1, p2, lw["ssd_conv_w"], lw["ssd_conv_b"].reshape(1, -1), lw["ssd_d_vec"],
                                   lw["ssd_norm_w"], n_seq, t_len)
    x = _matmul([(o_att, 2048, 0, 0), (o_dn, 1024, 0, 2), (o_ssd, 1024, 0, 3)], lw["w_out"], layer, tm=tm, tn=512,
                res=x, name="out_proj")
    h2 = _rmsnorm_cast(x, lw["norm2_w"], 512)
    g, tail_g, tail_u = _ffn_up_prompt(h2, lw["ffn_w_up"], layer, lw["ffn_conv_w"], lw["ffn_conv_b"].reshape(1, -1),
                                       n_seq, t_len, tm)
    x = _ffn_down(g, lw["ffn_w_down"], layer, x, tm)
    za3 = za.reshape(n_seq, t_len, ZA_COLS)
    zb3 = zb.reshape(n_seq, t_len, ZB_COLS)
    tps = t_len // tm
    new = (k_norm.reshape(n_seq, t_len, ATT_KV_HEADS, HEAD_DIM),
           za3[:, :, V_OFF:V_OFF + 512].reshape(n_seq, t_len, ATT_KV_HEADS, HEAD_DIM),
           za3[:, t_len - (CONV_W - 1):, DNQKV_OFF:DNQKV_OFF + DN_CONV_CH],
           dn_state,
           zb3[:, t_len - (CONV_W - 1):, SSDX_OFF:SSDX_OFF + SSD_CONV_CH],
           ssd_state,
           jnp.concatenate([tail_g[tps - 1::tps, _TAIL - 2:], tail_u[tps - 1::tps, _TAIL - 2:]], axis=-1))
    return x, new


_SAMPLE_ROWS = 16


def _sample_layer(x, lw, layer, cache_k, cache_v, page_table, dn_conv_buf, dn_state, ssd_conv_buf, ssd_state,
                  ffn_conv_buf):
    n_b = page_table.shape[0]
    rows = x.shape[0]
    h = _rmsnorm_cast(x, lw["norm1_w"], rows)
    za, zb2, zs = _in_proj_all(h, lw, layer, rows, "_smp")
    p1, p2 = _aux(zs, lw["bias_vec"], lw["alog_vec"], 1, rows, 1)
    zb = za[:n_b]
    zbs = zb2[:n_b]
    kmean = _kmean(cache_k, page_table, layer)
    kmean = kmean.reshape(n_b, kmean.shape[1], ATT_KV_HEADS * HEAD_DIM)
    q3 = zb[:, Q_OFF:Q_OFF + 2048].reshape(n_b, ATT_HEADS, HEAD_DIM)
    k3 = zb[:, K_OFF:K_OFF + 512].reshape(n_b, ATT_KV_HEADS, HEAD_DIM)
    v3 = zb[:, V_OFF:V_OFF + 512].reshape(n_b, ATT_KV_HEADS, HEAD_DIM)
    qn, kn, idx = _select(q3, k3, kmean, lw["attn_q_norm_w"], lw["attn_k_norm_w"])
    o_att = _attn_sample(cache_k, cache_v, page_table, idx[:, :, :MOBA_TOPK], qn, kn, v3, layer)
    dn_x = zb[:, DNQKV_OFF:DNQKV_OFF + DN_CONV_CH]
    ssd_x = zbs[:, SSDX_OFF:SSDX_OFF + SSD_CONV_CH]
    nh3 = 3 * DN_HEADS
    ng2 = 2 * SSD_GROUPS
    args = {
        "dx": dn_x.reshape(n_b, nh3, HEAD_DIM),
        "dbuf": dn_conv_buf.reshape(n_b, CONV_W - 1, nh3, HEAD_DIM),
        "dw": lw["dn_conv_w"].reshape(CONV_W, nh3, HEAD_DIM),
        "dbeta": p1[:n_b, BETA_LANE:BETA_LANE + DN_HEADS].reshape(n_b, DN_HEADS, 1),
        "dg": p2[:n_b, DNG_LANE:DNG_LANE + DN_HEADS].reshape(n_b, DN_HEADS, 1),
        "dz": zb[:, DNZ_OFF:DNZ_OFF + DN_WIDTH].reshape(n_b, DN_HEADS, HEAD_DIM),
        "dnw": lw["dn_norm_w"].reshape(1, HEAD_DIM),
        "dst": dn_state,
        "sx": ssd_x[:, :SSD_INNER].reshape(n_b, SSD_HEADS, SSD_HEAD_DIM),
        "sxbuf": ssd_conv_buf[:, :, :SSD_INNER].reshape(n_b, CONV_W - 1, SSD_HEADS, SSD_HEAD_DIM),
        "sxw": lw["ssd_conv_w"][:, :SSD_INNER].reshape(CONV_W, SSD_HEADS, SSD_HEAD_DIM),
        "sxb": lw["ssd_conv_b"][:SSD_INNER].reshape(SSD_HEADS, SSD_HEAD_DIM),
        "bc": ssd_x[:, SSD_INNER:].reshape(n_b, ng2, SSD_STATE),
        "bcbuf": ssd_conv_buf[:, :, SSD_INNER:].reshape(n_b, CONV_W - 1, ng2, SSD_STATE),
        "bcw": lw["ssd_conv_w"][:, SSD_INNER:].reshape(CONV_W, ng2, SSD_STATE),
        "bcb": lw["ssd_conv_b"][SSD_INNER:].reshape(ng2, SSD_STATE),
        "sdt": p1[:n_b, SSD_LANE:SSD_LANE + SSD_HEADS].reshape(n_b, SSD_HEADS, 1),
        "sa": p2[:n_b, SSD_LANE:SSD_LANE + SSD_HEADS].reshape(n_b, SSD_HEADS, 1),
        "sz": zbs[:, SSDZ_OFF:SSDZ_OFF + SSD_INNER].reshape(n_b, SSD_HEADS, SSD_HEAD_DIM),
        "sd": lw["ssd_D"].astype(F32).reshape(SSD_HEADS, 1),
        "snw": lw["ssd_norm_w"].reshape(SSD_HEADS, SSD_HEAD_DIM),
        "sst": ssd_state,
    }
    o_dn, dn_state_new, o_ssd, ssd_state_new = _mix_sample(args, n_b)
    pad = lambda a: jnp.pad(a.reshape(n_b, -1), ((0, rows - n_b), (0, 0))).astype(BF16)
    x = _matmul([(pad(o_att), 2048, 0, 0), (pad(o_dn), 1024, 0, 2), (pad(o_ssd), 1024, 0, 3)], lw["w_out"], layer,
                tm=rows, tn=512, res=x, name="out_proj_s")
    h2 = _rmsnorm_cast(x, lw["norm2_w"], rows)
    padf = lambda a: jnp.pad(a, ((0, rows - n_b), (0, 0)))
    g, y_g, y_u = _ffn_up_sample(h2, lw["ffn_w_up"], layer, lw["ffn_conv_w"], lw["ffn_conv_b"].reshape(1, -1),
                                 padf(ffn_conv_buf[:, 0]), padf(ffn_conv_buf[:, 1]))
    x = _ffn_down(g, lw["ffn_w_down"], layer, x, rows)
    y_new = jnp.concatenate([y_g[:n_b], y_u[:n_b]], axis=-1)
    new = (kn.reshape(n_b, 1, ATT_KV_HEADS, HEAD_DIM), v3.reshape(n_b, 1, ATT_KV_HEADS, HEAD_DIM),
           jnp.concatenate([dn_conv_buf[:, 1:], dn_x[:, None]], axis=1), dn_state_new,
           jnp.concatenate([ssd_conv_buf[:, 1:], ssd_x[:, None]], axis=1), ssd_state_new,
           jnp.stack([ffn_conv_buf[:, 1], y_new], axis=1))
    return x, new


def kernel(x_prompt, x_sample, cache_k, cache_v, page_table, state_dn_conv, state_dn, state_ssd_conv, state_ssd, state_ffn_conv, norm1_w, w_in, attn_q_norm_w, attn_k_norm_w, dn_conv_w, dn_A_log, dn_dt_bias, dn_norm_w, ssd_conv_w, ssd_conv_b, ssd_dt_bias, ssd_A_log, ssd_D, ssd_norm_w, w_out, norm2_w, ffn_w_up, ffn_conv_w, ffn_conv_b, ffn_w_down):
    depth = w_in.shape[0]
    n_seq, t_len, d_model = x_prompt.shape
    n_b = x_sample.shape[0]
    yp = x_prompt.reshape(n_seq * t_len, d_model)
    ys = jnp.pad(x_sample.reshape(n_b, d_model), ((0, _SAMPLE_ROWS - n_b), (0, 0)))
    wt_a, wt_b, wt_s = _split_w_in(w_in)
    outs_p, outs_s = [], []
    for l in range(depth):
        lw = {"norm1_w": norm1_w[l], "wt_a": wt_a, "wt_b": wt_b, "wt_s": wt_s, "attn_q_norm_w": attn_q_norm_w[l],
              "attn_k_norm_w": attn_k_norm_w[l], "dn_conv_w": dn_conv_w[l], "dn_norm_w": dn_norm_w[l],
              "ssd_conv_w": ssd_conv_w[l], "ssd_conv_b": ssd_conv_b[l], "ssd_D": ssd_D[l],
              "ssd_norm_w": ssd_norm_w[l], "w_out": w_out, "norm2_w": norm2_w[l], "ffn_w_up": ffn_w_up,
              "ffn_conv_w": ffn_conv_w[l], "ffn_conv_b": ffn_conv_b[l], "ffn_w_down": ffn_w_down,
              "bias_vec": _lane_vec(dn_dt_bias[l], ssd_dt_bias[l]),
              "alog_vec": _lane_vec(dn_A_log[l], ssd_A_log[l]),
              "ssd_d_vec": jnp.repeat(ssd_D[l].astype(F32), SSD_HEAD_DIM).reshape(1, SSD_INNER)}
        yp, new_p = _prompt_layer(yp, lw, l, n_seq, t_len)
        ys, new_s = _sample_layer(ys, lw, l, cache_k, cache_v, page_table, state_dn_conv[l], state_dn[l],
                                  state_ssd_conv[l], state_ssd[l], state_ffn_conv[l])
        outs_p.append(new_p)
        outs_s.append(new_s)
    st = lambda outs, i: jnp.stack([o[i] for o in outs])
    return (yp.reshape(n_seq, t_len, d_model), ys[:n_b].reshape(n_b, 1, d_model),
            st(outs_p, 0), st(outs_p, 1), st(outs_s, 0), st(outs_s, 1),
            st(outs_p, 2), st(outs_s, 2), st(outs_p, 3), st(outs_s, 3),
            st(outs_p, 4), st(outs_s, 4), st(outs_p, 5), st(outs_s, 5),
            st(outs_p, 6), st(outs_s, 6))
```

```python
import functools
import math

import jax
import jax.numpy as jnp
from jax import lax
from jax.experimental import pallas as pl
from jax.experimental.pallas import tpu as pltpu

F32 = jnp.float32
BF16 = jnp.bfloat16
HIGHEST = lax.Precision.HIGHEST

D_MODEL = 4096
HEAD_DIM = 128
ATT_HEADS = 16
ATT_KV_HEADS = 4
ATT_GROUP = 4
MOBA_BLOCK = 256
MOBA_TOPK = 3
DN_HEADS = 8
DN_WIDTH = 1024
DN_CONV_CH = 3072
SSD_INNER = 1024
SSD_HEAD_DIM = 64
SSD_HEADS = 16
SSD_GROUPS = 2
SSD_STATE = 128
SSD_CONV_CH = 1536
CONV_W = 4
CHUNK = 64
FFN_DIM = 11008
FFN_CONV_W = 3
EPS = 1e-6

Q_OFF, K_OFF, V_OFF = 0, 2048, 2560
DNQKV_OFF, DNZ_OFF = 3072, 6144
ZA_COLS = 7168
SSDX_OFF, SSDZ_OFF = 0, 1536
ZB_SRC, ZB_COLS = 7184, 2560
ZS_SRC = (7168, 9744)
LANE = 128
BETA_LANE, DNG_LANE, SSD_LANE = 0, 8, 16

VMEM_LIMIT_BYTES = 56 * 1024 * 1024
NEG_BIG = -1e30


def _cparams(*sem):
    return pltpu.CompilerParams(dimension_semantics=sem, vmem_limit_bytes=VMEM_LIMIT_BYTES)


def _silu(x):
    return x / (1.0 + jnp.exp(-x))


def _sigmoid(x):
    return 1.0 / (1.0 + jnp.exp(-x))


def _softplus(x):
    return jnp.maximum(x, 0.0) + jnp.log1p(jnp.exp(-jnp.abs(x)))


def _dot(a, b, precision=None):
    return jnp.dot(a, b, preferred_element_type=F32, precision=precision)


def _dot_nt(a, b, precision=None):
    return lax.dot_general(a, b, (((1,), (1,)), ((), ())), preferred_element_type=F32, precision=precision)


def _shift_rows(x, s):
    rows = lax.broadcasted_iota(jnp.int32, x.shape, 0)
    return jnp.where(rows < s, 0.0, pltpu.roll(x, s, axis=0))


def _causal_conv_rows(x, w_ref, width):
    y = x * w_ref[width - 1:width, :]
    for i in range(width - 1):
        y = y + _shift_rows(x, width - 1 - i) * w_ref[i:i + 1, :]
    return y


def _lane_col(x, lane):
    lanes = lax.broadcasted_iota(jnp.int32, x.shape, 1)
    return jnp.sum(jnp.where(lanes == lane, x, 0.0), axis=-1, keepdims=True)


def _decay_matrix(gc_col, c):
    ii = lax.broadcasted_iota(jnp.int32, (c, c), 0)
    jj = lax.broadcasted_iota(jnp.int32, (c, c), 1)
    gcb = jnp.broadcast_to(gc_col, (c, c))
    gc_row = jnp.sum(jnp.where(ii == jj, gcb, 0.0), axis=0, keepdims=True)
    low = ii >= jj
    gam = jnp.where(low, jnp.exp(jnp.where(low, gcb - gc_row, 0.0)), 0.0)
    return gam, ii, jj


def _rmsnorm_kernel(x_ref, w_ref, o_ref):
    x = x_ref[...]
    ms = jnp.mean(x * x, axis=-1, keepdims=True)
    o_ref[...] = (x * lax.rsqrt(ms + EPS) * w_ref[...]).astype(o_ref.dtype)


def _rmsnorm_cast(x, w, tm):
    m, d = x.shape
    return pl.pallas_call(
        _rmsnorm_kernel, out_shape=jax.ShapeDtypeStruct((m, d), BF16), grid=(m // tm,),
        in_specs=[pl.BlockSpec((tm, d), lambda i: (i, 0)), pl.BlockSpec((1, d), lambda i: (0, 0))],
        out_specs=pl.BlockSpec((tm, d), lambda i: (i, 0)),
        compiler_params=_cparams("arbitrary"), name="rmsnorm")(x, w.reshape(1, d))


def _matmul_kernel(*refs, n_parts, has_res):
    a_refs = refs[:n_parts]
    b_refs = refs[n_parts:2 * n_parts]
    o_ref = refs[-1]
    acc = None
    for a_ref, b_ref in zip(a_refs, b_refs):
        d = _dot(a_ref[...], b_ref[...].astype(BF16))
        acc = d if acc is None else acc + d
    if has_res:
        acc = acc + refs[2 * n_parts][...]
    o_ref[...] = acc.astype(o_ref.dtype)


def _matmul(a_parts, b, layer, *, tm, tn, res=None, out_dtype=F32, name="matmul"):
    m = a_parts[0][0].shape[0]
    n = b.shape[2]
    in_specs, args = [], []
    for arr, kp, cb, rb in a_parts:
        in_specs.append(pl.BlockSpec((tm, kp), functools.partial(lambda i, j, cb: (i, cb), cb=cb)))
        args.append(arr)
    for arr, kp, cb, rb in a_parts:
        in_specs.append(pl.BlockSpec((None, kp, tn), functools.partial(lambda i, j, rb: (layer, rb, j), rb=rb)))
        args.append(b)
    if res is not None:
        in_specs.append(pl.BlockSpec((tm, tn), lambda i, j: (i, j)))
        args.append(res)
    return pl.pallas_call(
        functools.partial(_matmul_kernel, n_parts=len(a_parts), has_res=res is not None),
        out_shape=jax.ShapeDtypeStruct((m, n), out_dtype), grid=(m // tm, n // tn),
        in_specs=in_specs, out_specs=pl.BlockSpec((tm, tn), lambda i, j: (i, j)),
        compiler_params=_cparams("arbitrary", "arbitrary"), name=name)(*args)


def _in_proj_kernel(a_ref, w_ref, o_ref, b_s):
    @pl.when(pl.program_id(1) == 0)
    def _():
        b_s[...] = w_ref[...].astype(BF16)

    o_ref[...] = _dot_nt(a_ref[...], b_s[...])


def _in_proj(h, wt, layer, n, *, tm, tn, name):
    m, k = h.shape
    return pl.pallas_call(
        _in_proj_kernel, out_shape=jax.ShapeDtypeStruct((m, n), F32), grid=(n // tn, m // tm),
        in_specs=[pl.BlockSpec((tm, k), lambda j, i: (i, 0)), pl.BlockSpec((None, tn, k), lambda j, i: (layer, j, 0))],
        out_specs=pl.BlockSpec((tm, tn), lambda j, i: (i, j)),
        scratch_shapes=[pltpu.VMEM((tn, k), BF16)],
        compiler_params=_cparams("arbitrary", "arbitrary"), name=name)(h, wt)


def _aux_kernel(s_ref, bias_ref, alog_ref, p1_ref, p2_ref, *, t_len, chunk):
    x = s_ref[...]
    lanes = lax.broadcasted_iota(jnp.int32, x.shape, 1)
    sp = _softplus(x + bias_ref[...])
    p1_ref[...] = jnp.where(lanes < DNG_LANE, _sigmoid(x), sp)
    g = -jnp.exp(alog_ref[...]) * sp
    if chunk == 1:
        p2_ref[...] = g
    else:
        ii = lax.broadcasted_iota(jnp.int32, (chunk, chunk), 0)
        jj = lax.broadcasted_iota(jnp.int32, (chunk, chunk), 1)
        tril = jnp.where(ii >= jj, 1.0, 0.0).astype(F32)
        for c in range(t_len // chunk):
            p2_ref[c * chunk:(c + 1) * chunk, :] = _dot(tril, g[c * chunk:(c + 1) * chunk, :], HIGHEST)


def _aux(z, bias_vec, alog_vec, n_seq, t_len, chunk):
    m = n_seq * t_len
    blk = pl.BlockSpec((t_len, LANE), lambda b: (b, 0))
    vec = pl.BlockSpec((1, LANE), lambda b: (0, 0))
    out = pl.BlockSpec((t_len, LANE), lambda b: (b, 0))
    return pl.pallas_call(
        functools.partial(_aux_kernel, t_len=t_len, chunk=chunk),
        out_shape=(jax.ShapeDtypeStruct((m, LANE), F32), jax.ShapeDtypeStruct((m, LANE), F32)),
        grid=(n_seq,), in_specs=[blk, vec, vec], out_specs=(out, out),
        compiler_params=_cparams("arbitrary"), name="aux")(z, bias_vec, alog_vec)


_ONES_ROWS = 16


def _attn_prompt_kernel(q_ref, k_ref, v_ref, wq_ref, wk_ref, o_ref, kn_ref, kb_ref, vt_ref, kmean_ref,
                        *, t_len):
    nb = t_len // MOBA_BLOCK
    g = pl.program_id(2)

    @pl.when(g == 0)
    def _():
        k = k_ref[...]
        kn = k * lax.rsqrt(jnp.mean(k * k, axis=-1, keepdims=True) + EPS) * wk_ref[...]
        kn_ref[...] = kn
        kb_ref[...] = kn.astype(BF16)
        for n in range(nb):
            kmean_ref[n:n + 1, :] = jnp.mean(kn[n * MOBA_BLOCK:(n + 1) * MOBA_BLOCK, :], axis=0, keepdims=True)
        vt_ref[0:HEAD_DIM, :] = v_ref[...].T.astype(BF16)
        vt_ref[HEAD_DIM:HEAD_DIM + _ONES_ROWS, :] = jnp.ones((_ONES_ROWS, t_len), BF16)

    q = q_ref[...]
    qn = q * lax.rsqrt(jnp.mean(q * q, axis=-1, keepdims=True) + EPS) * wq_ref[...]
    gate = _dot_nt(kmean_ref[...], qn, HIGHEST)
    blk = lax.broadcasted_iota(jnp.int32, (nb, t_len), 0)
    own = lax.broadcasted_iota(jnp.int32, (nb, t_len), 1) // MOBA_BLOCK
    valid = blk < own
    gm = jnp.where(valid, gate, -jnp.inf)
    cnt = jnp.zeros((nb, t_len), F32)
    for m in range(nb):
        row = gm[m:m + 1, :]
        beats = jnp.where(row > gm, 1.0, jnp.where(row == gm, jnp.where(blk > m, 1.0, 0.0), 0.0))
        cnt = cnt + beats
    bias = jnp.where(valid, jnp.where(cnt < MOBA_TOPK, 0.0, NEG_BIG), NEG_BIG)
    qs = (qn * (HEAD_DIM ** -0.5 * math.log2(math.e))).astype(BF16)
    kk = lax.broadcasted_iota(jnp.int32, (MOBA_BLOCK, MOBA_BLOCK), 0)
    qq = lax.broadcasted_iota(jnp.int32, (MOBA_BLOCK, MOBA_BLOCK), 1)
    causal = jnp.where(kk <= qq, 0.0, NEG_BIG)
    for qi in range(nb):
        n_keys = (qi + 1) * MOBA_BLOCK
        cols = slice(qi * MOBA_BLOCK, (qi + 1) * MOBA_BLOCK)
        st = _dot_nt(kb_ref[0:n_keys, :], qs[cols, :])
        blocks = [st[n * MOBA_BLOCK:(n + 1) * MOBA_BLOCK, :] + bias[n:n + 1, cols] for n in range(qi)]
        blocks.append(st[qi * MOBA_BLOCK:n_keys, :] + causal)
        mx = jnp.max(blocks[0], axis=0, keepdims=True)
        for blk_s in blocks[1:]:
            mx = jnp.maximum(mx, jnp.max(blk_s, axis=0, keepdims=True))
        p = [jnp.exp2(blk_s - mx).astype(BF16) for blk_s in blocks]
        p = jnp.concatenate(p, axis=0) if qi else p[0]
        ot = _dot(vt_ref[:, 0:n_keys], p)
        o = ot[0:HEAD_DIM, :] / ot[HEAD_DIM:HEAD_DIM + 1, :]
        o_ref[cols, :] = o.T.astype(o_ref.dtype)


def _attn_prompt(z, wq, wk, n_seq, t_len):
    m = n_seq * t_len
    qblk = pl.BlockSpec((t_len, HEAD_DIM), lambda b, k, g: (b, Q_OFF // HEAD_DIM + k * ATT_GROUP + g))
    kblk = pl.BlockSpec((t_len, HEAD_DIM), lambda b, k, g: (b, K_OFF // HEAD_DIM + k))
    vblk = pl.BlockSpec((t_len, HEAD_DIM), lambda b, k, g: (b, V_OFF // HEAD_DIM + k))
    wspec = pl.BlockSpec((1, HEAD_DIM), lambda b, k, g: (0, 0))
    return pl.pallas_call(
        functools.partial(_attn_prompt_kernel, t_len=t_len),
        out_shape=(jax.ShapeDtypeStruct((m, ATT_HEADS * HEAD_DIM), BF16),
                   jax.ShapeDtypeStruct((m, ATT_KV_HEADS * HEAD_DIM), F32)),
        grid=(n_seq, ATT_KV_HEADS, ATT_GROUP),
        in_specs=[qblk, kblk, vblk, wspec, wspec],
        out_specs=(pl.BlockSpec((t_len, HEAD_DIM), lambda b, k, g: (b, k * ATT_GROUP + g)),
                   pl.BlockSpec((t_len, HEAD_DIM), lambda b, k, g: (b, k))),
        scratch_shapes=[pltpu.VMEM((t_len, HEAD_DIM), BF16), pltpu.VMEM((HEAD_DIM + _ONES_ROWS, t_len), BF16),
                        pltpu.VMEM((t_len // MOBA_BLOCK, HEAD_DIM), F32)],
        compiler_params=_cparams("arbitrary", "arbitrary", "arbitrary"), name="attn_prompt")(
            z, z, z, wq.reshape(1, HEAD_DIM), wk.reshape(1, HEAD_DIM))


def _tri_inverse(a, c):
    ii = lax.broadcasted_iota(jnp.int32, (c, c), 0)
    jj = lax.broadcasted_iota(jnp.int32, (c, c), 1)
    eye = jnp.where(ii == jj, 1.0, 0.0).astype(F32)
    p = -a
    inv = eye + p
    steps = int(math.log2(c)) - 1
    for _ in range(steps):
        p = _dot(p, p)
        inv = inv + _dot(inv, p)
    return inv


_GDN_UNROLL = 8


def _gdn_prompt_kernel(zq_ref, zk_ref, zv_ref, wq_ref, wk_ref, wv_ref, p1_ref, p2_ref, zg_ref, nw_ref,
                       o_ref, s_ref, q_s, k_s, v_s, beta_s, gc_s, qp_s, op_s, mn_s, nn_s, *, t_len):
    h = pl.program_id(1)
    c = CHUNK
    d = HEAD_DIM

    def l2n(x):
        return x * lax.rsqrt(jnp.sum(x * x, axis=-1, keepdims=True) + EPS)

    q_s[...] = l2n(_silu(_causal_conv_rows(zq_ref[...], wq_ref, CONV_W))) * (HEAD_DIM ** -0.5)
    k_s[...] = l2n(_silu(_causal_conv_rows(zk_ref[...], wk_ref, CONV_W)))
    v_s[...] = _silu(_causal_conv_rows(zv_ref[...], wv_ref, CONV_W))
    beta_s[...] = jnp.broadcast_to(_lane_col(p1_ref[...], BETA_LANE + h), (t_len, HEAD_DIM))
    gc_s[...] = jnp.broadcast_to(_lane_col(p2_ref[...], DNG_LANE + h), (t_len, HEAD_DIM))

    ii = lax.broadcasted_iota(jnp.int32, (c, c), 0)
    jj = lax.broadcasted_iota(jnp.int32, (c, c), 1)
    eye = jnp.where(ii == jj, 1.0, 0.0).astype(F32)

    def prepare_group(gi, carry):
        n = _GDN_UNROLL
        chunks = [gi * n + u for u in range(n)]
        rows = [pl.ds(pl.multiple_of(ci * c, c), c) for ci in chunks]
        q = [q_s[r, :] for r in rows]
        k = [k_s[r, :] for r in rows]
        gcb = [gc_s[r, :] for r in rows]
        kb = [k[u] * beta_s[rows[u], :] for u in range(n)]
        gam = [_decay_matrix(gcb[u][:, 0:1], c)[0] for u in range(n)]
        kq = [_dot_nt(jnp.concatenate([kb[u], q[u]], axis=0), k[u]) for u in range(n)]
        attn = [kq[u][c:2 * c] * gam[u] for u in range(n)]
        p = [-jnp.where(ii > jj, kq[u][0:c] * gam[u], 0.0) for u in range(n)]
        inv = [eye + p[u] for u in range(n)]
        for _ in range(int(math.log2(c)) - 1):
            p = [_dot(p[u], p[u]) for u in range(n)]
            inv = [inv[u] + _dot(inv[u], p[u]) for u in range(n)]
        eg = [jnp.exp(gcb[u]) for u in range(n)]
        rhs = [jnp.concatenate([v_s[rows[u], :] * beta_s[rows[u], :], kb[u] * eg[u]], axis=1) for u in range(n)]
        uw = [_dot(inv[u], rhs[u]) for u in range(n)]
        auw = [_dot(attn[u], uw[u]) for u in range(n)]
        k_dec = [k[u] * jnp.exp(gcb[u][c - 1:c, :] - gcb[u]) for u in range(n)]
        kuw = [_dot(k_dec[u].T, uw[u]) for u in range(n)]
        for u in range(n):
            rd = pl.ds(pl.multiple_of(chunks[u] * d, d), d)
            op_s[rows[u], :] = auw[u][:, 0:d]
            qp_s[rows[u], :] = eg[u] * q[u] - auw[u][:, d:2 * d]
            nn_s[rd, :] = kuw[u][:, 0:d]
            mn_s[rd, :] = -kuw[u][:, d:2 * d]
        return carry

    lax.fori_loop(0, t_len // (c * _GDN_UNROLL), prepare_group, 0)

    def chunk_step(ci, s):
        r = pl.ds(pl.multiple_of(ci * c, c), c)
        rd = pl.ds(pl.multiple_of(ci * d, d), d)
        ms = _dot(jnp.concatenate([mn_s[rd, :], qp_s[r, :]], axis=0), s)
        op_s[r, :] = ms[d:d + c] + op_s[r, :]
        e = jnp.exp(gc_s[pl.ds(ci * c + c - 1, 1), :])
        return s * e[:, 0:1] + ms[0:d] + nn_s[rd, :]

    s_fin = lax.fori_loop(0, t_len // c, chunk_step, jnp.zeros((HEAD_DIM, HEAD_DIM), F32))
    s_ref[...] = s_fin
    o = op_s[...]
    on = o * lax.rsqrt(jnp.mean(o * o, axis=-1, keepdims=True) + EPS) * nw_ref[...]
    o_ref[...] = (on * _silu(zg_ref[...])).astype(o_ref.dtype)


def _gdn_prompt(z, p1, p2, conv_w, norm_w, n_seq, t_len):
    m = n_seq * t_len
    base = DNQKV_OFF // HEAD_DIM

    def zcol(off):
        return pl.BlockSpec((t_len, HEAD_DIM), functools.partial(lambda b, h, off: (b, off + h), off=off))

    def wcol(off):
        return pl.BlockSpec((CONV_W, HEAD_DIM), functools.partial(lambda b, h, off: (0, off + h), off=off))

    aux = pl.BlockSpec((t_len, LANE), lambda b, h: (b, 0))
    tbuf = pltpu.VMEM((t_len, HEAD_DIM), F32)
    return pl.pallas_call(
        functools.partial(_gdn_prompt_kernel, t_len=t_len),
        out_shape=(jax.ShapeDtypeStruct((m, DN_WIDTH), BF16),
                   jax.ShapeDtypeStruct((n_seq, DN_HEADS, HEAD_DIM, HEAD_DIM), F32)),
        grid=(n_seq, DN_HEADS),
        in_specs=[zcol(base), zcol(base + DN_HEADS), zcol(base + 2 * DN_HEADS),
                  wcol(0), wcol(DN_HEADS), wcol(2 * DN_HEADS), aux, aux,
                  zcol(DNZ_OFF // HEAD_DIM), pl.BlockSpec((1, HEAD_DIM), lambda b, h: (0, 0))],
        out_specs=(pl.BlockSpec((t_len, HEAD_DIM), lambda b, h: (b, h)),
                   pl.BlockSpec((None, None, HEAD_DIM, HEAD_DIM), lambda b, h: (b, h, 0, 0))),
        scratch_shapes=[tbuf] * 7 + [pltpu.VMEM((t_len // CHUNK * HEAD_DIM, HEAD_DIM), F32)] * 2,
        compiler_params=_cparams("arbitrary", "arbitrary"), name="gdn_prompt")(
            z, z, z, conv_w, conv_w, conv_w, p1, p2, z, norm_w.reshape(1, HEAD_DIM))


_GH = SSD_HEADS // SSD_GROUPS
_GW = _GH * SSD_HEAD_DIM
_SSD_UNROLL = 4


def _ssd_prompt_kernel(zx_ref, zb_ref, zc_ref, wx_ref, wb_ref, wc_ref, bx_ref, bb_ref, bc_ref, p1_ref, p2_ref,
                       zg_ref, d_ref, nw_ref, o_ref, st_ref, x_s, b_s, c_s, y_s, state_s, *, t_len):
    grp = pl.program_id(1)
    c = CHUNK
    x_s[...] = _silu(_causal_conv_rows(zx_ref[...], wx_ref, CONV_W) + bx_ref[...])
    b_s[...] = _silu(_causal_conv_rows(zb_ref[...], wb_ref, CONV_W) + bb_ref[...])
    c_s[...] = _silu(_causal_conv_rows(zc_ref[...], wc_ref, CONV_W) + bc_ref[...])
    state_s[...] = jnp.zeros_like(state_s)

    def chunk_group(gi, carry):
        n = _SSD_UNROLL
        rows = [pl.ds(pl.multiple_of((gi * n + u) * c, c), c) for u in range(n)]
        cm = [c_s[r, :] for r in rows]
        bm = [b_s[r, :] for r in rows]
        cb = [_dot_nt(cm[u], bm[u]) for u in range(n)]
        eac, dec, xs, xv, gam = [], [], [], [], []
        for u in range(n):
            x, p1, p2 = x_s[rows[u], :], p1_ref[rows[u], :], p2_ref[rows[u], :]
            eac_h, dec_h, xs_h, xv_h, gam_h = [], [], [], [], []
            for hh in range(_GH):
                lane = SSD_LANE + grp * _GH + hh
                ac = _lane_col(p2, lane)
                a_last = ac[c - 1:c, :]
                xv_hh = x[:, hh * SSD_HEAD_DIM:(hh + 1) * SSD_HEAD_DIM] * _lane_col(p1, lane)
                gam_h.append(_decay_matrix(ac, c)[0])
                xv_h.append(xv_hh)
                xs_h.append(xv_hh * jnp.exp(a_last - ac))
                eac_h.append(jnp.broadcast_to(jnp.exp(ac), (c, SSD_HEAD_DIM)))
                dec_h.append(jnp.broadcast_to(jnp.exp(a_last), (1, SSD_HEAD_DIM)))
            gam.append(gam_h)
            xv.append(xv_h)
            xs.append(jnp.concatenate(xs_h, axis=1))
            eac.append(jnp.concatenate(eac_h, axis=1))
            dec.append(jnp.concatenate(dec_h, axis=1))
        y_intra = [jnp.concatenate([_dot(cb[u] * gam[u][hh], xv[u][hh]) for hh in range(_GH)], axis=1)
                   for u in range(n)]
        new_states = [_dot(bm[u].T, xs[u]) for u in range(n)]
        s_in = [state_s[...]]
        for u in range(n):
            s_in.append(s_in[u] * dec[u] + new_states[u])
        state_s[...] = s_in[n]
        for u in range(n):
            y_s[rows[u], :] = y_intra[u] + eac[u] * _dot(cm[u], s_in[u])
        return carry

    lax.fori_loop(0, t_len // (c * _SSD_UNROLL), chunk_group, 0)
    for hh in range(_GH):
        st_ref[hh] = state_s[:, hh * SSD_HEAD_DIM:(hh + 1) * SSD_HEAD_DIM]
    y = (y_s[...] + d_ref[...] * x_s[...]) * _silu(zg_ref[...])
    o_ref[...] = (y * lax.rsqrt(jnp.mean(y * y, axis=-1, keepdims=True) + EPS) * nw_ref[...]).astype(o_ref.dtype)


def _ssd_prompt(z, p1, p2, conv_w, conv_b, d_vec, norm_w, n_seq, t_len):
    m = n_seq * t_len
    xb, bb, cbk = SSDX_OFF // _GW, (SSDX_OFF + SSD_INNER) // LANE, (SSDX_OFF + SSD_INNER + 2 * SSD_STATE) // LANE
    wb0, wc0 = SSD_INNER // LANE, (SSD_INNER + 2 * SSD_STATE) // LANE

    def spec(rows, width, off):
        return pl.BlockSpec((rows, width), functools.partial(lambda b, g, off: (0, off + g), off=off))

    def zspec(width, off):
        return pl.BlockSpec((t_len, width), functools.partial(lambda b, g, off: (b, off + g), off=off))

    aux = pl.BlockSpec((t_len, LANE), lambda b, g: (b, 0))
    return pl.pallas_call(
        functools.partial(_ssd_prompt_kernel, t_len=t_len),
        out_shape=(jax.ShapeDtypeStruct((m, SSD_INNER), BF16),
                   jax.ShapeDtypeStruct((n_seq, SSD_HEADS, SSD_STATE, SSD_HEAD_DIM), F32)),
        grid=(n_seq, SSD_GROUPS),
        in_specs=[zspec(_GW, xb), zspec(LANE, bb), zspec(LANE, cbk),
                  spec(CONV_W, _GW, 0), spec(CONV_W, LANE, wb0), spec(CONV_W, LANE, wc0),
                  spec(1, _GW, 0), spec(1, LANE, wb0), spec(1, LANE, wc0),
                  aux, aux, zspec(_GW, SSDZ_OFF // _GW), spec(1, _GW, 0), spec(1, _GW, 0)],
        out_specs=(pl.BlockSpec((t_len, _GW), lambda b, g: (b, g)),
                   pl.BlockSpec((None, _GH, SSD_STATE, SSD_HEAD_DIM), lambda b, g: (b, g, 0, 0))),
        scratch_shapes=[pltpu.VMEM((t_len, _GW), F32), pltpu.VMEM((t_len, LANE), F32), pltpu.VMEM((t_len, LANE), F32),
                        pltpu.VMEM((t_len, _GW), F32), pltpu.VMEM((SSD_STATE, _GW), F32)],
        compiler_params=_cparams("arbitrary", "arbitrary"), name="ssd_prompt")(
            z, z, z, conv_w, conv_w, conv_w, conv_b, conv_b, conv_b, p1, p2, z, d_vec, norm_w.reshape(1, SSD_INNER))


_FFN_TN = 256
_FFN_NJ = FFN_DIM // _FFN_TN
_TAIL = 8


def _ffn_up_prompt_kernel(a_ref, bg_ref, bu_ref, wg_ref, wu_ref, cg_ref, cu_ref, g_ref, sg_ref, su_ref, b_s, y_s,
                          *, tiles_per_seq, n_sub):
    i = pl.program_id(1)
    tn = _FFN_TN
    tm = a_ref.shape[0]

    @pl.when(i == 0)
    def _():
        b_s[:, 0:tn] = bg_ref[...].astype(BF16)
        b_s[:, tn:2 * tn] = bu_ref[...].astype(BF16)

    @pl.when(i % tiles_per_seq == 0)
    def _():
        y_s[0:_TAIL, :] = jnp.zeros((_TAIL, 2 * tn), F32)

    @pl.when(i % tiles_per_seq != 0)
    def _():
        y_s[0:_TAIL, :] = y_s[tm:tm + _TAIL, :]

    w = jnp.concatenate([wg_ref[...], wu_ref[...]], axis=1)
    c = jnp.concatenate([cg_ref[...], cu_ref[...]], axis=1)
    ts = tm // n_sub
    for s in range(n_sub):
        r0 = _TAIL + s * ts
        y = _dot(a_ref[s * ts:(s + 1) * ts, :], b_s[...])
        y_s[r0:r0 + ts, :] = y
        u = w[0:1, :] * y_s[r0 - 2:r0 - 2 + ts, :] + w[1:2, :] * y_s[r0 - 1:r0 - 1 + ts, :] + w[2:3, :] * y + c
        g_ref[s * ts:(s + 1) * ts, :] = (_silu(u[:, 0:tn]) * u[:, tn:2 * tn]).astype(g_ref.dtype)
    sg_ref[...] = y_s[tm:tm + _TAIL, 0:tn]
    su_ref[...] = y_s[tm:tm + _TAIL, tn:2 * tn]


def _ffn_up_prompt(h2, w_up, layer, conv_w, conv_b, n_seq, t_len, tm):
    m = n_seq * t_len
    k = h2.shape[1]
    nj = _FFN_NJ
    tiles_per_seq = t_len // tm
    tail = jax.ShapeDtypeStruct((m // tm, _TAIL, FFN_DIM), F32)
    tail_spec = pl.BlockSpec((None, _TAIL, _FFN_TN), lambda j, i: (i, 0, j))
    return pl.pallas_call(
        functools.partial(_ffn_up_prompt_kernel, tiles_per_seq=tiles_per_seq, n_sub=2),
        out_shape=(jax.ShapeDtypeStruct((m, FFN_DIM), BF16), tail, tail),
        grid=(nj, m // tm),
        in_specs=[pl.BlockSpec((tm, k), lambda j, i: (i, 0)),
                  pl.BlockSpec((None, k, _FFN_TN), lambda j, i: (layer, 0, j)),
                  pl.BlockSpec((None, k, _FFN_TN), lambda j, i: (layer, 0, j + nj)),
                  pl.BlockSpec((FFN_CONV_W, _FFN_TN), lambda j, i: (0, j)),
                  pl.BlockSpec((FFN_CONV_W, _FFN_TN), lambda j, i: (0, j + nj)),
                  pl.BlockSpec((1, _FFN_TN), lambda j, i: (0, j)), pl.BlockSpec((1, _FFN_TN), lambda j, i: (0, j + nj))],
        out_specs=(pl.BlockSpec((tm, _FFN_TN), lambda j, i: (i, j)), tail_spec, tail_spec),
        scratch_shapes=[pltpu.VMEM((k, 2 * _FFN_TN), BF16), pltpu.VMEM((_TAIL + tm, 2 * _FFN_TN), F32)],
        compiler_params=_cparams("arbitrary", "arbitrary"), name="ffn_up_prompt")(
            h2, w_up, w_up, conv_w, conv_w, conv_b, conv_b)


def _ffn_up_sample_kernel(a_ref, bg_ref, bu_ref, wg_ref, wu_ref, cg_ref, cu_ref, s0g_ref, s1g_ref, s0u_ref, s1u_ref,
                          g_ref, yg_ref, yu_ref):
    a = a_ref[...]
    yg = _dot(a, bg_ref[...].astype(BF16))
    yu = _dot(a, bu_ref[...].astype(BF16))
    ug = wg_ref[0:1, :] * s0g_ref[...] + wg_ref[1:2, :] * s1g_ref[...] + wg_ref[2:3, :] * yg + cg_ref[...]
    uu = wu_ref[0:1, :] * s0u_ref[...] + wu_ref[1:2, :] * s1u_ref[...] + wu_ref[2:3, :] * yu + cu_ref[...]
    g_ref[...] = (_silu(ug) * uu).astype(g_ref.dtype)
    yg_ref[...] = yg
    yu_ref[...] = yu


def _ffn_up_sample(h2, w_up, layer, conv_w, conv_b, s0, s1):
    m, k = h2.shape
    nj = _FFN_NJ
    lo = lambda j: (0, j)
    hi = lambda j: (0, j + nj)
    row = lambda f: pl.BlockSpec((m, _FFN_TN), f)
    ysd = jax.ShapeDtypeStruct((m, FFN_DIM), F32)
    return pl.pallas_call(
        _ffn_up_sample_kernel, out_shape=(jax.ShapeDtypeStruct((m, FFN_DIM), BF16), ysd, ysd), grid=(nj,),
        in_specs=[pl.BlockSpec((m, k), lambda j: (0, 0)),
                  pl.BlockSpec((None, k, _FFN_TN), lambda j: (layer, 0, j)),
                  pl.BlockSpec((None, k, _FFN_TN), lambda j: (layer, 0, j + nj)),
                  pl.BlockSpec((FFN_CONV_W, _FFN_TN), lo), pl.BlockSpec((FFN_CONV_W, _FFN_TN), hi),
                  pl.BlockSpec((1, _FFN_TN), lo), pl.BlockSpec((1, _FFN_TN), hi),
                  row(lo), row(lo), row(hi), row(hi)],
        out_specs=(row(lo), row(lo), row(lo)),
        compiler_params=_cparams("arbitrary"), name="ffn_up_sample")(
            h2, w_up, w_up, conv_w, conv_w, conv_b, conv_b, s0, s1, s0, s1)


_PAGES_PER_STEP = 16


def _kmean_kernel(pt_ref, *refs):
    o_ref = refs[-1]
    page = refs[0].shape[0]
    per_block = MOBA_BLOCK // page
    for n in range(_PAGES_PER_STEP // per_block):
        acc = jnp.sum(refs[n * per_block][...], axis=0)
        for r in range(1, per_block):
            acc = acc + jnp.sum(refs[n * per_block + r][...], axis=0)
        o_ref[n] = acc * (1.0 / MOBA_BLOCK)


def _kmean(cache_k, page_table, layer):
    n_b, n_pages = page_table.shape
    page, n_kv, d = cache_k.shape[2:]
    per_block = MOBA_BLOCK // page
    n_blocks = n_pages // per_block
    steps = n_pages // _PAGES_PER_STEP
    in_specs = [pl.BlockSpec((None, None, page, n_kv, d),
                             functools.partial(lambda b, s, pt, r: (layer, pt[b, s * _PAGES_PER_STEP + r], 0, 0, 0), r=r))
                for r in range(_PAGES_PER_STEP)]
    grid_spec = pltpu.PrefetchScalarGridSpec(
        num_scalar_prefetch=1, grid=(n_b, steps), in_specs=in_specs,
        out_specs=pl.BlockSpec((None, _PAGES_PER_STEP // per_block, n_kv, d), lambda b, s, pt: (b, s, 0, 0)))
    return pl.pallas_call(
        _kmean_kernel, out_shape=jax.ShapeDtypeStruct((n_b, n_blocks, n_kv, d), F32), grid_spec=grid_spec,
        compiler_params=_cparams("arbitrary", "arbitrary"), name="kmean")(page_table, *([cache_k] * _PAGES_PER_STEP))


def _select_kernel(q_ref, k_ref, kmean_ref, wq_ref, wk_ref, qn_ref, kn_ref, idx_ref):
    n_b, n_h, _ = q_ref.shape
    n_blocks = kmean_ref.shape[1]
    q = q_ref[...]
    qn = q * lax.rsqrt(jnp.mean(q * q, axis=-1, keepdims=True) + EPS) * wq_ref[...]
    qn_ref[...] = qn
    k = k_ref[...]
    kn_ref[...] = k * lax.rsqrt(jnp.mean(k * k, axis=-1, keepdims=True) + EPS) * wk_ref[...]
    head = lax.broadcasted_iota(jnp.int32, (n_h, n_blocks), 0)
    lane = lax.broadcasted_iota(jnp.int32, (n_h, n_blocks), 1).astype(F32)
    lane_out = lax.broadcasted_iota(jnp.int32, (n_h, LANE), 1)
    for b in range(n_b):
        gate = jnp.zeros((n_h, n_blocks), F32)
        for kv in range(ATT_KV_HEADS):
            gk = _dot_nt(qn[b], kmean_ref[b, :, kv * HEAD_DIM:(kv + 1) * HEAD_DIM], HIGHEST)
            gate = jnp.where(head // ATT_GROUP == kv, gk, gate)
        out = jnp.zeros((n_h, LANE), F32)
        for s in range(MOBA_TOPK):
            mx = jnp.max(gate, axis=-1, keepdims=True)
            pick = jnp.min(jnp.where(gate == mx, lane, float(n_blocks)), axis=-1, keepdims=True)
            out = jnp.where(lane_out == s, pick, out)
            gate = jnp.where(lane == pick, -jnp.inf, gate)
        idx_ref[b] = out.astype(jnp.int32)


def _select(q3, k3, kmean, wq, wk):
    n_b, n_h, d = q3.shape
    full = lambda shape: pl.BlockSpec(shape, lambda i: (0,) * len(shape))
    return pl.pallas_call(
        _select_kernel,
        out_shape=(jax.ShapeDtypeStruct(q3.shape, F32), jax.ShapeDtypeStruct(k3.shape, F32),
                   jax.ShapeDtypeStruct((n_b, n_h, LANE), jnp.int32)),
        grid=(1,),
        in_specs=[full(q3.shape), full(k3.shape), full(kmean.shape), full((1, 1, d)), full((1, 1, d))],
        out_specs=(full(q3.shape), full(k3.shape), full((n_b, n_h, LANE))),
        compiler_params=_cparams("arbitrary"), name="select")(q3, k3, kmean, wq.reshape(1, 1, d), wk.reshape(1, 1, d))


def _attn_sample_kernel(pt_ref, idx_ref, qn_ref, kn_ref, vn_ref, *refs, n_sel, per_block):
    o_ref = refs[-1]
    pages = refs[:-1]
    n_pg = n_sel * per_block
    k_refs, v_refs = pages[:n_pg], pages[n_pg:]
    h = pl.program_id(1)
    kv = h // ATT_GROUP
    q = qn_ref[pl.ds(h, 1), :] * (HEAD_DIM ** -0.5)
    k_new = kn_ref[pl.ds(kv, 1), :]
    v_new = vn_ref[pl.ds(kv, 1), :]
    s_own = jnp.sum(q * k_new, axis=-1, keepdims=True)
    n_rows = k_refs[0].shape[0]
    mine = lax.broadcasted_iota(jnp.int32, (n_rows, 1), 0) % ATT_KV_HEADS == kv
    logits = [jnp.where(mine, jnp.sum(r[...] * q, axis=-1, keepdims=True), NEG_BIG) for r in k_refs]
    mx = s_own
    for s in logits:
        mx = jnp.maximum(mx, jnp.max(s, axis=0, keepdims=True))
    p_own = jnp.exp(s_own - mx)
    den = p_own
    acc = p_own * v_new
    for s, v_ref in zip(logits, v_refs):
        p = jnp.exp(s - mx)
        den = den + jnp.sum(p, axis=0, keepdims=True)
        acc = acc + jnp.sum(p * v_ref[...], axis=0, keepdims=True)
    o_ref[pl.ds(h, 1), :] = acc / den


def _attn_sample(cache_k, cache_v, page_table, idx, qn, kn, vn, layer):
    n_b, n_h, d = qn.shape
    depth, n_pool, page, n_kv = cache_k.shape[:4]
    per_block = MOBA_BLOCK // page
    n_sel = idx.shape[2]
    cache_k = cache_k.reshape(depth, n_pool, page * n_kv, d)
    cache_v = cache_v.reshape(depth, n_pool, page * n_kv, d)

    def page_spec(s, r):
        def imap(b, h, pt, ix):
            return (layer, pt[b, ix[b, h * n_sel + s] * per_block + r], 0, 0)
        return pl.BlockSpec((None, None, page * n_kv, d), imap)

    pspecs = [page_spec(s, r) for s in range(n_sel) for r in range(per_block)]
    slab = lambda n: pl.BlockSpec((None, n, d), lambda b, h, pt, ix: (b, 0, 0))
    grid_spec = pltpu.PrefetchScalarGridSpec(
        num_scalar_prefetch=2, grid=(n_b, n_h),
        in_specs=[slab(n_h), slab(ATT_KV_HEADS), slab(ATT_KV_HEADS)] + pspecs + pspecs,
        out_specs=slab(n_h))
    n_pg = len(pspecs)
    return pl.pallas_call(
        functools.partial(_attn_sample_kernel, n_sel=n_sel, per_block=per_block),
        out_shape=jax.ShapeDtypeStruct((n_b, n_h, d), F32), grid_spec=grid_spec,
        compiler_params=_cparams("arbitrary", "arbitrary"), name="attn_sample")(
            page_table, idx.reshape(n_b, n_h * n_sel), qn, kn, vn, *([cache_k] * n_pg), *([cache_v] * n_pg))


def _rows_to_cols(x):
    r, n = x.shape
    return jnp.concatenate([x, jnp.zeros((n - r, n), x.dtype)], axis=0).T


def _mix_sample_kernel(dx_ref, dbuf_ref, dw_ref, dbeta_ref, dg_ref, dz_ref, dnw_ref, dst_ref,
                       sx_ref, sxbuf_ref, sxw_ref, sxb_ref, bc_ref, bcbuf_ref, bcw_ref, bcb_ref,
                       sdt_ref, sa_ref, sz_ref, sd_ref, snw_ref, sst_ref,
                       dno_ref, dnst_ref, so_ref, sso_ref, y_s):
    x = dx_ref[...]
    conv = x * dw_ref[CONV_W - 1]
    for i in range(CONV_W - 1):
        conv = conv + dbuf_ref[i] * dw_ref[i]
    act = _silu(conv)
    nh = DN_HEADS
    q, k, v = act[0:nh], act[nh:2 * nh], act[2 * nh:3 * nh]
    q = q * lax.rsqrt(jnp.sum(q * q, axis=-1, keepdims=True) + EPS) * (HEAD_DIM ** -0.5)
    k = k * lax.rsqrt(jnp.sum(k * k, axis=-1, keepdims=True) + EPS)
    beta = dbeta_ref[...]
    eg = jnp.exp(dg_ref[...])
    qk = jnp.sum(q * k, axis=-1, keepdims=True)
    qt, kt = _rows_to_cols(q), _rows_to_cols(k)
    o_rows = []
    for h in range(nh):
        s0 = dst_ref[h]
        kcol, qcol = kt[:, h:h + 1], qt[:, h:h + 1]
        ks = jnp.sum(kcol * s0, axis=0, keepdims=True)
        qs = jnp.sum(qcol * s0, axis=0, keepdims=True)
        b_h, e_h = beta[h:h + 1, :], eg[h:h + 1, :]
        v_new = v[h:h + 1, :] * b_h - (b_h * e_h) * ks
        o_rows.append(e_h * qs + qk[h:h + 1, :] * v_new)
        dnst_ref[h] = s0 * e_h + kcol * v_new
    o = jnp.concatenate(o_rows, axis=0)
    on = o * lax.rsqrt(jnp.mean(o * o, axis=-1, keepdims=True) + EPS) * dnw_ref[...]
    dno_ref[...] = on * _silu(dz_ref[...])
    xs = sx_ref[...] * sxw_ref[CONV_W - 1] + sxb_ref[...]
    bc = bc_ref[...] * bcw_ref[CONV_W - 1] + bcb_ref[...]
    for i in range(CONV_W - 1):
        xs = xs + sxbuf_ref[i] * sxw_ref[i]
        bc = bc + bcbuf_ref[i] * bcw_ref[i]
    xs = _silu(xs)
    bc = _silu(bc)
    bct = _rows_to_cols(bc)
    cb = jnp.sum(bc[0:SSD_GROUPS] * bc[SSD_GROUPS:2 * SSD_GROUPS], axis=-1, keepdims=True)
    dt = sdt_ref[...]
    ea = jnp.exp(sa_ref[...])
    for h in range(SSD_HEADS):
        grp = h // _GH
        s0 = sst_ref[h]
        bcol, ccol = bct[:, grp:grp + 1], bct[:, SSD_GROUPS + grp:SSD_GROUPS + grp + 1]
        xv = xs[h:h + 1, :] * dt[h:h + 1, :]
        e_h = ea[h:h + 1, :]
        y_s[h:h + 1, :] = cb[grp:grp + 1, :] * xv + e_h * jnp.sum(ccol * s0, axis=0, keepdims=True)
        sso_ref[h] = s0 * e_h + bcol * xv
    y = (y_s[...] + sd_ref[...] * xs) * _silu(sz_ref[...])
    norm_rows = []
    for grp in range(SSD_GROUPS):
        yg = y[grp * _GH:(grp + 1) * _GH]
        ms = jnp.sum(jnp.sum(yg * yg, axis=-1, keepdims=True), axis=0, keepdims=True) * (1.0 / _GW)
        norm_rows.append(yg * lax.rsqrt(ms + EPS))
    so_ref[...] = jnp.concatenate(norm_rows, axis=0) * snw_ref[...]


def _mix_sample(args, n_b):
    def per_b(shape):
        nd = len(shape) - 1
        return pl.BlockSpec((None,) + tuple(shape[1:]), lambda b: (b,) + (0,) * nd)

    def shared(shape):
        nd = len(shape)
        return pl.BlockSpec(tuple(shape), lambda b: (0,) * nd)

    names_per_b = {"dx", "dbuf", "dbeta", "dg", "dz", "dst", "sx", "sxbuf", "bc", "bcbuf", "sdt", "sa", "sz", "sst"}
    order = ["dx", "dbuf", "dw", "dbeta", "dg", "dz", "dnw", "dst", "sx", "sxbuf", "sxw", "sxb", "bc", "bcbuf", "bcw",
             "bcb", "sdt", "sa", "sz", "sd", "snw", "sst"]
    in_specs = [per_b(args[n].shape) if n in names_per_b else shared(args[n].shape) for n in order]
    outs = (jax.ShapeDtypeStruct((n_b, DN_HEADS, HEAD_DIM), F32),
            jax.ShapeDtypeStruct((n_b, DN_HEADS, HEAD_DIM, HEAD_DIM), F32),
            jax.ShapeDtypeStruct((n_b, SSD_HEADS, SSD_HEAD_DIM), F32),
            jax.ShapeDtypeStruct((n_b, SSD_HEADS, SSD_STATE, SSD_HEAD_DIM), F32))
    return pl.pallas_call(
        _mix_sample_kernel, out_shape=outs, grid=(n_b,), in_specs=in_specs,
        out_specs=tuple(per_b(o.shape) for o in outs),
        scratch_shapes=[pltpu.VMEM((SSD_HEADS, SSD_HEAD_DIM), F32)],
        compiler_params=_cparams("arbitrary"), name="mix_sample")(*[args[n] for n in order])


def _split_w_in(w_in):
    depth, k, _ = w_in.shape
    wt = jnp.swapaxes(w_in, 1, 2)
    wt_b = wt[:, ZB_SRC:ZB_SRC + ZB_COLS]
    wt_s = jnp.concatenate([wt[:, ZS_SRC[0]:ZS_SRC[0] + 16], wt[:, ZS_SRC[1]:ZS_SRC[1] + 16],
                            jnp.zeros((depth, LANE - 32, k), w_in.dtype)], axis=1)
    return wt, wt_b, wt_s


def _in_proj_all(h, lw, layer, tm, suffix=""):
    za = _in_proj(h, lw["wt_a"], layer, ZA_COLS, tm=tm, tn=512, name="in_proj_a" + suffix)
    zb = _in_proj(h, lw["wt_b"], layer, ZB_COLS, tm=tm, tn=512, name="in_proj_b" + suffix)
    zs = _in_proj(h, lw["wt_s"], layer, LANE, tm=tm, tn=LANE, name="in_proj_s" + suffix)
    return za, zb, zs


def _lane_vec(dn_vals, ssd_vals):
    v = jnp.zeros((LANE,), F32)
    v = v.at[DNG_LANE:DNG_LANE + DN_HEADS].set(dn_vals.astype(F32))
    v = v.at[SSD_LANE:SSD_LANE + SSD_HEADS].set(ssd_vals.astype(F32))
    return v.reshape(1, LANE)


def _ffn_down(g, w_down, layer, x, tm):
    half = FFN_DIM // 2
    for part in range(2):
        x = _matmul([(g, half, part, part)], w_down, layer, tm=tm, tn=256, res=x, name="ffn_down")
    return x


def _prompt_layer(x, lw, layer, n_seq, t_len):
    tm = 1024
    h = _rmsnorm_cast(x, lw["norm1_w"], 512)
    za, zb, zs = _in_proj_all(h, lw, layer, tm)
    p1, p2 = _aux(zs, lw["bias_vec"], lw["alog_vec"], n_seq, t_len, CHUNK)
    o_att, k_norm = _attn_prompt(za, lw["attn_q_norm_w"], lw["attn_k_norm_w"], n_seq, t_len)
    o_dn, dn_state = _gdn_prompt(za, p1, p2, lw["dn_conv_w"], lw["dn_norm_w"], n_seq, t_len)
    o_ssd, ssd_state = _ssd_prompt(zb, p1, p2, lw["ssd_conv_w"], lw["ssd_conv_b"].reshape(1, -1), lw["ssd_d_vec"],
                                   lw["ssd_norm_w"], n_seq, t_len)
    x = _matmul([(o_att, 2048, 0, 0), (o_dn, 1024, 0, 2), (o_ssd, 1024, 0, 3)], lw["w_out"], layer, tm=tm, tn=512,
                res=x, name="out_proj")
    h2 = _rmsnorm_cast(x, lw["norm2_w"], 512)
    g, tail_g, tail_u = _ffn_up_prompt(h2, lw["ffn_w_up"], layer, lw["ffn_conv_w"], lw["ffn_conv_b"].reshape(1, -1),
                                       n_seq, t_len, tm)
    x = _ffn_down(g, lw["ffn_w_down"], layer, x, tm)
    za3 = za.reshape(n_seq, t_len, ZA_COLS)
    zb3 = zb.reshape(n_seq, t_len, ZB_COLS)
    tps = t_len // tm
    new = (k_norm.reshape(n_seq, t_len, ATT_KV_HEADS, HEAD_DIM),
           za3[:, :, V_OFF:V_OFF + 512].reshape(n_seq, t_len, ATT_KV_HEADS, HEAD_DIM),
           za3[:, t_len - (CONV_W - 1):, DNQKV_OFF:DNQKV_OFF + DN_CONV_CH],
           dn_state,
           zb3[:, t_len - (CONV_W - 1):, SSDX_OFF:SSDX_OFF + SSD_CONV_CH],
           ssd_state,
           jnp.concatenate([tail_g[tps - 1::tps, _TAIL - 2:], tail_u[tps - 1::tps, _TAIL - 2:]], axis=-1))
    return x, new


_SAMPLE_ROWS = 16


def _sample_layer(x, lw, layer, cache_k, cache_v, page_table, dn_conv_buf, dn_state, ssd_conv_buf, ssd_state,
                  ffn_conv_buf):
    n_b = page_table.shape[0]
    rows = x.shape[0]
    h = _rmsnorm_cast(x, lw["norm1_w"], rows)
    za, zb2, zs = _in_proj_all(h, lw, layer, rows, "_smp")
    p1, p2 = _aux(zs, lw["bias_vec"], lw["alog_vec"], 1, rows, 1)
    zb = za[:n_b]
    zbs = zb2[:n_b]
    kmean = _kmean(cache_k, page_table, layer)
    kmean = kmean.reshape(n_b, kmean.shape[1], ATT_KV_HEADS * HEAD_DIM)
    q3 = zb[:, Q_OFF:Q_OFF + 2048].reshape(n_b, ATT_HEADS, HEAD_DIM)
    k3 = zb[:, K_OFF:K_OFF + 512].reshape(n_b, ATT_KV_HEADS, HEAD_DIM)
    v3 = zb[:, V_OFF:V_OFF + 512].reshape(n_b, ATT_KV_HEADS, HEAD_DIM)
    qn, kn, idx = _select(q3, k3, kmean, lw["attn_q_norm_w"], lw["attn_k_norm_w"])
    o_att = _attn_sample(cache_k, cache_v, page_table, idx[:, :, :MOBA_TOPK], qn, kn, v3, layer)
    dn_x = zb[:, DNQKV_OFF:DNQKV_OFF + DN_CONV_CH]
    ssd_x = zbs[:, SSDX_OFF:SSDX_OFF + SSD_CONV_CH]
    nh3 = 3 * DN_HEADS
    ng2 = 2 * SSD_GROUPS
    args = {
        "dx": dn_x.reshape(n_b, nh3, HEAD_DIM),
        "dbuf": dn_conv_buf.reshape(n_b, CONV_W - 1, nh3, HEAD_DIM),
        "dw": lw["dn_conv_w"].reshape(CONV_W, nh3, HEAD_DIM),
        "dbeta": p1[:n_b, BETA_LANE:BETA_LANE + DN_HEADS].reshape(n_b, DN_HEADS, 1),
        "dg": p2[:n_b, DNG_LANE:DNG_LANE + DN_HEADS].reshape(n_b, DN_HEADS, 1),
        "dz": zb[:, DNZ_OFF:DNZ_OFF + DN_WIDTH].reshape(n_b, DN_HEADS, HEAD_DIM),
        "dnw": lw["dn_norm_w"].reshape(1, HEAD_DIM),
        "dst": dn_state,
        "sx": ssd_x[:, :SSD_INNER].reshape(n_b, SSD_HEADS, SSD_HEAD_DIM),
        "sxbuf": ssd_conv_buf[:, :, :SSD_INNER].reshape(n_b, CONV_W - 1, SSD_HEADS, SSD_HEAD_DIM),
        "sxw": lw["ssd_conv_w"][:, :SSD_INNER].reshape(CONV_W, SSD_HEADS, SSD_HEAD_DIM),
        "sxb": lw["ssd_conv_b"][:SSD_INNER].reshape(SSD_HEADS, SSD_HEAD_DIM),
        "bc": ssd_x[:, SSD_INNER:].reshape(n_b, ng2, SSD_STATE),
        "bcbuf": ssd_conv_buf[:, :, SSD_INNER:].reshape(n_b, CONV_W - 1, ng2, SSD_STATE),
        "bcw": lw["ssd_conv_w"][:, SSD_INNER:].reshape(CONV_W, ng2, SSD_STATE),
        "bcb": lw["ssd_conv_b"][SSD_INNER:].reshape(ng2, SSD_STATE),
        "sdt": p1[:n_b, SSD_LANE:SSD_LANE + SSD_HEADS].reshape(n_b, SSD_HEADS, 1),
        "sa": p2[:n_b, SSD_LANE:SSD_LANE + SSD_HEADS].reshape(n_b, SSD_HEADS, 1),
        "sz": zbs[:, SSDZ_OFF:SSDZ_OFF + SSD_INNER].reshape(n_b, SSD_HEADS, SSD_HEAD_DIM),
        "sd": lw["ssd_D"].astype(F32).reshape(SSD_HEADS, 1),
        "snw": lw["ssd_norm_w"].reshape(SSD_HEADS, SSD_HEAD_DIM),
        "sst": ssd_state,
    }
    o_dn, dn_state_new, o_ssd, ssd_state_new = _mix_sample(args, n_b)
    pad = lambda a: jnp.pad(a.reshape(n_b, -1), ((0, rows - n_b), (0, 0))).astype(BF16)
    x = _matmul([(pad(o_att), 2048, 0, 0), (pad(o_dn), 1024, 0, 2), (pad(o_ssd), 1024, 0, 3)], lw["w_out"], layer,
                tm=rows, tn=512, res=x, name="out_proj_s")
    h2 = _rmsnorm_cast(x, lw["norm2_w"], rows)
    padf = lambda a: jnp.pad(a, ((0, rows - n_b), (0, 0)))
    g, y_g, y_u = _ffn_up_sample(h2, lw["ffn_w_up"], layer, lw["ffn_conv_w"], lw["ffn_conv_b"].reshape(1, -1),
                                 padf(ffn_conv_buf[:, 0]), padf(ffn_conv_buf[:, 1]))
    x = _ffn_down(g, lw["ffn_w_down"], layer, x, rows)
    y_new = jnp.concatenate([y_g[:n_b], y_u[:n_b]], axis=-1)
    new = (kn.reshape(n_b, 1, ATT_KV_HEADS, HEAD_DIM), v3.reshape(n_b, 1, ATT_KV_HEADS, HEAD_DIM),
           jnp.concatenate([dn_conv_buf[:, 1:], dn_x[:, None]], axis=1), dn_state_new,
           jnp.concatenate([ssd_conv_buf[:, 1:], ssd_x[:, None]], axis=1), ssd_state_new,
           jnp.stack([ffn_conv_buf[:, 1], y_new], axis=1))
    return x, new


def kernel(x_prompt, x_sample, cache_k, cache_v, page_table, state_dn_conv, state_dn, state_ssd_conv, state_ssd, state_ffn_conv, norm1_w, w_in, attn_q_norm_w, attn_k_norm_w, dn_conv_w, dn_A_log, dn_dt_bias, dn_norm_w, ssd_conv_w, ssd_conv_b, ssd_dt_bias, ssd_A_log, ssd_D, ssd_norm_w, w_out, norm2_w, ffn_w_up, ffn_conv_w, ffn_conv_b, ffn_w_down):
    depth = w_in.shape[0]
    n_seq, t_len, d_model = x_prompt.shape
    n_b = x_sample.shape[0]
    yp = x_prompt.reshape(n_seq * t_len, d_model)
    ys = jnp.pad(x_sample.reshape(n_b, d_model), ((0, _SAMPLE_ROWS - n_b), (0, 0)))
    wt_a, wt_b, wt_s = _split_w_in(w_in)
    outs_p, outs_s = [], []
    for l in range(depth):
        lw = {"norm1_w": norm1_w[l], "wt_a": wt_a, "wt_b": wt_b, "wt_s": wt_s, "attn_q_norm_w": attn_q_norm_w[l],
              "attn_k_norm_w": attn_k_norm_w[l], "dn_conv_w": dn_conv_w[l], "dn_norm_w": dn_norm_w[l],
              "ssd_conv_w": ssd_conv_w[l], "ssd_conv_b": ssd_conv_b[l], "ssd_D": ssd_D[l],
              "ssd_norm_w": ssd_norm_w[l], "w_out": w_out, "norm2_w": norm2_w[l], "ffn_w_up": ffn_w_up,
              "ffn_conv_w": ffn_conv_w[l], "ffn_conv_b": ffn_conv_b[l], "ffn_w_down": ffn_w_down,
              "bias_vec": _lane_vec(dn_dt_bias[l], ssd_dt_bias[l]),
              "alog_vec": _lane_vec(dn_A_log[l], ssd_A_log[l]),
              "ssd_d_vec": jnp.repeat(ssd_D[l].astype(F32), SSD_HEAD_DIM).reshape(1, SSD_INNER)}
        yp, new_p = _prompt_layer(yp, lw, l, n_seq, t_len)
        ys, new_s = _sample_layer(ys, lw, l, cache_k, cache_v, page_table, state_dn_conv[l], state_dn[l],
                                  state_ssd_conv[l], state_ssd[l], state_ffn_conv[l])
        outs_p.append(new_p)
        outs_s.append(new_s)
    st = lambda outs, i: jnp.stack([o[i] for o in outs])
    return (yp.reshape(n_seq, t_len, d_model), ys[:n_b].reshape(n_b, 1, d_model),
            st(outs_p, 0), st(outs_p, 1), st(outs_s, 0), st(outs_s, 1),
            st(outs_p, 2), st(outs_s, 2), st(outs_p, 3), st(outs_s, 3),
            st(outs_p, 4), st(outs_s, 4), st(outs_p, 5), st(outs_s, 5),
            st(outs_p, 6), st(outs_s, 6))
```

```python
import functools
import math

import jax
import jax.numpy as jnp
from jax import lax
from jax.experimental import pallas as pl
from jax.experimental.pallas import tpu as pltpu

F32 = jnp.float32
BF16 = jnp.bfloat16
HIGHEST = lax.Precision.HIGHEST

D_MODEL = 4096
HEAD_DIM = 128
ATT_HEADS = 16
ATT_KV_HEADS = 4
ATT_GROUP = 4
MOBA_BLOCK = 256
MOBA_TOPK = 3
DN_HEADS = 8
DN_WIDTH = 1024
DN_CONV_CH = 3072
SSD_INNER = 1024
SSD_HEAD_DIM = 64
SSD_HEADS = 16
SSD_GROUPS = 2
SSD_STATE = 128
SSD_CONV_CH = 1536
CONV_W = 4
CHUNK = 64
FFN_DIM = 11008
FFN_CONV_W = 3
EPS = 1e-6

Q_OFF, K_OFF, V_OFF = 0, 2048, 2560
DNQKV_OFF, DNZ_OFF = 3072, 6144
ZA_COLS = 7168
SSDX_OFF, SSDZ_OFF = 0, 1536
ZB_SRC, ZB_COLS = 7184, 2560
ZS_SRC = (7168, 9744)
LANE = 128
BETA_LANE, DNG_LANE, SSD_LANE = 0, 8, 16

VMEM_LIMIT_BYTES = 56 * 1024 * 1024
NEG_BIG = -1e30


def _cparams(*sem):
    return pltpu.CompilerParams(dimension_semantics=sem, vmem_limit_bytes=VMEM_LIMIT_BYTES)


def _silu(x):
    return x / (1.0 + jnp.exp(-x))


def _sigmoid(x):
    return 1.0 / (1.0 + jnp.exp(-x))


def _softplus(x):
    return jnp.maximum(x, 0.0) + jnp.log1p(jnp.exp(-jnp.abs(x)))


def _dot(a, b, precision=None):
    return jnp.dot(a, b, preferred_element_type=F32, precision=precision)


def _dot_nt(a, b, precision=None):
    return lax.dot_general(a, b, (((1,), (1,)), ((), ())), preferred_element_type=F32, precision=precision)


def _shift_rows(x, s):
    y = pltpu.roll(x, s, axis=0)
    head_rows = lax.broadcasted_iota(jnp.int32, (8, x.shape[1]), 0)
    return jnp.concatenate([jnp.where(head_rows < s, 0.0, y[0:8]), y[8:]], axis=0)


def _causal_conv_rows(x, w_ref, width):
    y = x * w_ref[width - 1:width, :]
    for i in range(width - 1):
        y = y + _shift_rows(x, width - 1 - i) * w_ref[i:i + 1, :]
    return y


def _lane_col(x, lane):
    lanes = lax.broadcasted_iota(jnp.int32, x.shape, 1)
    return jnp.sum(jnp.where(lanes == lane, x, 0.0), axis=-1, keepdims=True)


def _decay_matrix(gc_col, c):
    ii = lax.broadcasted_iota(jnp.int32, (c, c), 0)
    jj = lax.broadcasted_iota(jnp.int32, (c, c), 1)
    gcb = jnp.broadcast_to(gc_col, (c, c))
    gc_row = jnp.sum(jnp.where(ii == jj, gcb, 0.0), axis=0, keepdims=True)
    low = ii >= jj
    gam = jnp.where(low, jnp.exp(jnp.where(low, gcb - gc_row, 0.0)), 0.0)
    return gam, ii, jj


def _rmsnorm_kernel(x_ref, w_ref, o_ref):
    x = x_ref[...]
    ms = jnp.mean(x * x, axis=-1, keepdims=True)
    o_ref[...] = (x * lax.rsqrt(ms + EPS) * w_ref[...]).astype(o_ref.dtype)


def _rmsnorm_cast(x, w, tm):
    m, d = x.shape
    return pl.pallas_call(
        _rmsnorm_kernel, out_shape=jax.ShapeDtypeStruct((m, d), BF16), grid=(m // tm,),
        in_specs=[pl.BlockSpec((tm, d), lambda i: (i, 0)), pl.BlockSpec((1, d), lambda i: (0, 0))],
        out_specs=pl.BlockSpec((tm, d), lambda i: (i, 0)),
        compiler_params=_cparams("arbitrary"), name="rmsnorm")(x, w.reshape(1, d))


def _matmul_kernel(*refs, n_parts, has_res):
    a_refs = refs[:n_parts]
    b_refs = refs[n_parts:2 * n_parts]
    o_ref = refs[-1]
    acc = None
    for a_ref, b_ref in zip(a_refs, b_refs):
        d = _dot(a_ref[...], b_ref[...].astype(BF16))
        acc = d if acc is None else acc + d
    if has_res:
        acc = acc + refs[2 * n_parts][...]
    o_ref[...] = acc.astype(o_ref.dtype)


def _matmul(a_parts, b, layer, *, tm, tn, res=None, out_dtype=F32, name="matmul"):
    m = a_parts[0][0].shape[0]
    n = b.shape[2]
    in_specs, args = [], []
    for arr, kp, cb, rb in a_parts:
        in_specs.append(pl.BlockSpec((tm, kp), functools.partial(lambda i, j, cb: (i, cb), cb=cb)))
        args.append(arr)
    for arr, kp, cb, rb in a_parts:
        in_specs.append(pl.BlockSpec((None, kp, tn), functools.partial(lambda i, j, rb: (layer, rb, j), rb=rb)))
        args.append(b)
    if res is not None:
        in_specs.append(pl.BlockSpec((tm, tn), lambda i, j: (i, j)))
        args.append(res)
    return pl.pallas_call(
        functools.partial(_matmul_kernel, n_parts=len(a_parts), has_res=res is not None),
        out_shape=jax.ShapeDtypeStruct((m, n), out_dtype), grid=(m // tm, n // tn),
        in_specs=in_specs, out_specs=pl.BlockSpec((tm, tn), lambda i, j: (i, j)),
        compiler_params=_cparams("arbitrary", "arbitrary"), name=name)(*args)


def _in_proj_kernel(a_ref, w_ref, o_ref, b_s):
    @pl.when(pl.program_id(1) == 0)
    def _():
        b_s[...] = w_ref[...].astype(BF16)

    o_ref[...] = _dot_nt(a_ref[...], b_s[...])


def _in_proj(h, wt, layer, n, *, tm, tn, name):
    m, k = h.shape
    return pl.pallas_call(
        _in_proj_kernel, out_shape=jax.ShapeDtypeStruct((m, n), F32), grid=(n // tn, m // tm),
        in_specs=[pl.BlockSpec((tm, k), lambda j, i: (i, 0)), pl.BlockSpec((None, tn, k), lambda j, i: (layer, j, 0))],
        out_specs=pl.BlockSpec((tm, tn), lambda j, i: (i, j)),
        scratch_shapes=[pltpu.VMEM((tn, k), BF16)],
        compiler_params=_cparams("arbitrary", "arbitrary"), name=name)(h, wt)


def _aux_kernel(s_ref, bias_ref, alog_ref, p1_ref, p2_ref, *, t_len, chunk):
    x = s_ref[...]
    lanes = lax.broadcasted_iota(jnp.int32, x.shape, 1)
    sp = _softplus(x + bias_ref[...])
    p1_ref[...] = jnp.where(lanes < DNG_LANE, _sigmoid(x), sp)
    g = -jnp.exp(alog_ref[...]) * sp
    if chunk == 1:
        p2_ref[...] = g
    else:
        ii = lax.broadcasted_iota(jnp.int32, (chunk, chunk), 0)
        jj = lax.broadcasted_iota(jnp.int32, (chunk, chunk), 1)
        tril = jnp.where(ii >= jj, 1.0, 0.0).astype(F32)
        for c in range(t_len // chunk):
            p2_ref[c * chunk:(c + 1) * chunk, :] = _dot(tril, g[c * chunk:(c + 1) * chunk, :], HIGHEST)


def _aux(z, bias_vec, alog_vec, n_seq, t_len, chunk):
    m = n_seq * t_len
    blk = pl.BlockSpec((t_len, LANE), lambda b: (b, 0))
    vec = pl.BlockSpec((1, LANE), lambda b: (0, 0))
    out = pl.BlockSpec((t_len, LANE), lambda b: (b, 0))
    return pl.pallas_call(
        functools.partial(_aux_kernel, t_len=t_len, chunk=chunk),
        out_shape=(jax.ShapeDtypeStruct((m, LANE), F32), jax.ShapeDtypeStruct((m, LANE), F32)),
        grid=(n_seq,), in_specs=[blk, vec, vec], out_specs=(out, out),
        compiler_params=_cparams("arbitrary"), name="aux")(z, bias_vec, alog_vec)


_ONES_ROWS = 16


def _attn_prompt_kernel(q_ref, k_ref, v_ref, wq_ref, wk_ref, o_ref, kn_ref, kb_ref, vt_ref, kmean_ref,
                        *, t_len):
    nb = t_len // MOBA_BLOCK
    g = pl.program_id(2)

    @pl.when(g == 0)
    def _():
        k = k_ref[...]
        kn = k * lax.rsqrt(jnp.mean(k * k, axis=-1, keepdims=True) + EPS) * wk_ref[...]
        kn_ref[...] = kn
        kb_ref[...] = kn.astype(BF16)
        for n in range(nb):
            kmean_ref[n:n + 1, :] = jnp.mean(kn[n * MOBA_BLOCK:(n + 1) * MOBA_BLOCK, :], axis=0, keepdims=True)
        vt_ref[0:HEAD_DIM, :] = v_ref[...].T.astype(BF16)
        vt_ref[HEAD_DIM:HEAD_DIM + _ONES_ROWS, :] = jnp.ones((_ONES_ROWS, t_len), BF16)

    q = q_ref[...]
    qn = q * lax.rsqrt(jnp.mean(q * q, axis=-1, keepdims=True) + EPS) * wq_ref[...]
    gate = _dot_nt(kmean_ref[...], qn, HIGHEST)
    blk = lax.broadcasted_iota(jnp.int32, (nb, t_len), 0)
    own = lax.broadcasted_iota(jnp.int32, (nb, t_len), 1) // MOBA_BLOCK
    valid = blk < own
    gm = jnp.where(valid, gate, -jnp.inf)
    cnt = jnp.zeros((nb, t_len), F32)
    for m in range(nb):
        row = gm[m:m + 1, :]
        beats = jnp.where(row > gm, 1.0, jnp.where(row == gm, jnp.where(blk > m, 1.0, 0.0), 0.0))
        cnt = cnt + beats
    bias = jnp.where(valid, jnp.where(cnt < MOBA_TOPK, 0.0, NEG_BIG), NEG_BIG)
    qs = (qn * (HEAD_DIM ** -0.5 * math.log2(math.e))).astype(BF16)
    kk = lax.broadcasted_iota(jnp.int32, (MOBA_BLOCK, MOBA_BLOCK), 0)
    qq = lax.broadcasted_iota(jnp.int32, (MOBA_BLOCK, MOBA_BLOCK), 1)
    causal = jnp.where(kk <= qq, 0.0, NEG_BIG)
    for qi in range(nb):
        n_keys = (qi + 1) * MOBA_BLOCK
        cols = slice(qi * MOBA_BLOCK, (qi + 1) * MOBA_BLOCK)
        st = _dot_nt(kb_ref[0:n_keys, :], qs[cols, :])
        blocks = [st[n * MOBA_BLOCK:(n + 1) * MOBA_BLOCK, :] + bias[n:n + 1, cols] for n in range(qi)]
        blocks.append(st[qi * MOBA_BLOCK:n_keys, :] + causal)
        mx = jnp.max(blocks[0], axis=0, keepdims=True)
        for blk_s in blocks[1:]:
            mx = jnp.maximum(mx, jnp.max(blk_s, axis=0, keepdims=True))
        p = [jnp.exp2(blk_s - mx).astype(BF16) for blk_s in blocks]
        p = jnp.concatenate(p, axis=0) if qi else p[0]
        ot = _dot(vt_ref[:, 0:n_keys], p)
        o = ot[0:HEAD_DIM, :] / ot[HEAD_DIM:HEAD_DIM + 1, :]
        o_ref[cols, :] = o.T.astype(o_ref.dtype)


def _attn_prompt(z, wq, wk, n_seq, t_len):
    m = n_seq * t_len
    qblk = pl.BlockSpec((t_len, HEAD_DIM), lambda b, k, g: (b, Q_OFF // HEAD_DIM + k * ATT_GROUP + g))
    kblk = pl.BlockSpec((t_len, HEAD_DIM), lambda b, k, g: (b, K_OFF // HEAD_DIM + k))
    vblk = pl.BlockSpec((t_len, HEAD_DIM), lambda b, k, g: (b, V_OFF // HEAD_DIM + k))
    wspec = pl.BlockSpec((1, HEAD_DIM), lambda b, k, g: (0, 0))
    return pl.pallas_call(
        functools.partial(_attn_prompt_kernel, t_len=t_len),
        out_shape=(jax.ShapeDtypeStruct((m, ATT_HEADS * HEAD_DIM), BF16),
                   jax.ShapeDtypeStruct((m, ATT_KV_HEADS * HEAD_DIM), F32)),
        grid=(n_seq, ATT_KV_HEADS, ATT_GROUP),
        in_specs=[qblk, kblk, vblk, wspec, wspec],
        out_specs=(pl.BlockSpec((t_len, HEAD_DIM), lambda b, k, g: (b, k * ATT_GROUP + g)),
                   pl.BlockSpec((t_len, HEAD_DIM), lambda b, k, g: (b, k))),
        scratch_shapes=[pltpu.VMEM((t_len, HEAD_DIM), BF16), pltpu.VMEM((HEAD_DIM + _ONES_ROWS, t_len), BF16),
                        pltpu.VMEM((t_len // MOBA_BLOCK, HEAD_DIM), F32)],
        compiler_params=_cparams("arbitrary", "arbitrary", "arbitrary"), name="attn_prompt")(
            z, z, z, wq.reshape(1, HEAD_DIM), wk.reshape(1, HEAD_DIM))


def _tri_inverse(a, c):
    ii = lax.broadcasted_iota(jnp.int32, (c, c), 0)
    jj = lax.broadcasted_iota(jnp.int32, (c, c), 1)
    eye = jnp.where(ii == jj, 1.0, 0.0).astype(F32)
    p = -a
    inv = eye + p
    steps = int(math.log2(c)) - 1
    for _ in range(steps):
        p = _dot(p, p)
        inv = inv + _dot(inv, p)
    return inv


_GDN_UNROLL = 8
_GDN_HEADS_PER_STEP = 2


def _gdn_prompt_kernel(zq_ref, zk_ref, zv_ref, wq_ref, wk_ref, wv_ref, p1_ref, p2_ref, zg_ref, nw_ref,
                       o_ref, s_ref, q_s, k_s, v_s, beta_s, gc_s, qp_s, op_s, mn_s, nn_s, *, t_len):
    hp = pl.program_id(1)
    c = CHUNK
    d = HEAD_DIM

    def l2n(x):
        return x * lax.rsqrt(jnp.sum(x * x, axis=-1, keepdims=True) + EPS)

    ii = lax.broadcasted_iota(jnp.int32, (c, c), 0)
    jj = lax.broadcasted_iota(jnp.int32, (c, c), 1)
    eye = jnp.where(ii == jj, 1.0, 0.0).astype(F32)

    for hh in range(_GDN_HEADS_PER_STEP):
        lanes = slice(hh * d, (hh + 1) * d)
        h = hp * _GDN_HEADS_PER_STEP + hh
        q_s[hh] = l2n(_silu(_causal_conv_rows(zq_ref[:, lanes], wq_ref.at[:, lanes], CONV_W))) * (HEAD_DIM ** -0.5)
        k_s[hh] = l2n(_silu(_causal_conv_rows(zk_ref[:, lanes], wk_ref.at[:, lanes], CONV_W)))
        v_s[hh] = _silu(_causal_conv_rows(zv_ref[:, lanes], wv_ref.at[:, lanes], CONV_W))
        beta_s[hh] = jnp.broadcast_to(_lane_col(p1_ref[...], BETA_LANE + h), (t_len, HEAD_DIM))
        gc_s[hh] = jnp.broadcast_to(_lane_col(p2_ref[...], DNG_LANE + h), (t_len, HEAD_DIM))
        lax.fori_loop(0, t_len // (c * _GDN_UNROLL),
                      functools.partial(_gdn_prepare_group, refs=(q_s.at[hh], k_s.at[hh], v_s.at[hh], beta_s.at[hh],
                                                                  gc_s.at[hh], qp_s.at[hh], op_s.at[hh], mn_s.at[hh],
                                                                  nn_s.at[hh]), consts=(ii, jj, eye)), 0)

    def chunk_step(ci, states):
        r = pl.ds(pl.multiple_of(ci * c, c), c)
        rd = pl.ds(pl.multiple_of(ci * d, d), d)
        ms = [_dot(jnp.concatenate([mn_s[hh, rd, :], qp_s[hh, r, :]], axis=0), states[hh])
              for hh in range(_GDN_HEADS_PER_STEP)]
        new = []
        for hh in range(_GDN_HEADS_PER_STEP):
            op_s[hh, r, :] = ms[hh][d:d + c] + op_s[hh, r, :]
            e = jnp.exp(gc_s[hh, pl.ds(ci * c + c - 1, 1), :])
            new.append(states[hh] * e[:, 0:1] + ms[hh][0:d] + nn_s[hh, rd, :])
        return tuple(new)

    zero = jnp.zeros((HEAD_DIM, HEAD_DIM), F32)
    s_fin = lax.fori_loop(0, t_len // c, chunk_step, (zero,) * _GDN_HEADS_PER_STEP)
    for hh in range(_GDN_HEADS_PER_STEP):
        lanes = slice(hh * d, (hh + 1) * d)
        s_ref[hh] = s_fin[hh]
        o = op_s[hh]
        on = o * lax.rsqrt(jnp.mean(o * o, axis=-1, keepdims=True) + EPS) * nw_ref[...]
        o_ref[:, lanes] = (on * _silu(zg_ref[:, lanes])).astype(o_ref.dtype)


def _gdn_prepare_group(gi, carry, *, refs, consts):
    q_s, k_s, v_s, beta_s, gc_s, qp_s, op_s, mn_s, nn_s = refs
    ii, jj, eye = consts
    c = CHUNK
    d = HEAD_DIM
    n = _GDN_UNROLL
    chunks = [gi * n + u for u in range(n)]
    rows = [pl.ds(pl.multiple_of(ci * c, c), c) for ci in chunks]
    q = [q_s[r, :] for r in rows]
    k = [k_s[r, :] for r in rows]
    gcb = [gc_s[r, :] for r in rows]
    kb = [k[u] * beta_s[rows[u], :] for u in range(n)]
    gam = [_decay_matrix(gcb[u][:, 0:1], c)[0] for u in range(n)]
    kq = [_dot_nt(jnp.concatenate([kb[u], q[u]], axis=0), k[u]) for u in range(n)]
    attn = [kq[u][c:2 * c] * gam[u] for u in range(n)]
    p = [-jnp.where(ii > jj, kq[u][0:c] * gam[u], 0.0) for u in range(n)]
    inv = [eye + p[u] for u in range(n)]
    for _ in range(int(math.log2(c)) - 1):
        p = [_dot(p[u], p[u]) for u in range(n)]
        inv = [inv[u] + _dot(inv[u], p[u]) for u in range(n)]
    eg = [jnp.exp(gcb[u]) for u in range(n)]
    rhs = [jnp.concatenate([v_s[rows[u], :] * beta_s[rows[u], :], kb[u] * eg[u]], axis=1) for u in range(n)]
    uw = [_dot(inv[u], rhs[u]) for u in range(n)]
    auw = [_dot(attn[u], uw[u]) for u in range(n)]
    k_dec = [k[u] * jnp.exp(gcb[u][c - 1:c, :] - gcb[u]) for u in range(n)]
    kuw = [_dot(k_dec[u].T, uw[u]) for u in range(n)]
    for u in range(n):
        rd = pl.ds(pl.multiple_of(chunks[u] * d, d), d)
        op_s[rows[u], :] = auw[u][:, 0:d]
        qp_s[rows[u], :] = eg[u] * q[u] - auw[u][:, d:2 * d]
        nn_s[rd, :] = kuw[u][:, 0:d]
        mn_s[rd, :] = -kuw[u][:, d:2 * d]
    return carry


def _gdn_prompt(z, p1, p2, conv_w, norm_w, n_seq, t_len):
    m = n_seq * t_len
    hps = _GDN_HEADS_PER_STEP
    width = hps * HEAD_DIM
    base = DNQKV_OFF // width

    def zcol(off):
        return pl.BlockSpec((t_len, width), functools.partial(lambda b, h, off: (b, off + h), off=off))

    def wcol(off):
        return pl.BlockSpec((CONV_W, width), functools.partial(lambda b, h, off: (0, off + h), off=off))

    aux = pl.BlockSpec((t_len, LANE), lambda b, h: (b, 0))
    tbuf = pltpu.VMEM((hps, t_len, HEAD_DIM), F32)
    groups = DN_HEADS // hps
    return pl.pallas_call(
        functools.partial(_gdn_prompt_kernel, t_len=t_len),
        out_shape=(jax.ShapeDtypeStruct((m, DN_WIDTH), BF16),
                   jax.ShapeDtypeStruct((n_seq, DN_HEADS, HEAD_DIM, HEAD_DIM), F32)),
        grid=(n_seq, groups),
        in_specs=[zcol(base), zcol(base + groups), zcol(base + 2 * groups),
                  wcol(0), wcol(groups), wcol(2 * groups), aux, aux,
                  zcol(DNZ_OFF // width), pl.BlockSpec((1, HEAD_DIM), lambda b, h: (0, 0))],
        out_specs=(pl.BlockSpec((t_len, width), lambda b, h: (b, h)),
                   pl.BlockSpec((None, hps, HEAD_DIM, HEAD_DIM), lambda b, h: (b, h, 0, 0))),
        scratch_shapes=[tbuf] * 7 + [pltpu.VMEM((hps, t_len // CHUNK * HEAD_DIM, HEAD_DIM), F32)] * 2,
        compiler_params=_cparams("arbitrary", "arbitrary"), name="gdn_prompt")(
            z, z, z, conv_w, conv_w, conv_w, p1, p2, z, norm_w.reshape(1, HEAD_DIM))


_GH = SSD_HEADS // SSD_GROUPS
_GW = _GH * SSD_HEAD_DIM
_SSD_UNROLL = 4


def _ssd_prompt_kernel(zx_ref, zb_ref, zc_ref, wx_ref, wb_ref, wc_ref, bx_ref, bb_ref, bc_ref, p1_ref, p2_ref,
                       zg_ref, d_ref, nw_ref, o_ref, st_ref, x_s, b_s, c_s, y_s, state_s, *, t_len):
    grp = pl.program_id(1)
    c = CHUNK
    x_s[...] = _silu(_causal_conv_rows(zx_ref[...], wx_ref, CONV_W) + bx_ref[...])
    b_s[...] = _silu(_causal_conv_rows(zb_ref[...], wb_ref, CONV_W) + bb_ref[...])
    c_s[...] = _silu(_causal_conv_rows(zc_ref[...], wc_ref, CONV_W) + bc_ref[...])
    state_s[...] = jnp.zeros_like(state_s)

    def chunk_group(gi, carry):
        n = _SSD_UNROLL
        rows = [pl.ds(pl.multiple_of((gi * n + u) * c, c), c) for u in range(n)]
        cm = [c_s[r, :] for r in rows]
        bm = [b_s[r, :] for r in rows]
        cb = [_dot_nt(cm[u], bm[u]) for u in range(n)]
        eac, dec, xs, xv, gam = [], [], [], [], []
        for u in range(n):
            x, p1, p2 = x_s[rows[u], :], p1_ref[rows[u], :], p2_ref[rows[u], :]
            eac_h, dec_h, xs_h, xv_h, gam_h = [], [], [], [], []
            for hh in range(_GH):
                lane = SSD_LANE + grp * _GH + hh
                ac = _lane_col(p2, lane)
                a_last = ac[c - 1:c, :]
                xv_hh = x[:, hh * SSD_HEAD_DIM:(hh + 1) * SSD_HEAD_DIM] * _lane_col(p1, lane)
                gam_h.append(_decay_matrix(ac, c)[0])
                xv_h.append(xv_hh)
                xs_h.append(xv_hh * jnp.exp(a_last - ac))
                eac_h.append(jnp.broadcast_to(jnp.exp(ac), (c, SSD_HEAD_DIM)))
                dec_h.append(jnp.broadcast_to(jnp.exp(a_last), (1, SSD_HEAD_DIM)))
            gam.append(gam_h)
            xv.append(xv_h)
            xs.append(jnp.concatenate(xs_h, axis=1))
            eac.append(jnp.concatenate(eac_h, axis=1))
            dec.append(jnp.concatenate(dec_h, axis=1))
        y_intra = [jnp.concatenate([_dot(cb[u] * gam[u][hh], xv[u][hh]) for hh in range(_GH)], axis=1)
                   for u in range(n)]
        new_states = [_dot(bm[u].T, xs[u]) for u in range(n)]
        s_in = [state_s[...]]
        for u in range(n):
            s_in.append(s_in[u] * dec[u] + new_states[u])
        state_s[...] = s_in[n]
        for u in range(n):
            y_s[rows[u], :] = y_intra[u] + eac[u] * _dot(cm[u], s_in[u])
        return carry

    lax.fori_loop(0, t_len // (c * _SSD_UNROLL), chunk_group, 0)
    for hh in range(_GH):
        st_ref[hh] = state_s[:, hh * SSD_HEAD_DIM:(hh + 1) * SSD_HEAD_DIM]
    y = (y_s[...] + d_ref[...] * x_s[...]) * _silu(zg_ref[...])
    o_ref[...] = (y * lax.rsqrt(jnp.mean(y * y, axis=-1, keepdims=True) + EPS) * nw_ref[...]).astype(o_ref.dtype)


def _ssd_prompt(z, p1, p2, conv_w, conv_b, d_vec, norm_w, n_seq, t_len):
    m = n_seq * t_len
    xb, bb, cbk = SSDX_OFF // _GW, (SSDX_OFF + SSD_INNER) // LANE, (SSDX_OFF + SSD_INNER + 2 * SSD_STATE) // LANE
    wb0, wc0 = SSD_INNER // LANE, (SSD_INNER + 2 * SSD_STATE) // LANE

    def spec(rows, width, off):
        return pl.BlockSpec((rows, width), functools.partial(lambda b, g, off: (0, off + g), off=off))

    def zspec(width, off):
        return pl.BlockSpec((t_len, width), functools.partial(lambda b, g, off: (b, off + g), off=off))

    aux = pl.BlockSpec((t_len, LANE), lambda b, g: (b, 0))
    return pl.pallas_call(
        functools.partial(_ssd_prompt_kernel, t_len=t_len),
        out_shape=(jax.ShapeDtypeStruct((m, SSD_INNER), BF16),
                   jax.ShapeDtypeStruct((n_seq, SSD_HEADS, SSD_STATE, SSD_HEAD_DIM), F32)),
        grid=(n_seq, SSD_GROUPS),
        in_specs=[zspec(_GW, xb), zspec(LANE, bb), zspec(LANE, cbk),
                  spec(CONV_W, _GW, 0), spec(CONV_W, LANE, wb0), spec(CONV_W, LANE, wc0),
                  spec(1, _GW, 0), spec(1, LANE, wb0), spec(1, LANE, wc0),
                  aux, aux, zspec(_GW, SSDZ_OFF // _GW), spec(1, _GW, 0), spec(1, _GW, 0)],
        out_specs=(pl.BlockSpec((t_len, _GW), lambda b, g: (b, g)),
                   pl.BlockSpec((None, _GH, SSD_STATE, SSD_HEAD_DIM), lambda b, g: (b, g, 0, 0))),
        scratch_shapes=[pltpu.VMEM((t_len, _GW), F32), pltpu.VMEM((t_len, LANE), F32), pltpu.VMEM((t_len, LANE), F32),
                        pltpu.VMEM((t_len, _GW), F32), pltpu.VMEM((SSD_STATE, _GW), F32)],
        compiler_params=_cparams("arbitrary", "arbitrary"), name="ssd_prompt")(
            z, z, z, conv_w, conv_w, conv_w, conv_b, conv_b, conv_b, p1, p2, z, d_vec, norm_w.reshape(1, SSD_INNER))


_FFN_TN = 256
_FFN_NJ = FFN_DIM // _FFN_TN
_TAIL = 8


def _ffn_up_prompt_kernel(a_ref, bg_ref, bu_ref, wg_ref, wu_ref, cg_ref, cu_ref, g_ref, sg_ref, su_ref, b_s, y_s,
                          *, tiles_per_seq, n_sub):
    i = pl.program_id(1)
    tn = _FFN_TN
    tm = a_ref.shape[0]

    @pl.when(i == 0)
    def _():
        b_s[:, 0:tn] = bg_ref[...].astype(BF16)
        b_s[:, tn:2 * tn] = bu_ref[...].astype(BF16)

    @pl.when(i % tiles_per_seq == 0)
    def _():
        y_s[0:_TAIL, :] = jnp.zeros((_TAIL, 2 * tn), F32)

    @pl.when(i % tiles_per_seq != 0)
    def _():
        y_s[0:_TAIL, :] = y_s[tm:tm + _TAIL, :]

    w = jnp.concatenate([wg_ref[...], wu_ref[...]], axis=1)
    c = jnp.concatenate([cg_ref[...], cu_ref[...]], axis=1)
    ts = tm // n_sub
    for s in range(n_sub):
        r0 = _TAIL + s * ts
        y = _dot(a_ref[s * ts:(s + 1) * ts, :], b_s[...])
        y_s[r0:r0 + ts, :] = y
        u = w[0:1, :] * y_s[r0 - 2:r0 - 2 + ts, :] + w[1:2, :] * y_s[r0 - 1:r0 - 1 + ts, :] + w[2:3, :] * y + c
        g_ref[s * ts:(s + 1) * ts, :] = (_silu(u[:, 0:tn]) * u[:, tn:2 * tn]).astype(g_ref.dtype)
    sg_ref[...] = y_s[tm:tm + _TAIL, 0:tn]
    su_ref[...] = y_s[tm:tm + _TAIL, tn:2 * tn]


def _ffn_up_prompt(h2, w_up, layer, conv_w, conv_b, n_seq, t_len, tm):
    m = n_seq * t_len
    k = h2.shape[1]
    nj = _FFN_NJ
    tiles_per_seq = t_len // tm
    tail = jax.ShapeDtypeStruct((m // tm, _TAIL, FFN_DIM), F32)
    tail_spec = pl.BlockSpec((None, _TAIL, _FFN_TN), lambda j, i: (i, 0, j))
    return pl.pallas_call(
        functools.partial(_ffn_up_prompt_kernel, tiles_per_seq=tiles_per_seq, n_sub=2),
        out_shape=(jax.ShapeDtypeStruct((m, FFN_DIM), BF16), tail, tail),
        grid=(nj, m // tm),
        in_specs=[pl.BlockSpec((tm, k), lambda j, i: (i, 0)),
                  pl.BlockSpec((None, k, _FFN_TN), lambda j, i: (layer, 0, j)),
                  pl.BlockSpec((None, k, _FFN_TN), lambda j, i: (layer, 0, j + nj)),
                  pl.BlockSpec((FFN_CONV_W, _FFN_TN), lambda j, i: (0, j)),
                  pl.BlockSpec((FFN_CONV_W, _FFN_TN), lambda j, i: (0, j + nj)),
                  pl.BlockSpec((1, _FFN_TN), lambda j, i: (0, j)), pl.BlockSpec((1, _FFN_TN), lambda j, i: (0, j + nj))],
        out_specs=(pl.BlockSpec((tm, _FFN_TN), lambda j, i: (i, j)), tail_spec, tail_spec),
        scratch_shapes=[pltpu.VMEM((k, 2 * _FFN_TN), BF16), pltpu.VMEM((_TAIL + tm, 2 * _FFN_TN), F32)],
        compiler_params=_cparams("arbitrary", "arbitrary"), name="ffn_up_prompt")(
            h2, w_up, w_up, conv_w, conv_w, conv_b, conv_b)


def _ffn_up_sample_kernel(a_ref, bg_ref, bu_ref, wg_ref, wu_ref, cg_ref, cu_ref, s0g_ref, s1g_ref, s0u_ref, s1u_ref,
                          g_ref, yg_ref, yu_ref):
    a = a_ref[...]
    yg = _dot(a, bg_ref[...].astype(BF16))
    yu = _dot(a, bu_ref[...].astype(BF16))
    ug = wg_ref[0:1, :] * s0g_ref[...] + wg_ref[1:2, :] * s1g_ref[...] + wg_ref[2:3, :] * yg + cg_ref[...]
    uu = wu_ref[0:1, :] * s0u_ref[...] + wu_ref[1:2, :] * s1u_ref[...] + wu_ref[2:3, :] * yu + cu_ref[...]
    g_ref[...] = (_silu(ug) * uu).astype(g_ref.dtype)
    yg_ref[...] = yg
    yu_ref[...] = yu


def _ffn_up_sample(h2, w_up, layer, conv_w, conv_b, s0, s1):
    m, k = h2.shape
    nj = _FFN_NJ
    lo = lambda j: (0, j)
    hi = lambda j: (0, j + nj)
    row = lambda f: pl.BlockSpec((m, _FFN_TN), f)
    ysd = jax.ShapeDtypeStruct((m, FFN_DIM), F32)
    return pl.pallas_call(
        _ffn_up_sample_kernel, out_shape=(jax.ShapeDtypeStruct((m, FFN_DIM), BF16), ysd, ysd), grid=(nj,),
        in_specs=[pl.BlockSpec((m, k), lambda j: (0, 0)),
                  pl.BlockSpec((None, k, _FFN_TN), lambda j: (layer, 0, j)),
                  pl.BlockSpec((None, k, _FFN_TN), lambda j: (layer, 0, j + nj)),
                  pl.BlockSpec((FFN_CONV_W, _FFN_TN), lo), pl.BlockSpec((FFN_CONV_W, _FFN_TN), hi),
                  pl.BlockSpec((1, _FFN_TN), lo), pl.BlockSpec((1, _FFN_TN), hi),
                  row(lo), row(lo), row(hi), row(hi)],
        out_specs=(row(lo), row(lo), row(lo)),
        compiler_params=_cparams("arbitrary"), name="ffn_up_sample")(
            h2, w_up, w_up, conv_w, conv_w, conv_b, conv_b, s0, s1, s0, s1)


_PAGES_PER_STEP = 16


def _kmean_kernel(pt_ref, *refs):
    o_ref = refs[-1]
    page = refs[0].shape[0]
    per_block = MOBA_BLOCK // page
    for n in range(_PAGES_PER_STEP // per_block):
        acc = jnp.sum(refs[n * per_block][...], axis=0)
        for r in range(1, per_block):
            acc = acc + jnp.sum(refs[n * per_block + r][...], axis=0)
        o_ref[n] = acc * (1.0 / MOBA_BLOCK)


def _kmean(cache_k, page_table, layer):
    n_b, n_pages = page_table.shape
    page, n_kv, d = cache_k.shape[2:]
    per_block = MOBA_BLOCK // page
    n_blocks = n_pages // per_block
    steps = n_pages // _PAGES_PER_STEP
    in_specs = [pl.BlockSpec((None, None, page, n_kv, d),
                             functools.partial(lambda b, s, pt, r: (layer, pt[b, s * _PAGES_PER_STEP + r], 0, 0, 0), r=r))
                for r in range(_PAGES_PER_STEP)]
    grid_spec = pltpu.PrefetchScalarGridSpec(
        num_scalar_prefetch=1, grid=(n_b, steps), in_specs=in_specs,
        out_specs=pl.BlockSpec((None, _PAGES_PER_STEP // per_block, n_kv, d), lambda b, s, pt: (b, s, 0, 0)))
    return pl.pallas_call(
        _kmean_kernel, out_shape=jax.ShapeDtypeStruct((n_b, n_blocks, n_kv, d), F32), grid_spec=grid_spec,
        compiler_params=_cparams("arbitrary", "arbitrary"), name="kmean")(page_table, *([cache_k] * _PAGES_PER_STEP))


def _select_kernel(q_ref, k_ref, kmean_ref, wq_ref, wk_ref, qn_ref, kn_ref, idx_ref):
    n_b, n_h, _ = q_ref.shape
    n_blocks = kmean_ref.shape[1]
    q = q_ref[...]
    qn = q * lax.rsqrt(jnp.mean(q * q, axis=-1, keepdims=True) + EPS) * wq_ref[...]
    qn_ref[...] = qn
    k = k_ref[...]
    kn_ref[...] = k * lax.rsqrt(jnp.mean(k * k, axis=-1, keepdims=True) + EPS) * wk_ref[...]
    head = lax.broadcasted_iota(jnp.int32, (n_h, n_blocks), 0)
    lane = lax.broadcasted_iota(jnp.int32, (n_h, n_blocks), 1).astype(F32)
    lane_out = lax.broadcasted_iota(jnp.int32, (n_h, LANE), 1)
    for b in range(n_b):
        gate = jnp.zeros((n_h, n_blocks), F32)
        for kv in range(ATT_KV_HEADS):
            gk = _dot_nt(qn[b], kmean_ref[b, :, kv * HEAD_DIM:(kv + 1) * HEAD_DIM], HIGHEST)
            gate = jnp.where(head // ATT_GROUP == kv, gk, gate)
        out = jnp.zeros((n_h, LANE), F32)
        for s in range(MOBA_TOPK):
            mx = jnp.max(gate, axis=-1, keepdims=True)
            pick = jnp.min(jnp.where(gate == mx, lane, float(n_blocks)), axis=-1, keepdims=True)
            out = jnp.where(lane_out == s, pick, out)
            gate = jnp.where(lane == pick, -jnp.inf, gate)
        idx_ref[b] = out.astype(jnp.int32)


def _select(q3, k3, kmean, wq, wk):
    n_b, n_h, d = q3.shape
    full = lambda shape: pl.BlockSpec(shape, lambda i: (0,) * len(shape))
    return pl.pallas_call(
        _select_kernel,
        out_shape=(jax.ShapeDtypeStruct(q3.shape, F32), jax.ShapeDtypeStruct(k3.shape, F32),
                   jax.ShapeDtypeStruct((n_b, n_h, LANE), jnp.int32)),
        grid=(1,),
        in_specs=[full(q3.shape), full(k3.shape), full(kmean.shape), full((1, 1, d)), full((1, 1, d))],
        out_specs=(full(q3.shape), full(k3.shape), full((n_b, n_h, LANE))),
        compiler_params=_cparams("arbitrary"), name="select")(q3, k3, kmean, wq.reshape(1, 1, d), wk.reshape(1, 1, d))


def _attn_sample_kernel(pt_ref, idx_ref, qn_ref, kn_ref, vn_ref, *refs, n_sel, per_block):
    o_ref = refs[-1]
    pages = refs[:-1]
    n_pg = n_sel * per_block
    k_refs, v_refs = pages[:n_pg], pages[n_pg:]
    h = pl.program_id(1)
    kv = h // ATT_GROUP
    q = qn_ref[pl.ds(h, 1), :] * (HEAD_DIM ** -0.5)
    k_new = kn_ref[pl.ds(kv, 1), :]
    v_new = vn_ref[pl.ds(kv, 1), :]
    s_own = jnp.sum(q * k_new, axis=-1, keepdims=True)
    n_rows = k_refs[0].shape[0]
    mine = lax.broadcasted_iota(jnp.int32, (n_rows, 1), 0) % ATT_KV_HEADS == kv
    logits = [jnp.where(mine, jnp.sum(r[...] * q, axis=-1, keepdims=True), NEG_BIG) for r in k_refs]
    mx = s_own
    for s in logits:
        mx = jnp.maximum(mx, jnp.max(s, axis=0, keepdims=True))
    p_own = jnp.exp(s_own - mx)
    den = p_own
    acc = p_own * v_new
    for s, v_ref in zip(logits, v_refs):
        p = jnp.exp(s - mx)
        den = den + jnp.sum(p, axis=0, keepdims=True)
        acc = acc + jnp.sum(p * v_ref[...], axis=0, keepdims=True)
    o_ref[pl.ds(h, 1), :] = acc / den


def _attn_sample(cache_k, cache_v, page_table, idx, qn, kn, vn, layer):
    n_b, n_h, d = qn.shape
    depth, n_pool, page, n_kv = cache_k.shape[:4]
    per_block = MOBA_BLOCK // page
    n_sel = idx.shape[2]
    cache_k = cache_k.reshape(depth, n_pool, page * n_kv, d)
    cache_v = cache_v.reshape(depth, n_pool, page * n_kv, d)

    def page_spec(s, r):
        def imap(b, h, pt, ix):
            return (layer, pt[b, ix[b, h * n_sel + s] * per_block + r], 0, 0)
        return pl.BlockSpec((None, None, page * n_kv, d), imap)

    pspecs = [page_spec(s, r) for s in range(n_sel) for r in range(per_block)]
    slab = lambda n: pl.BlockSpec((None, n, d), lambda b, h, pt, ix: (b, 0, 0))
    grid_spec = pltpu.PrefetchScalarGridSpec(
        num_scalar_prefetch=2, grid=(n_b, n_h),
        in_specs=[slab(n_h), slab(ATT_KV_HEADS), slab(ATT_KV_HEADS)] + pspecs + pspecs,
        out_specs=slab(n_h))
    n_pg = len(pspecs)
    return pl.pallas_call(
        functools.partial(_attn_sample_kernel, n_sel=n_sel, per_block=per_block),
        out_shape=jax.ShapeDtypeStruct((n_b, n_h, d), F32), grid_spec=grid_spec,
        compiler_params=_cparams("arbitrary", "arbitrary"), name="attn_sample")(
            page_table, idx.reshape(n_b, n_h * n_sel), qn, kn, vn, *([cache_k] * n_pg), *([cache_v] * n_pg))


def _rows_to_cols(x):
    r, n = x.shape
    return jnp.concatenate([x, jnp.zeros((n - r, n), x.dtype)], axis=0).T


def _mix_sample_kernel(dx_ref, dbuf_ref, dw_ref, dbeta_ref, dg_ref, dz_ref, dnw_ref, dst_ref,
                       sx_ref, sxbuf_ref, sxw_ref, sxb_ref, bc_ref, bcbuf_ref, bcw_ref, bcb_ref,
                       sdt_ref, sa_ref, sz_ref, sd_ref, snw_ref, sst_ref,
                       dno_ref, dnst_ref, so_ref, sso_ref, y_s):
    x = dx_ref[...]
    conv = x * dw_ref[CONV_W - 1]
    for i in range(CONV_W - 1):
        conv = conv + dbuf_ref[i] * dw_ref[i]
    act = _silu(conv)
    nh = DN_HEADS
    q, k, v = act[0:nh], act[nh:2 * nh], act[2 * nh:3 * nh]
    q = q * lax.rsqrt(jnp.sum(q * q, axis=-1, keepdims=True) + EPS) * (HEAD_DIM ** -0.5)
    k = k * lax.rsqrt(jnp.sum(k * k, axis=-1, keepdims=True) + EPS)
    beta = dbeta_ref[...]
    eg = jnp.exp(dg_ref[...])
    qk = jnp.sum(q * k, axis=-1, keepdims=True)
    qt, kt = _rows_to_cols(q), _rows_to_cols(k)
    o_rows = []
    for h in range(nh):
        s0 = dst_ref[h]
        kcol, qcol = kt[:, h:h + 1], qt[:, h:h + 1]
        ks = jnp.sum(kcol * s0, axis=0, keepdims=True)
        qs = jnp.sum(qcol * s0, axis=0, keepdims=True)
        b_h, e_h = beta[h:h + 1, :], eg[h:h + 1, :]
        v_new = v[h:h + 1, :] * b_h - (b_h * e_h) * ks
        o_rows.append(e_h * qs + qk[h:h + 1, :] * v_new)
        dnst_ref[h] = s0 * e_h + kcol * v_new
    o = jnp.concatenate(o_rows, axis=0)
    on = o * lax.rsqrt(jnp.mean(o * o, axis=-1, keepdims=True) + EPS) * dnw_ref[...]
    dno_ref[...] = on * _silu(dz_ref[...])
    xs = sx_ref[...] * sxw_ref[CONV_W - 1] + sxb_ref[...]
    bc = bc_ref[...] * bcw_ref[CONV_W - 1] + bcb_ref[...]
    for i in range(CONV_W - 1):
        xs = xs + sxbuf_ref[i] * sxw_ref[i]
        bc = bc + bcbuf_ref[i] * bcw_ref[i]
    xs = _silu(xs)
    bc = _silu(bc)
    bct = _rows_to_cols(bc)
    cb = jnp.sum(bc[0:SSD_GROUPS] * bc[SSD_GROUPS:2 * SSD_GROUPS], axis=-1, keepdims=True)
    dt = sdt_ref[...]
    ea = jnp.exp(sa_ref[...])
    for h in range(SSD_HEADS):
        grp = h // _GH
        s0 = sst_ref[h]
        bcol, ccol = bct[:, grp:grp + 1], bct[:, SSD_GROUPS + grp:SSD_GROUPS + grp + 1]
        xv = xs[h:h + 1, :] * dt[h:h + 1, :]
        e_h = ea[h:h + 1, :]
        y_s[h:h + 1, :] = cb[grp:grp + 1, :] * xv + e_h * jnp.sum(ccol * s0, axis=0, keepdims=True)
        sso_ref[h] = s0 * e_h + bcol * xv
    y = (y_s[...] + sd_ref[...] * xs) * _silu(sz_ref[...])
    norm_rows = []
    for grp in range(SSD_GROUPS):
        yg = y[grp * _GH:(grp + 1) * _GH]
        ms = jnp.sum(jnp.sum(yg * yg, axis=-1, keepdims=True), axis=0, keepdims=True) * (1.0 / _GW)
        norm_rows.append(yg * lax.rsqrt(ms + EPS))
    so_ref[...] = jnp.concatenate(norm_rows, axis=0) * snw_ref[...]


def _mix_sample(args, n_b):
    def per_b(shape):
        nd = len(shape) - 1
        return pl.BlockSpec((None,) + tuple(shape[1:]), lambda b: (b,) + (0,) * nd)

    def shared(shape):
        nd = len(shape)
        return pl.BlockSpec(tuple(shape), lambda b: (0,) * nd)

    names_per_b = {"dx", "dbuf", "dbeta", "dg", "dz", "dst", "sx", "sxbuf", "bc", "bcbuf", "sdt", "sa", "sz", "sst"}
    order = ["dx", "dbuf", "dw", "dbeta", "dg", "dz", "dnw", "dst", "sx", "sxbuf", "sxw", "sxb", "bc", "bcbuf", "bcw",
             "bcb", "sdt", "sa", "sz", "sd", "snw", "sst"]
    in_specs = [per_b(args[n].shape) if n in names_per_b else shared(args[n].shape) for n in order]
    outs = (jax.ShapeDtypeStruct((n_b, DN_HEADS, HEAD_DIM), F32),
            jax.ShapeDtypeStruct((n_b, DN_HEADS, HEAD_DIM, HEAD_DIM), F32),
            jax.ShapeDtypeStruct((n_b, SSD_HEADS, SSD_HEAD_DIM), F32),
            jax.ShapeDtypeStruct((n_b, SSD_HEADS, SSD_STATE, SSD_HEAD_DIM), F32))
    return pl.pallas_call(
        _mix_sample_kernel, out_shape=outs, grid=(n_b,), in_specs=in_specs,
        out_specs=tuple(per_b(o.shape) for o in outs),
        scratch_shapes=[pltpu.VMEM((SSD_HEADS, SSD_HEAD_DIM), F32)],
        compiler_params=_cparams("arbitrary"), name="mix_sample")(*[args[n] for n in order])


def _split_w_in(w_in):
    depth, k, _ = w_in.shape
    wt = jnp.swapaxes(w_in, 1, 2)
    wt_b = wt[:, ZB_SRC:ZB_SRC + ZB_COLS]
    wt_s = jnp.concatenate([wt[:, ZS_SRC[0]:ZS_SRC[0] + 16], wt[:, ZS_SRC[1]:ZS_SRC[1] + 16],
                            jnp.zeros((depth, LANE - 32, k), w_in.dtype)], axis=1)
    return wt, wt_b, wt_s


def _in_proj_all(h, lw, layer, tm, suffix=""):
    za = _in_proj(h, lw["wt_a"], layer, ZA_COLS, tm=tm, tn=512, name="in_proj_a" + suffix)
    zb = _in_proj(h, lw["wt_b"], layer, ZB_COLS, tm=tm, tn=512, name="in_proj_b" + suffix)
    zs = _in_proj(h, lw["wt_s"], layer, LANE, tm=tm, tn=LANE, name="in_proj_s" + suffix)
    return za, zb, zs


def _lane_vec(dn_vals, ssd_vals):
    v = jnp.zeros((LANE,), F32)
    v = v.at[DNG_LANE:DNG_LANE + DN_HEADS].set(dn_vals.astype(F32))
    v = v.at[SSD_LANE:SSD_LANE + SSD_HEADS].set(ssd_vals.astype(F32))
    return v.reshape(1, LANE)


def _ffn_down(g, w_down, layer, x, tm):
    half = FFN_DIM // 2
    for part in range(2):
        x = _matmul([(g, half, part, part)], w_down, layer, tm=tm, tn=256, res=x, name="ffn_down")
    return x


def _prompt_layer(x, lw, layer, n_seq, t_len):
    tm = 1024
    h = _rmsnorm_cast(x, lw["norm1_w"], 512)
    za, zb, zs = _in_proj_all(h, lw, layer, tm)
    p1, p2 = _aux(zs, lw["bias_vec"], lw["alog_vec"], n_seq, t_len, CHUNK)
    o_att, k_norm = _attn_prompt(za, lw["attn_q_norm_w"], lw["attn_k_norm_w"], n_seq, t_len)
    o_dn, dn_state = _gdn_prompt(za, p1, p2, lw["dn_conv_w"], lw["dn_norm_w"], n_seq, t_len)
    o_ssd, ssd_state = _ssd_prompt(zb, p1, p2, lw["ssd_conv_w"], lw["ssd_conv_b"].reshape(1, -1), lw["ssd_d_vec"],
                                   lw["ssd_norm_w"], n_seq, t_len)
    x = _matmul([(o_att, 2048, 0, 0), (o_dn, 1024, 0, 2), (o_ssd, 1024, 0, 3)], lw["w_out"], layer, tm=tm, tn=512,
                res=x, name="out_proj")
    h2 = _rmsnorm_cast(x, lw["norm2_w"], 512)
    g, tail_g, tail_u = _ffn_up_prompt(h2, lw["ffn_w_up"], layer, lw["ffn_conv_w"], lw["ffn_conv_b"].reshape(1, -1),
                                       n_seq, t_len, tm)
    x = _ffn_down(g, lw["ffn_w_down"], layer, x, tm)
    za3 = za.reshape(n_seq, t_len, ZA_COLS)
    zb3 = zb.reshape(n_seq, t_len, ZB_COLS)
    tps = t_len // tm
    new = (k_norm.reshape(n_seq, t_len, ATT_KV_HEADS, HEAD_DIM),
           za3[:, :, V_OFF:V_OFF + 512].reshape(n_seq, t_len, ATT_KV_HEADS, HEAD_DIM),
           za3[:, t_len - (CONV_W - 1):, DNQKV_OFF:DNQKV_OFF + DN_CONV_CH],
           dn_state,
           zb3[:, t_len - (CONV_W - 1):, SSDX_OFF:SSDX_OFF + SSD_CONV_CH],
           ssd_state,
           jnp.concatenate([tail_g[tps - 1::tps, _TAIL - 2:], tail_u[tps - 1::tps, _TAIL - 2:]], axis=-1))
    return x, new


_SAMPLE_ROWS = 16


def _sample_layer(x, lw, layer, cache_k, cache_v, page_table, dn_conv_buf, dn_state, ssd_conv_buf, ssd_state,
                  ffn_conv_buf):
    n_b = page_table.shape[0]
    rows = x.shape[0]
    h = _rmsnorm_cast(x, lw["norm1_w"], rows)
    za, zb2, zs = _in_proj_all(h, lw, layer, rows, "_smp")
    p1, p2 = _aux(zs, lw["bias_vec"], lw["alog_vec"], 1, rows, 1)
    zb = za[:n_b]
    zbs = zb2[:n_b]
    kmean = _kmean(cache_k, page_table, layer)
    kmean = kmean.reshape(n_b, kmean.shape[1], ATT_KV_HEADS * HEAD_DIM)
    q3 = zb[:, Q_OFF:Q_OFF + 2048].reshape(n_b, ATT_HEADS, HEAD_DIM)
    k3 = zb[:, K_OFF:K_OFF + 512].reshape(n_b, ATT_KV_HEADS, HEAD_DIM)
    v3 = zb[:, V_OFF:V_OFF + 512].reshape(n_b, ATT_KV_HEADS, HEAD_DIM)
    qn, kn, idx = _select(q3, k3, kmean, lw["attn_q_norm_w"], lw["attn_k_norm_w"])
    o_att = _attn_sample(cache_k, cache_v, page_table, idx[:, :, :MOBA_TOPK], qn, kn, v3, layer)
    dn_x = zb[:, DNQKV_OFF:DNQKV_OFF + DN_CONV_CH]
    ssd_x = zbs[:, SSDX_OFF:SSDX_OFF + SSD_CONV_CH]
    nh3 = 3 * DN_HEADS
    ng2 = 2 * SSD_GROUPS
    args = {
        "dx": dn_x.reshape(n_b, nh3, HEAD_DIM),
        "dbuf": dn_conv_buf.reshape(n_b, CONV_W - 1, nh3, HEAD_DIM),
        "dw": lw["dn_conv_w"].reshape(CONV_W, nh3, HEAD_DIM),
        "dbeta": p1[:n_b, BETA_LANE:BETA_LANE + DN_HEADS].reshape(n_b, DN_HEADS, 1),
        "dg": p2[:n_b, DNG_LANE:DNG_LANE + DN_HEADS].reshape(n_b, DN_HEADS, 1),
        "dz": zb[:, DNZ_OFF:DNZ_OFF + DN_WIDTH].reshape(n_b, DN_HEADS, HEAD_DIM),
        "dnw": lw["dn_norm_w"].reshape(1, HEAD_DIM),
        "dst": dn_state,
        "sx": ssd_x[:, :SSD_INNER].reshape(n_b, SSD_HEADS, SSD_HEAD_DIM),
        "sxbuf": ssd_conv_buf[:, :, :SSD_INNER].reshape(n_b, CONV_W - 1, SSD_HEADS, SSD_HEAD_DIM),
        "sxw": lw["ssd_conv_w"][:, :SSD_INNER].reshape(CONV_W, SSD_HEADS, SSD_HEAD_DIM),
        "sxb": lw["ssd_conv_b"][:SSD_INNER].reshape(SSD_HEADS, SSD_HEAD_DIM),
        "bc": ssd_x[:, SSD_INNER:].reshape(n_b, ng2, SSD_STATE),
        "bcbuf": ssd_conv_buf[:, :, SSD_INNER:].reshape(n_b, CONV_W - 1, ng2, SSD_STATE),
        "bcw": lw["ssd_conv_w"][:, SSD_INNER:].reshape(CONV_W, ng2, SSD_STATE),
        "bcb": lw["ssd_conv_b"][SSD_INNER:].reshape(ng2, SSD_STATE),
        "sdt": p1[:n_b, SSD_LANE:SSD_LANE + SSD_HEADS].reshape(n_b, SSD_HEADS, 1),
        "sa": p2[:n_b, SSD_LANE:SSD_LANE + SSD_HEADS].reshape(n_b, SSD_HEADS, 1),
        "sz": zbs[:, SSDZ_OFF:SSDZ_OFF + SSD_INNER].reshape(n_b, SSD_HEADS, SSD_HEAD_DIM),
        "sd": lw["ssd_D"].astype(F32).reshape(SSD_HEADS, 1),
        "snw": lw["ssd_norm_w"].reshape(SSD_HEADS, SSD_HEAD_DIM),
        "sst": ssd_state,
    }
    o_dn, dn_state_new, o_ssd, ssd_state_new = _mix_sample(args, n_b)
    pad = lambda a: jnp.pad(a.reshape(n_b, -1), ((0, rows - n_b), (0, 0))).astype(BF16)
    x = _matmul([(pad(o_att), 2048, 0, 0), (pad(o_dn), 1024, 0, 2), (pad(o_ssd), 1024, 0, 3)], lw["w_out"], layer,
                tm=rows, tn=512, res=x, name="out_proj_s")
    h2 = _rmsnorm_cast(x, lw["norm2_w"], rows)
    padf = lambda a: jnp.pad(a, ((0, rows - n_b), (0, 0)))
    g, y_g, y_u = _ffn_up_sample(h2, lw["ffn_w_up"], layer, lw["ffn_conv_w"], lw["ffn_conv_b"].reshape(1, -1),
                                 padf(ffn_conv_buf[:, 0]), padf(ffn_conv_buf[:, 1]))
    x = _ffn_down(g, lw["ffn_w_down"], layer, x, rows)
    y_new = jnp.concatenate([y_g[:n_b], y_u[:n_b]], axis=-1)
    new = (kn.reshape(n_b, 1, ATT_KV_HEADS, HEAD_DIM), v3.reshape(n_b, 1, ATT_KV_HEADS, HEAD_DIM),
           jnp.concatenate([dn_conv_buf[:, 1:], dn_x[:, None]], axis=1), dn_state_new,
           jnp.concatenate([ssd_conv_buf[:, 1:], ssd_x[:, None]], axis=1), ssd_state_new,
           jnp.stack([ffn_conv_buf[:, 1], y_new], axis=1))
    return x, new


def kernel(x_prompt, x_sample, cache_k, cache_v, page_table, state_dn_conv, state_dn, state_ssd_conv, state_ssd, state_ffn_conv, norm1_w, w_in, attn_q_norm_w, attn_k_norm_w, dn_conv_w, dn_A_log, dn_dt_bias, dn_norm_w, ssd_conv_w, ssd_conv_b, ssd_dt_bias, ssd_A_log, ssd_D, ssd_norm_w, w_out, norm2_w, ffn_w_up, ffn_conv_w, ffn_conv_b, ffn_w_down):
    depth = w_in.shape[0]
    n_seq, t_len, d_model = x_prompt.shape
    n_b = x_sample.shape[0]
    yp = x_prompt.reshape(n_seq * t_len, d_model)
    ys = jnp.pad(x_sample.reshape(n_b, d_model), ((0, _SAMPLE_ROWS - n_b), (0, 0)))
    wt_a, wt_b, wt_s = _split_w_in(w_in)
    outs_p, outs_s = [], []
    for l in range(depth):
        lw = {"norm1_w": norm1_w[l], "wt_a": wt_a, "wt_b": wt_b, "wt_s": wt_s, "attn_q_norm_w": attn_q_norm_w[l],
              "attn_k_norm_w": attn_k_norm_w[l], "dn_conv_w": dn_conv_w[l], "dn_norm_w": dn_norm_w[l],
              "ssd_conv_w": ssd_conv_w[l], "ssd_conv_b": ssd_conv_b[l], "ssd_D": ssd_D[l],
              "ssd_norm_w": ssd_norm_w[l], "w_out": w_out, "norm2_w": norm2_w[l], "ffn_w_up": ffn_w_up,
              "ffn_conv_w": ffn_conv_w[l], "ffn_conv_b": ffn_conv_b[l], "ffn_w_down": ffn_w_down,
              "bias_vec": _lane_vec(dn_dt_bias[l], ssd_dt_bias[l]),
              "alog_vec": _lane_vec(dn_A_log[l], ssd_A_log[l]),
              "ssd_d_vec": jnp.repeat(ssd_D[l].astype(F32), SSD_HEAD_DIM).reshape(1, SSD_INNER)}
        yp, new_p = _prompt_layer(yp, lw, l, n_seq, t_len)
        ys, new_s = _sample_layer(ys, lw, l, cache_k, cache_v, page_table, state_dn_conv[l], state_dn[l],
                                  state_ssd_conv[l], state_ssd[l], state_ffn_conv[l])
        outs_p.append(new_p)
        outs_s.append(new_s)
    st = lambda outs, i: jnp.stack([o[i] for o in outs])
    return (yp.reshape(n_seq, t_len, d_model), ys[:n_b].reshape(n_b, 1, d_model),
            st(outs_p, 0), st(outs_p, 1), st(outs_s, 0), st(outs_s, 1),
            st(outs_p, 2), st(outs_s, 2), st(outs_p, 3), st(outs_s, 3),
            st(outs_p, 4), st(outs_s, 4), st(outs_p, 5), st(outs_s, 5),
            st(outs_p, 6), st(outs_s, 6))
```

```python
import functools
import math

import jax
import jax.numpy as jnp
from jax import lax
from jax.experimental import pallas as pl
from jax.experimental.pallas import tpu as pltpu

F32 = jnp.float32
BF16 = jnp.bfloat16
HIGHEST = lax.Precision.HIGHEST

D_MODEL = 4096
HEAD_DIM = 128
ATT_HEADS = 16
ATT_KV_HEADS = 4
ATT_GROUP = 4
MOBA_BLOCK = 256
MOBA_TOPK = 3
DN_HEADS = 8
DN_WIDTH = 1024
DN_CONV_CH = 3072
SSD_INNER = 1024
SSD_HEAD_DIM = 64
SSD_HEADS = 16
SSD_GROUPS = 2
SSD_STATE = 128
SSD_CONV_CH = 1536
CONV_W = 4
CHUNK = 64
FFN_DIM = 11008
FFN_CONV_W = 3
EPS = 1e-6

Q_OFF, K_OFF, V_OFF = 0, 2048, 2560
DNQKV_OFF, DNZ_OFF = 3072, 6144
ZA_COLS = 7168
SSDX_OFF, SSDZ_OFF = 0, 1536
ZB_SRC, ZB_COLS = 7184, 2560
ZS_SRC = (7168, 9744)
LANE = 128
BETA_LANE, DNG_LANE, SSD_LANE = 0, 8, 16

VMEM_LIMIT_BYTES = 56 * 1024 * 1024
NEG_BIG = -1e30


def _cparams(*sem):
    return pltpu.CompilerParams(dimension_semantics=sem, vmem_limit_bytes=VMEM_LIMIT_BYTES)


def _silu(x):
    return x / (1.0 + jnp.exp(-x))


def _sigmoid(x):
    return 1.0 / (1.0 + jnp.exp(-x))


def _softplus(x):
    return jnp.maximum(x, 0.0) + jnp.log1p(jnp.exp(-jnp.abs(x)))


def _dot(a, b, precision=None):
    return jnp.dot(a, b, preferred_element_type=F32, precision=precision)


def _dot_nt(a, b, precision=None):
    return lax.dot_general(a, b, (((1,), (1,)), ((), ())), preferred_element_type=F32, precision=precision)


def _shift_rows(x, s):
    y = pltpu.roll(x, s, axis=0)
    head_rows = lax.broadcasted_iota(jnp.int32, (8, x.shape[1]), 0)
    return jnp.concatenate([jnp.where(head_rows < s, 0.0, y[0:8]), y[8:]], axis=0)


def _causal_conv_rows(x, w_ref, width):
    y = x * w_ref[width - 1:width, :]
    for i in range(width - 1):
        y = y + _shift_rows(x, width - 1 - i) * w_ref[i:i + 1, :]
    return y


def _lane_col(x, lane):
    lanes = lax.broadcasted_iota(jnp.int32, x.shape, 1)
    return jnp.sum(jnp.where(lanes == lane, x, 0.0), axis=-1, keepdims=True)


def _decay_matrix(gc_col, c):
    ii = lax.broadcasted_iota(jnp.int32, (c, c), 0)
    jj = lax.broadcasted_iota(jnp.int32, (c, c), 1)
    gcb = jnp.broadcast_to(gc_col, (c, c))
    gc_row = jnp.sum(jnp.where(ii == jj, gcb, 0.0), axis=0, keepdims=True)
    low = ii >= jj
    gam = jnp.where(low, jnp.exp(jnp.where(low, gcb - gc_row, 0.0)), 0.0)
    return gam, ii, jj


def _rmsnorm_kernel(x_ref, w_ref, o_ref):
    x = x_ref[...]
    ms = jnp.mean(x * x, axis=-1, keepdims=True)
    o_ref[...] = (x * lax.rsqrt(ms + EPS) * w_ref[...]).astype(o_ref.dtype)


def _rmsnorm_cast(x, w, tm):
    m, d = x.shape
    return pl.pallas_call(
        _rmsnorm_kernel, out_shape=jax.ShapeDtypeStruct((m, d), BF16), grid=(m // tm,),
        in_specs=[pl.BlockSpec((tm, d), lambda i: (i, 0)), pl.BlockSpec((1, d), lambda i: (0, 0))],
        out_specs=pl.BlockSpec((tm, d), lambda i: (i, 0)),
        compiler_params=_cparams("arbitrary"), name="rmsnorm")(x, w.reshape(1, d))


def _matmul_kernel(*refs, n_parts, has_res):
    a_refs = refs[:n_parts]
    b_refs = refs[n_parts:2 * n_parts]
    o_ref = refs[-1]
    acc = None
    for a_ref, b_ref in zip(a_refs, b_refs):
        d = _dot(a_ref[...], b_ref[...].astype(BF16))
        acc = d if acc is None else acc + d
    if has_res:
        acc = acc + refs[2 * n_parts][...]
    o_ref[...] = acc.astype(o_ref.dtype)


def _matmul(a_parts, b, layer, *, tm, tn, res=None, out_dtype=F32, a_single_buffer=False, name="matmul"):
    m = a_parts[0][0].shape[0]
    n = b.shape[2]
    in_specs, args = [], []
    a_mode = dict(pipeline_mode=pl.Buffered(1)) if a_single_buffer else {}
    for arr, kp, cb, rb in a_parts:
        in_specs.append(pl.BlockSpec((tm, kp), functools.partial(lambda i, j, cb: (i, cb), cb=cb), **a_mode))
        args.append(arr)
    for arr, kp, cb, rb in a_parts:
        in_specs.append(pl.BlockSpec((None, kp, tn), functools.partial(lambda i, j, rb: (layer, rb, j), rb=rb)))
        args.append(b)
    if res is not None:
        in_specs.append(pl.BlockSpec((tm, tn), lambda i, j: (i, j)))
        args.append(res)
    return pl.pallas_call(
        functools.partial(_matmul_kernel, n_parts=len(a_parts), has_res=res is not None),
        out_shape=jax.ShapeDtypeStruct((m, n), out_dtype), grid=(m // tm, n // tn),
        in_specs=in_specs, out_specs=pl.BlockSpec((tm, tn), lambda i, j: (i, j)),
        compiler_params=_cparams("arbitrary", "arbitrary"), name=name)(*args)


def _in_proj_kernel(a_ref, w_ref, o_ref, b_s):
    @pl.when(pl.program_id(1) == 0)
    def _():
        b_s[...] = w_ref[...].astype(BF16)

    o_ref[...] = _dot_nt(a_ref[...], b_s[...])


def _in_proj(h, wt, layer, n, *, tm, tn, name):
    m, k = h.shape
    return pl.pallas_call(
        _in_proj_kernel, out_shape=jax.ShapeDtypeStruct((m, n), F32), grid=(n // tn, m // tm),
        in_specs=[pl.BlockSpec((tm, k), lambda j, i: (i, 0)), pl.BlockSpec((None, tn, k), lambda j, i: (layer, j, 0))],
        out_specs=pl.BlockSpec((tm, tn), lambda j, i: (i, j)),
        scratch_shapes=[pltpu.VMEM((tn, k), BF16)],
        compiler_params=_cparams("arbitrary", "arbitrary"), name=name)(h, wt)


def _aux_kernel(s_ref, bias_ref, alog_ref, p1_ref, p2_ref, *, t_len, chunk):
    x = s_ref[...]
    lanes = lax.broadcasted_iota(jnp.int32, x.shape, 1)
    sp = _softplus(x + bias_ref[...])
    p1_ref[...] = jnp.where(lanes < DNG_LANE, _sigmoid(x), sp)
    g = -jnp.exp(alog_ref[...]) * sp
    if chunk == 1:
        p2_ref[...] = g
    else:
        ii = lax.broadcasted_iota(jnp.int32, (chunk, chunk), 0)
        jj = lax.broadcasted_iota(jnp.int32, (chunk, chunk), 1)
        tril = jnp.where(ii >= jj, 1.0, 0.0).astype(F32)
        for c in range(t_len // chunk):
            p2_ref[c * chunk:(c + 1) * chunk, :] = _dot(tril, g[c * chunk:(c + 1) * chunk, :], HIGHEST)


def _aux(z, bias_vec, alog_vec, n_seq, t_len, chunk):
    m = n_seq * t_len
    blk = pl.BlockSpec((t_len, LANE), lambda b: (b, 0))
    vec = pl.BlockSpec((1, LANE), lambda b: (0, 0))
    out = pl.BlockSpec((t_len, LANE), lambda b: (b, 0))
    return pl.pallas_call(
        functools.partial(_aux_kernel, t_len=t_len, chunk=chunk),
        out_shape=(jax.ShapeDtypeStruct((m, LANE), F32), jax.ShapeDtypeStruct((m, LANE), F32)),
        grid=(n_seq,), in_specs=[blk, vec, vec], out_specs=(out, out),
        compiler_params=_cparams("arbitrary"), name="aux")(z, bias_vec, alog_vec)


_ONES_ROWS = 16


def _attn_prompt_kernel(q_ref, k_ref, v_ref, wq_ref, wk_ref, o_ref, kn_ref, kb_ref, vt_ref, kmean_ref,
                        *, t_len):
    nb = t_len // MOBA_BLOCK
    g = pl.program_id(2)

    @pl.when(g == 0)
    def _():
        k = k_ref[...]
        kn = k * lax.rsqrt(jnp.mean(k * k, axis=-1, keepdims=True) + EPS) * wk_ref[...]
        kn_ref[...] = kn
        kb_ref[...] = kn.astype(BF16)
        for n in range(nb):
            kmean_ref[n:n + 1, :] = jnp.mean(kn[n * MOBA_BLOCK:(n + 1) * MOBA_BLOCK, :], axis=0, keepdims=True)
        vt_ref[0:HEAD_DIM, :] = v_ref[...].T.astype(BF16)
        vt_ref[HEAD_DIM:HEAD_DIM + _ONES_ROWS, :] = jnp.ones((_ONES_ROWS, t_len), BF16)

    q = q_ref[...]
    qn = q * lax.rsqrt(jnp.mean(q * q, axis=-1, keepdims=True) + EPS) * wq_ref[...]
    gate = _dot_nt(kmean_ref[...], qn, HIGHEST)
    blk = lax.broadcasted_iota(jnp.int32, (nb, t_len), 0)
    own = lax.broadcasted_iota(jnp.int32, (nb, t_len), 1) // MOBA_BLOCK
    valid = blk < own
    gm = jnp.where(valid, gate, -jnp.inf)
    cnt = jnp.zeros((nb, t_len), F32)
    for m in range(nb):
        row = gm[m:m + 1, :]
        beats = jnp.where(row > gm, 1.0, jnp.where(row == gm, jnp.where(blk > m, 1.0, 0.0), 0.0))
        cnt = cnt + beats
    bias = jnp.where(valid, jnp.where(cnt < MOBA_TOPK, 0.0, NEG_BIG), NEG_BIG)
    qs = (qn * (HEAD_DIM ** -0.5 * math.log2(math.e))).astype(BF16)
    kk = lax.broadcasted_iota(jnp.int32, (MOBA_BLOCK, MOBA_BLOCK), 0)
    qq = lax.broadcasted_iota(jnp.int32, (MOBA_BLOCK, MOBA_BLOCK), 1)
    causal = jnp.where(kk <= qq, 0.0, NEG_BIG)
    for qi in range(nb):
        n_keys = (qi + 1) * MOBA_BLOCK
        cols = slice(qi * MOBA_BLOCK, (qi + 1) * MOBA_BLOCK)
        st = _dot_nt(kb_ref[0:n_keys, :], qs[cols, :])
        blocks = [st[n * MOBA_BLOCK:(n + 1) * MOBA_BLOCK, :] + bias[n:n + 1, cols] for n in range(qi)]
        blocks.append(st[qi * MOBA_BLOCK:n_keys, :] + causal)
        mx = jnp.max(blocks[0], axis=0, keepdims=True)
        for blk_s in blocks[1:]:
            mx = jnp.maximum(mx, jnp.max(blk_s, axis=0, keepdims=True))
        p = [jnp.exp2(blk_s - mx).astype(BF16) for blk_s in blocks]
        p = jnp.concatenate(p, axis=0) if qi else p[0]
        ot = _dot(vt_ref[:, 0:n_keys], p)
        o = ot[0:HEAD_DIM, :] / ot[HEAD_DIM:HEAD_DIM + 1, :]
        o_ref[cols, :] = o.T.astype(o_ref.dtype)


def _attn_prompt(z, wq, wk, n_seq, t_len):
    m = n_seq * t_len
    qblk = pl.BlockSpec((t_len, HEAD_DIM), lambda b, k, g: (b, Q_OFF // HEAD_DIM + k * ATT_GROUP + g))
    kblk = pl.BlockSpec((t_len, HEAD_DIM), lambda b, k, g: (b, K_OFF // HEAD_DIM + k))
    vblk = pl.BlockSpec((t_len, HEAD_DIM), lambda b, k, g: (b, V_OFF // HEAD_DIM + k))
    wspec = pl.BlockSpec((1, HEAD_DIM), lambda b, k, g: (0, 0))
    return pl.pallas_call(
        functools.partial(_attn_prompt_kernel, t_len=t_len),
        out_shape=(jax.ShapeDtypeStruct((m, ATT_HEADS * HEAD_DIM), BF16),
                   jax.ShapeDtypeStruct((m, ATT_KV_HEADS * HEAD_DIM), F32)),
        grid=(n_seq, ATT_KV_HEADS, ATT_GROUP),
        in_specs=[qblk, kblk, vblk, wspec, wspec],
        out_specs=(pl.BlockSpec((t_len, HEAD_DIM), lambda b, k, g: (b, k * ATT_GROUP + g)),
                   pl.BlockSpec((t_len, HEAD_DIM), lambda b, k, g: (b, k))),
        scratch_shapes=[pltpu.VMEM((t_len, HEAD_DIM), BF16), pltpu.VMEM((HEAD_DIM + _ONES_ROWS, t_len), BF16),
                        pltpu.VMEM((t_len // MOBA_BLOCK, HEAD_DIM), F32)],
        compiler_params=_cparams("arbitrary", "arbitrary", "arbitrary"), name="attn_prompt")(
            z, z, z, wq.reshape(1, HEAD_DIM), wk.reshape(1, HEAD_DIM))


def _tri_inverse(a, c):
    ii = lax.broadcasted_iota(jnp.int32, (c, c), 0)
    jj = lax.broadcasted_iota(jnp.int32, (c, c), 1)
    eye = jnp.where(ii == jj, 1.0, 0.0).astype(F32)
    p = -a
    inv = eye + p
    steps = int(math.log2(c)) - 1
    for _ in range(steps):
        p = _dot(p, p)
        inv = inv + _dot(inv, p)
    return inv


_GDN_UNROLL = 8
_GDN_HEADS_PER_STEP = 2


def _gdn_prompt_kernel(zq_ref, zk_ref, zv_ref, wq_ref, wk_ref, wv_ref, p1_ref, p2_ref, zg_ref, nw_ref,
                       o_ref, s_ref, q_s, k_s, v_s, beta_s, gc_s, qp_s, op_s, mn_s, nn_s, *, t_len):
    hp = pl.program_id(1)
    c = CHUNK
    d = HEAD_DIM

    def l2n(x):
        return x * lax.rsqrt(jnp.sum(x * x, axis=-1, keepdims=True) + EPS)

    ii = lax.broadcasted_iota(jnp.int32, (c, c), 0)
    jj = lax.broadcasted_iota(jnp.int32, (c, c), 1)
    eye = jnp.where(ii == jj, 1.0, 0.0).astype(F32)

    for hh in range(_GDN_HEADS_PER_STEP):
        lanes = slice(hh * d, (hh + 1) * d)
        h = hp * _GDN_HEADS_PER_STEP + hh
        q_s[hh] = l2n(_silu(_causal_conv_rows(zq_ref[:, lanes], wq_ref.at[:, lanes], CONV_W))) * (HEAD_DIM ** -0.5)
        k_s[hh] = l2n(_silu(_causal_conv_rows(zk_ref[:, lanes], wk_ref.at[:, lanes], CONV_W)))
        v_s[hh] = _silu(_causal_conv_rows(zv_ref[:, lanes], wv_ref.at[:, lanes], CONV_W))
        beta_s[hh] = jnp.broadcast_to(_lane_col(p1_ref[...], BETA_LANE + h), (t_len, HEAD_DIM))
        gc_s[hh] = jnp.broadcast_to(_lane_col(p2_ref[...], DNG_LANE + h), (t_len, HEAD_DIM))
        lax.fori_loop(0, t_len // (c * _GDN_UNROLL),
                      functools.partial(_gdn_prepare_group, refs=(q_s.at[hh], k_s.at[hh], v_s.at[hh], beta_s.at[hh],
                                                                  gc_s.at[hh], qp_s.at[hh], op_s.at[hh], mn_s.at[hh],
                                                                  nn_s.at[hh]), consts=(ii, jj, eye)), 0)

    def chunk_step(ci, states):
        r = pl.ds(pl.multiple_of(ci * c, c), c)
        rd = pl.ds(pl.multiple_of(ci * d, d), d)
        ms = [_dot(jnp.concatenate([mn_s[hh, rd, :], qp_s[hh, r, :]], axis=0), states[hh])
              for hh in range(_GDN_HEADS_PER_STEP)]
        new = []
        for hh in range(_GDN_HEADS_PER_STEP):
            op_s[hh, r, :] = ms[hh][d:d + c] + op_s[hh, r, :]
            e = jnp.exp(gc_s[hh, pl.ds(ci * c + c - 1, 1), :])
            new.append(states[hh] * e[:, 0:1] + ms[hh][0:d] + nn_s[hh, rd, :])
        return tuple(new)

    zero = jnp.zeros((HEAD_DIM, HEAD_DIM), F32)
    s_fin = lax.fori_loop(0, t_len // c, chunk_step, (zero,) * _GDN_HEADS_PER_STEP)
    for hh in range(_GDN_HEADS_PER_STEP):
        lanes = slice(hh * d, (hh + 1) * d)
        s_ref[hh] = s_fin[hh]
        o = op_s[hh]
        on = o * lax.rsqrt(jnp.mean(o * o, axis=-1, keepdims=True) + EPS) * nw_ref[...]
        o_ref[:, lanes] = (on * _silu(zg_ref[:, lanes])).astype(o_ref.dtype)


def _gdn_prepare_group(gi, carry, *, refs, consts):
    q_s, k_s, v_s, beta_s, gc_s, qp_s, op_s, mn_s, nn_s = refs
    ii, jj, eye = consts
    c = CHUNK
    d = HEAD_DIM
    n = _GDN_UNROLL
    chunks = [gi * n + u for u in range(n)]
    rows = [pl.ds(pl.multiple_of(ci * c, c), c) for ci in chunks]
    q = [q_s[r, :] for r in rows]
    k = [k_s[r, :] for r in rows]
    gcb = [gc_s[r, :] for r in rows]
    kb = [k[u] * beta_s[rows[u], :] for u in range(n)]
    gam = [_decay_matrix(gcb[u][:, 0:1], c)[0] for u in range(n)]
    kq = [_dot_nt(jnp.concatenate([kb[u], q[u]], axis=0), k[u]) for u in range(n)]
    attn = [kq[u][c:2 * c] * gam[u] for u in range(n)]
    p = [-jnp.where(ii > jj, kq[u][0:c] * gam[u], 0.0) for u in range(n)]
    inv = [eye + p[u] for u in range(n)]
    for _ in range(int(math.log2(c)) - 1):
        p = [_dot(p[u], p[u]) for u in range(n)]
        inv = [inv[u] + _dot(inv[u], p[u]) for u in range(n)]
    eg = [jnp.exp(gcb[u]) for u in range(n)]
    rhs = [jnp.concatenate([v_s[rows[u], :] * beta_s[rows[u], :], kb[u] * eg[u]], axis=1) for u in range(n)]
    uw = [_dot(inv[u], rhs[u]) for u in range(n)]
    auw = [_dot(attn[u], uw[u]) for u in range(n)]
    k_dec = [k[u] * jnp.exp(gcb[u][c - 1:c, :] - gcb[u]) for u in range(n)]
    kuw = [_dot(k_dec[u].T, uw[u]) for u in range(n)]
    for u in range(n):
        rd = pl.ds(pl.multiple_of(chunks[u] * d, d), d)
        op_s[rows[u], :] = auw[u][:, 0:d]
        qp_s[rows[u], :] = eg[u] * q[u] - auw[u][:, d:2 * d]
        nn_s[rd, :] = kuw[u][:, 0:d]
        mn_s[rd, :] = -kuw[u][:, d:2 * d]
    return carry


def _gdn_prompt(z, p1, p2, conv_w, norm_w, n_seq, t_len):
    m = n_seq * t_len
    hps = _GDN_HEADS_PER_STEP
    width = hps * HEAD_DIM
    base = DNQKV_OFF // width

    def zcol(off):
        return pl.BlockSpec((t_len, width), functools.partial(lambda b, h, off: (b, off + h), off=off))

    def wcol(off):
        return pl.BlockSpec((CONV_W, width), functools.partial(lambda b, h, off: (0, off + h), off=off))

    aux = pl.BlockSpec((t_len, LANE), lambda b, h: (b, 0))
    tbuf = pltpu.VMEM((hps, t_len, HEAD_DIM), F32)
    groups = DN_HEADS // hps
    return pl.pallas_call(
        functools.partial(_gdn_prompt_kernel, t_len=t_len),
        out_shape=(jax.ShapeDtypeStruct((m, DN_WIDTH), BF16),
                   jax.ShapeDtypeStruct((n_seq, DN_HEADS, HEAD_DIM, HEAD_DIM), F32)),
        grid=(n_seq, groups),
        in_specs=[zcol(base), zcol(base + groups), zcol(base + 2 * groups),
                  wcol(0), wcol(groups), wcol(2 * groups), aux, aux,
                  zcol(DNZ_OFF // width), pl.BlockSpec((1, HEAD_DIM), lambda b, h: (0, 0))],
        out_specs=(pl.BlockSpec((t_len, width), lambda b, h: (b, h)),
                   pl.BlockSpec((None, hps, HEAD_DIM, HEAD_DIM), lambda b, h: (b, h, 0, 0))),
        scratch_shapes=[tbuf] * 7 + [pltpu.VMEM((hps, t_len // CHUNK * HEAD_DIM, HEAD_DIM), F32)] * 2,
        compiler_params=_cparams("arbitrary", "arbitrary"), name="gdn_prompt")(
            z, z, z, conv_w, conv_w, conv_w, p1, p2, z, norm_w.reshape(1, HEAD_DIM))


_GH = SSD_HEADS // SSD_GROUPS
_GW = _GH * SSD_HEAD_DIM
_SSD_UNROLL = 4


def _ssd_prompt_kernel(zx_ref, zb_ref, zc_ref, wx_ref, wb_ref, wc_ref, bx_ref, bb_ref, bc_ref, p1_ref, p2_ref,
                       zg_ref, d_ref, nw_ref, o_ref, st_ref, x_s, b_s, c_s, y_s, state_s, *, t_len):
    grp = pl.program_id(1)
    c = CHUNK
    x_s[...] = _silu(_causal_conv_rows(zx_ref[...], wx_ref, CONV_W) + bx_ref[...])
    b_s[...] = _silu(_causal_conv_rows(zb_ref[...], wb_ref, CONV_W) + bb_ref[...])
    c_s[...] = _silu(_causal_conv_rows(zc_ref[...], wc_ref, CONV_W) + bc_ref[...])
    state_s[...] = jnp.zeros_like(state_s)

    def chunk_group(gi, carry):
        n = _SSD_UNROLL
        rows = [pl.ds(pl.multiple_of((gi * n + u) * c, c), c) for u in range(n)]
        cm = [c_s[r, :] for r in rows]
        bm = [b_s[r, :] for r in rows]
        cb = [_dot_nt(cm[u], bm[u]) for u in range(n)]
        eac, dec, xs, xv, gam = [], [], [], [], []
        for u in range(n):
            x, p1, p2 = x_s[rows[u], :], p1_ref[rows[u], :], p2_ref[rows[u], :]
            eac_h, dec_h, xs_h, xv_h, gam_h = [], [], [], [], []
            for hh in range(_GH):
                lane = SSD_LANE + grp * _GH + hh
                ac = _lane_col(p2, lane)
                a_last = ac[c - 1:c, :]
                xv_hh = x[:, hh * SSD_HEAD_DIM:(hh + 1) * SSD_HEAD_DIM] * _lane_col(p1, lane)
                gam_h.append(_decay_matrix(ac, c)[0])
                xv_h.append(xv_hh)
                xs_h.append(xv_hh * jnp.exp(a_last - ac))
                eac_h.append(jnp.broadcast_to(jnp.exp(ac), (c, SSD_HEAD_DIM)))
                dec_h.append(jnp.broadcast_to(jnp.exp(a_last), (1, SSD_HEAD_DIM)))
            gam.append(gam_h)
            xv.append(xv_h)
            xs.append(jnp.concatenate(xs_h, axis=1))
            eac.append(jnp.concatenate(eac_h, axis=1))
            dec.append(jnp.concatenate(dec_h, axis=1))
        y_intra = [jnp.concatenate([_dot(cb[u] * gam[u][hh], xv[u][hh]) for hh in range(_GH)], axis=1)
                   for u in range(n)]
        new_states = [_dot(bm[u].T, xs[u]) for u in range(n)]
        s_in = [state_s[...]]
        for u in range(n):
            s_in.append(s_in[u] * dec[u] + new_states[u])
        state_s[...] = s_in[n]
        for u in range(n):
            y_s[rows[u], :] = y_intra[u] + eac[u] * _dot(cm[u], s_in[u])
        return carry

    lax.fori_loop(0, t_len // (c * _SSD_UNROLL), chunk_group, 0)
    for hh in range(_GH):
        st_ref[hh] = state_s[:, hh * SSD_HEAD_DIM:(hh + 1) * SSD_HEAD_DIM]
    y = (y_s[...] + d_ref[...] * x_s[...]) * _silu(zg_ref[...])
    o_ref[...] = (y * lax.rsqrt(jnp.mean(y * y, axis=-1, keepdims=True) + EPS) * nw_ref[...]).astype(o_ref.dtype)


def _ssd_prompt(z, p1, p2, conv_w, conv_b, d_vec, norm_w, n_seq, t_len):
    m = n_seq * t_len
    xb, bb, cbk = SSDX_OFF // _GW, (SSDX_OFF + SSD_INNER) // LANE, (SSDX_OFF + SSD_INNER + 2 * SSD_STATE) // LANE
    wb0, wc0 = SSD_INNER // LANE, (SSD_INNER + 2 * SSD_STATE) // LANE

    def spec(rows, width, off):
        return pl.BlockSpec((rows, width), functools.partial(lambda b, g, off: (0, off + g), off=off))

    def zspec(width, off):
        return pl.BlockSpec((t_len, width), functools.partial(lambda b, g, off: (b, off + g), off=off))

    aux = pl.BlockSpec((t_len, LANE), lambda b, g: (b, 0))
    return pl.pallas_call(
        functools.partial(_ssd_prompt_kernel, t_len=t_len),
        out_shape=(jax.ShapeDtypeStruct((m, SSD_INNER), BF16),
                   jax.ShapeDtypeStruct((n_seq, SSD_HEADS, SSD_STATE, SSD_HEAD_DIM), F32)),
        grid=(n_seq, SSD_GROUPS),
        in_specs=[zspec(_GW, xb), zspec(LANE, bb), zspec(LANE, cbk),
                  spec(CONV_W, _GW, 0), spec(CONV_W, LANE, wb0), spec(CONV_W, LANE, wc0),
                  spec(1, _GW, 0), spec(1, LANE, wb0), spec(1, LANE, wc0),
                  aux, aux, zspec(_GW, SSDZ_OFF // _GW), spec(1, _GW, 0), spec(1, _GW, 0)],
        out_specs=(pl.BlockSpec((t_len, _GW), lambda b, g: (b, g)),
                   pl.BlockSpec((None, _GH, SSD_STATE, SSD_HEAD_DIM), lambda b, g: (b, g, 0, 0))),
        scratch_shapes=[pltpu.VMEM((t_len, _GW), F32), pltpu.VMEM((t_len, LANE), F32), pltpu.VMEM((t_len, LANE), F32),
                        pltpu.VMEM((t_len, _GW), F32), pltpu.VMEM((SSD_STATE, _GW), F32)],
        compiler_params=_cparams("arbitrary", "arbitrary"), name="ssd_prompt")(
            z, z, z, conv_w, conv_w, conv_w, conv_b, conv_b, conv_b, p1, p2, z, d_vec, norm_w.reshape(1, SSD_INNER))


_FFN_TN = 256
_FFN_NJ = FFN_DIM // _FFN_TN
_TAIL = 8


def _ffn_up_prompt_kernel(a_ref, bg_ref, bu_ref, wg_ref, wu_ref, cg_ref, cu_ref, g_ref, sg_ref, su_ref, b_s, y_s,
                          *, tiles_per_seq, n_sub):
    i = pl.program_id(1)
    tn = _FFN_TN
    tm = a_ref.shape[0]

    @pl.when(i == 0)
    def _():
        b_s[:, 0:tn] = bg_ref[...].astype(BF16)
        b_s[:, tn:2 * tn] = bu_ref[...].astype(BF16)

    @pl.when(i % tiles_per_seq == 0)
    def _():
        y_s[0:_TAIL, :] = jnp.zeros((_TAIL, 2 * tn), F32)

    @pl.when(i % tiles_per_seq != 0)
    def _():
        y_s[0:_TAIL, :] = y_s[tm:tm + _TAIL, :]

    w = jnp.concatenate([wg_ref[...], wu_ref[...]], axis=1)
    c = jnp.concatenate([cg_ref[...], cu_ref[...]], axis=1)
    ts = tm // n_sub
    for s in range(n_sub):
        r0 = _TAIL + s * ts
        y = _dot(a_ref[s * ts:(s + 1) * ts, :], b_s[...])
        y_s[r0:r0 + ts, :] = y
        u = w[0:1, :] * y_s[r0 - 2:r0 - 2 + ts, :] + w[1:2, :] * y_s[r0 - 1:r0 - 1 + ts, :] + w[2:3, :] * y + c
        g_ref[s * ts:(s + 1) * ts, :] = (_silu(u[:, 0:tn]) * u[:, tn:2 * tn]).astype(g_ref.dtype)
    sg_ref[...] = y_s[tm:tm + _TAIL, 0:tn]
    su_ref[...] = y_s[tm:tm + _TAIL, tn:2 * tn]


def _ffn_up_prompt(h2, w_up, layer, conv_w, conv_b, n_seq, t_len, tm):
    m = n_seq * t_len
    k = h2.shape[1]
    nj = _FFN_NJ
    tiles_per_seq = t_len // tm
    tail = jax.ShapeDtypeStruct((m // tm, _TAIL, FFN_DIM), F32)
    tail_spec = pl.BlockSpec((None, _TAIL, _FFN_TN), lambda j, i: (i, 0, j))
    return pl.pallas_call(
        functools.partial(_ffn_up_prompt_kernel, tiles_per_seq=tiles_per_seq, n_sub=2),
        out_shape=(jax.ShapeDtypeStruct((m, FFN_DIM), BF16), tail, tail),
        grid=(nj, m // tm),
        in_specs=[pl.BlockSpec((tm, k), lambda j, i: (i, 0)),
                  pl.BlockSpec((None, k, _FFN_TN), lambda j, i: (layer, 0, j)),
                  pl.BlockSpec((None, k, _FFN_TN), lambda j, i: (layer, 0, j + nj)),
                  pl.BlockSpec((FFN_CONV_W, _FFN_TN), lambda j, i: (0, j)),
                  pl.BlockSpec((FFN_CONV_W, _FFN_TN), lambda j, i: (0, j + nj)),
                  pl.BlockSpec((1, _FFN_TN), lambda j, i: (0, j)), pl.BlockSpec((1, _FFN_TN), lambda j, i: (0, j + nj))],
        out_specs=(pl.BlockSpec((tm, _FFN_TN), lambda j, i: (i, j)), tail_spec, tail_spec),
        scratch_shapes=[pltpu.VMEM((k, 2 * _FFN_TN), BF16), pltpu.VMEM((_TAIL + tm, 2 * _FFN_TN), F32)],
        compiler_params=_cparams("arbitrary", "arbitrary"), name="ffn_up_prompt")(
            h2, w_up, w_up, conv_w, conv_w, conv_b, conv_b)


def _ffn_up_sample_kernel(a_ref, bg_ref, bu_ref, wg_ref, wu_ref, cg_ref, cu_ref, s0g_ref, s1g_ref, s0u_ref, s1u_ref,
                          g_ref, yg_ref, yu_ref):
    a = a_ref[...]
    yg = _dot(a, bg_ref[...].astype(BF16))
    yu = _dot(a, bu_ref[...].astype(BF16))
    ug = wg_ref[0:1, :] * s0g_ref[...] + wg_ref[1:2, :] * s1g_ref[...] + wg_ref[2:3, :] * yg + cg_ref[...]
    uu = wu_ref[0:1, :] * s0u_ref[...] + wu_ref[1:2, :] * s1u_ref[...] + wu_ref[2:3, :] * yu + cu_ref[...]
    g_ref[...] = (_silu(ug) * uu).astype(g_ref.dtype)
    yg_ref[...] = yg
    yu_ref[...] = yu


def _ffn_up_sample(h2, w_up, layer, conv_w, conv_b, s0, s1):
    m, k = h2.shape
    nj = _FFN_NJ
    lo = lambda j: (0, j)
    hi = lambda j: (0, j + nj)
    row = lambda f: pl.BlockSpec((m, _FFN_TN), f)
    ysd = jax.ShapeDtypeStruct((m, FFN_DIM), F32)
    return pl.pallas_call(
        _ffn_up_sample_kernel, out_shape=(jax.ShapeDtypeStruct((m, FFN_DIM), BF16), ysd, ysd), grid=(nj,),
        in_specs=[pl.BlockSpec((m, k), lambda j: (0, 0)),
                  pl.BlockSpec((None, k, _FFN_TN), lambda j: (layer, 0, j)),
                  pl.BlockSpec((None, k, _FFN_TN), lambda j: (layer, 0, j + nj)),
                  pl.BlockSpec((FFN_CONV_W, _FFN_TN), lo), pl.BlockSpec((FFN_CONV_W, _FFN_TN), hi),
                  pl.BlockSpec((1, _FFN_TN), lo), pl.BlockSpec((1, _FFN_TN), hi),
                  row(lo), row(lo), row(hi), row(hi)],
        out_specs=(row(lo), row(lo), row(lo)),
        compiler_params=_cparams("arbitrary"), name="ffn_up_sample")(
            h2, w_up, w_up, conv_w, conv_w, conv_b, conv_b, s0, s1, s0, s1)


_PAGES_PER_STEP = 16


def _kmean_kernel(pt_ref, *refs):
    o_ref = refs[-1]
    page = refs[0].shape[0]
    per_block = MOBA_BLOCK // page
    for n in range(_PAGES_PER_STEP // per_block):
        acc = jnp.sum(refs[n * per_block][...], axis=0)
        for r in range(1, per_block):
            acc = acc + jnp.sum(refs[n * per_block + r][...], axis=0)
        o_ref[n] = acc * (1.0 / MOBA_BLOCK)


def _kmean(cache_k, page_table, layer):
    n_b, n_pages = page_table.shape
    page, n_kv, d = cache_k.shape[2:]
    per_block = MOBA_BLOCK // page
    n_blocks = n_pages // per_block
    steps = n_pages // _PAGES_PER_STEP
    in_specs = [pl.BlockSpec((None, None, page, n_kv, d),
                             functools.partial(lambda b, s, pt, r: (layer, pt[b, s * _PAGES_PER_STEP + r], 0, 0, 0), r=r))
                for r in range(_PAGES_PER_STEP)]
    grid_spec = pltpu.PrefetchScalarGridSpec(
        num_scalar_prefetch=1, grid=(n_b, steps), in_specs=in_specs,
        out_specs=pl.BlockSpec((None, _PAGES_PER_STEP // per_block, n_kv, d), lambda b, s, pt: (b, s, 0, 0)))
    return pl.pallas_call(
        _kmean_kernel, out_shape=jax.ShapeDtypeStruct((n_b, n_blocks, n_kv, d), F32), grid_spec=grid_spec,
        compiler_params=_cparams("arbitrary", "arbitrary"), name="kmean")(page_table, *([cache_k] * _PAGES_PER_STEP))


def _select_kernel(q_ref, k_ref, kmean_ref, wq_ref, wk_ref, qn_ref, kn_ref, idx_ref):
    n_b, n_h, _ = q_ref.shape
    n_blocks = kmean_ref.shape[1]
    q = q_ref[...]
    qn = q * lax.rsqrt(jnp.mean(q * q, axis=-1, keepdims=True) + EPS) * wq_ref[...]
    qn_ref[...] = qn
    k = k_ref[...]
    kn_ref[...] = k * lax.rsqrt(jnp.mean(k * k, axis=-1, keepdims=True) + EPS) * wk_ref[...]
    head = lax.broadcasted_iota(jnp.int32, (n_h, n_blocks), 0)
    lane = lax.broadcasted_iota(jnp.int32, (n_h, n_blocks), 1).astype(F32)
    lane_out = lax.broadcasted_iota(jnp.int32, (n_h, LANE), 1)
    for b in range(n_b):
        gate = jnp.zeros((n_h, n_blocks), F32)
        for kv in range(ATT_KV_HEADS):
            gk = _dot_nt(qn[b], kmean_ref[b, :, kv * HEAD_DIM:(kv + 1) * HEAD_DIM], HIGHEST)
            gate = jnp.where(head // ATT_GROUP == kv, gk, gate)
        out = jnp.zeros((n_h, LANE), F32)
        for s in range(MOBA_TOPK):
            mx = jnp.max(gate, axis=-1, keepdims=True)
            pick = jnp.min(jnp.where(gate == mx, lane, float(n_blocks)), axis=-1, keepdims=True)
            out = jnp.where(lane_out == s, pick, out)
            gate = jnp.where(lane == pick, -jnp.inf, gate)
        idx_ref[b] = out.astype(jnp.int32)


def _select(q3, k3, kmean, wq, wk):
    n_b, n_h, d = q3.shape
    full = lambda shape: pl.BlockSpec(shape, lambda i: (0,) * len(shape))
    return pl.pallas_call(
        _select_kernel,
        out_shape=(jax.ShapeDtypeStruct(q3.shape, F32), jax.ShapeDtypeStruct(k3.shape, F32),
                   jax.ShapeDtypeStruct((n_b, n_h, LANE), jnp.int32)),
        grid=(1,),
        in_specs=[full(q3.shape), full(k3.shape), full(kmean.shape), full((1, 1, d)), full((1, 1, d))],
        out_specs=(full(q3.shape), full(k3.shape), full((n_b, n_h, LANE))),
        compiler_params=_cparams("arbitrary"), name="select")(q3, k3, kmean, wq.reshape(1, 1, d), wk.reshape(1, 1, d))


def _attn_sample_kernel(pt_ref, idx_ref, qn_ref, kn_ref, vn_ref, *refs, n_sel, per_block):
    o_ref = refs[-1]
    pages = refs[:-1]
    n_pg = n_sel * per_block
    k_refs, v_refs = pages[:n_pg], pages[n_pg:]
    h = pl.program_id(1)
    kv = h // ATT_GROUP
    q = qn_ref[pl.ds(h, 1), :] * (HEAD_DIM ** -0.5)
    k_new = kn_ref[pl.ds(kv, 1), :]
    v_new = vn_ref[pl.ds(kv, 1), :]
    s_own = jnp.sum(q * k_new, axis=-1, keepdims=True)
    n_rows = k_refs[0].shape[0]
    mine = lax.broadcasted_iota(jnp.int32, (n_rows, 1), 0) % ATT_KV_HEADS == kv
    logits = [jnp.where(mine, jnp.sum(r[...] * q, axis=-1, keepdims=True), NEG_BIG) for r in k_refs]
    mx = s_own
    for s in logits:
        mx = jnp.maximum(mx, jnp.max(s, axis=0, keepdims=True))
    p_own = jnp.exp(s_own - mx)
    den = p_own
    acc = p_own * v_new
    for s, v_ref in zip(logits, v_refs):
        p = jnp.exp(s - mx)
        den = den + jnp.sum(p, axis=0, keepdims=True)
        acc = acc + jnp.sum(p * v_ref[...], axis=0, keepdims=True)
    o_ref[pl.ds(h, 1), :] = acc / den


def _attn_sample(cache_k, cache_v, page_table, idx, qn, kn, vn, layer):
    n_b, n_h, d = qn.shape
    depth, n_pool, page, n_kv = cache_k.shape[:4]
    per_block = MOBA_BLOCK // page
    n_sel = idx.shape[2]
    cache_k = cache_k.reshape(depth, n_pool, page * n_kv, d)
    cache_v = cache_v.reshape(depth, n_pool, page * n_kv, d)

    def page_spec(s, r):
        def imap(b, h, pt, ix):
            return (layer, pt[b, ix[b, h * n_sel + s] * per_block + r], 0, 0)
        return pl.BlockSpec((None, None, page * n_kv, d), imap)

    pspecs = [page_spec(s, r) for s in range(n_sel) for r in range(per_block)]
    slab = lambda n: pl.BlockSpec((None, n, d), lambda b, h, pt, ix: (b, 0, 0))
    grid_spec = pltpu.PrefetchScalarGridSpec(
        num_scalar_prefetch=2, grid=(n_b, n_h),
        in_specs=[slab(n_h), slab(ATT_KV_HEADS), slab(ATT_KV_HEADS)] + pspecs + pspecs,
        out_specs=slab(n_h))
    n_pg = len(pspecs)
    return pl.pallas_call(
        functools.partial(_attn_sample_kernel, n_sel=n_sel, per_block=per_block),
        out_shape=jax.ShapeDtypeStruct((n_b, n_h, d), F32), grid_spec=grid_spec,
        compiler_params=_cparams("arbitrary", "arbitrary"), name="attn_sample")(
            page_table, idx.reshape(n_b, n_h * n_sel), qn, kn, vn, *([cache_k] * n_pg), *([cache_v] * n_pg))


def _rows_to_cols(x):
    r, n = x.shape
    return jnp.concatenate([x, jnp.zeros((n - r, n), x.dtype)], axis=0).T


def _mix_sample_kernel(dx_ref, dbuf_ref, dw_ref, dbeta_ref, dg_ref, dz_ref, dnw_ref, dst_ref,
                       sx_ref, sxbuf_ref, sxw_ref, sxb_ref, bc_ref, bcbuf_ref, bcw_ref, bcb_ref,
                       sdt_ref, sa_ref, sz_ref, sd_ref, snw_ref, sst_ref,
                       dno_ref, dnst_ref, so_ref, sso_ref, y_s):
    x = dx_ref[...]
    conv = x * dw_ref[CONV_W - 1]
    for i in range(CONV_W - 1):
        conv = conv + dbuf_ref[i] * dw_ref[i]
    act = _silu(conv)
    nh = DN_HEADS
    q, k, v = act[0:nh], act[nh:2 * nh], act[2 * nh:3 * nh]
    q = q * lax.rsqrt(jnp.sum(q * q, axis=-1, keepdims=True) + EPS) * (HEAD_DIM ** -0.5)
    k = k * lax.rsqrt(jnp.sum(k * k, axis=-1, keepdims=True) + EPS)
    beta = dbeta_ref[...]
    eg = jnp.exp(dg_ref[...])
    qk = jnp.sum(q * k, axis=-1, keepdims=True)
    qt, kt = _rows_to_cols(q), _rows_to_cols(k)
    o_rows = []
    for h in range(nh):
        s0 = dst_ref[h]
        kcol, qcol = kt[:, h:h + 1], qt[:, h:h + 1]
        ks = jnp.sum(kcol * s0, axis=0, keepdims=True)
        qs = jnp.sum(qcol * s0, axis=0, keepdims=True)
        b_h, e_h = beta[h:h + 1, :], eg[h:h + 1, :]
        v_new = v[h:h + 1, :] * b_h - (b_h * e_h) * ks
        o_rows.append(e_h * qs + qk[h:h + 1, :] * v_new)
        dnst_ref[h] = s0 * e_h + kcol * v_new
    o = jnp.concatenate(o_rows, axis=0)
    on = o * lax.rsqrt(jnp.mean(o * o, axis=-1, keepdims=True) + EPS) * dnw_ref[...]
    dno_ref[...] = on * _silu(dz_ref[...])
    xs = sx_ref[...] * sxw_ref[CONV_W - 1] + sxb_ref[...]
    bc = bc_ref[...] * bcw_ref[CONV_W - 1] + bcb_ref[...]
    for i in range(CONV_W - 1):
        xs = xs + sxbuf_ref[i] * sxw_ref[i]
        bc = bc + bcbuf_ref[i] * bcw_ref[i]
    xs = _silu(xs)
    bc = _silu(bc)
    bct = _rows_to_cols(bc)
    cb = jnp.sum(bc[0:SSD_GROUPS] * bc[SSD_GROUPS:2 * SSD_GROUPS], axis=-1, keepdims=True)
    dt = sdt_ref[...]
    ea = jnp.exp(sa_ref[...])
    for h in range(SSD_HEADS):
        grp = h // _GH
        s0 = sst_ref[h]
        bcol, ccol = bct[:, grp:grp + 1], bct[:, SSD_GROUPS + grp:SSD_GROUPS + grp + 1]
        xv = xs[h:h + 1, :] * dt[h:h + 1, :]
        e_h = ea[h:h + 1, :]
        y_s[h:h + 1, :] = cb[grp:grp + 1, :] * xv + e_h * jnp.sum(ccol * s0, axis=0, keepdims=True)
        sso_ref[h] = s0 * e_h + bcol * xv
    y = (y_s[...] + sd_ref[...] * xs) * _silu(sz_ref[...])
    norm_rows = []
    for grp in range(SSD_GROUPS):
        yg = y[grp * _GH:(grp + 1) * _GH]
        ms = jnp.sum(jnp.sum(yg * yg, axis=-1, keepdims=True), axis=0, keepdims=True) * (1.0 / _GW)
        norm_rows.append(yg * lax.rsqrt(ms + EPS))
    so_ref[...] = jnp.concatenate(norm_rows, axis=0) * snw_ref[...]


def _mix_sample(args, n_b):
    def per_b(shape):
        nd = len(shape) - 1
        return pl.BlockSpec((None,) + tuple(shape[1:]), lambda b: (b,) + (0,) * nd)

    def shared(shape):
        nd = len(shape)
        return pl.BlockSpec(tuple(shape), lambda b: (0,) * nd)

    names_per_b = {"dx", "dbuf", "dbeta", "dg", "dz", "dst", "sx", "sxbuf", "bc", "bcbuf", "sdt", "sa", "sz", "sst"}
    order = ["dx", "dbuf", "dw", "dbeta", "dg", "dz", "dnw", "dst", "sx", "sxbuf", "sxw", "sxb", "bc", "bcbuf", "bcw",
             "bcb", "sdt", "sa", "sz", "sd", "snw", "sst"]
    in_specs = [per_b(args[n].shape) if n in names_per_b else shared(args[n].shape) for n in order]
    outs = (jax.ShapeDtypeStruct((n_b, DN_HEADS, HEAD_DIM), F32),
            jax.ShapeDtypeStruct((n_b, DN_HEADS, HEAD_DIM, HEAD_DIM), F32),
            jax.ShapeDtypeStruct((n_b, SSD_HEADS, SSD_HEAD_DIM), F32),
            jax.ShapeDtypeStruct((n_b, SSD_HEADS, SSD_STATE, SSD_HEAD_DIM), F32))
    return pl.pallas_call(
        _mix_sample_kernel, out_shape=outs, grid=(n_b,), in_specs=in_specs,
        out_specs=tuple(per_b(o.shape) for o in outs),
        scratch_shapes=[pltpu.VMEM((SSD_HEADS, SSD_HEAD_DIM), F32)],
        compiler_params=_cparams("arbitrary"), name="mix_sample")(*[args[n] for n in order])


def _split_w_in(w_in):
    depth, k, _ = w_in.shape
    wt = jnp.swapaxes(w_in, 1, 2)
    wt_b = wt[:, ZB_SRC:ZB_SRC + ZB_COLS]
    wt_s = jnp.concatenate([wt[:, ZS_SRC[0]:ZS_SRC[0] + 16], wt[:, ZS_SRC[1]:ZS_SRC[1] + 16],
                            jnp.zeros((depth, LANE - 32, k), w_in.dtype)], axis=1)
    return wt, wt_b, wt_s


def _in_proj_all(h, lw, layer, tm, suffix=""):
    za = _in_proj(h, lw["wt_a"], layer, ZA_COLS, tm=tm, tn=512, name="in_proj_a" + suffix)
    zb = _in_proj(h, lw["wt_b"], layer, ZB_COLS, tm=tm, tn=512, name="in_proj_b" + suffix)
    zs = _in_proj(h, lw["wt_s"], layer, LANE, tm=tm, tn=LANE, name="in_proj_s" + suffix)
    return za, zb, zs


def _lane_vec(dn_vals, ssd_vals):
    v = jnp.zeros((LANE,), F32)
    v = v.at[DNG_LANE:DNG_LANE + DN_HEADS].set(dn_vals.astype(F32))
    v = v.at[SSD_LANE:SSD_LANE + SSD_HEADS].set(ssd_vals.astype(F32))
    return v.reshape(1, LANE)


def _ffn_down(g, w_down, layer, x, tm):
    return _matmul([(g, FFN_DIM, 0, 0)], w_down, layer, tm=tm, tn=256, res=x, a_single_buffer=tm >= 512,
                   name="ffn_down")


def _prompt_layer(x, lw, layer, n_seq, t_len):
    tm = 1024
    h = _rmsnorm_cast(x, lw["norm1_w"], 512)
    za, zb, zs = _in_proj_all(h, lw, layer, tm)
    p1, p2 = _aux(zs, lw["bias_vec"], lw["alog_vec"], n_seq, t_len, CHUNK)
    o_att, k_norm = _attn_prompt(za, lw["attn_q_norm_w"], lw["attn_k_norm_w"], n_seq, t_len)
    o_dn, dn_state = _gdn_prompt(za, p1, p2, lw["dn_conv_w"], lw["dn_norm_w"], n_seq, t_len)
    o_ssd, ssd_state = _ssd_prompt(zb, p1, p2, lw["ssd_conv_w"], lw["ssd_conv_b"].reshape(1, -1), lw["ssd_d_vec"],
                                   lw["ssd_norm_w"], n_seq, t_len)
    x = _matmul([(o_att, 2048, 0, 0), (o_dn, 1024, 0, 2), (o_ssd, 1024, 0, 3)], lw["w_out"], layer, tm=tm, tn=512,
                res=x, name="out_proj")
    h2 = _rmsnorm_cast(x, lw["norm2_w"], 512)
    g, tail_g, tail_u = _ffn_up_prompt(h2, lw["ffn_w_up"], layer, lw["ffn_conv_w"], lw["ffn_conv_b"].reshape(1, -1),
                                       n_seq, t_len, tm)
    x = _ffn_down(g, lw["ffn_w_down"], layer, x, tm)
    za3 = za.reshape(n_seq, t_len, ZA_COLS)
    zb3 = zb.reshape(n_seq, t_len, ZB_COLS)
    tps = t_len // tm
    new = (k_norm.reshape(n_seq, t_len, ATT_KV_HEADS, HEAD_DIM),
           za3[:, :, V_OFF:V_OFF + 512].reshape(n_seq, t_len, ATT_KV_HEADS, HEAD_DIM),
           za3[:, t_len - (CONV_W - 1):, DNQKV_OFF:DNQKV_OFF + DN_CONV_CH],
           dn_state,
           zb3[:, t_len - (CONV_W - 1):, SSDX_OFF:SSDX_OFF + SSD_CONV_CH],
           ssd_state,
           jnp.concatenate([tail_g[tps - 1::tps, _TAIL - 2:], tail_u[tps - 1::tps, _TAIL - 2:]], axis=-1))
    return x, new


_SAMPLE_ROWS = 16


def _sample_layer(x, lw, layer, cache_k, cache_v, page_table, dn_conv_buf, dn_state, ssd_conv_buf, ssd_state,
                  ffn_conv_buf):
    n_b = page_table.shape[0]
    rows = x.shape[0]
    h = _rmsnorm_cast(x, lw["norm1_w"], rows)
    za, zb2, zs = _in_proj_all(h, lw, layer, rows, "_smp")
    p1, p2 = _aux(zs, lw["bias_vec"], lw["alog_vec"], 1, rows, 1)
    zb = za[:n_b]
    zbs = zb2[:n_b]
    kmean = _kmean(cache_k, page_table, layer)
    kmean = kmean.reshape(n_b, kmean.shape[1], ATT_KV_HEADS * HEAD_DIM)
    q3 = zb[:, Q_OFF:Q_OFF + 2048].reshape(n_b, ATT_HEADS, HEAD_DIM)
    k3 = zb[:, K_OFF:K_OFF + 512].reshape(n_b, ATT_KV_HEADS, HEAD_DIM)
    v3 = zb[:, V_OFF:V_OFF + 512].reshape(n_b, ATT_KV_HEADS, HEAD_DIM)
    qn, kn, idx = _select(q3, k3, kmean, lw["attn_q_norm_w"], lw["attn_k_norm_w"])
    o_att = _attn_sample(cache_k, cache_v, page_table, idx[:, :, :MOBA_TOPK], qn, kn, v3, layer)
    dn_x = zb[:, DNQKV_OFF:DNQKV_OFF + DN_CONV_CH]
    ssd_x = zbs[:, SSDX_OFF:SSDX_OFF + SSD_CONV_CH]
    nh3 = 3 * DN_HEADS
    ng2 = 2 * SSD_GROUPS
    args = {
        "dx": dn_x.reshape(n_b, nh3, HEAD_DIM),
        "dbuf": dn_conv_buf.reshape(n_b, CONV_W - 1, nh3, HEAD_DIM),
        "dw": lw["dn_conv_w"].reshape(CONV_W, nh3, HEAD_DIM),
        "dbeta": p1[:n_b, BETA_LANE:BETA_LANE + DN_HEADS].reshape(n_b, DN_HEADS, 1),
        "dg": p2[:n_b, DNG_LANE:DNG_LANE + DN_HEADS].reshape(n_b, DN_HEADS, 1),
        "dz": zb[:, DNZ_OFF:DNZ_OFF + DN_WIDTH].reshape(n_b, DN_HEADS, HEAD_DIM),
        "dnw": lw["dn_norm_w"].reshape(1, HEAD_DIM),
        "dst": dn_state,
        "sx": ssd_x[:, :SSD_INNER].reshape(n_b, SSD_HEADS, SSD_HEAD_DIM),
        "sxbuf": ssd_conv_buf[:, :, :SSD_INNER].reshape(n_b, CONV_W - 1, SSD_HEADS, SSD_HEAD_DIM),
        "sxw": lw["ssd_conv_w"][:, :SSD_INNER].reshape(CONV_W, SSD_HEADS, SSD_HEAD_DIM),
        "sxb": lw["ssd_conv_b"][:SSD_INNER].reshape(SSD_HEADS, SSD_HEAD_DIM),
        "bc": ssd_x[:, SSD_INNER:].reshape(n_b, ng2, SSD_STATE),
        "bcbuf": ssd_conv_buf[:, :, SSD_INNER:].reshape(n_b, CONV_W - 1, ng2, SSD_STATE),
        "bcw": lw["ssd_conv_w"][:, SSD_INNER:].reshape(CONV_W, ng2, SSD_STATE),
        "bcb": lw["ssd_conv_b"][SSD_INNER:].reshape(ng2, SSD_STATE),
        "sdt": p1[:n_b, SSD_LANE:SSD_LANE + SSD_HEADS].reshape(n_b, SSD_HEADS, 1),
        "sa": p2[:n_b, SSD_LANE:SSD_LANE + SSD_HEADS].reshape(n_b, SSD_HEADS, 1),
        "sz": zbs[:, SSDZ_OFF:SSDZ_OFF + SSD_INNER].reshape(n_b, SSD_HEADS, SSD_HEAD_DIM),
        "sd": lw["ssd_D"].astype(F32).reshape(SSD_HEADS, 1),
        "snw": lw["ssd_norm_w"].reshape(SSD_HEADS, SSD_HEAD_DIM),
        "sst": ssd_state,
    }
    o_dn, dn_state_new, o_ssd, ssd_state_new = _mix_sample(args, n_b)
    pad = lambda a: jnp.pad(a.reshape(n_b, -1), ((0, rows - n_b), (0, 0))).astype(BF16)
    x = _matmul([(pad(o_att), 2048, 0, 0), (pad(o_dn), 1024, 0, 2), (pad(o_ssd), 1024, 0, 3)], lw["w_out"], layer,
                tm=rows, tn=512, res=x, name="out_proj_s")
    h2 = _rmsnorm_cast(x, lw["norm2_w"], rows)
    padf = lambda a: jnp.pad(a, ((0, rows - n_b), (0, 0)))
    g, y_g, y_u = _ffn_up_sample(h2, lw["ffn_w_up"], layer, lw["ffn_conv_w"], lw["ffn_conv_b"].reshape(1, -1),
                                 padf(ffn_conv_buf[:, 0]), padf(ffn_conv_buf[:, 1]))
    x = _ffn_down(g, lw["ffn_w_down"], layer, x, rows)
    y_new = jnp.concatenate([y_g[:n_b], y_u[:n_b]], axis=-1)
    new = (kn.reshape(n_b, 1, ATT_KV_HEADS, HEAD_DIM), v3.reshape(n_b, 1, ATT_KV_HEADS, HEAD_DIM),
           jnp.concatenate([dn_conv_buf[:, 1:], dn_x[:, None]], axis=1), dn_state_new,
           jnp.concatenate([ssd_conv_buf[:, 1:], ssd_x[:, None]], axis=1), ssd_state_new,
           jnp.stack([ffn_conv_buf[:, 1], y_new], axis=1))
    return x, new


def kernel(x_prompt, x_sample, cache_k, cache_v, page_table, state_dn_conv, state_dn, state_ssd_conv, state_ssd, state_ffn_conv, norm1_w, w_in, attn_q_norm_w, attn_k_norm_w, dn_conv_w, dn_A_log, dn_dt_bias, dn_norm_w, ssd_conv_w, ssd_conv_b, ssd_dt_bias, ssd_A_log, ssd_D, ssd_norm_w, w_out, norm2_w, ffn_w_up, ffn_conv_w, ffn_conv_b, ffn_w_down):
    depth = w_in.shape[0]
    n_seq, t_len, d_model = x_prompt.shape
    n_b = x_sample.shape[0]
    yp = x_prompt.reshape(n_seq * t_len, d_model)
    ys = jnp.pad(x_sample.reshape(n_b, d_model), ((0, _SAMPLE_ROWS - n_b), (0, 0)))
    wt_a, wt_b, wt_s = _split_w_in(w_in)
    outs_p, outs_s = [], []
    for l in range(depth):
        lw = {"norm1_w": norm1_w[l], "wt_a": wt_a, "wt_b": wt_b, "wt_s": wt_s, "attn_q_norm_w": attn_q_norm_w[l],
              "attn_k_norm_w": attn_k_norm_w[l], "dn_conv_w": dn_conv_w[l], "dn_norm_w": dn_norm_w[l],
              "ssd_conv_w": ssd_conv_w[l], "ssd_conv_b": ssd_conv_b[l], "ssd_D": ssd_D[l],
              "ssd_norm_w": ssd_norm_w[l], "w_out": w_out, "norm2_w": norm2_w[l], "ffn_w_up": ffn_w_up,
              "ffn_conv_w": ffn_conv_w[l], "ffn_conv_b": ffn_conv_b[l], "ffn_w_down": ffn_w_down,
              "bias_vec": _lane_vec(dn_dt_bias[l], ssd_dt_bias[l]),
              "alog_vec": _lane_vec(dn_A_log[l], ssd_A_log[l]),
              "ssd_d_vec": jnp.repeat(ssd_D[l].astype(F32), SSD_HEAD_DIM).reshape(1, SSD_INNER)}
        yp, new_p = _prompt_layer(yp, lw, l, n_seq, t_len)
        ys, new_s = _sample_layer(ys, lw, l, cache_k, cache_v, page_table, state_dn_conv[l], state_dn[l],
                                  state_ssd_conv[l], state_ssd[l], state_ffn_conv[l])
        outs_p.append(new_p)
        outs_s.append(new_s)
    st = lambda outs, i: jnp.stack([o[i] for o in outs])
    return (yp.reshape(n_seq, t_len, d_model), ys[:n_b].reshape(n_b, 1, d_model),
            st(outs_p, 0), st(outs_p, 1), st(outs_s, 0), st(outs_s, 1),
            st(outs_p, 2), st(outs_s, 2), st(outs_p, 3), st(outs_s, 3),
            st(outs_p, 4), st(outs_s, 4), st(outs_p, 5), st(outs_s, 5),
            st(outs_p, 6), st(outs_s, 6))
```

```python
import functools
import math

import jax
import jax.numpy as jnp
from jax import lax
from jax.experimental import pallas as pl
from jax.experimental.pallas import tpu as pltpu

F32 = jnp.float32
BF16 = jnp.bfloat16
HIGHEST = lax.Precision.HIGHEST

D_MODEL = 4096
HEAD_DIM = 128
ATT_HEADS = 16
ATT_KV_HEADS = 4
ATT_GROUP = 4
MOBA_BLOCK = 256
MOBA_TOPK = 3
DN_HEADS = 8
DN_WIDTH = 1024
DN_CONV_CH = 3072
SSD_INNER = 1024
SSD_HEAD_DIM = 64
SSD_HEADS = 16
SSD_GROUPS = 2
SSD_STATE = 128
SSD_CONV_CH = 1536
CONV_W = 4
CHUNK = 64
FFN_DIM = 11008
FFN_CONV_W = 3
EPS = 1e-6

Q_OFF, K_OFF, V_OFF = 0, 2048, 2560
DNQKV_OFF, DNZ_OFF = 3072, 6144
ZA_COLS = 7168
SSDX_OFF, SSDZ_OFF = 0, 1536
ZB_SRC, ZB_COLS = 7184, 2560
ZS_SRC = (7168, 9744)
LANE = 128
BETA_LANE, DNG_LANE, SSD_LANE = 0, 8, 16

VMEM_LIMIT_BYTES = 56 * 1024 * 1024
NEG_BIG = -1e30


def _cparams(*sem):
    return pltpu.CompilerParams(dimension_semantics=sem, vmem_limit_bytes=VMEM_LIMIT_BYTES)


def _silu(x):
    return x / (1.0 + jnp.exp(-x))


def _sigmoid(x):
    return 1.0 / (1.0 + jnp.exp(-x))


def _softplus(x):
    return jnp.maximum(x, 0.0) + jnp.log1p(jnp.exp(-jnp.abs(x)))


def _dot(a, b, precision=None):
    return jnp.dot(a, b, preferred_element_type=F32, precision=precision)


def _dot_nt(a, b, precision=None):
    return lax.dot_general(a, b, (((1,), (1,)), ((), ())), preferred_element_type=F32, precision=precision)


def _shift_rows(x, s):
    y = pltpu.roll(x, s, axis=0)
    head_rows = lax.broadcasted_iota(jnp.int32, (8, x.shape[1]), 0)
    return jnp.concatenate([jnp.where(head_rows < s, 0.0, y[0:8]), y[8:]], axis=0)


def _causal_conv_rows(x, w_ref, width):
    y = x * w_ref[width - 1:width, :]
    for i in range(width - 1):
        y = y + _shift_rows(x, width - 1 - i) * w_ref[i:i + 1, :]
    return y


def _lane_col(x, lane):
    lanes = lax.broadcasted_iota(jnp.int32, x.shape, 1)
    return jnp.sum(jnp.where(lanes == lane, x, 0.0), axis=-1, keepdims=True)


def _decay_matrix(gc_col, c):
    ii = lax.broadcasted_iota(jnp.int32, (c, c), 0)
    jj = lax.broadcasted_iota(jnp.int32, (c, c), 1)
    gcb = jnp.broadcast_to(gc_col, (c, c))
    gc_row = jnp.sum(jnp.where(ii == jj, gcb, 0.0), axis=0, keepdims=True)
    low = ii >= jj
    gam = jnp.where(low, jnp.exp(jnp.where(low, gcb - gc_row, 0.0)), 0.0)
    return gam, ii, jj


def _rmsnorm_kernel(x_ref, w_ref, o_ref):
    x = x_ref[...]
    ms = jnp.mean(x * x, axis=-1, keepdims=True)
    o_ref[...] = (x * lax.rsqrt(ms + EPS) * w_ref[...]).astype(o_ref.dtype)


def _rmsnorm_cast(x, w, tm):
    m, d = x.shape
    return pl.pallas_call(
        _rmsnorm_kernel, out_shape=jax.ShapeDtypeStruct((m, d), BF16), grid=(m // tm,),
        in_specs=[pl.BlockSpec((tm, d), lambda i: (i, 0)), pl.BlockSpec((1, d), lambda i: (0, 0))],
        out_specs=pl.BlockSpec((tm, d), lambda i: (i, 0)),
        compiler_params=_cparams("arbitrary"), name="rmsnorm")(x, w.reshape(1, d))


def _matmul_kernel(*refs, n_parts, has_res):
    a_refs = refs[:n_parts]
    b_refs = refs[n_parts:2 * n_parts]
    o_ref = refs[-1]
    acc = None
    for a_ref, b_ref in zip(a_refs, b_refs):
        d = _dot(a_ref[...], b_ref[...].astype(BF16))
        acc = d if acc is None else acc + d
    if has_res:
        acc = acc + refs[2 * n_parts][...]
    o_ref[...] = acc.astype(o_ref.dtype)


def _matmul(a_parts, b, layer, *, tm, tn, res=None, out_dtype=F32, a_single_buffer=False, name="matmul"):
    m = a_parts[0][0].shape[0]
    n = b.shape[2]
    in_specs, args = [], []
    a_mode = dict(pipeline_mode=pl.Buffered(1)) if a_single_buffer else {}
    for arr, kp, cb, rb in a_parts:
        in_specs.append(pl.BlockSpec((tm, kp), functools.partial(lambda i, j, cb: (i, cb), cb=cb), **a_mode))
        args.append(arr)
    for arr, kp, cb, rb in a_parts:
        in_specs.append(pl.BlockSpec((None, kp, tn), functools.partial(lambda i, j, rb: (layer, rb, j), rb=rb)))
        args.append(b)
    if res is not None:
        in_specs.append(pl.BlockSpec((tm, tn), lambda i, j: (i, j)))
        args.append(res)
    return pl.pallas_call(
        functools.partial(_matmul_kernel, n_parts=len(a_parts), has_res=res is not None),
        out_shape=jax.ShapeDtypeStruct((m, n), out_dtype), grid=(m // tm, n // tn),
        in_specs=in_specs, out_specs=pl.BlockSpec((tm, tn), lambda i, j: (i, j)),
        compiler_params=_cparams("arbitrary", "arbitrary"), name=name)(*args)


def _in_proj_kernel(a_ref, as_ref, w_ref, o_ref, os_ref, b_s):
    @pl.when(pl.program_id(1) == 0)
    def _():
        b_s[...] = w_ref[...].astype(BF16)
        os_ref[...] = _dot_nt(as_ref[...], b_s[...])

    o_ref[...] = _dot_nt(a_ref[...], b_s[...])


def _in_proj(h, hs, wt, layer, n, *, tm, tn, name):
    m, k = h.shape
    ms = hs.shape[0]
    return pl.pallas_call(
        _in_proj_kernel,
        out_shape=(jax.ShapeDtypeStruct((m, n), F32), jax.ShapeDtypeStruct((ms, n), F32)), grid=(n // tn, m // tm),
        in_specs=[pl.BlockSpec((tm, k), lambda j, i: (i, 0)), pl.BlockSpec((ms, k), lambda j, i: (0, 0)),
                  pl.BlockSpec((None, tn, k), lambda j, i: (layer, j, 0))],
        out_specs=(pl.BlockSpec((tm, tn), lambda j, i: (i, j)), pl.BlockSpec((ms, tn), lambda j, i: (0, j))),
        scratch_shapes=[pltpu.VMEM((tn, k), BF16)],
        compiler_params=_cparams("arbitrary", "arbitrary"), name=name)(h, hs, wt)


def _aux_kernel(s_ref, bias_ref, alog_ref, p1_ref, p2_ref, *, t_len, chunk):
    x = s_ref[...]
    lanes = lax.broadcasted_iota(jnp.int32, x.shape, 1)
    sp = _softplus(x + bias_ref[...])
    p1_ref[...] = jnp.where(lanes < DNG_LANE, _sigmoid(x), sp)
    g = -jnp.exp(alog_ref[...]) * sp
    if chunk == 1:
        p2_ref[...] = g
    else:
        ii = lax.broadcasted_iota(jnp.int32, (chunk, chunk), 0)
        jj = lax.broadcasted_iota(jnp.int32, (chunk, chunk), 1)
        tril = jnp.where(ii >= jj, 1.0, 0.0).astype(F32)
        for c in range(t_len // chunk):
            p2_ref[c * chunk:(c + 1) * chunk, :] = _dot(tril, g[c * chunk:(c + 1) * chunk, :], HIGHEST)


def _aux(z, bias_vec, alog_vec, n_seq, t_len, chunk):
    m = n_seq * t_len
    blk = pl.BlockSpec((t_len, LANE), lambda b: (b, 0))
    vec = pl.BlockSpec((1, LANE), lambda b: (0, 0))
    out = pl.BlockSpec((t_len, LANE), lambda b: (b, 0))
    return pl.pallas_call(
        functools.partial(_aux_kernel, t_len=t_len, chunk=chunk),
        out_shape=(jax.ShapeDtypeStruct((m, LANE), F32), jax.ShapeDtypeStruct((m, LANE), F32)),
        grid=(n_seq,), in_specs=[blk, vec, vec], out_specs=(out, out),
        compiler_params=_cparams("arbitrary"), name="aux")(z, bias_vec, alog_vec)


_ONES_ROWS = 16


def _attn_prompt_kernel(q_ref, k_ref, v_ref, wq_ref, wk_ref, o_ref, kn_ref, kb_ref, vt_ref, kmean_ref,
                        *, t_len):
    nb = t_len // MOBA_BLOCK
    g = pl.program_id(2)

    @pl.when(g == 0)
    def _():
        k = k_ref[...]
        kn = k * lax.rsqrt(jnp.mean(k * k, axis=-1, keepdims=True) + EPS) * wk_ref[...]
        kn_ref[...] = kn
        kb_ref[...] = kn.astype(BF16)
        for n in range(nb):
            kmean_ref[n:n + 1, :] = jnp.mean(kn[n * MOBA_BLOCK:(n + 1) * MOBA_BLOCK, :], axis=0, keepdims=True)
        vt_ref[0:HEAD_DIM, :] = v_ref[...].T.astype(BF16)
        vt_ref[HEAD_DIM:HEAD_DIM + _ONES_ROWS, :] = jnp.ones((_ONES_ROWS, t_len), BF16)

    q = q_ref[...]
    qn = q * lax.rsqrt(jnp.mean(q * q, axis=-1, keepdims=True) + EPS) * wq_ref[...]
    gate = _dot_nt(kmean_ref[...], qn, HIGHEST)
    blk = lax.broadcasted_iota(jnp.int32, (nb, t_len), 0)
    own = lax.broadcasted_iota(jnp.int32, (nb, t_len), 1) // MOBA_BLOCK
    valid = blk < own
    gm = jnp.where(valid, gate, -jnp.inf)
    cnt = jnp.zeros((nb, t_len), F32)
    for m in range(nb):
        row = gm[m:m + 1, :]
        beats = jnp.where(row > gm, 1.0, jnp.where(row == gm, jnp.where(blk > m, 1.0, 0.0), 0.0))
        cnt = cnt + beats
    bias = jnp.where(valid, jnp.where(cnt < MOBA_TOPK, 0.0, NEG_BIG), NEG_BIG)
    qs = (qn * (HEAD_DIM ** -0.5 * math.log2(math.e))).astype(BF16)
    kk = lax.broadcasted_iota(jnp.int32, (MOBA_BLOCK, MOBA_BLOCK), 0)
    qq = lax.broadcasted_iota(jnp.int32, (MOBA_BLOCK, MOBA_BLOCK), 1)
    causal = jnp.where(kk <= qq, 0.0, NEG_BIG)
    for qi in range(nb):
        n_keys = (qi + 1) * MOBA_BLOCK
        cols = slice(qi * MOBA_BLOCK, (qi + 1) * MOBA_BLOCK)
        st = _dot_nt(kb_ref[0:n_keys, :], qs[cols, :])
        blocks = [st[n * MOBA_BLOCK:(n + 1) * MOBA_BLOCK, :] + bias[n:n + 1, cols] for n in range(qi)]
        blocks.append(st[qi * MOBA_BLOCK:n_keys, :] + causal)
        mx = jnp.max(blocks[0], axis=0, keepdims=True)
        for blk_s in blocks[1:]:
            mx = jnp.maximum(mx, jnp.max(blk_s, axis=0, keepdims=True))
        p = [jnp.exp2(blk_s - mx).astype(BF16) for blk_s in blocks]
        p = jnp.concatenate(p, axis=0) if qi else p[0]
        ot = _dot(vt_ref[:, 0:n_keys], p)
        o = ot[0:HEAD_DIM, :] / ot[HEAD_DIM:HEAD_DIM + 1, :]
        o_ref[cols, :] = o.T.astype(o_ref.dtype)


def _attn_prompt(z, wq, wk, n_seq, t_len):
    m = n_seq * t_len
    qblk = pl.BlockSpec((t_len, HEAD_DIM), lambda b, k, g: (b, Q_OFF // HEAD_DIM + k * ATT_GROUP + g))
    kblk = pl.BlockSpec((t_len, HEAD_DIM), lambda b, k, g: (b, K_OFF // HEAD_DIM + k))
    vblk = pl.BlockSpec((t_len, HEAD_DIM), lambda b, k, g: (b, V_OFF // HEAD_DIM + k))
    wspec = pl.BlockSpec((1, HEAD_DIM), lambda b, k, g: (0, 0))
    return pl.pallas_call(
        functools.partial(_attn_prompt_kernel, t_len=t_len),
        out_shape=(jax.ShapeDtypeStruct((m, ATT_HEADS * HEAD_DIM), BF16),
                   jax.ShapeDtypeStruct((m, ATT_KV_HEADS * HEAD_DIM), F32)),
        grid=(n_seq, ATT_KV_HEADS, ATT_GROUP),
        in_specs=[qblk, kblk, vblk, wspec, wspec],
        out_specs=(pl.BlockSpec((t_len, HEAD_DIM), lambda b, k, g: (b, k * ATT_GROUP + g)),
                   pl.BlockSpec((t_len, HEAD_DIM), lambda b, k, g: (b, k))),
        scratch_shapes=[pltpu.VMEM((t_len, HEAD_DIM), BF16), pltpu.VMEM((HEAD_DIM + _ONES_ROWS, t_len), BF16),
                        pltpu.VMEM((t_len // MOBA_BLOCK, HEAD_DIM), F32)],
        compiler_params=_cparams("arbitrary", "arbitrary", "arbitrary"), name="attn_prompt")(
            z, z, z, wq.reshape(1, HEAD_DIM), wk.reshape(1, HEAD_DIM))


def _tri_inverse(a, c):
    ii = lax.broadcasted_iota(jnp.int32, (c, c), 0)
    jj = lax.broadcasted_iota(jnp.int32, (c, c), 1)
    eye = jnp.where(ii == jj, 1.0, 0.0).astype(F32)
    p = -a
    inv = eye + p
    steps = int(math.log2(c)) - 1
    for _ in range(steps):
        p = _dot(p, p)
        inv = inv + _dot(inv, p)
    return inv


_GDN_UNROLL = 8
_GDN_HEADS_PER_STEP = 2


def _gdn_prompt_kernel(zq_ref, zk_ref, zv_ref, wq_ref, wk_ref, wv_ref, p1_ref, p2_ref, zg_ref, nw_ref,
                       o_ref, s_ref, q_s, k_s, v_s, beta_s, gc_s, qp_s, op_s, mn_s, nn_s, *, t_len):
    hp = pl.program_id(1)
    c = CHUNK
    d = HEAD_DIM

    def l2n(x):
        return x * lax.rsqrt(jnp.sum(x * x, axis=-1, keepdims=True) + EPS)

    ii = lax.broadcasted_iota(jnp.int32, (c, c), 0)
    jj = lax.broadcasted_iota(jnp.int32, (c, c), 1)
    eye = jnp.where(ii == jj, 1.0, 0.0).astype(F32)

    for hh in range(_GDN_HEADS_PER_STEP):
        lanes = slice(hh * d, (hh + 1) * d)
        h = hp * _GDN_HEADS_PER_STEP + hh
        q_s[hh] = l2n(_silu(_causal_conv_rows(zq_ref[:, lanes], wq_ref.at[:, lanes], CONV_W))) * (HEAD_DIM ** -0.5)
        k_s[hh] = l2n(_silu(_causal_conv_rows(zk_ref[:, lanes], wk_ref.at[:, lanes], CONV_W)))
        v_s[hh] = _silu(_causal_conv_rows(zv_ref[:, lanes], wv_ref.at[:, lanes], CONV_W))
        beta_s[hh] = jnp.broadcast_to(_lane_col(p1_ref[...], BETA_LANE + h), (t_len, HEAD_DIM))
        gc_s[hh] = jnp.broadcast_to(_lane_col(p2_ref[...], DNG_LANE + h), (t_len, HEAD_DIM))
        lax.fori_loop(0, t_len // (c * _GDN_UNROLL),
                      functools.partial(_gdn_prepare_group, refs=(q_s.at[hh], k_s.at[hh], v_s.at[hh], beta_s.at[hh],
                                                                  gc_s.at[hh], qp_s.at[hh], op_s.at[hh], mn_s.at[hh],
                                                                  nn_s.at[hh]), consts=(ii, jj, eye)), 0)

    def chunk_step(ci, states):
        r = pl.ds(pl.multiple_of(ci * c, c), c)
        rd = pl.ds(pl.multiple_of(ci * d, d), d)
        ms = [_dot(jnp.concatenate([mn_s[hh, rd, :], qp_s[hh, r, :]], axis=0), states[hh])
              for hh in range(_GDN_HEADS_PER_STEP)]
        new = []
        for hh in range(_GDN_HEADS_PER_STEP):
            op_s[hh, r, :] = ms[hh][d:d + c] + op_s[hh, r, :]
            e = jnp.exp(gc_s[hh, pl.ds(ci * c + c - 1, 1), :])
            new.append(states[hh] * e[:, 0:1] + ms[hh][0:d] + nn_s[hh, rd, :])
        return tuple(new)

    zero = jnp.zeros((HEAD_DIM, HEAD_DIM), F32)
    s_fin = lax.fori_loop(0, t_len // c, chunk_step, (zero,) * _GDN_HEADS_PER_STEP)
    for hh in range(_GDN_HEADS_PER_STEP):
        lanes = slice(hh * d, (hh + 1) * d)
        s_ref[hh] = s_fin[hh]
        o = op_s[hh]
        on = o * lax.rsqrt(jnp.mean(o * o, axis=-1, keepdims=True) + EPS) * nw_ref[...]
        o_ref[:, lanes] = (on * _silu(zg_ref[:, lanes])).astype(o_ref.dtype)


def _gdn_prepare_group(gi, carry, *, refs, consts):
    q_s, k_s, v_s, beta_s, gc_s, qp_s, op_s, mn_s, nn_s = refs
    ii, jj, eye = consts
    c = CHUNK
    d = HEAD_DIM
    n = _GDN_UNROLL
    chunks = [gi * n + u for u in range(n)]
    rows = [pl.ds(pl.multiple_of(ci * c, c), c) for ci in chunks]
    q = [q_s[r, :] for r in rows]
    k = [k_s[r, :] for r in rows]
    gcb = [gc_s[r, :] for r in rows]
    kb = [k[u] * beta_s[rows[u], :] for u in range(n)]
    gam = [_decay_matrix(gcb[u][:, 0:1], c)[0] for u in range(n)]
    kq = [_dot_nt(jnp.concatenate([kb[u], q[u]], axis=0), k[u]) for u in range(n)]
    attn = [kq[u][c:2 * c] * gam[u] for u in range(n)]
    p = [-jnp.where(ii > jj, kq[u][0:c] * gam[u], 0.0) for u in range(n)]
    inv = [eye + p[u] for u in range(n)]
    for _ in range(int(math.log2(c)) - 1):
        p = [_dot(p[u], p[u]) for u in range(n)]
        inv = [inv[u] + _dot(inv[u], p[u]) for u in range(n)]
    eg = [jnp.exp(gcb[u]) for u in range(n)]
    rhs = [jnp.concatenate([v_s[rows[u], :] * beta_s[rows[u], :], kb[u] * eg[u]], axis=1) for u in range(n)]
    uw = [_dot(inv[u], rhs[u]) for u in range(n)]
    auw = [_dot(attn[u], uw[u]) for u in range(n)]
    k_dec = [k[u] * jnp.exp(gcb[u][c - 1:c, :] - gcb[u]) for u in range(n)]
    kuw = [_dot(k_dec[u].T, uw[u]) for u in range(n)]
    for u in range(n):
        rd = pl.ds(pl.multiple_of(chunks[u] * d, d), d)
        op_s[rows[u], :] = auw[u][:, 0:d]
        qp_s[rows[u], :] = eg[u] * q[u] - auw[u][:, d:2 * d]
        nn_s[rd, :] = kuw[u][:, 0:d]
        mn_s[rd, :] = -kuw[u][:, d:2 * d]
    return carry


def _gdn_prompt(z, p1, p2, conv_w, norm_w, n_seq, t_len):
    m = n_seq * t_len
    hps = _GDN_HEADS_PER_STEP
    width = hps * HEAD_DIM
    base = DNQKV_OFF // width

    def zcol(off):
        return pl.BlockSpec((t_len, width), functools.partial(lambda b, h, off: (b, off + h), off=off))

    def wcol(off):
        return pl.BlockSpec((CONV_W, width), functools.partial(lambda b, h, off: (0, off + h), off=off))

    aux = pl.BlockSpec((t_len, LANE), lambda b, h: (b, 0))
    tbuf = pltpu.VMEM((hps, t_len, HEAD_DIM), F32)
    groups = DN_HEADS // hps
    return pl.pallas_call(
        functools.partial(_gdn_prompt_kernel, t_len=t_len),
        out_shape=(jax.ShapeDtypeStruct((m, DN_WIDTH), BF16),
                   jax.ShapeDtypeStruct((n_seq, DN_HEADS, HEAD_DIM, HEAD_DIM), F32)),
        grid=(n_seq, groups),
        in_specs=[zcol(base), zcol(base + groups), zcol(base + 2 * groups),
                  wcol(0), wcol(groups), wcol(2 * groups), aux, aux,
                  zcol(DNZ_OFF // width), pl.BlockSpec((1, HEAD_DIM), lambda b, h: (0, 0))],
        out_specs=(pl.BlockSpec((t_len, width), lambda b, h: (b, h)),
                   pl.BlockSpec((None, hps, HEAD_DIM, HEAD_DIM), lambda b, h: (b, h, 0, 0))),
        scratch_shapes=[tbuf] * 7 + [pltpu.VMEM((hps, t_len // CHUNK * HEAD_DIM, HEAD_DIM), F32)] * 2,
        compiler_params=_cparams("arbitrary", "arbitrary"), name="gdn_prompt")(
            z, z, z, conv_w, conv_w, conv_w, p1, p2, z, norm_w.reshape(1, HEAD_DIM))


_GH = SSD_HEADS // SSD_GROUPS
_GW = _GH * SSD_HEAD_DIM
_SSD_UNROLL = 4


def _ssd_prompt_kernel(zx_ref, zb_ref, zc_ref, wx_ref, wb_ref, wc_ref, bx_ref, bb_ref, bc_ref, p1_ref, p2_ref,
                       zg_ref, d_ref, nw_ref, o_ref, st_ref, x_s, b_s, c_s, y_s, state_s, *, t_len):
    grp = pl.program_id(1)
    c = CHUNK
    x_s[...] = _silu(_causal_conv_rows(zx_ref[...], wx_ref, CONV_W) + bx_ref[...])
    b_s[...] = _silu(_causal_conv_rows(zb_ref[...], wb_ref, CONV_W) + bb_ref[...])
    c_s[...] = _silu(_causal_conv_rows(zc_ref[...], wc_ref, CONV_W) + bc_ref[...])
    state_s[...] = jnp.zeros_like(state_s)

    def chunk_group(gi, carry):
        n = _SSD_UNROLL
        rows = [pl.ds(pl.multiple_of((gi * n + u) * c, c), c) for u in range(n)]
        cm = [c_s[r, :] for r in rows]
        bm = [b_s[r, :] for r in rows]
        cb = [_dot_nt(cm[u], bm[u]) for u in range(n)]
        eac, dec, xs, xv, gam = [], [], [], [], []
        for u in range(n):
            x, p1, p2 = x_s[rows[u], :], p1_ref[rows[u], :], p2_ref[rows[u], :]
            eac_h, dec_h, xs_h, xv_h, gam_h = [], [], [], [], []
            for hh in range(_GH):
                lane = SSD_LANE + grp * _GH + hh
                ac = _lane_col(p2, lane)
                a_last = ac[c - 1:c, :]
                xv_hh = x[:, hh * SSD_HEAD_DIM:(hh + 1) * SSD_HEAD_DIM] * _lane_col(p1, lane)
                gam_h.append(_decay_matrix(ac, c)[0])
                xv_h.append(xv_hh)
                xs_h.append(xv_hh * jnp.exp(a_last - ac))
                eac_h.append(jnp.broadcast_to(jnp.exp(ac), (c, SSD_HEAD_DIM)))
                dec_h.append(jnp.broadcast_to(jnp.exp(a_last), (1, SSD_HEAD_DIM)))
            gam.append(gam_h)
            xv.append(xv_h)
            xs.append(jnp.concatenate(xs_h, axis=1))
            eac.append(jnp.concatenate(eac_h, axis=1))
            dec.append(jnp.concatenate(dec_h, axis=1))
        y_intra = [jnp.concatenate([_dot(cb[u] * gam[u][hh], xv[u][hh]) for hh in range(_GH)], axis=1)
                   for u in range(n)]
        new_states = [_dot(bm[u].T, xs[u]) for u in range(n)]
        s_in = [state_s[...]]
        for u in range(n):
            s_in.append(s_in[u] * dec[u] + new_states[u])
        state_s[...] = s_in[n]
        for u in range(n):
            y_s[rows[u], :] = y_intra[u] + eac[u] * _dot(cm[u], s_in[u])
        return carry

    lax.fori_loop(0, t_len // (c * _SSD_UNROLL), chunk_group, 0)
    for hh in range(_GH):
        st_ref[hh] = state_s[:, hh * SSD_HEAD_DIM:(hh + 1) * SSD_HEAD_DIM]
    y = (y_s[...] + d_ref[...] * x_s[...]) * _silu(zg_ref[...])
    o_ref[...] = (y * lax.rsqrt(jnp.mean(y * y, axis=-1, keepdims=True) + EPS) * nw_ref[...]).astype(o_ref.dtype)


def _ssd_prompt(z, p1, p2, conv_w, conv_b, d_vec, norm_w, n_seq, t_len):
    m = n_seq * t_len
    xb, bb, cbk = SSDX_OFF // _GW, (SSDX_OFF + SSD_INNER) // LANE, (SSDX_OFF + SSD_INNER + 2 * SSD_STATE) // LANE
    wb0, wc0 = SSD_INNER // LANE, (SSD_INNER + 2 * SSD_STATE) // LANE

    def spec(rows, width, off):
        return pl.BlockSpec((rows, width), functools.partial(lambda b, g, off: (0, off + g), off=off))

    def zspec(width, off):
        return pl.BlockSpec((t_len, width), functools.partial(lambda b, g, off: (b, off + g), off=off))

    aux = pl.BlockSpec((t_len, LANE), lambda b, g: (b, 0))
    return pl.pallas_call(
        functools.partial(_ssd_prompt_kernel, t_len=t_len),
        out_shape=(jax.ShapeDtypeStruct((m, SSD_INNER), BF16),
                   jax.ShapeDtypeStruct((n_seq, SSD_HEADS, SSD_STATE, SSD_HEAD_DIM), F32)),
        grid=(n_seq, SSD_GROUPS),
        in_specs=[zspec(_GW, xb), zspec(LANE, bb), zspec(LANE, cbk),
                  spec(CONV_W, _GW, 0), spec(CONV_W, LANE, wb0), spec(CONV_W, LANE, wc0),
                  spec(1, _GW, 0), spec(1, LANE, wb0), spec(1, LANE, wc0),
                  aux, aux, zspec(_GW, SSDZ_OFF // _GW), spec(1, _GW, 0), spec(1, _GW, 0)],
        out_specs=(pl.BlockSpec((t_len, _GW), lambda b, g: (b, g)),
                   pl.BlockSpec((None, _GH, SSD_STATE, SSD_HEAD_DIM), lambda b, g: (b, g, 0, 0))),
        scratch_shapes=[pltpu.VMEM((t_len, _GW), F32), pltpu.VMEM((t_len, LANE), F32), pltpu.VMEM((t_len, LANE), F32),
                        pltpu.VMEM((t_len, _GW), F32), pltpu.VMEM((SSD_STATE, _GW), F32)],
        compiler_params=_cparams("arbitrary", "arbitrary"), name="ssd_prompt")(
            z, z, z, conv_w, conv_w, conv_w, conv_b, conv_b, conv_b, p1, p2, z, d_vec, norm_w.reshape(1, SSD_INNER))


_FFN_TN = 256
_FFN_NJ = FFN_DIM // _FFN_TN
_TAIL = 8


def _ffn_up_kernel(a_ref, as_ref, bg_ref, bu_ref, wg_ref, wu_ref, cg_ref, cu_ref, s0g_ref, s1g_ref, s0u_ref, s1u_ref,
                   g_ref, sg_ref, su_ref, gs_ref, ygs_ref, yus_ref, b_s, y_s, *, tiles_per_seq, n_sub):
    i = pl.program_id(1)
    tn = _FFN_TN
    tm = a_ref.shape[0]

    @pl.when(i == 0)
    def _():
        b_s[:, 0:tn] = bg_ref[...].astype(BF16)
        b_s[:, tn:2 * tn] = bu_ref[...].astype(BF16)
        ys = _dot(as_ref[...], b_s[...])
        yg, yu = ys[:, 0:tn], ys[:, tn:2 * tn]
        ug = wg_ref[0:1, :] * s0g_ref[...] + wg_ref[1:2, :] * s1g_ref[...] + wg_ref[2:3, :] * yg + cg_ref[...]
        uu = wu_ref[0:1, :] * s0u_ref[...] + wu_ref[1:2, :] * s1u_ref[...] + wu_ref[2:3, :] * yu + cu_ref[...]
        gs_ref[...] = (_silu(ug) * uu).astype(gs_ref.dtype)
        ygs_ref[...] = yg
        yus_ref[...] = yu

    @pl.when(i % tiles_per_seq == 0)
    def _():
        y_s[0:_TAIL, :] = jnp.zeros((_TAIL, 2 * tn), F32)

    @pl.when(i % tiles_per_seq != 0)
    def _():
        y_s[0:_TAIL, :] = y_s[tm:tm + _TAIL, :]

    w = jnp.concatenate([wg_ref[...], wu_ref[...]], axis=1)
    c = jnp.concatenate([cg_ref[...], cu_ref[...]], axis=1)
    ts = tm // n_sub
    for s in range(n_sub):
        r0 = _TAIL + s * ts
        y = _dot(a_ref[s * ts:(s + 1) * ts, :], b_s[...])
        y_s[r0:r0 + ts, :] = y
        u = w[0:1, :] * y_s[r0 - 2:r0 - 2 + ts, :] + w[1:2, :] * y_s[r0 - 1:r0 - 1 + ts, :] + w[2:3, :] * y + c
        g_ref[s * ts:(s + 1) * ts, :] = (_silu(u[:, 0:tn]) * u[:, tn:2 * tn]).astype(g_ref.dtype)
    sg_ref[...] = y_s[tm:tm + _TAIL, 0:tn]
    su_ref[...] = y_s[tm:tm + _TAIL, tn:2 * tn]


def _ffn_up(h2, h2s, w_up, layer, conv_w, conv_b, s0, s1, n_seq, t_len, tm):
    m = n_seq * t_len
    k = h2.shape[1]
    ms = h2s.shape[0]
    nj = _FFN_NJ
    tiles_per_seq = t_len // tm
    lo = lambda j, i: (0, j)
    hi = lambda j, i: (0, j + nj)
    srow = lambda f: pl.BlockSpec((ms, _FFN_TN), f)
    tail = jax.ShapeDtypeStruct((m // tm, _TAIL, FFN_DIM), F32)
    tail_spec = pl.BlockSpec((None, _TAIL, _FFN_TN), lambda j, i: (i, 0, j))
    ysd = jax.ShapeDtypeStruct((ms, FFN_DIM), F32)
    return pl.pallas_call(
        functools.partial(_ffn_up_kernel, tiles_per_seq=tiles_per_seq, n_sub=2),
        out_shape=(jax.ShapeDtypeStruct((m, FFN_DIM), BF16), tail, tail,
                   jax.ShapeDtypeStruct((ms, FFN_DIM), BF16), ysd, ysd),
        grid=(nj, m // tm),
        in_specs=[pl.BlockSpec((tm, k), lambda j, i: (i, 0)), pl.BlockSpec((ms, k), lambda j, i: (0, 0)),
                  pl.BlockSpec((None, k, _FFN_TN), lambda j, i: (layer, 0, j)),
                  pl.BlockSpec((None, k, _FFN_TN), lambda j, i: (layer, 0, j + nj)),
                  pl.BlockSpec((FFN_CONV_W, _FFN_TN), lo), pl.BlockSpec((FFN_CONV_W, _FFN_TN), hi),
                  pl.BlockSpec((1, _FFN_TN), lo), pl.BlockSpec((1, _FFN_TN), hi),
                  srow(lo), srow(lo), srow(hi), srow(hi)],
        out_specs=(pl.BlockSpec((tm, _FFN_TN), lambda j, i: (i, j)), tail_spec, tail_spec,
                   srow(lo), srow(lo), srow(lo)),
        scratch_shapes=[pltpu.VMEM((k, 2 * _FFN_TN), BF16), pltpu.VMEM((_TAIL + tm, 2 * _FFN_TN), F32)],
        compiler_params=_cparams("arbitrary", "arbitrary"), name="ffn_up")(
            h2, h2s, w_up, w_up, conv_w, conv_w, conv_b, conv_b, s0, s1, s0, s1)


_PAGES_PER_STEP = 16


def _kmean_kernel(pt_ref, *refs):
    o_ref = refs[-1]
    page = refs[0].shape[0]
    per_block = MOBA_BLOCK // page
    for n in range(_PAGES_PER_STEP // per_block):
        acc = jnp.sum(refs[n * per_block][...], axis=0)
        for r in range(1, per_block):
            acc = acc + jnp.sum(refs[n * per_block + r][...], axis=0)
        o_ref[n] = acc * (1.0 / MOBA_BLOCK)


def _kmean(cache_k, page_table, layer):
    n_b, n_pages = page_table.shape
    page, n_kv, d = cache_k.shape[2:]
    per_block = MOBA_BLOCK // page
    n_blocks = n_pages // per_block
    steps = n_pages // _PAGES_PER_STEP
    in_specs = [pl.BlockSpec((None, None, page, n_kv, d),
                             functools.partial(lambda b, s, pt, r: (layer, pt[b, s * _PAGES_PER_STEP + r], 0, 0, 0), r=r))
                for r in range(_PAGES_PER_STEP)]
    grid_spec = pltpu.PrefetchScalarGridSpec(
        num_scalar_prefetch=1, grid=(n_b, steps), in_specs=in_specs,
        out_specs=pl.BlockSpec((None, _PAGES_PER_STEP // per_block, n_kv, d), lambda b, s, pt: (b, s, 0, 0)))
    return pl.pallas_call(
        _kmean_kernel, out_shape=jax.ShapeDtypeStruct((n_b, n_blocks, n_kv, d), F32), grid_spec=grid_spec,
        compiler_params=_cparams("arbitrary", "arbitrary"), name="kmean")(page_table, *([cache_k] * _PAGES_PER_STEP))


def _select_kernel(q_ref, k_ref, kmean_ref, wq_ref, wk_ref, qn_ref, kn_ref, idx_ref):
    n_b, n_h, _ = q_ref.shape
    n_blocks = kmean_ref.shape[1]
    q = q_ref[...]
    qn = q * lax.rsqrt(jnp.mean(q * q, axis=-1, keepdims=True) + EPS) * wq_ref[...]
    qn_ref[...] = qn
    k = k_ref[...]
    kn_ref[...] = k * lax.rsqrt(jnp.mean(k * k, axis=-1, keepdims=True) + EPS) * wk_ref[...]
    head = lax.broadcasted_iota(jnp.int32, (n_h, n_blocks), 0)
    lane = lax.broadcasted_iota(jnp.int32, (n_h, n_blocks), 1).astype(F32)
    lane_out = lax.broadcasted_iota(jnp.int32, (n_h, LANE), 1)
    for b in range(n_b):
        gate = jnp.zeros((n_h, n_blocks), F32)
        for kv in range(ATT_KV_HEADS):
            gk = _dot_nt(qn[b], kmean_ref[b, :, kv * HEAD_DIM:(kv + 1) * HEAD_DIM], HIGHEST)
            gate = jnp.where(head // ATT_GROUP == kv, gk, gate)
        out = jnp.zeros((n_h, LANE), F32)
        for s in range(MOBA_TOPK):
            mx = jnp.max(gate, axis=-1, keepdims=True)
            pick = jnp.min(jnp.where(gate == mx, lane, float(n_blocks)), axis=-1, keepdims=True)
            out = jnp.where(lane_out == s, pick, out)
            gate = jnp.where(lane == pick, -jnp.inf, gate)
        idx_ref[b] = out.astype(jnp.int32)


def _select(q3, k3, kmean, wq, wk):
    n_b, n_h, d = q3.shape
    full = lambda shape: pl.BlockSpec(shape, lambda i: (0,) * len(shape))
    return pl.pallas_call(
        _select_kernel,
        out_shape=(jax.ShapeDtypeStruct(q3.shape, F32), jax.ShapeDtypeStruct(k3.shape, F32),
                   jax.ShapeDtypeStruct((n_b, n_h, LANE), jnp.int32)),
        grid=(1,),
        in_specs=[full(q3.shape), full(k3.shape), full(kmean.shape), full((1, 1, d)), full((1, 1, d))],
        out_specs=(full(q3.shape), full(k3.shape), full((n_b, n_h, LANE))),
        compiler_params=_cparams("arbitrary"), name="select")(q3, k3, kmean, wq.reshape(1, 1, d), wk.reshape(1, 1, d))


def _attn_sample_kernel(pt_ref, idx_ref, qn_ref, kn_ref, vn_ref, *refs, n_sel, per_block):
    o_ref = refs[-1]
    pages = refs[:-1]
    n_pg = n_sel * per_block
    k_refs, v_refs = pages[:n_pg], pages[n_pg:]
    h = pl.program_id(1)
    kv = h // ATT_GROUP
    q = qn_ref[pl.ds(h, 1), :] * (HEAD_DIM ** -0.5)
    k_new = kn_ref[pl.ds(kv, 1), :]
    v_new = vn_ref[pl.ds(kv, 1), :]
    s_own = jnp.sum(q * k_new, axis=-1, keepdims=True)
    mine = pl.ds(kv, k_refs[0].shape[0] // ATT_KV_HEADS, stride=ATT_KV_HEADS)
    logits = [jnp.sum(r[mine, :] * q, axis=-1, keepdims=True) for r in k_refs]
    mx = s_own
    for s in logits:
        mx = jnp.maximum(mx, jnp.max(s, axis=0, keepdims=True))
    p_own = jnp.exp(s_own - mx)
    den = p_own
    acc = p_own * v_new
    for s, v_ref in zip(logits, v_refs):
        p = jnp.exp(s - mx)
        den = den + jnp.sum(p, axis=0, keepdims=True)
        acc = acc + jnp.sum(p * v_ref[mine, :], axis=0, keepdims=True)
    o_ref[pl.ds(h, 1), :] = acc / den


def _attn_sample(cache_k, cache_v, page_table, idx, qn, kn, vn, layer):
    n_b, n_h, d = qn.shape
    depth, n_pool, page, n_kv = cache_k.shape[:4]
    per_block = MOBA_BLOCK // page
    n_sel = idx.shape[2]
    cache_k = cache_k.reshape(depth, n_pool, page * n_kv, d)
    cache_v = cache_v.reshape(depth, n_pool, page * n_kv, d)

    def page_spec(s, r):
        def imap(b, h, pt, ix):
            return (layer, pt[b, ix[b, h * n_sel + s] * per_block + r], 0, 0)
        return pl.BlockSpec((None, None, page * n_kv, d), imap)

    pspecs = [page_spec(s, r) for s in range(n_sel) for r in range(per_block)]
    slab = lambda n: pl.BlockSpec((None, n, d), lambda b, h, pt, ix: (b, 0, 0))
    grid_spec = pltpu.PrefetchScalarGridSpec(
        num_scalar_prefetch=2, grid=(n_b, n_h),
        in_specs=[slab(n_h), slab(ATT_KV_HEADS), slab(ATT_KV_HEADS)] + pspecs + pspecs,
        out_specs=slab(n_h))
    n_pg = len(pspecs)
    return pl.pallas_call(
        functools.partial(_attn_sample_kernel, n_sel=n_sel, per_block=per_block),
        out_shape=jax.ShapeDtypeStruct((n_b, n_h, d), F32), grid_spec=grid_spec,
        compiler_params=_cparams("arbitrary", "arbitrary"), name="attn_sample")(
            page_table, idx.reshape(n_b, n_h * n_sel), qn, kn, vn, *([cache_k] * n_pg), *([cache_v] * n_pg))


def _rows_to_cols(x):
    r, n = x.shape
    return jnp.concatenate([x, jnp.zeros((n - r, n), x.dtype)], axis=0).T


def _mix_sample_kernel(dx_ref, dbuf_ref, dw_ref, dbeta_ref, dg_ref, dz_ref, dnw_ref, dst_ref,
                       sx_ref, sxbuf_ref, sxw_ref, sxb_ref, bc_ref, bcbuf_ref, bcw_ref, bcb_ref,
                       sdt_ref, sa_ref, sz_ref, sd_ref, snw_ref, sst_ref,
                       dno_ref, dnst_ref, so_ref, sso_ref, y_s):
    x = dx_ref[...]
    conv = x * dw_ref[CONV_W - 1]
    for i in range(CONV_W - 1):
        conv = conv + dbuf_ref[i] * dw_ref[i]
    act = _silu(conv)
    nh = DN_HEADS
    q, k, v = act[0:nh], act[nh:2 * nh], act[2 * nh:3 * nh]
    q = q * lax.rsqrt(jnp.sum(q * q, axis=-1, keepdims=True) + EPS) * (HEAD_DIM ** -0.5)
    k = k * lax.rsqrt(jnp.sum(k * k, axis=-1, keepdims=True) + EPS)
    beta = dbeta_ref[...]
    eg = jnp.exp(dg_ref[...])
    qk = jnp.sum(q * k, axis=-1, keepdims=True)
    qt, kt = _rows_to_cols(q), _rows_to_cols(k)
    o_rows = []
    for h in range(nh):
        s0 = dst_ref[h]
        kcol, qcol = kt[:, h:h + 1], qt[:, h:h + 1]
        ks = jnp.sum(kcol * s0, axis=0, keepdims=True)
        qs = jnp.sum(qcol * s0, axis=0, keepdims=True)
        b_h, e_h = beta[h:h + 1, :], eg[h:h + 1, :]
        v_new = v[h:h + 1, :] * b_h - (b_h * e_h) * ks
        o_rows.append(e_h * qs + qk[h:h + 1, :] * v_new)
        dnst_ref[h] = s0 * e_h + kcol * v_new
    o = jnp.concatenate(o_rows, axis=0)
    on = o * lax.rsqrt(jnp.mean(o * o, axis=-1, keepdims=True) + EPS) * dnw_ref[...]
    dno_ref[...] = on * _silu(dz_ref[...])
    xs = sx_ref[...] * sxw_ref[CONV_W - 1] + sxb_ref[...]
    bc = bc_ref[...] * bcw_ref[CONV_W - 1] + bcb_ref[...]
    for i in range(CONV_W - 1):
        xs = xs + sxbuf_ref[i] * sxw_ref[i]
        bc = bc + bcbuf_ref[i] * bcw_ref[i]
    xs = _silu(xs)
    bc = _silu(bc)
    bct = _rows_to_cols(bc)
    cb = jnp.sum(bc[0:SSD_GROUPS] * bc[SSD_GROUPS:2 * SSD_GROUPS], axis=-1, keepdims=True)
    dt = sdt_ref[...]
    ea = jnp.exp(sa_ref[...])
    for h in range(SSD_HEADS):
        grp = h // _GH
        s0 = sst_ref[h]
        bcol, ccol = bct[:, grp:grp + 1], bct[:, SSD_GROUPS + grp:SSD_GROUPS + grp + 1]
        xv = xs[h:h + 1, :] * dt[h:h + 1, :]
        e_h = ea[h:h + 1, :]
        y_s[h:h + 1, :] = cb[grp:grp + 1, :] * xv + e_h * jnp.sum(ccol * s0, axis=0, keepdims=True)
        sso_ref[h] = s0 * e_h + bcol * xv
    y = (y_s[...] + sd_ref[...] * xs) * _silu(sz_ref[...])
    norm_rows = []
    for grp in range(SSD_GROUPS):
        yg = y[grp * _GH:(grp + 1) * _GH]
        ms = jnp.sum(jnp.sum(yg * yg, axis=-1, keepdims=True), axis=0, keepdims=True) * (1.0 / _GW)
        norm_rows.append(yg * lax.rsqrt(ms + EPS))
    so_ref[...] = jnp.concatenate(norm_rows, axis=0) * snw_ref[...]


def _mix_sample(args, n_b):
    def per_b(shape):
        nd = len(shape) - 1
        return pl.BlockSpec((None,) + tuple(shape[1:]), lambda b: (b,) + (0,) * nd)

    def shared(shape):
        nd = len(shape)
        return pl.BlockSpec(tuple(shape), lambda b: (0,) * nd)

    names_per_b = {"dx", "dbuf", "dbeta", "dg", "dz", "dst", "sx", "sxbuf", "bc", "bcbuf", "sdt", "sa", "sz", "sst"}
    order = ["dx", "dbuf", "dw", "dbeta", "dg", "dz", "dnw", "dst", "sx", "sxbuf", "sxw", "sxb", "bc", "bcbuf", "bcw",
             "bcb", "sdt", "sa", "sz", "sd", "snw", "sst"]
    in_specs = [per_b(args[n].shape) if n in names_per_b else shared(args[n].shape) for n in order]
    outs = (jax.ShapeDtypeStruct((n_b, DN_HEADS, HEAD_DIM), F32),
            jax.ShapeDtypeStruct((n_b, DN_HEADS, HEAD_DIM, HEAD_DIM), F32),
            jax.ShapeDtypeStruct((n_b, SSD_HEADS, SSD_HEAD_DIM), F32),
            jax.ShapeDtypeStruct((n_b, SSD_HEADS, SSD_STATE, SSD_HEAD_DIM), F32))
    return pl.pallas_call(
        _mix_sample_kernel, out_shape=outs, grid=(n_b,), in_specs=in_specs,
        out_specs=tuple(per_b(o.shape) for o in outs),
        scratch_shapes=[pltpu.VMEM((SSD_HEADS, SSD_HEAD_DIM), F32)],
        compiler_params=_cparams("arbitrary"), name="mix_sample")(*[args[n] for n in order])


def _split_w_in(w_in):
    depth, k, _ = w_in.shape
    wt = jnp.swapaxes(w_in, 1, 2)
    wt_b = wt[:, ZB_SRC:ZB_SRC + ZB_COLS]
    wt_s = jnp.concatenate([wt[:, ZS_SRC[0]:ZS_SRC[0] + 16], wt[:, ZS_SRC[1]:ZS_SRC[1] + 16],
                            jnp.zeros((depth, LANE - 32, k), w_in.dtype)], axis=1)
    return wt, wt_b, wt_s


def _in_proj_all(h, hs, lw, layer, tm):
    return [_in_proj(h, hs, lw["wt_a"], layer, ZA_COLS, tm=tm, tn=512, name="in_proj_a"),
            _in_proj(h, hs, lw["wt_b"], layer, ZB_COLS, tm=tm, tn=512, name="in_proj_b"),
            _in_proj(h, hs, lw["wt_s"], layer, LANE, tm=tm, tn=LANE, name="in_proj_s")]


def _lane_vec(dn_vals, ssd_vals):
    v = jnp.zeros((LANE,), F32)
    v = v.at[DNG_LANE:DNG_LANE + DN_HEADS].set(dn_vals.astype(F32))
    v = v.at[SSD_LANE:SSD_LANE + SSD_HEADS].set(ssd_vals.astype(F32))
    return v.reshape(1, LANE)


def _ffn_down(g, w_down, layer, x, tm):
    return _matmul([(g, FFN_DIM, 0, 0)], w_down, layer, tm=tm, tn=256, res=x, a_single_buffer=tm >= 512,
                   name="ffn_down")


_PROMPT_TM = 1024


def _prompt_mix(x, za, zb, zs, lw, layer, n_seq, t_len):
    tm = _PROMPT_TM
    p1, p2 = _aux(zs, lw["bias_vec"], lw["alog_vec"], n_seq, t_len, CHUNK)
    o_att, k_norm = _attn_prompt(za, lw["attn_q_norm_w"], lw["attn_k_norm_w"], n_seq, t_len)
    o_dn, dn_state = _gdn_prompt(za, p1, p2, lw["dn_conv_w"], lw["dn_norm_w"], n_seq, t_len)
    o_ssd, ssd_state = _ssd_prompt(zb, p1, p2, lw["ssd_conv_w"], lw["ssd_conv_b"].reshape(1, -1), lw["ssd_d_vec"],
                                   lw["ssd_norm_w"], n_seq, t_len)
    x = _matmul([(o_att, 2048, 0, 0), (o_dn, 1024, 0, 2), (o_ssd, 1024, 0, 3)], lw["w_out"], layer, tm=tm, tn=512,
                res=x, name="out_proj")
    h2 = _rmsnorm_cast(x, lw["norm2_w"], 512)
    za3 = za.reshape(n_seq, t_len, ZA_COLS)
    zb3 = zb.reshape(n_seq, t_len, ZB_COLS)
    new = [k_norm.reshape(n_seq, t_len, ATT_KV_HEADS, HEAD_DIM),
           za3[:, :, V_OFF:V_OFF + 512].reshape(n_seq, t_len, ATT_KV_HEADS, HEAD_DIM),
           za3[:, t_len - (CONV_W - 1):, DNQKV_OFF:DNQKV_OFF + DN_CONV_CH],
           dn_state,
           zb3[:, t_len - (CONV_W - 1):, SSDX_OFF:SSDX_OFF + SSD_CONV_CH],
           ssd_state]
    return x, h2, new


_SAMPLE_ROWS = 16


def _sample_mix(x, za, zb2, zs, lw, layer, cache_k, cache_v, page_table, dn_conv_buf, dn_state, ssd_conv_buf,
                ssd_state):
    n_b = page_table.shape[0]
    rows = x.shape[0]
    p1, p2 = _aux(zs, lw["bias_vec"], lw["alog_vec"], 1, rows, 1)
    zb = za[:n_b]
    zbs = zb2[:n_b]
    kmean = _kmean(cache_k, page_table, layer)
    kmean = kmean.reshape(n_b, kmean.shape[1], ATT_KV_HEADS * HEAD_DIM)
    q3 = zb[:, Q_OFF:Q_OFF + 2048].reshape(n_b, ATT_HEADS, HEAD_DIM)
    k3 = zb[:, K_OFF:K_OFF + 512].reshape(n_b, ATT_KV_HEADS, HEAD_DIM)
    v3 = zb[:, V_OFF:V_OFF + 512].reshape(n_b, ATT_KV_HEADS, HEAD_DIM)
    qn, kn, idx = _select(q3, k3, kmean, lw["attn_q_norm_w"], lw["attn_k_norm_w"])
    o_att = _attn_sample(cache_k, cache_v, page_table, idx[:, :, :MOBA_TOPK], qn, kn, v3, layer)
    dn_x = zb[:, DNQKV_OFF:DNQKV_OFF + DN_CONV_CH]
    ssd_x = zbs[:, SSDX_OFF:SSDX_OFF + SSD_CONV_CH]
    nh3 = 3 * DN_HEADS
    ng2 = 2 * SSD_GROUPS
    args = {
        "dx": dn_x.reshape(n_b, nh3, HEAD_DIM),
        "dbuf": dn_conv_buf.reshape(n_b, CONV_W - 1, nh3, HEAD_DIM),
        "dw": lw["dn_conv_w"].reshape(CONV_W, nh3, HEAD_DIM),
        "dbeta": p1[:n_b, BETA_LANE:BETA_LANE + DN_HEADS].reshape(n_b, DN_HEADS, 1),
        "dg": p2[:n_b, DNG_LANE:DNG_LANE + DN_HEADS].reshape(n_b, DN_HEADS, 1),
        "dz": zb[:, DNZ_OFF:DNZ_OFF + DN_WIDTH].reshape(n_b, DN_HEADS, HEAD_DIM),
        "dnw": lw["dn_norm_w"].reshape(1, HEAD_DIM),
        "dst": dn_state,
        "sx": ssd_x[:, :SSD_INNER].reshape(n_b, SSD_HEADS, SSD_HEAD_DIM),
        "sxbuf": ssd_conv_buf[:, :, :SSD_INNER].reshape(n_b, CONV_W - 1, SSD_HEADS, SSD_HEAD_DIM),
        "sxw": lw["ssd_conv_w"][:, :SSD_INNER].reshape(CONV_W, SSD_HEADS, SSD_HEAD_DIM),
        "sxb": lw["ssd_conv_b"][:SSD_INNER].reshape(SSD_HEADS, SSD_HEAD_DIM),
        "bc": ssd_x[:, SSD_INNER:].reshape(n_b, ng2, SSD_STATE),
        "bcbuf": ssd_conv_buf[:, :, SSD_INNER:].reshape(n_b, CONV_W - 1, ng2, SSD_STATE),
        "bcw": lw["ssd_conv_w"][:, SSD_INNER:].reshape(CONV_W, ng2, SSD_STATE),
        "bcb": lw["ssd_conv_b"][SSD_INNER:].reshape(ng2, SSD_STATE),
        "sdt": p1[:n_b, SSD_LANE:SSD_LANE + SSD_HEADS].reshape(n_b, SSD_HEADS, 1),
        "sa": p2[:n_b, SSD_LANE:SSD_LANE + SSD_HEADS].reshape(n_b, SSD_HEADS, 1),
        "sz": zbs[:, SSDZ_OFF:SSDZ_OFF + SSD_INNER].reshape(n_b, SSD_HEADS, SSD_HEAD_DIM),
        "sd": lw["ssd_D"].astype(F32).reshape(SSD_HEADS, 1),
        "snw": lw["ssd_norm_w"].reshape(SSD_HEADS, SSD_HEAD_DIM),
        "sst": ssd_state,
    }
    o_dn, dn_state_new, o_ssd, ssd_state_new = _mix_sample(args, n_b)
    pad = lambda a: jnp.pad(a.reshape(n_b, -1), ((0, rows - n_b), (0, 0))).astype(BF16)
    x = _matmul([(pad(o_att), 2048, 0, 0), (pad(o_dn), 1024, 0, 2), (pad(o_ssd), 1024, 0, 3)], lw["w_out"], layer,
                tm=rows, tn=512, res=x, name="out_proj_s")
    h2 = _rmsnorm_cast(x, lw["norm2_w"], rows)
    new = [kn.reshape(n_b, 1, ATT_KV_HEADS, HEAD_DIM), v3.reshape(n_b, 1, ATT_KV_HEADS, HEAD_DIM),
           jnp.concatenate([dn_conv_buf[:, 1:], dn_x[:, None]], axis=1), dn_state_new,
           jnp.concatenate([ssd_conv_buf[:, 1:], ssd_x[:, None]], axis=1), ssd_state_new]
    return x, h2, new


def _layer(xp, xs, lw, layer, n_seq, t_len, cache_k, cache_v, page_table, dn_conv_buf, dn_state, ssd_conv_buf,
           ssd_state, ffn_conv_buf):
    tm = _PROMPT_TM
    n_b = page_table.shape[0]
    rows = xs.shape[0]
    h = _rmsnorm_cast(xp, lw["norm1_w"], 512)
    hs = _rmsnorm_cast(xs, lw["norm1_w"], rows)
    (za, za_s), (zb, zb_s), (zs, zs_s) = _in_proj_all(h, hs, lw, layer, tm)
    xp, h2, new_p = _prompt_mix(xp, za, zb, zs, lw, layer, n_seq, t_len)
    xs, h2s, new_s = _sample_mix(xs, za_s, zb_s, zs_s, lw, layer, cache_k, cache_v, page_table, dn_conv_buf, dn_state,
                                 ssd_conv_buf, ssd_state)
    padf = lambda a: jnp.pad(a, ((0, rows - n_b), (0, 0)))
    g, tail_g, tail_u, gs, y_g, y_u = _ffn_up(h2, h2s, lw["ffn_w_up"], layer, lw["ffn_conv_w"],
                                              lw["ffn_conv_b"].reshape(1, -1), padf(ffn_conv_buf[:, 0]),
                                              padf(ffn_conv_buf[:, 1]), n_seq, t_len, tm)
    xp = _ffn_down(g, lw["ffn_w_down"], layer, xp, tm)
    xs = _ffn_down(gs, lw["ffn_w_down"], layer, xs, rows)
    tps = t_len // tm
    new_p.append(jnp.concatenate([tail_g[tps - 1::tps, _TAIL - 2:], tail_u[tps - 1::tps, _TAIL - 2:]], axis=-1))
    y_new = jnp.concatenate([y_g[:n_b], y_u[:n_b]], axis=-1)
    new_s.append(jnp.stack([ffn_conv_buf[:, 1], y_new], axis=1))
    return xp, xs, new_p, new_s


def kernel(x_prompt, x_sample, cache_k, cache_v, page_table, state_dn_conv, state_dn, state_ssd_conv, state_ssd, state_ffn_conv, norm1_w, w_in, attn_q_norm_w, attn_k_norm_w, dn_conv_w, dn_A_log, dn_dt_bias, dn_norm_w, ssd_conv_w, ssd_conv_b, ssd_dt_bias, ssd_A_log, ssd_D, ssd_norm_w, w_out, norm2_w, ffn_w_up, ffn_conv_w, ffn_conv_b, ffn_w_down):
    depth = w_in.shape[0]
    n_seq, t_len, d_model = x_prompt.shape
    n_b = x_sample.shape[0]
    yp = x_prompt.reshape(n_seq * t_len, d_model)
    ys = jnp.pad(x_sample.reshape(n_b, d_model), ((0, _SAMPLE_ROWS - n_b), (0, 0)))
    wt_a, wt_b, wt_s = _split_w_in(w_in)
    outs_p, outs_s = [], []
    for l in range(depth):
        lw = {"norm1_w": norm1_w[l], "wt_a": wt_a, "wt_b": wt_b, "wt_s": wt_s, "attn_q_norm_w": attn_q_norm_w[l],
              "attn_k_norm_w": attn_k_norm_w[l], "dn_conv_w": dn_conv_w[l], "dn_norm_w": dn_norm_w[l],
              "ssd_conv_w": ssd_conv_w[l], "ssd_conv_b": ssd_conv_b[l], "ssd_D": ssd_D[l],
              "ssd_norm_w": ssd_norm_w[l], "w_out": w_out, "norm2_w": norm2_w[l], "ffn_w_up": ffn_w_up,
              "ffn_conv_w": ffn_conv_w[l], "ffn_conv_b": ffn_conv_b[l], "ffn_w_down": ffn_w_down,
              "bias_vec": _lane_vec(dn_dt_bias[l], ssd_dt_bias[l]),
              "alog_vec": _lane_vec(dn_A_log[l], ssd_A_log[l]),
              "ssd_d_vec": jnp.repeat(ssd_D[l].astype(F32), SSD_HEAD_DIM).reshape(1, SSD_INNER)}
        yp, ys, new_p, new_s = _layer(yp, ys, lw, l, n_seq, t_len, cache_k, cache_v, page_table, state_dn_conv[l],
                                      state_dn[l], state_ssd_conv[l], state_ssd[l], state_ffn_conv[l])
        outs_p.append(new_p)
        outs_s.append(new_s)
    st = lambda outs, i: jnp.stack([o[i] for o in outs])
    return (yp.reshape(n_seq, t_len, d_model), ys[:n_b].reshape(n_b, 1, d_model),
            st(outs_p, 0), st(outs_p, 1), st(outs_s, 0), st(outs_s, 1),
            st(outs_p, 2), st(outs_s, 2), st(outs_p, 3), st(outs_s, 3),
            st(outs_p, 4), st(outs_s, 4), st(outs_p, 5), st(outs_s, 5),
            st(outs_p, 6), st(outs_s, 6))
```

```python
import functools
import math

import jax
import jax.numpy as jnp
from jax import lax
from jax.experimental import pallas as pl
from jax.experimental.pallas import tpu as pltpu

F32 = jnp.float32
BF16 = jnp.bfloat16
HIGHEST = lax.Precision.HIGHEST

D_MODEL = 4096
HEAD_DIM = 128
ATT_HEADS = 16
ATT_KV_HEADS = 4
ATT_GROUP = 4
MOBA_BLOCK = 256
MOBA_TOPK = 3
DN_HEADS = 8
DN_WIDTH = 1024
DN_CONV_CH = 3072
SSD_INNER = 1024
SSD_HEAD_DIM = 64
SSD_HEADS = 16
SSD_GROUPS = 2
SSD_STATE = 128
SSD_CONV_CH = 1536
CONV_W = 4
CHUNK = 64
FFN_DIM = 11008
FFN_CONV_W = 3
EPS = 1e-6

Q_OFF, K_OFF, V_OFF = 0, 2048, 2560
DNQKV_OFF, DNZ_OFF = 3072, 6144
ZA_COLS = 7168
SSDX_OFF, SSDZ_OFF = 0, 1536
ZB_SRC, ZB_COLS = 7184, 2560
ZS_SRC = (7168, 9744)
LANE = 128
BETA_LANE, DNG_LANE, SSD_LANE = 0, 8, 16

VMEM_LIMIT_BYTES = 56 * 1024 * 1024
NEG_BIG = -1e30


def _cparams(*sem):
    return pltpu.CompilerParams(dimension_semantics=sem, vmem_limit_bytes=VMEM_LIMIT_BYTES)


def _silu(x):
    return x / (1.0 + jnp.exp(-x))


def _sigmoid(x):
    return 1.0 / (1.0 + jnp.exp(-x))


def _softplus(x):
    return jnp.maximum(x, 0.0) + jnp.log1p(jnp.exp(-jnp.abs(x)))


def _dot(a, b, precision=None):
    return jnp.dot(a, b, preferred_element_type=F32, precision=precision)


def _dot_nt(a, b, precision=None):
    return lax.dot_general(a, b, (((1,), (1,)), ((), ())), preferred_element_type=F32, precision=precision)


def _shift_rows(x, s):
    y = pltpu.roll(x, s, axis=0)
    head_rows = lax.broadcasted_iota(jnp.int32, (8, x.shape[1]), 0)
    return jnp.concatenate([jnp.where(head_rows < s, 0.0, y[0:8]), y[8:]], axis=0)


def _causal_conv_rows(x, w_ref, width):
    y = x * w_ref[width - 1:width, :]
    for i in range(width - 1):
        y = y + _shift_rows(x, width - 1 - i) * w_ref[i:i + 1, :]
    return y


def _lane_col(x, lane):
    lanes = lax.broadcasted_iota(jnp.int32, x.shape, 1)
    return jnp.sum(jnp.where(lanes == lane, x, 0.0), axis=-1, keepdims=True)


def _decay_matrix(gc_col, c):
    ii = lax.broadcasted_iota(jnp.int32, (c, c), 0)
    jj = lax.broadcasted_iota(jnp.int32, (c, c), 1)
    gcb = jnp.broadcast_to(gc_col, (c, c))
    gc_row = jnp.sum(jnp.where(ii == jj, gcb, 0.0), axis=0, keepdims=True)
    low = ii >= jj
    gam = jnp.where(low, jnp.exp(jnp.where(low, gcb - gc_row, 0.0)), 0.0)
    return gam, ii, jj


def _rmsnorm_kernel(x_ref, w_ref, o_ref):
    x = x_ref[...]
    ms = jnp.mean(x * x, axis=-1, keepdims=True)
    o_ref[...] = (x * lax.rsqrt(ms + EPS) * w_ref[...]).astype(o_ref.dtype)


def _rmsnorm_cast(x, w, tm):
    m, d = x.shape
    return pl.pallas_call(
        _rmsnorm_kernel, out_shape=jax.ShapeDtypeStruct((m, d), BF16), grid=(m // tm,),
        in_specs=[pl.BlockSpec((tm, d), lambda i: (i, 0)), pl.BlockSpec((1, d), lambda i: (0, 0))],
        out_specs=pl.BlockSpec((tm, d), lambda i: (i, 0)),
        compiler_params=_cparams("arbitrary"), name="rmsnorm")(x, w.reshape(1, d))


def _matmul_kernel(*refs, n_parts, has_res):
    a_refs = refs[:n_parts]
    b_refs = refs[n_parts:2 * n_parts]
    o_ref = refs[-1]
    acc = None
    for a_ref, b_ref in zip(a_refs, b_refs):
        d = _dot(a_ref[...], b_ref[...].astype(BF16))
        acc = d if acc is None else acc + d
    if has_res:
        acc = acc + refs[2 * n_parts][...]
    o_ref[...] = acc.astype(o_ref.dtype)


def _matmul(a_parts, b, layer, *, tm, tn, res=None, out_dtype=F32, a_single_buffer=False, name="matmul"):
    m = a_parts[0][0].shape[0]
    n = b.shape[2]
    in_specs, args = [], []
    a_mode = dict(pipeline_mode=pl.Buffered(1)) if a_single_buffer else {}
    for arr, kp, cb, rb in a_parts:
        in_specs.append(pl.BlockSpec((tm, kp), functools.partial(lambda i, j, cb: (i, cb), cb=cb), **a_mode))
        args.append(arr)
    for arr, kp, cb, rb in a_parts:
        in_specs.append(pl.BlockSpec((None, kp, tn), functools.partial(lambda i, j, rb: (layer, rb, j), rb=rb)))
        args.append(b)
    if res is not None:
        in_specs.append(pl.BlockSpec((tm, tn), lambda i, j: (i, j)))
        args.append(res)
    return pl.pallas_call(
        functools.partial(_matmul_kernel, n_parts=len(a_parts), has_res=res is not None),
        out_shape=jax.ShapeDtypeStruct((m, n), out_dtype), grid=(m // tm, n // tn),
        in_specs=in_specs, out_specs=pl.BlockSpec((tm, tn), lambda i, j: (i, j)),
        compiler_params=_cparams("arbitrary", "arbitrary"), name=name)(*args)


def _out_proj_kernel(*refs, widths):
    n = len(widths)
    a_refs, as_refs = refs[:n], refs[n:2 * n]
    w_ref, res_ref, ress_ref, o_ref, os_ref, b_s = refs[2 * n:]

    def project(parts, res):
        acc, row = res[...], 0
        for part, kp in zip(parts, widths):
            acc = acc + _dot(part[...], b_s[row:row + kp, :])
            row += kp
        return acc

    @pl.when(pl.program_id(1) == 0)
    def _():
        b_s[...] = w_ref[...].astype(BF16)
        os_ref[...] = project(as_refs, ress_ref)

    o_ref[...] = project(a_refs, res_ref)


def _out_proj(parts, parts_s, w, layer, res, res_s, *, tm, tn):
    widths = tuple(p.shape[1] for p in parts)
    m, ms = parts[0].shape[0], parts_s[0].shape[0]
    k, n = w.shape[1], w.shape[2]
    assert sum(widths) == k
    in_specs = [pl.BlockSpec((tm, kp), lambda j, i: (i, 0)) for kp in widths]
    in_specs += [pl.BlockSpec((ms, kp), lambda j, i: (0, 0)) for kp in widths]
    in_specs += [pl.BlockSpec((None, k, tn), lambda j, i: (layer, 0, j)),
                 pl.BlockSpec((tm, tn), lambda j, i: (i, j)), pl.BlockSpec((ms, tn), lambda j, i: (0, j))]
    return pl.pallas_call(
        functools.partial(_out_proj_kernel, widths=widths),
        out_shape=(jax.ShapeDtypeStruct((m, n), F32), jax.ShapeDtypeStruct((ms, n), F32)), grid=(n // tn, m // tm),
        in_specs=in_specs,
        out_specs=(pl.BlockSpec((tm, tn), lambda j, i: (i, j)), pl.BlockSpec((ms, tn), lambda j, i: (0, j))),
        scratch_shapes=[pltpu.VMEM((k, tn), BF16)],
        compiler_params=_cparams("arbitrary", "arbitrary"), name="out_proj")(*parts, *parts_s, w, res, res_s)


def _in_proj_kernel(a_ref, as_ref, w_ref, o_ref, os_ref, b_s):
    @pl.when(pl.program_id(1) == 0)
    def _():
        b_s[...] = w_ref[...].T.astype(BF16)
        os_ref[...] = _dot(as_ref[...], b_s[...])

    o_ref[...] = _dot(a_ref[...], b_s[...])


def _in_proj(h, hs, wt, layer, n, *, tm, tn, name):
    m, k = h.shape
    ms = hs.shape[0]
    return pl.pallas_call(
        _in_proj_kernel,
        out_shape=(jax.ShapeDtypeStruct((m, n), F32), jax.ShapeDtypeStruct((ms, n), F32)), grid=(n // tn, m // tm),
        in_specs=[pl.BlockSpec((tm, k), lambda j, i: (i, 0)), pl.BlockSpec((ms, k), lambda j, i: (0, 0)),
                  pl.BlockSpec((None, tn, k), lambda j, i: (layer, j, 0))],
        out_specs=(pl.BlockSpec((tm, tn), lambda j, i: (i, j)), pl.BlockSpec((ms, tn), lambda j, i: (0, j))),
        scratch_shapes=[pltpu.VMEM((k, tn), BF16)],
        compiler_params=_cparams("arbitrary", "arbitrary"), name=name)(h, hs, wt)


def _aux_kernel(s_ref, bias_ref, alog_ref, p1_ref, p2_ref, *, t_len, chunk):
    x = s_ref[...]
    lanes = lax.broadcasted_iota(jnp.int32, x.shape, 1)
    sp = _softplus(x + bias_ref[...])
    p1_ref[...] = jnp.where(lanes < DNG_LANE, _sigmoid(x), sp)
    g = -jnp.exp(alog_ref[...]) * sp
    if chunk == 1:
        p2_ref[...] = g
    else:
        ii = lax.broadcasted_iota(jnp.int32, (chunk, chunk), 0)
        jj = lax.broadcasted_iota(jnp.int32, (chunk, chunk), 1)
        tril = jnp.where(ii >= jj, 1.0, 0.0).astype(F32)
        for c in range(t_len // chunk):
            p2_ref[c * chunk:(c + 1) * chunk, :] = _dot(tril, g[c * chunk:(c + 1) * chunk, :], HIGHEST)


def _aux(z, bias_vec, alog_vec, n_seq, t_len, chunk):
    m = n_seq * t_len
    blk = pl.BlockSpec((t_len, LANE), lambda b: (b, 0))
    vec = pl.BlockSpec((1, LANE), lambda b: (0, 0))
    out = pl.BlockSpec((t_len, LANE), lambda b: (b, 0))
    return pl.pallas_call(
        functools.partial(_aux_kernel, t_len=t_len, chunk=chunk),
        out_shape=(jax.ShapeDtypeStruct((m, LANE), F32), jax.ShapeDtypeStruct((m, LANE), F32)),
        grid=(n_seq,), in_specs=[blk, vec, vec], out_specs=(out, out),
        compiler_params=_cparams("arbitrary"), name="aux")(z, bias_vec, alog_vec)


_ONES_ROWS = 16


def _attn_prompt_kernel(q_ref, k_ref, v_ref, wq_ref, wk_ref, o_ref, kn_ref, kb_ref, vt_ref, kmean_ref,
                        *, t_len):
    nb = t_len // MOBA_BLOCK
    g = pl.program_id(2)

    @pl.when(g == 0)
    def _():
        k = k_ref[...]
        kn = k * lax.rsqrt(jnp.mean(k * k, axis=-1, keepdims=True) + EPS) * wk_ref[...]
        kn_ref[...] = kn
        kb_ref[...] = kn.astype(BF16)
        for n in range(nb):
            kmean_ref[n:n + 1, :] = jnp.mean(kn[n * MOBA_BLOCK:(n + 1) * MOBA_BLOCK, :], axis=0, keepdims=True)
        vt_ref[0:HEAD_DIM, :] = v_ref[...].T.astype(BF16)
        vt_ref[HEAD_DIM:HEAD_DIM + _ONES_ROWS, :] = jnp.ones((_ONES_ROWS, t_len), BF16)

    q = q_ref[...]
    qn = q * lax.rsqrt(jnp.mean(q * q, axis=-1, keepdims=True) + EPS) * wq_ref[...]
    gate = _dot_nt(kmean_ref[...], qn, HIGHEST)
    blk = lax.broadcasted_iota(jnp.int32, (nb, t_len), 0)
    own = lax.broadcasted_iota(jnp.int32, (nb, t_len), 1) // MOBA_BLOCK
    valid = blk < own
    gm = jnp.where(valid, gate, -jnp.inf)
    cnt = jnp.zeros((nb, t_len), F32)
    for m in range(nb):
        row = gm[m:m + 1, :]
        beats = jnp.where(row > gm, 1.0, jnp.where(row == gm, jnp.where(blk > m, 1.0, 0.0), 0.0))
        cnt = cnt + beats
    bias = jnp.where(valid, jnp.where(cnt < MOBA_TOPK, 0.0, NEG_BIG), NEG_BIG)
    qs = (qn * (HEAD_DIM ** -0.5 * math.log2(math.e))).astype(BF16)
    kk = lax.broadcasted_iota(jnp.int32, (MOBA_BLOCK, MOBA_BLOCK), 0)
    qq = lax.broadcasted_iota(jnp.int32, (MOBA_BLOCK, MOBA_BLOCK), 1)
    causal = jnp.where(kk <= qq, 0.0, NEG_BIG)
    for qi in range(nb):
        n_keys = (qi + 1) * MOBA_BLOCK
        cols = slice(qi * MOBA_BLOCK, (qi + 1) * MOBA_BLOCK)
        st = _dot_nt(kb_ref[0:n_keys, :], qs[cols, :])
        blocks = [st[n * MOBA_BLOCK:(n + 1) * MOBA_BLOCK, :] + bias[n:n + 1, cols] for n in range(qi)]
        blocks.append(st[qi * MOBA_BLOCK:n_keys, :] + causal)
        mx = jnp.max(blocks[0], axis=0, keepdims=True)
        for blk_s in blocks[1:]:
            mx = jnp.maximum(mx, jnp.max(blk_s, axis=0, keepdims=True))
        p = [jnp.exp2(blk_s - mx).astype(BF16) for blk_s in blocks]
        p = jnp.concatenate(p, axis=0) if qi else p[0]
        ot = _dot(vt_ref[:, 0:n_keys], p)
        o = ot[0:HEAD_DIM, :] / ot[HEAD_DIM:HEAD_DIM + 1, :]
        o_ref[cols, :] = o.T.astype(o_ref.dtype)


def _attn_prompt(z, wq, wk, n_seq, t_len):
    m = n_seq * t_len
    qblk = pl.BlockSpec((t_len, HEAD_DIM), lambda b, k, g: (b, Q_OFF // HEAD_DIM + k * ATT_GROUP + g))
    kblk = pl.BlockSpec((t_len, HEAD_DIM), lambda b, k, g: (b, K_OFF // HEAD_DIM + k))
    vblk = pl.BlockSpec((t_len, HEAD_DIM), lambda b, k, g: (b, V_OFF // HEAD_DIM + k))
    wspec = pl.BlockSpec((1, HEAD_DIM), lambda b, k, g: (0, 0))
    return pl.pallas_call(
        functools.partial(_attn_prompt_kernel, t_len=t_len),
        out_shape=(jax.ShapeDtypeStruct((m, ATT_HEADS * HEAD_DIM), BF16),
                   jax.ShapeDtypeStruct((m, ATT_KV_HEADS * HEAD_DIM), F32)),
        grid=(n_seq, ATT_KV_HEADS, ATT_GROUP),
        in_specs=[qblk, kblk, vblk, wspec, wspec],
        out_specs=(pl.BlockSpec((t_len, HEAD_DIM), lambda b, k, g: (b, k * ATT_GROUP + g)),
                   pl.BlockSpec((t_len, HEAD_DIM), lambda b, k, g: (b, k))),
        scratch_shapes=[pltpu.VMEM((t_len, HEAD_DIM), BF16), pltpu.VMEM((HEAD_DIM + _ONES_ROWS, t_len), BF16),
                        pltpu.VMEM((t_len // MOBA_BLOCK, HEAD_DIM), F32)],
        compiler_params=_cparams("arbitrary", "arbitrary", "arbitrary"), name="attn_prompt")(
            z, z, z, wq.reshape(1, HEAD_DIM), wk.reshape(1, HEAD_DIM))


def _tri_inverse(a, c):
    ii = lax.broadcasted_iota(jnp.int32, (c, c), 0)
    jj = lax.broadcasted_iota(jnp.int32, (c, c), 1)
    eye = jnp.where(ii == jj, 1.0, 0.0).astype(F32)
    p = -a
    inv = eye + p
    steps = int(math.log2(c)) - 1
    for _ in range(steps):
        p = _dot(p, p)
        inv = inv + _dot(inv, p)
    return inv


_GDN_UNROLL = 8
_GDN_HEADS_PER_STEP = 2


def _gdn_prompt_kernel(zq_ref, zk_ref, zv_ref, wq_ref, wk_ref, wv_ref, p1_ref, p2_ref, zg_ref, nw_ref,
                       o_ref, s_ref, q_s, k_s, v_s, beta_s, gc_s, qp_s, op_s, mn_s, nn_s, *, t_len):
    hp = pl.program_id(1)
    c = CHUNK
    d = HEAD_DIM

    def l2n(x):
        return x * lax.rsqrt(jnp.sum(x * x, axis=-1, keepdims=True) + EPS)

    ii = lax.broadcasted_iota(jnp.int32, (c, c), 0)
    jj = lax.broadcasted_iota(jnp.int32, (c, c), 1)
    eye = jnp.where(ii == jj, 1.0, 0.0).astype(F32)

    for hh in range(_GDN_HEADS_PER_STEP):
        lanes = slice(hh * d, (hh + 1) * d)
        h = hp * _GDN_HEADS_PER_STEP + hh
        q_s[hh] = l2n(_silu(_causal_conv_rows(zq_ref[:, lanes], wq_ref.at[:, lanes], CONV_W))) * (HEAD_DIM ** -0.5)
        k_s[hh] = l2n(_silu(_causal_conv_rows(zk_ref[:, lanes], wk_ref.at[:, lanes], CONV_W)))
        v_s[hh] = _silu(_causal_conv_rows(zv_ref[:, lanes], wv_ref.at[:, lanes], CONV_W))
        beta_s[hh] = jnp.broadcast_to(_lane_col(p1_ref[...], BETA_LANE + h), (t_len, HEAD_DIM))
        gc_s[hh] = jnp.broadcast_to(_lane_col(p2_ref[...], DNG_LANE + h), (t_len, HEAD_DIM))
        lax.fori_loop(0, t_len // (c * _GDN_UNROLL),
                      functools.partial(_gdn_prepare_group, refs=(q_s.at[hh], k_s.at[hh], v_s.at[hh], beta_s.at[hh],
                                                                  gc_s.at[hh], qp_s.at[hh], op_s.at[hh], mn_s.at[hh],
                                                                  nn_s.at[hh]), consts=(ii, jj, eye)), 0)

    def chunk_step(ci, states):
        r = pl.ds(pl.multiple_of(ci * c, c), c)
        rd = pl.ds(pl.multiple_of(ci * d, d), d)
        ms = [_dot(jnp.concatenate([mn_s[hh, rd, :], qp_s[hh, r, :]], axis=0), states[hh])
              for hh in range(_GDN_HEADS_PER_STEP)]
        new = []
        for hh in range(_GDN_HEADS_PER_STEP):
            op_s[hh, r, :] = ms[hh][d:d + c] + op_s[hh, r, :]
            e = jnp.exp(gc_s[hh, pl.ds(ci * c + c - 1, 1), :])
            new.append(states[hh] * e[:, 0:1] + ms[hh][0:d] + nn_s[hh, rd, :])
        return tuple(new)

    zero = jnp.zeros((HEAD_DIM, HEAD_DIM), F32)
    s_fin = lax.fori_loop(0, t_len // c, chunk_step, (zero,) * _GDN_HEADS_PER_STEP)
    for hh in range(_GDN_HEADS_PER_STEP):
        lanes = slice(hh * d, (hh + 1) * d)
        s_ref[hh] = s_fin[hh]
        o = op_s[hh]
        on = o * lax.rsqrt(jnp.mean(o * o, axis=-1, keepdims=True) + EPS) * nw_ref[...]
        o_ref[:, lanes] = (on * _silu(zg_ref[:, lanes])).astype(o_ref.dtype)


def _gdn_prepare_group(gi, carry, *, refs, consts):
    q_s, k_s, v_s, beta_s, gc_s, qp_s, op_s, mn_s, nn_s = refs
    ii, jj, eye = consts
    c = CHUNK
    d = HEAD_DIM
    n = _GDN_UNROLL
    chunks = [gi * n + u for u in range(n)]
    rows = [pl.ds(pl.multiple_of(ci * c, c), c) for ci in chunks]
    q = [q_s[r, :] for r in rows]
    k = [k_s[r, :] for r in rows]
    gcb = [gc_s[r, :] for r in rows]
    kb = [k[u] * beta_s[rows[u], :] for u in range(n)]
    gam = [_decay_matrix(gcb[u][:, 0:1], c)[0] for u in range(n)]
    kq = [_dot_nt(jnp.concatenate([kb[u], q[u]], axis=0), k[u]) for u in range(n)]
    attn = [kq[u][c:2 * c] * gam[u] for u in range(n)]
    p = [-jnp.where(ii > jj, kq[u][0:c] * gam[u], 0.0) for u in range(n)]
    inv = [eye + p[u] for u in range(n)]
    for _ in range(int(math.log2(c)) - 1):
        p = [_dot(p[u], p[u]) for u in range(n)]
        inv = [inv[u] + _dot(inv[u], p[u]) for u in range(n)]
    eg = [jnp.exp(gcb[u]) for u in range(n)]
    rhs = [jnp.concatenate([v_s[rows[u], :] * beta_s[rows[u], :], kb[u] * eg[u]], axis=1) for u in range(n)]
    uw = [_dot(inv[u], rhs[u]) for u in range(n)]
    auw = [_dot(attn[u], uw[u]) for u in range(n)]
    k_dec = [k[u] * jnp.exp(gcb[u][c - 1:c, :] - gcb[u]) for u in range(n)]
    kuw = [_dot(k_dec[u].T, uw[u]) for u in range(n)]
    for u in range(n):
        rd = pl.ds(pl.multiple_of(chunks[u] * d, d), d)
        op_s[rows[u], :] = auw[u][:, 0:d]
        qp_s[rows[u], :] = eg[u] * q[u] - auw[u][:, d:2 * d]
        nn_s[rd, :] = kuw[u][:, 0:d]
        mn_s[rd, :] = -kuw[u][:, d:2 * d]
    return carry


def _gdn_prompt(z, p1, p2, conv_w, norm_w, n_seq, t_len):
    m = n_seq * t_len
    hps = _GDN_HEADS_PER_STEP
    width = hps * HEAD_DIM
    base = DNQKV_OFF // width

    def zcol(off):
        return pl.BlockSpec((t_len, width), functools.partial(lambda b, h, off: (b, off + h), off=off))

    def wcol(off):
        return pl.BlockSpec((CONV_W, width), functools.partial(lambda b, h, off: (0, off + h), off=off))

    aux = pl.BlockSpec((t_len, LANE), lambda b, h: (b, 0))
    tbuf = pltpu.VMEM((hps, t_len, HEAD_DIM), F32)
    groups = DN_HEADS // hps
    return pl.pallas_call(
        functools.partial(_gdn_prompt_kernel, t_len=t_len),
        out_shape=(jax.ShapeDtypeStruct((m, DN_WIDTH), BF16),
                   jax.ShapeDtypeStruct((n_seq, DN_HEADS, HEAD_DIM, HEAD_DIM), F32)),
        grid=(n_seq, groups),
        in_specs=[zcol(base), zcol(base + groups), zcol(base + 2 * groups),
                  wcol(0), wcol(groups), wcol(2 * groups), aux, aux,
                  zcol(DNZ_OFF // width), pl.BlockSpec((1, HEAD_DIM), lambda b, h: (0, 0))],
        out_specs=(pl.BlockSpec((t_len, width), lambda b, h: (b, h)),
                   pl.BlockSpec((None, hps, HEAD_DIM, HEAD_DIM), lambda b, h: (b, h, 0, 0))),
        scratch_shapes=[tbuf] * 7 + [pltpu.VMEM((hps, t_len // CHUNK * HEAD_DIM, HEAD_DIM), F32)] * 2,
        compiler_params=_cparams("arbitrary", "arbitrary"), name="gdn_prompt")(
            z, z, z, conv_w, conv_w, conv_w, p1, p2, z, norm_w.reshape(1, HEAD_DIM))


_GH = SSD_HEADS // SSD_GROUPS
_GW = _GH * SSD_HEAD_DIM
_SSD_UNROLL = 4


def _ssd_prompt_kernel(zx_ref, zb_ref, zc_ref, wx_ref, wb_ref, wc_ref, bx_ref, bb_ref, bc_ref, p1_ref, p2_ref,
                       zg_ref, d_ref, nw_ref, o_ref, st_ref, x_s, b_s, c_s, y_s, state_s, *, t_len):
    grp = pl.program_id(1)
    c = CHUNK
    x_s[...] = _silu(_causal_conv_rows(zx_ref[...], wx_ref, CONV_W) + bx_ref[...])
    b_s[...] = _silu(_causal_conv_rows(zb_ref[...], wb_ref, CONV_W) + bb_ref[...])
    c_s[...] = _silu(_causal_conv_rows(zc_ref[...], wc_ref, CONV_W) + bc_ref[...])
    state_s[...] = jnp.zeros_like(state_s)

    def chunk_group(gi, carry):
        n = _SSD_UNROLL
        rows = [pl.ds(pl.multiple_of((gi * n + u) * c, c), c) for u in range(n)]
        cm = [c_s[r, :] for r in rows]
        bm = [b_s[r, :] for r in rows]
        cb = [_dot_nt(cm[u], bm[u]) for u in range(n)]
        eac, dec, xs, xv, gam = [], [], [], [], []
        for u in range(n):
            x, p1, p2 = x_s[rows[u], :], p1_ref[rows[u], :], p2_ref[rows[u], :]
            eac_h, dec_h, xs_h, xv_h, gam_h = [], [], [], [], []
            for hh in range(_GH):
                lane = SSD_LANE + grp * _GH + hh
                ac = _lane_col(p2, lane)
                a_last = ac[c - 1:c, :]
                xv_hh = x[:, hh * SSD_HEAD_DIM:(hh + 1) * SSD_HEAD_DIM] * _lane_col(p1, lane)
                gam_h.append(_decay_matrix(ac, c)[0])
                xv_h.append(xv_hh)
                xs_h.append(xv_hh * jnp.exp(a_last - ac))
                eac_h.append(jnp.broadcast_to(jnp.exp(ac), (c, SSD_HEAD_DIM)))
                dec_h.append(jnp.broadcast_to(jnp.exp(a_last), (1, SSD_HEAD_DIM)))
            gam.append(gam_h)
            xv.append(xv_h)
            xs.append(jnp.concatenate(xs_h, axis=1))
            eac.append(jnp.concatenate(eac_h, axis=1))
            dec.append(jnp.concatenate(dec_h, axis=1))
        y_intra = [jnp.concatenate([_dot(cb[u] * gam[u][hh], xv[u][hh]) for hh in range(_GH)], axis=1)
                   for u in range(n)]
        new_states = [_dot(bm[u].T, xs[u]) for u in range(n)]
        s_in = [state_s[...]]
        for u in range(n):
            s_in.append(s_in[u] * dec[u] + new_states[u])
        state_s[...] = s_in[n]
        for u in range(n):
            y_s[rows[u], :] = y_intra[u] + eac[u] * _dot(cm[u], s_in[u])
        return carry

    lax.fori_loop(0, t_len // (c * _SSD_UNROLL), chunk_group, 0)
    for hh in range(_GH):
        st_ref[hh] = state_s[:, hh * SSD_HEAD_DIM:(hh + 1) * SSD_HEAD_DIM]
    y = (y_s[...] + d_ref[...] * x_s[...]) * _silu(zg_ref[...])
    o_ref[...] = (y * lax.rsqrt(jnp.mean(y * y, axis=-1, keepdims=True) + EPS) * nw_ref[...]).astype(o_ref.dtype)


def _ssd_prompt(z, p1, p2, conv_w, conv_b, d_vec, norm_w, n_seq, t_len):
    m = n_seq * t_len
    xb, bb, cbk = SSDX_OFF // _GW, (SSDX_OFF + SSD_INNER) // LANE, (SSDX_OFF + SSD_INNER + 2 * SSD_STATE) // LANE
    wb0, wc0 = SSD_INNER // LANE, (SSD_INNER + 2 * SSD_STATE) // LANE

    def spec(rows, width, off):
        return pl.BlockSpec((rows, width), functools.partial(lambda b, g, off: (0, off + g), off=off))

    def zspec(width, off):
        return pl.BlockSpec((t_len, width), functools.partial(lambda b, g, off: (b, off + g), off=off))

    aux = pl.BlockSpec((t_len, LANE), lambda b, g: (b, 0))
    return pl.pallas_call(
        functools.partial(_ssd_prompt_kernel, t_len=t_len),
        out_shape=(jax.ShapeDtypeStruct((m, SSD_INNER), BF16),
                   jax.ShapeDtypeStruct((n_seq, SSD_HEADS, SSD_STATE, SSD_HEAD_DIM), F32)),
        grid=(n_seq, SSD_GROUPS),
        in_specs=[zspec(_GW, xb), zspec(LANE, bb), zspec(LANE, cbk),
                  spec(CONV_W, _GW, 0), spec(CONV_W, LANE, wb0), spec(CONV_W, LANE, wc0),
                  spec(1, _GW, 0), spec(1, LANE, wb0), spec(1, LANE, wc0),
                  aux, aux, zspec(_GW, SSDZ_OFF // _GW), spec(1, _GW, 0), spec(1, _GW, 0)],
        out_specs=(pl.BlockSpec((t_len, _GW), lambda b, g: (b, g)),
                   pl.BlockSpec((None, _GH, SSD_STATE, SSD_HEAD_DIM), lambda b, g: (b, g, 0, 0))),
        scratch_shapes=[pltpu.VMEM((t_len, _GW), F32), pltpu.VMEM((t_len, LANE), F32), pltpu.VMEM((t_len, LANE), F32),
                        pltpu.VMEM((t_len, _GW), F32), pltpu.VMEM((SSD_STATE, _GW), F32)],
        compiler_params=_cparams("arbitrary", "arbitrary"), name="ssd_prompt")(
            z, z, z, conv_w, conv_w, conv_w, conv_b, conv_b, conv_b, p1, p2, z, d_vec, norm_w.reshape(1, SSD_INNER))


_FFN_TN = 256
_FFN_NJ = FFN_DIM // _FFN_TN
_TAIL = 8


def _ffn_up_kernel(a_ref, as_ref, bg_ref, bu_ref, wg_ref, wu_ref, cg_ref, cu_ref, s0g_ref, s1g_ref, s0u_ref, s1u_ref,
                   g_ref, sg_ref, su_ref, gs_ref, ygs_ref, yus_ref, b_s, y_s, *, tiles_per_seq, n_sub):
    i = pl.program_id(1)
    tn = _FFN_TN
    tm = a_ref.shape[0]

    @pl.when(i == 0)
    def _():
        b_s[:, 0:tn] = bg_ref[...].astype(BF16)
        b_s[:, tn:2 * tn] = bu_ref[...].astype(BF16)
        ys = _dot(as_ref[...], b_s[...])
        yg, yu = ys[:, 0:tn], ys[:, tn:2 * tn]
        ug = wg_ref[0:1, :] * s0g_ref[...] + wg_ref[1:2, :] * s1g_ref[...] + wg_ref[2:3, :] * yg + cg_ref[...]
        uu = wu_ref[0:1, :] * s0u_ref[...] + wu_ref[1:2, :] * s1u_ref[...] + wu_ref[2:3, :] * yu + cu_ref[...]
        gs_ref[...] = (_silu(ug) * uu).astype(gs_ref.dtype)
        ygs_ref[...] = yg
        yus_ref[...] = yu

    @pl.when(i % tiles_per_seq == 0)
    def _():
        y_s[0:_TAIL, :] = jnp.zeros((_TAIL, 2 * tn), F32)

    @pl.when(i % tiles_per_seq != 0)
    def _():
        y_s[0:_TAIL, :] = y_s[tm:tm + _TAIL, :]

    w = jnp.concatenate([wg_ref[...], wu_ref[...]], axis=1)
    c = jnp.concatenate([cg_ref[...], cu_ref[...]], axis=1)
    ts = tm // n_sub
    for s in range(n_sub):
        r0 = _TAIL + s * ts
        y = _dot(a_ref[s * ts:(s + 1) * ts, :], b_s[...])
        y_s[r0:r0 + ts, :] = y
        u = w[0:1, :] * y_s[r0 - 2:r0 - 2 + ts, :] + w[1:2, :] * y_s[r0 - 1:r0 - 1 + ts, :] + w[2:3, :] * y + c
        g_ref[s * ts:(s + 1) * ts, :] = (_silu(u[:, 0:tn]) * u[:, tn:2 * tn]).astype(g_ref.dtype)
    sg_ref[...] = y_s[tm:tm + _TAIL, 0:tn]
    su_ref[...] = y_s[tm:tm + _TAIL, tn:2 * tn]


def _ffn_up(h2, h2s, w_up, layer, conv_w, conv_b, s0, s1, n_seq, t_len, tm):
    m = n_seq * t_len
    k = h2.shape[1]
    ms = h2s.shape[0]
    nj = _FFN_NJ
    tiles_per_seq = t_len // tm
    lo = lambda j, i: (0, j)
    hi = lambda j, i: (0, j + nj)
    srow = lambda f: pl.BlockSpec((ms, _FFN_TN), f)
    tail = jax.ShapeDtypeStruct((m // tm, _TAIL, FFN_DIM), F32)
    tail_spec = pl.BlockSpec((None, _TAIL, _FFN_TN), lambda j, i: (i, 0, j))
    ysd = jax.ShapeDtypeStruct((ms, FFN_DIM), F32)
    return pl.pallas_call(
        functools.partial(_ffn_up_kernel, tiles_per_seq=tiles_per_seq, n_sub=2),
        out_shape=(jax.ShapeDtypeStruct((m, FFN_DIM), BF16), tail, tail,
                   jax.ShapeDtypeStruct((ms, FFN_DIM), BF16), ysd, ysd),
        grid=(nj, m // tm),
        in_specs=[pl.BlockSpec((tm, k), lambda j, i: (i, 0)), pl.BlockSpec((ms, k), lambda j, i: (0, 0)),
                  pl.BlockSpec((None, k, _FFN_TN), lambda j, i: (layer, 0, j)),
                  pl.BlockSpec((None, k, _FFN_TN), lambda j, i: (layer, 0, j + nj)),
                  pl.BlockSpec((FFN_CONV_W, _FFN_TN), lo), pl.BlockSpec((FFN_CONV_W, _FFN_TN), hi),
                  pl.BlockSpec((1, _FFN_TN), lo), pl.BlockSpec((1, _FFN_TN), hi),
                  srow(lo), srow(lo), srow(hi), srow(hi)],
        out_specs=(pl.BlockSpec((tm, _FFN_TN), lambda j, i: (i, j)), tail_spec, tail_spec,
                   srow(lo), srow(lo), srow(lo)),
        scratch_shapes=[pltpu.VMEM((k, 2 * _FFN_TN), BF16), pltpu.VMEM((_TAIL + tm, 2 * _FFN_TN), F32)],
        compiler_params=_cparams("arbitrary", "arbitrary"), name="ffn_up")(
            h2, h2s, w_up, w_up, conv_w, conv_w, conv_b, conv_b, s0, s1, s0, s1)


_PAGES_PER_STEP = 16


def _kmean_kernel(pt_ref, *refs):
    o_ref = refs[-1]
    page = refs[0].shape[0]
    per_block = MOBA_BLOCK // page
    for n in range(_PAGES_PER_STEP // per_block):
        acc = jnp.sum(refs[n * per_block][...], axis=0)
        for r in range(1, per_block):
            acc = acc + jnp.sum(refs[n * per_block + r][...], axis=0)
        o_ref[n] = acc * (1.0 / MOBA_BLOCK)


def _kmean(cache_k, page_table, layer):
    n_b, n_pages = page_table.shape
    page, n_kv, d = cache_k.shape[2:]
    per_block = MOBA_BLOCK // page
    n_blocks = n_pages // per_block
    steps = n_pages // _PAGES_PER_STEP
    in_specs = [pl.BlockSpec((None, None, page, n_kv, d),
                             functools.partial(lambda b, s, pt, r: (layer, pt[b, s * _PAGES_PER_STEP + r], 0, 0, 0), r=r))
                for r in range(_PAGES_PER_STEP)]
    grid_spec = pltpu.PrefetchScalarGridSpec(
        num_scalar_prefetch=1, grid=(n_b, steps), in_specs=in_specs,
        out_specs=pl.BlockSpec((None, _PAGES_PER_STEP // per_block, n_kv, d), lambda b, s, pt: (b, s, 0, 0)))
    return pl.pallas_call(
        _kmean_kernel, out_shape=jax.ShapeDtypeStruct((n_b, n_blocks, n_kv, d), F32), grid_spec=grid_spec,
        compiler_params=_cparams("arbitrary", "arbitrary"), name="kmean")(page_table, *([cache_k] * _PAGES_PER_STEP))


def _select_kernel(q_ref, k_ref, kmean_ref, wq_ref, wk_ref, qn_ref, kn_ref, idx_ref):
    n_b, n_h, _ = q_ref.shape
    n_blocks = kmean_ref.shape[1]
    q = q_ref[...]
    qn = q * lax.rsqrt(jnp.mean(q * q, axis=-1, keepdims=True) + EPS) * wq_ref[...]
    qn_ref[...] = qn
    k = k_ref[...]
    kn_ref[...] = k * lax.rsqrt(jnp.mean(k * k, axis=-1, keepdims=True) + EPS) * wk_ref[...]
    head = lax.broadcasted_iota(jnp.int32, (n_h, n_blocks), 0)
    lane = lax.broadcasted_iota(jnp.int32, (n_h, n_blocks), 1).astype(F32)
    lane_out = lax.broadcasted_iota(jnp.int32, (n_h, LANE), 1)
    for b in range(n_b):
        gate = jnp.zeros((n_h, n_blocks), F32)
        for kv in range(ATT_KV_HEADS):
            gk = _dot_nt(qn[b], kmean_ref[b, :, kv * HEAD_DIM:(kv + 1) * HEAD_DIM], HIGHEST)
            gate = jnp.where(head // ATT_GROUP == kv, gk, gate)
        out = jnp.zeros((n_h, LANE), F32)
        for s in range(MOBA_TOPK):
            mx = jnp.max(gate, axis=-1, keepdims=True)
            pick = jnp.min(jnp.where(gate == mx, lane, float(n_blocks)), axis=-1, keepdims=True)
            out = jnp.where(lane_out == s, pick, out)
            gate = jnp.where(lane == pick, -jnp.inf, gate)
        idx_ref[b] = out.astype(jnp.int32)


def _select(q3, k3, kmean, wq, wk):
    n_b, n_h, d = q3.shape
    full = lambda shape: pl.BlockSpec(shape, lambda i: (0,) * len(shape))
    return pl.pallas_call(
        _select_kernel,
        out_shape=(jax.ShapeDtypeStruct(q3.shape, F32), jax.ShapeDtypeStruct(k3.shape, F32),
                   jax.ShapeDtypeStruct((n_b, n_h, LANE), jnp.int32)),
        grid=(1,),
        in_specs=[full(q3.shape), full(k3.shape), full(kmean.shape), full((1, 1, d)), full((1, 1, d))],
        out_specs=(full(q3.shape), full(k3.shape), full((n_b, n_h, LANE))),
        compiler_params=_cparams("arbitrary"), name="select")(q3, k3, kmean, wq.reshape(1, 1, d), wk.reshape(1, 1, d))


def _attn_sample_kernel(pt_ref, idx_ref, qn_ref, kn_ref, vn_ref, *refs, n_sel, per_block):
    o_ref = refs[-1]
    pages = refs[:-1]
    n_pg = n_sel * per_block
    k_refs, v_refs = pages[:n_pg], pages[n_pg:]
    h = pl.program_id(1)
    kv = h // ATT_GROUP
    q = qn_ref[pl.ds(h, 1), :] * (HEAD_DIM ** -0.5)
    k_new = kn_ref[pl.ds(kv, 1), :]
    v_new = vn_ref[pl.ds(kv, 1), :]
    s_own = jnp.sum(q * k_new, axis=-1, keepdims=True)
    mine = pl.ds(kv, k_refs[0].shape[0] // ATT_KV_HEADS, stride=ATT_KV_HEADS)
    logits = [jnp.sum(r[mine, :] * q, axis=-1, keepdims=True) for r in k_refs]
    mx = s_own
    for s in logits:
        mx = jnp.maximum(mx, jnp.max(s, axis=0, keepdims=True))
    p_own = jnp.exp(s_own - mx)
    den = p_own
    acc = p_own * v_new
    for s, v_ref in zip(logits, v_refs):
        p = jnp.exp(s - mx)
        den = den + jnp.sum(p, axis=0, keepdims=True)
        acc = acc + jnp.sum(p * v_ref[mine, :], axis=0, keepdims=True)
    o_ref[pl.ds(h, 1), :] = acc / den


def _attn_sample(cache_k, cache_v, page_table, idx, qn, kn, vn, layer):
    n_b, n_h, d = qn.shape
    depth, n_pool, page, n_kv = cache_k.shape[:4]
    per_block = MOBA_BLOCK // page
    n_sel = idx.shape[2]
    cache_k = cache_k.reshape(depth, n_pool, page * n_kv, d)
    cache_v = cache_v.reshape(depth, n_pool, page * n_kv, d)

    def page_spec(s, r):
        def imap(b, h, pt, ix):
            return (layer, pt[b, ix[b, h * n_sel + s] * per_block + r], 0, 0)
        return pl.BlockSpec((None, None, page * n_kv, d), imap)

    pspecs = [page_spec(s, r) for s in range(n_sel) for r in range(per_block)]
    slab = lambda n: pl.BlockSpec((None, n, d), lambda b, h, pt, ix: (b, 0, 0))
    grid_spec = pltpu.PrefetchScalarGridSpec(
        num_scalar_prefetch=2, grid=(n_b, n_h),
        in_specs=[slab(n_h), slab(ATT_KV_HEADS), slab(ATT_KV_HEADS)] + pspecs + pspecs,
        out_specs=slab(n_h))
    n_pg = len(pspecs)
    return pl.pallas_call(
        functools.partial(_attn_sample_kernel, n_sel=n_sel, per_block=per_block),
        out_shape=jax.ShapeDtypeStruct((n_b, n_h, d), F32), grid_spec=grid_spec,
        compiler_params=_cparams("arbitrary", "arbitrary"), name="attn_sample")(
            page_table, idx.reshape(n_b, n_h * n_sel), qn, kn, vn, *([cache_k] * n_pg), *([cache_v] * n_pg))


def _rows_to_cols(x):
    r, n = x.shape
    return jnp.concatenate([x, jnp.zeros((n - r, n), x.dtype)], axis=0).T


def _mix_sample_kernel(dx_ref, dbuf_ref, dw_ref, dbeta_ref, dg_ref, dz_ref, dnw_ref, dst_ref,
                       sx_ref, sxbuf_ref, sxw_ref, sxb_ref, bc_ref, bcbuf_ref, bcw_ref, bcb_ref,
                       sdt_ref, sa_ref, sz_ref, sd_ref, snw_ref, sst_ref,
                       dno_ref, dnst_ref, so_ref, sso_ref, y_s):
    x = dx_ref[...]
    conv = x * dw_ref[CONV_W - 1]
    for i in range(CONV_W - 1):
        conv = conv + dbuf_ref[i] * dw_ref[i]
    act = _silu(conv)
    nh = DN_HEADS
    q, k, v = act[0:nh], act[nh:2 * nh], act[2 * nh:3 * nh]
    q = q * lax.rsqrt(jnp.sum(q * q, axis=-1, keepdims=True) + EPS) * (HEAD_DIM ** -0.5)
    k = k * lax.rsqrt(jnp.sum(k * k, axis=-1, keepdims=True) + EPS)
    beta = dbeta_ref[...]
    eg = jnp.exp(dg_ref[...])
    qk = jnp.sum(q * k, axis=-1, keepdims=True)
    qt, kt = _rows_to_cols(q), _rows_to_cols(k)
    o_rows = []
    for h in range(nh):
        s0 = dst_ref[h]
        kcol, qcol = kt[:, h:h + 1], qt[:, h:h + 1]
        ks = jnp.sum(kcol * s0, axis=0, keepdims=True)
        qs = jnp.sum(qcol * s0, axis=0, keepdims=True)
        b_h, e_h = beta[h:h + 1, :], eg[h:h + 1, :]
        v_new = v[h:h + 1, :] * b_h - (b_h * e_h) * ks
        o_rows.append(e_h * qs + qk[h:h + 1, :] * v_new)
        dnst_ref[h] = s0 * e_h + kcol * v_new
    o = jnp.concatenate(o_rows, axis=0)
    on = o * lax.rsqrt(jnp.mean(o * o, axis=-1, keepdims=True) + EPS) * dnw_ref[...]
    dno_ref[...] = on * _silu(dz_ref[...])
    xs = sx_ref[...] * sxw_ref[CONV_W - 1] + sxb_ref[...]
    bc = bc_ref[...] * bcw_ref[CONV_W - 1] + bcb_ref[...]
    for i in range(CONV_W - 1):
        xs = xs + sxbuf_ref[i] * sxw_ref[i]
        bc = bc + bcbuf_ref[i] * bcw_ref[i]
    xs = _silu(xs)
    bc = _silu(bc)
    bct = _rows_to_cols(bc)
    cb = jnp.sum(bc[0:SSD_GROUPS] * bc[SSD_GROUPS:2 * SSD_GROUPS], axis=-1, keepdims=True)
    dt = sdt_ref[...]
    ea = jnp.exp(sa_ref[...])
    for h in range(SSD_HEADS):
        grp = h // _GH
        s0 = sst_ref[h]
        bcol, ccol = bct[:, grp:grp + 1], bct[:, SSD_GROUPS + grp:SSD_GROUPS + grp + 1]
        xv = xs[h:h + 1, :] * dt[h:h + 1, :]
        e_h = ea[h:h + 1, :]
        y_s[h:h + 1, :] = cb[grp:grp + 1, :] * xv + e_h * jnp.sum(ccol * s0, axis=0, keepdims=True)
        sso_ref[h] = s0 * e_h + bcol * xv
    y = (y_s[...] + sd_ref[...] * xs) * _silu(sz_ref[...])
    norm_rows = []
    for grp in range(SSD_GROUPS):
        yg = y[grp * _GH:(grp + 1) * _GH]
        ms = jnp.sum(jnp.sum(yg * yg, axis=-1, keepdims=True), axis=0, keepdims=True) * (1.0 / _GW)
        norm_rows.append(yg * lax.rsqrt(ms + EPS))
    so_ref[...] = jnp.concatenate(norm_rows, axis=0) * snw_ref[...]


def _mix_sample(args, n_b):
    def per_b(shape):
        nd = len(shape) - 1
        return pl.BlockSpec((None,) + tuple(shape[1:]), lambda b: (b,) + (0,) * nd)

    def shared(shape):
        nd = len(shape)
        return pl.BlockSpec(tuple(shape), lambda b: (0,) * nd)

    names_per_b = {"dx", "dbuf", "dbeta", "dg", "dz", "dst", "sx", "sxbuf", "bc", "bcbuf", "sdt", "sa", "sz", "sst"}
    order = ["dx", "dbuf", "dw", "dbeta", "dg", "dz", "dnw", "dst", "sx", "sxbuf", "sxw", "sxb", "bc", "bcbuf", "bcw",
             "bcb", "sdt", "sa", "sz", "sd", "snw", "sst"]
    in_specs = [per_b(args[n].shape) if n in names_per_b else shared(args[n].shape) for n in order]
    outs = (jax.ShapeDtypeStruct((n_b, DN_HEADS, HEAD_DIM), F32),
            jax.ShapeDtypeStruct((n_b, DN_HEADS, HEAD_DIM, HEAD_DIM), F32),
            jax.ShapeDtypeStruct((n_b, SSD_HEADS, SSD_HEAD_DIM), F32),
            jax.ShapeDtypeStruct((n_b, SSD_HEADS, SSD_STATE, SSD_HEAD_DIM), F32))
    return pl.pallas_call(
        _mix_sample_kernel, out_shape=outs, grid=(n_b,), in_specs=in_specs,
        out_specs=tuple(per_b(o.shape) for o in outs),
        scratch_shapes=[pltpu.VMEM((SSD_HEADS, SSD_HEAD_DIM), F32)],
        compiler_params=_cparams("arbitrary"), name="mix_sample")(*[args[n] for n in order])


def _split_w_in(w_in):
    depth, k, _ = w_in.shape
    wt = jnp.swapaxes(w_in, 1, 2)
    wt_b = wt[:, ZB_SRC:ZB_SRC + ZB_COLS]
    wt_s = jnp.concatenate([wt[:, ZS_SRC[0]:ZS_SRC[0] + 16], wt[:, ZS_SRC[1]:ZS_SRC[1] + 16],
                            jnp.zeros((depth, LANE - 32, k), w_in.dtype)], axis=1)
    return wt, wt_b, wt_s


def _in_proj_all(h, hs, lw, layer, tm):
    return [_in_proj(h, hs, lw["wt_a"], layer, ZA_COLS, tm=tm, tn=512, name="in_proj_a"),
            _in_proj(h, hs, lw["wt_b"], layer, ZB_COLS, tm=tm, tn=512, name="in_proj_b"),
            _in_proj(h, hs, lw["wt_s"], layer, LANE, tm=tm, tn=LANE, name="in_proj_s")]


def _lane_vec(dn_vals, ssd_vals):
    v = jnp.zeros((LANE,), F32)
    v = v.at[DNG_LANE:DNG_LANE + DN_HEADS].set(dn_vals.astype(F32))
    v = v.at[SSD_LANE:SSD_LANE + SSD_HEADS].set(ssd_vals.astype(F32))
    return v.reshape(1, LANE)


def _ffn_down(g, w_down, layer, x, tm):
    return _matmul([(g, FFN_DIM, 0, 0)], w_down, layer, tm=tm, tn=256, res=x, a_single_buffer=tm >= 512,
                   name="ffn_down")


_PROMPT_TM = 1024


def _prompt_mix(za, zb, zs, lw, n_seq, t_len):
    p1, p2 = _aux(zs, lw["bias_vec"], lw["alog_vec"], n_seq, t_len, CHUNK)
    o_att, k_norm = _attn_prompt(za, lw["attn_q_norm_w"], lw["attn_k_norm_w"], n_seq, t_len)
    o_dn, dn_state = _gdn_prompt(za, p1, p2, lw["dn_conv_w"], lw["dn_norm_w"], n_seq, t_len)
    o_ssd, ssd_state = _ssd_prompt(zb, p1, p2, lw["ssd_conv_w"], lw["ssd_conv_b"].reshape(1, -1), lw["ssd_d_vec"],
                                   lw["ssd_norm_w"], n_seq, t_len)
    za3 = za.reshape(n_seq, t_len, ZA_COLS)
    zb3 = zb.reshape(n_seq, t_len, ZB_COLS)
    new = [k_norm.reshape(n_seq, t_len, ATT_KV_HEADS, HEAD_DIM),
           za3[:, :, V_OFF:V_OFF + 512].reshape(n_seq, t_len, ATT_KV_HEADS, HEAD_DIM),
           za3[:, t_len - (CONV_W - 1):, DNQKV_OFF:DNQKV_OFF + DN_CONV_CH],
           dn_state,
           zb3[:, t_len - (CONV_W - 1):, SSDX_OFF:SSDX_OFF + SSD_CONV_CH],
           ssd_state]
    return [o_att, o_dn, o_ssd], new


_SAMPLE_ROWS = 16


def _sample_mix(za, zb2, zs, lw, layer, cache_k, cache_v, page_table, dn_conv_buf, dn_state, ssd_conv_buf,
                ssd_state):
    n_b = page_table.shape[0]
    rows = za.shape[0]
    p1, p2 = _aux(zs, lw["bias_vec"], lw["alog_vec"], 1, rows, 1)
    zb = za[:n_b]
    zbs = zb2[:n_b]
    kmean = _kmean(cache_k, page_table, layer)
    kmean = kmean.reshape(n_b, kmean.shape[1], ATT_KV_HEADS * HEAD_DIM)
    q3 = zb[:, Q_OFF:Q_OFF + 2048].reshape(n_b, ATT_HEADS, HEAD_DIM)
    k3 = zb[:, K_OFF:K_OFF + 512].reshape(n_b, ATT_KV_HEADS, HEAD_DIM)
    v3 = zb[:, V_OFF:V_OFF + 512].reshape(n_b, ATT_KV_HEADS, HEAD_DIM)
    qn, kn, idx = _select(q3, k3, kmean, lw["attn_q_norm_w"], lw["attn_k_norm_w"])
    o_att = _attn_sample(cache_k, cache_v, page_table, idx[:, :, :MOBA_TOPK], qn, kn, v3, layer)
    dn_x = zb[:, DNQKV_OFF:DNQKV_OFF + DN_CONV_CH]
    ssd_x = zbs[:, SSDX_OFF:SSDX_OFF + SSD_CONV_CH]
    nh3 = 3 * DN_HEADS
    ng2 = 2 * SSD_GROUPS
    args = {
        "dx": dn_x.reshape(n_b, nh3, HEAD_DIM),
        "dbuf": dn_conv_buf.reshape(n_b, CONV_W - 1, nh3, HEAD_DIM),
        "dw": lw["dn_conv_w"].reshape(CONV_W, nh3, HEAD_DIM),
        "dbeta": p1[:n_b, BETA_LANE:BETA_LANE + DN_HEADS].reshape(n_b, DN_HEADS, 1),
        "dg": p2[:n_b, DNG_LANE:DNG_LANE + DN_HEADS].reshape(n_b, DN_HEADS, 1),
        "dz": zb[:, DNZ_OFF:DNZ_OFF + DN_WIDTH].reshape(n_b, DN_HEADS, HEAD_DIM),
        "dnw": lw["dn_norm_w"].reshape(1, HEAD_DIM),
        "dst": dn_state,
        "sx": ssd_x[:, :SSD_INNER].reshape(n_b, SSD_HEADS, SSD_HEAD_DIM),
        "sxbuf": ssd_conv_buf[:, :, :SSD_INNER].reshape(n_b, CONV_W - 1, SSD_HEADS, SSD_HEAD_DIM),
        "sxw": lw["ssd_conv_w"][:, :SSD_INNER].reshape(CONV_W, SSD_HEADS, SSD_HEAD_DIM),
        "sxb": lw["ssd_conv_b"][:SSD_INNER].reshape(SSD_HEADS, SSD_HEAD_DIM),
        "bc": ssd_x[:, SSD_INNER:].reshape(n_b, ng2, SSD_STATE),
        "bcbuf": ssd_conv_buf[:, :, SSD_INNER:].reshape(n_b, CONV_W - 1, ng2, SSD_STATE),
        "bcw": lw["ssd_conv_w"][:, SSD_INNER:].reshape(CONV_W, ng2, SSD_STATE),
        "bcb": lw["ssd_conv_b"][SSD_INNER:].reshape(ng2, SSD_STATE),
        "sdt": p1[:n_b, SSD_LANE:SSD_LANE + SSD_HEADS].reshape(n_b, SSD_HEADS, 1),
        "sa": p2[:n_b, SSD_LANE:SSD_LANE + SSD_HEADS].reshape(n_b, SSD_HEADS, 1),
        "sz": zbs[:, SSDZ_OFF:SSDZ_OFF + SSD_INNER].reshape(n_b, SSD_HEADS, SSD_HEAD_DIM),
        "sd": lw["ssd_D"].astype(F32).reshape(SSD_HEADS, 1),
        "snw": lw["ssd_norm_w"].reshape(SSD_HEADS, SSD_HEAD_DIM),
        "sst": ssd_state,
    }
    o_dn, dn_state_new, o_ssd, ssd_state_new = _mix_sample(args, n_b)
    pad = lambda a: jnp.pad(a.reshape(n_b, -1), ((0, rows - n_b), (0, 0))).astype(BF16)
    new = [kn.reshape(n_b, 1, ATT_KV_HEADS, HEAD_DIM), v3.reshape(n_b, 1, ATT_KV_HEADS, HEAD_DIM),
           jnp.concatenate([dn_conv_buf[:, 1:], dn_x[:, None]], axis=1), dn_state_new,
           jnp.concatenate([ssd_conv_buf[:, 1:], ssd_x[:, None]], axis=1), ssd_state_new]
    return [pad(o_att), pad(o_dn), pad(o_ssd)], new


def _layer(xp, xs, lw, layer, n_seq, t_len, cache_k, cache_v, page_table, dn_conv_buf, dn_state, ssd_conv_buf,
           ssd_state, ffn_conv_buf):
    tm = _PROMPT_TM
    n_b = page_table.shape[0]
    rows = xs.shape[0]
    h = _rmsnorm_cast(xp, lw["norm1_w"], 512)
    hs = _rmsnorm_cast(xs, lw["norm1_w"], rows)
    (za, za_s), (zb, zb_s), (zs, zs_s) = _in_proj_all(h, hs, lw, layer, tm)
    mix_p, new_p = _prompt_mix(za, zb, zs, lw, n_seq, t_len)
    mix_s, new_s = _sample_mix(za_s, zb_s, zs_s, lw, layer, cache_k, cache_v, page_table, dn_conv_buf, dn_state,
                               ssd_conv_buf, ssd_state)
    xp, xs = _out_proj(mix_p, mix_s, lw["w_out"], layer, xp, xs, tm=tm, tn=512)
    h2 = _rmsnorm_cast(xp, lw["norm2_w"], 512)
    h2s = _rmsnorm_cast(xs, lw["norm2_w"], rows)
    padf = lambda a: jnp.pad(a, ((0, rows - n_b), (0, 0)))
    g, tail_g, tail_u, gs, y_g, y_u = _ffn_up(h2, h2s, lw["ffn_w_up"], layer, lw["ffn_conv_w"],
                                              lw["ffn_conv_b"].reshape(1, -1), padf(ffn_conv_buf[:, 0]),
                                              padf(ffn_conv_buf[:, 1]), n_seq, t_len, tm)
    xp = _ffn_down(g, lw["ffn_w_down"], layer, xp, tm)
    xs = _ffn_down(gs, lw["ffn_w_down"], layer, xs, rows)
    tps = t_len // tm
    new_p.append(jnp.concatenate([tail_g[tps - 1::tps, _TAIL - 2:], tail_u[tps - 1::tps, _TAIL - 2:]], axis=-1))
    y_new = jnp.concatenate([y_g[:n_b], y_u[:n_b]], axis=-1)
    new_s.append(jnp.stack([ffn_conv_buf[:, 1], y_new], axis=1))
    return xp, xs, new_p, new_s


def kernel(x_prompt, x_sample, cache_k, cache_v, page_table, state_dn_conv, state_dn, state_ssd_conv, state_ssd, state_ffn_conv, norm1_w, w_in, attn_q_norm_w, attn_k_norm_w, dn_conv_w, dn_A_log, dn_dt_bias, dn_norm_w, ssd_conv_w, ssd_conv_b, ssd_dt_bias, ssd_A_log, ssd_D, ssd_norm_w, w_out, norm2_w, ffn_w_up, ffn_conv_w, ffn_conv_b, ffn_w_down):
    depth = w_in.shape[0]
    n_seq, t_len, d_model = x_prompt.shape
    n_b = x_sample.shape[0]
    yp = x_prompt.reshape(n_seq * t_len, d_model)
    ys = jnp.pad(x_sample.reshape(n_b, d_model), ((0, _SAMPLE_ROWS - n_b), (0, 0)))
    wt_a, wt_b, wt_s = _split_w_in(w_in)
    outs_p, outs_s = [], []
    for l in range(depth):
        lw = {"norm1_w": norm1_w[l], "wt_a": wt_a, "wt_b": wt_b, "wt_s": wt_s, "attn_q_norm_w": attn_q_norm_w[l],
              "attn_k_norm_w": attn_k_norm_w[l], "dn_conv_w": dn_conv_w[l], "dn_norm_w": dn_norm_w[l],
              "ssd_conv_w": ssd_conv_w[l], "ssd_conv_b": ssd_conv_b[l], "ssd_D": ssd_D[l],
              "ssd_norm_w": ssd_norm_w[l], "w_out": w_out, "norm2_w": norm2_w[l], "ffn_w_up": ffn_w_up,
              "ffn_conv_w": ffn_conv_w[l], "ffn_conv_b": ffn_conv_b[l], "ffn_w_down": ffn_w_down,
              "bias_vec": _lane_vec(dn_dt_bias[l], ssd_dt_bias[l]),
              "alog_vec": _lane_vec(dn_A_log[l], ssd_A_log[l]),
              "ssd_d_vec": jnp.repeat(ssd_D[l].astype(F32), SSD_HEAD_DIM).reshape(1, SSD_INNER)}
        yp, ys, new_p, new_s = _layer(yp, ys, lw, l, n_seq, t_len, cache_k, cache_v, page_table, state_dn_conv[l],
                                      state_dn[l], state_ssd_conv[l], state_ssd[l], state_ffn_conv[l])
        outs_p.append(new_p)
        outs_s.append(new_s)
    st = lambda outs, i: jnp.stack([o[i] for o in outs])
    return (yp.reshape(n_seq, t_len, d_model), ys[:n_b].reshape(n_b, 1, d_model),
            st(outs_p, 0), st(outs_p, 1), st(outs_s, 0), st(outs_s, 1),
            st(outs_p, 2), st(outs_s, 2), st(outs_p, 3), st(outs_s, 3),
            st(outs_p, 4), st(outs_s, 4), st(outs_p, 5), st(outs_s, 5),
            st(outs_p, 6), st(outs_s, 6))
```

```python
import functools
import math

import jax
import jax.numpy as jnp
from jax import lax
from jax.experimental import pallas as pl
from jax.experimental.pallas import tpu as pltpu

F32 = jnp.float32
BF16 = jnp.bfloat16
HIGHEST = lax.Precision.HIGHEST

D_MODEL = 4096
HEAD_DIM = 128
ATT_HEADS = 16
ATT_KV_HEADS = 4
ATT_GROUP = 4
MOBA_BLOCK = 256
MOBA_TOPK = 3
DN_HEADS = 8
DN_WIDTH = 1024
DN_CONV_CH = 3072
SSD_INNER = 1024
SSD_HEAD_DIM = 64
SSD_HEADS = 16
SSD_GROUPS = 2
SSD_STATE = 128
SSD_CONV_CH = 1536
CONV_W = 4
CHUNK = 64
FFN_DIM = 11008
FFN_CONV_W = 3
EPS = 1e-6

Q_OFF, K_OFF, V_OFF = 0, 2048, 2560
DNQKV_OFF, DNZ_OFF = 3072, 6144
ZA_COLS = 7168
SSDX_OFF, SSDZ_OFF = 0, 1536
ZB_SRC, ZB_COLS = 7184, 2560
ZS_SRC = (7168, 9744)
LANE = 128
BETA_LANE, DNG_LANE, SSD_LANE = 0, 8, 16

VMEM_LIMIT_BYTES = 56 * 1024 * 1024
NEG_BIG = -1e30


def _cparams(*sem):
    return pltpu.CompilerParams(dimension_semantics=sem, vmem_limit_bytes=VMEM_LIMIT_BYTES)


def _silu(x):
    return x / (1.0 + jnp.exp(-x))


def _sigmoid(x):
    return 1.0 / (1.0 + jnp.exp(-x))


def _softplus(x):
    return jnp.maximum(x, 0.0) + jnp.log1p(jnp.exp(-jnp.abs(x)))


def _dot(a, b, precision=None):
    return jnp.dot(a, b, preferred_element_type=F32, precision=precision)


def _dot_nt(a, b, precision=None):
    return lax.dot_general(a, b, (((1,), (1,)), ((), ())), preferred_element_type=F32, precision=precision)


def _shift_rows(x, s):
    y = pltpu.roll(x, s, axis=0)
    head_rows = lax.broadcasted_iota(jnp.int32, (8, x.shape[1]), 0)
    return jnp.concatenate([jnp.where(head_rows < s, 0.0, y[0:8]), y[8:]], axis=0)


def _causal_conv_rows(x, w_ref, width):
    y = x * w_ref[width - 1:width, :]
    for i in range(width - 1):
        y = y + _shift_rows(x, width - 1 - i) * w_ref[i:i + 1, :]
    return y


def _lane_col(x, lane):
    lanes = lax.broadcasted_iota(jnp.int32, x.shape, 1)
    return jnp.sum(jnp.where(lanes == lane, x, 0.0), axis=-1, keepdims=True)


def _decay_matrix(gc_col, c):
    ii = lax.broadcasted_iota(jnp.int32, (c, c), 0)
    jj = lax.broadcasted_iota(jnp.int32, (c, c), 1)
    gcb = jnp.broadcast_to(gc_col, (c, c))
    gc_row = jnp.sum(jnp.where(ii == jj, gcb, 0.0), axis=0, keepdims=True)
    low = ii >= jj
    gam = jnp.where(low, jnp.exp(jnp.where(low, gcb - gc_row, 0.0)), 0.0)
    return gam, ii, jj


def _rmsnorm_kernel(x_ref, w_ref, o_ref):
    x = x_ref[...]
    ms = jnp.mean(x * x, axis=-1, keepdims=True)
    o_ref[...] = (x * lax.rsqrt(ms + EPS) * w_ref[...]).astype(o_ref.dtype)


def _rmsnorm_cast(x, w, tm):
    m, d = x.shape
    return pl.pallas_call(
        _rmsnorm_kernel, out_shape=jax.ShapeDtypeStruct((m, d), BF16), grid=(m // tm,),
        in_specs=[pl.BlockSpec((tm, d), lambda i: (i, 0)), pl.BlockSpec((1, d), lambda i: (0, 0))],
        out_specs=pl.BlockSpec((tm, d), lambda i: (i, 0)),
        compiler_params=_cparams("arbitrary"), name="rmsnorm")(x, w.reshape(1, d))


def _ffn_down_kernel(g_ref, w_ref, res_ref, o_ref):
    o_ref[...] = res_ref[...] + _dot(g_ref[...], w_ref[...].astype(BF16))


def _ffn_down(g, w_down, layer, res, tm):
    m, k = g.shape
    n = w_down.shape[2]
    tn = 256
    g_mode = dict(pipeline_mode=pl.Buffered(1)) if tm >= 512 else {}
    return pl.pallas_call(
        _ffn_down_kernel, out_shape=jax.ShapeDtypeStruct((m, n), F32), grid=(m // tm, n // tn),
        in_specs=[pl.BlockSpec((tm, k), lambda i, j: (i, 0), **g_mode),
                  pl.BlockSpec((None, k, tn), lambda i, j: (layer, 0, j)),
                  pl.BlockSpec((tm, tn), lambda i, j: (i, j))],
        out_specs=pl.BlockSpec((tm, tn), lambda i, j: (i, j)),
        compiler_params=_cparams("arbitrary", "arbitrary"), name="ffn_down")(g, w_down, res)


def _out_proj_kernel(*refs, widths):
    n = len(widths)
    a_refs, as_refs = refs[:n], refs[n:2 * n]
    w_ref, res_ref, ress_ref, o_ref, os_ref, b_s = refs[2 * n:]

    def project(parts, res):
        acc, row = res[...], 0
        for part, kp in zip(parts, widths):
            acc = acc + _dot(part[...], b_s[row:row + kp, :])
            row += kp
        return acc

    @pl.when(pl.program_id(1) == 0)
    def _():
        b_s[...] = w_ref[...].astype(BF16)
        os_ref[...] = project(as_refs, ress_ref)

    o_ref[...] = project(a_refs, res_ref)


def _out_proj(parts, parts_s, w, layer, res, res_s, *, tm, tn):
    widths = tuple(p.shape[1] for p in parts)
    m, ms = parts[0].shape[0], parts_s[0].shape[0]
    k, n = w.shape[1], w.shape[2]
    assert sum(widths) == k
    in_specs = [pl.BlockSpec((tm, kp), lambda j, i: (i, 0)) for kp in widths]
    in_specs += [pl.BlockSpec((ms, kp), lambda j, i: (0, 0)) for kp in widths]
    in_specs += [pl.BlockSpec((None, k, tn), lambda j, i: (layer, 0, j)),
                 pl.BlockSpec((tm, tn), lambda j, i: (i, j)), pl.BlockSpec((ms, tn), lambda j, i: (0, j))]
    return pl.pallas_call(
        functools.partial(_out_proj_kernel, widths=widths),
        out_shape=(jax.ShapeDtypeStruct((m, n), F32), jax.ShapeDtypeStruct((ms, n), F32)), grid=(n // tn, m // tm),
        in_specs=in_specs,
        out_specs=(pl.BlockSpec((tm, tn), lambda j, i: (i, j)), pl.BlockSpec((ms, tn), lambda j, i: (0, j))),
        scratch_shapes=[pltpu.VMEM((k, tn), BF16)],
        compiler_params=_cparams("arbitrary", "arbitrary"), name="out_proj")(*parts, *parts_s, w, res, res_s)


def _in_proj_kernel(a_ref, as_ref, w_ref, o_ref, os_ref, b_s):
    @pl.when(pl.program_id(1) == 0)
    def _():
        b_s[...] = w_ref[...].T.astype(BF16)
        os_ref[...] = _dot(as_ref[...], b_s[...])

    o_ref[...] = _dot(a_ref[...], b_s[...])


def _in_proj(h, hs, wt, layer, n, *, tm, tn, name):
    m, k = h.shape
    ms = hs.shape[0]
    return pl.pallas_call(
        _in_proj_kernel,
        out_shape=(jax.ShapeDtypeStruct((m, n), F32), jax.ShapeDtypeStruct((ms, n), F32)), grid=(n // tn, m // tm),
        in_specs=[pl.BlockSpec((tm, k), lambda j, i: (i, 0)), pl.BlockSpec((ms, k), lambda j, i: (0, 0)),
                  pl.BlockSpec((None, tn, k), lambda j, i: (layer, j, 0))],
        out_specs=(pl.BlockSpec((tm, tn), lambda j, i: (i, j)), pl.BlockSpec((ms, tn), lambda j, i: (0, j))),
        scratch_shapes=[pltpu.VMEM((k, tn), BF16)],
        compiler_params=_cparams("arbitrary", "arbitrary"), name=name)(h, hs, wt)


def _aux_kernel(s_ref, bias_ref, alog_ref, p1_ref, p2_ref, *, t_len, chunk):
    x = s_ref[...]
    lanes = lax.broadcasted_iota(jnp.int32, x.shape, 1)
    sp = _softplus(x + bias_ref[...])
    p1_ref[...] = jnp.where(lanes < DNG_LANE, _sigmoid(x), sp)
    g = -jnp.exp(alog_ref[...]) * sp
    if chunk == 1:
        p2_ref[...] = g
    else:
        ii = lax.broadcasted_iota(jnp.int32, (chunk, chunk), 0)
        jj = lax.broadcasted_iota(jnp.int32, (chunk, chunk), 1)
        tril = jnp.where(ii >= jj, 1.0, 0.0).astype(F32)
        for c in range(t_len // chunk):
            p2_ref[c * chunk:(c + 1) * chunk, :] = _dot(tril, g[c * chunk:(c + 1) * chunk, :], HIGHEST)


def _aux(z, bias_vec, alog_vec, n_seq, t_len, chunk):
    m = n_seq * t_len
    blk = pl.BlockSpec((t_len, LANE), lambda b: (b, 0))
    vec = pl.BlockSpec((1, LANE), lambda b: (0, 0))
    out = pl.BlockSpec((t_len, LANE), lambda b: (b, 0))
    return pl.pallas_call(
        functools.partial(_aux_kernel, t_len=t_len, chunk=chunk),
        out_shape=(jax.ShapeDtypeStruct((m, LANE), F32), jax.ShapeDtypeStruct((m, LANE), F32)),
        grid=(n_seq,), in_specs=[blk, vec, vec], out_specs=(out, out),
        compiler_params=_cparams("arbitrary"), name="aux")(z, bias_vec, alog_vec)


_ONES_ROWS = 16


def _attn_prompt_kernel(q_ref, k_ref, v_ref, wq_ref, wk_ref, o_ref, kn_ref, kb_ref, vt_ref, kmean_ref,
                        *, t_len):
    nb = t_len // MOBA_BLOCK
    g = pl.program_id(2)

    @pl.when(g == 0)
    def _():
        k = k_ref[...]
        kn = k * lax.rsqrt(jnp.mean(k * k, axis=-1, keepdims=True) + EPS) * wk_ref[...]
        kn_ref[...] = kn
        kb_ref[...] = kn.astype(BF16)
        for n in range(nb):
            kmean_ref[n:n + 1, :] = jnp.mean(kn[n * MOBA_BLOCK:(n + 1) * MOBA_BLOCK, :], axis=0, keepdims=True)
        vt_ref[0:HEAD_DIM, :] = v_ref[...].T.astype(BF16)
        vt_ref[HEAD_DIM:HEAD_DIM + _ONES_ROWS, :] = jnp.ones((_ONES_ROWS, t_len), BF16)

    q = q_ref[...]
    qn = q * lax.rsqrt(jnp.mean(q * q, axis=-1, keepdims=True) + EPS) * wq_ref[...]
    gate = _dot_nt(kmean_ref[...], qn, HIGHEST)
    blk = lax.broadcasted_iota(jnp.int32, (nb, t_len), 0)
    own = lax.broadcasted_iota(jnp.int32, (nb, t_len), 1) // MOBA_BLOCK
    valid = blk < own
    gm = jnp.where(valid, gate, -jnp.inf)
    cnt = jnp.zeros((nb, t_len), F32)
    for m in range(nb):
        row = gm[m:m + 1, :]
        beats = jnp.where(row > gm, 1.0, jnp.where(row == gm, jnp.where(blk > m, 1.0, 0.0), 0.0))
        cnt = cnt + beats
    bias = jnp.where(valid, jnp.where(cnt < MOBA_TOPK, 0.0, NEG_BIG), NEG_BIG)
    qs = (qn * (HEAD_DIM ** -0.5 * math.log2(math.e))).astype(BF16)
    kk = lax.broadcasted_iota(jnp.int32, (MOBA_BLOCK, MOBA_BLOCK), 0)
    qq = lax.broadcasted_iota(jnp.int32, (MOBA_BLOCK, MOBA_BLOCK), 1)
    causal = jnp.where(kk <= qq, 0.0, NEG_BIG)
    for qi in range(nb):
        n_keys = (qi + 1) * MOBA_BLOCK
        cols = slice(qi * MOBA_BLOCK, (qi + 1) * MOBA_BLOCK)
        st = _dot_nt(kb_ref[0:n_keys, :], qs[cols, :])
        blocks = [st[n * MOBA_BLOCK:(n + 1) * MOBA_BLOCK, :] + bias[n:n + 1, cols] for n in range(qi)]
        blocks.append(st[qi * MOBA_BLOCK:n_keys, :] + causal)
        mx = jnp.max(blocks[0], axis=0, keepdims=True)
        for blk_s in blocks[1:]:
            mx = jnp.maximum(mx, jnp.max(blk_s, axis=0, keepdims=True))
        p = [jnp.exp2(blk_s - mx).astype(BF16) for blk_s in blocks]
        p = jnp.concatenate(p, axis=0) if qi else p[0]
        ot = _dot(vt_ref[:, 0:n_keys], p)
        o = ot[0:HEAD_DIM, :] / ot[HEAD_DIM:HEAD_DIM + 1, :]
        o_ref[cols, :] = o.T.astype(o_ref.dtype)


def _attn_prompt(z, wq, wk, n_seq, t_len):
    m = n_seq * t_len
    qblk = pl.BlockSpec((t_len, HEAD_DIM), lambda b, k, g: (b, Q_OFF // HEAD_DIM + k * ATT_GROUP + g))
    kblk = pl.BlockSpec((t_len, HEAD_DIM), lambda b, k, g: (b, K_OFF // HEAD_DIM + k))
    vblk = pl.BlockSpec((t_len, HEAD_DIM), lambda b, k, g: (b, V_OFF // HEAD_DIM + k))
    wspec = pl.BlockSpec((1, HEAD_DIM), lambda b, k, g: (0, 0))
    return pl.pallas_call(
        functools.partial(_attn_prompt_kernel, t_len=t_len),
        out_shape=(jax.ShapeDtypeStruct((m, ATT_HEADS * HEAD_DIM), BF16),
                   jax.ShapeDtypeStruct((m, ATT_KV_HEADS * HEAD_DIM), F32)),
        grid=(n_seq, ATT_KV_HEADS, ATT_GROUP),
        in_specs=[qblk, kblk, vblk, wspec, wspec],
        out_specs=(pl.BlockSpec((t_len, HEAD_DIM), lambda b, k, g: (b, k * ATT_GROUP + g)),
                   pl.BlockSpec((t_len, HEAD_DIM), lambda b, k, g: (b, k))),
        scratch_shapes=[pltpu.VMEM((t_len, HEAD_DIM), BF16), pltpu.VMEM((HEAD_DIM + _ONES_ROWS, t_len), BF16),
                        pltpu.VMEM((t_len // MOBA_BLOCK, HEAD_DIM), F32)],
        compiler_params=_cparams("arbitrary", "arbitrary", "arbitrary"), name="attn_prompt")(
            z, z, z, wq.reshape(1, HEAD_DIM), wk.reshape(1, HEAD_DIM))


_GDN_UNROLL = 8
_GDN_HEADS_PER_STEP = 2


def _gdn_prompt_kernel(zq_ref, zk_ref, zv_ref, wq_ref, wk_ref, wv_ref, p1_ref, p2_ref, zg_ref, nw_ref,
                       o_ref, s_ref, q_s, k_s, v_s, beta_s, gc_s, qp_s, op_s, mn_s, nn_s, *, t_len):
    hp = pl.program_id(1)
    c = CHUNK
    d = HEAD_DIM

    def l2n(x):
        return x * lax.rsqrt(jnp.sum(x * x, axis=-1, keepdims=True) + EPS)

    ii = lax.broadcasted_iota(jnp.int32, (c, c), 0)
    jj = lax.broadcasted_iota(jnp.int32, (c, c), 1)
    eye = jnp.where(ii == jj, 1.0, 0.0).astype(F32)

    for hh in range(_GDN_HEADS_PER_STEP):
        lanes = slice(hh * d, (hh + 1) * d)
        h = hp * _GDN_HEADS_PER_STEP + hh
        q_s[hh] = l2n(_silu(_causal_conv_rows(zq_ref[:, lanes], wq_ref.at[:, lanes], CONV_W))) * (HEAD_DIM ** -0.5)
        k_s[hh] = l2n(_silu(_causal_conv_rows(zk_ref[:, lanes], wk_ref.at[:, lanes], CONV_W)))
        v_s[hh] = _silu(_causal_conv_rows(zv_ref[:, lanes], wv_ref.at[:, lanes], CONV_W))
        beta_s[hh] = jnp.broadcast_to(_lane_col(p1_ref[...], BETA_LANE + h), (t_len, HEAD_DIM))
        gc_s[hh] = jnp.broadcast_to(_lane_col(p2_ref[...], DNG_LANE + h), (t_len, HEAD_DIM))
        lax.fori_loop(0, t_len // (c * _GDN_UNROLL),
                      functools.partial(_gdn_prepare_group, refs=(q_s.at[hh], k_s.at[hh], v_s.at[hh], beta_s.at[hh],
                                                                  gc_s.at[hh], qp_s.at[hh], op_s.at[hh], mn_s.at[hh],
                                                                  nn_s.at[hh]), consts=(ii, jj, eye)), 0)

    def chunk_step(ci, states):
        r = pl.ds(pl.multiple_of(ci * c, c), c)
        rd = pl.ds(pl.multiple_of(ci * d, d), d)
        ms = [_dot(jnp.concatenate([mn_s[hh, rd, :], qp_s[hh, r, :]], axis=0), states[hh])
              for hh in range(_GDN_HEADS_PER_STEP)]
        new = []
        for hh in range(_GDN_HEADS_PER_STEP):
            op_s[hh, r, :] = ms[hh][d:d + c] + op_s[hh, r, :]
            e = jnp.exp(gc_s[hh, pl.ds(ci * c + c - 1, 1), :])
            new.append(states[hh] * e[:, 0:1] + ms[hh][0:d] + nn_s[hh, rd, :])
        return tuple(new)

    zero = jnp.zeros((HEAD_DIM, HEAD_DIM), F32)
    s_fin = lax.fori_loop(0, t_len // c, chunk_step, (zero,) * _GDN_HEADS_PER_STEP)
    for hh in range(_GDN_HEADS_PER_STEP):
        lanes = slice(hh * d, (hh + 1) * d)
        s_ref[hh] = s_fin[hh]
        o = op_s[hh]
        on = o * lax.rsqrt(jnp.mean(o * o, axis=-1, keepdims=True) + EPS) * nw_ref[...]
        o_ref[:, lanes] = (on * _silu(zg_ref[:, lanes])).astype(o_ref.dtype)


def _gdn_prepare_group(gi, carry, *, refs, consts):
    q_s, k_s, v_s, beta_s, gc_s, qp_s, op_s, mn_s, nn_s = refs
    ii, jj, eye = consts
    c = CHUNK
    d = HEAD_DIM
    n = _GDN_UNROLL
    chunks = [gi * n + u for u in range(n)]
    rows = [pl.ds(pl.multiple_of(ci * c, c), c) for ci in chunks]
    q = [q_s[r, :] for r in rows]
    k = [k_s[r, :] for r in rows]
    gcb = [gc_s[r, :] for r in rows]
    kb = [k[u] * beta_s[rows[u], :] for u in range(n)]
    gam = [_decay_matrix(gcb[u][:, 0:1], c)[0] for u in range(n)]
    kq = [_dot_nt(jnp.concatenate([kb[u], q[u]], axis=0), k[u]) for u in range(n)]
    attn = [kq[u][c:2 * c] * gam[u] for u in range(n)]
    p = [-jnp.where(ii > jj, kq[u][0:c] * gam[u], 0.0) for u in range(n)]
    inv = [eye + p[u] for u in range(n)]
    for _ in range(int(math.log2(c)) - 1):
        p = [_dot(p[u], p[u]) for u in range(n)]
        inv = [inv[u] + _dot(inv[u], p[u]) for u in range(n)]
    eg = [jnp.exp(gcb[u]) for u in range(n)]
    rhs = [jnp.concatenate([v_s[rows[u], :] * beta_s[rows[u], :], kb[u] * eg[u]], axis=1) for u in range(n)]
    uw = [_dot(inv[u], rhs[u]) for u in range(n)]
    auw = [_dot(attn[u], uw[u]) for u in range(n)]
    k_dec = [k[u] * jnp.exp(gcb[u][c - 1:c, :] - gcb[u]) for u in range(n)]
    kuw = [_dot(k_dec[u].T, uw[u]) for u in range(n)]
    for u in range(n):
        rd = pl.ds(pl.multiple_of(chunks[u] * d, d), d)
        op_s[rows[u], :] = auw[u][:, 0:d]
        qp_s[rows[u], :] = eg[u] * q[u] - auw[u][:, d:2 * d]
        nn_s[rd, :] = kuw[u][:, 0:d]
        mn_s[rd, :] = -kuw[u][:, d:2 * d]
    return carry


def _gdn_prompt(z, p1, p2, conv_w, norm_w, n_seq, t_len):
    m = n_seq * t_len
    hps = _GDN_HEADS_PER_STEP
    width = hps * HEAD_DIM
    base = DNQKV_OFF // width

    def zcol(off):
        return pl.BlockSpec((t_len, width), functools.partial(lambda b, h, off: (b, off + h), off=off))

    def wcol(off):
        return pl.BlockSpec((CONV_W, width), functools.partial(lambda b, h, off: (0, off + h), off=off))

    aux = pl.BlockSpec((t_len, LANE), lambda b, h: (b, 0))
    tbuf = pltpu.VMEM((hps, t_len, HEAD_DIM), F32)
    groups = DN_HEADS // hps
    return pl.pallas_call(
        functools.partial(_gdn_prompt_kernel, t_len=t_len),
        out_shape=(jax.ShapeDtypeStruct((m, DN_WIDTH), BF16),
                   jax.ShapeDtypeStruct((n_seq, DN_HEADS, HEAD_DIM, HEAD_DIM), F32)),
        grid=(n_seq, groups),
        in_specs=[zcol(base), zcol(base + groups), zcol(base + 2 * groups),
                  wcol(0), wcol(groups), wcol(2 * groups), aux, aux,
                  zcol(DNZ_OFF // width), pl.BlockSpec((1, HEAD_DIM), lambda b, h: (0, 0))],
        out_specs=(pl.BlockSpec((t_len, width), lambda b, h: (b, h)),
                   pl.BlockSpec((None, hps, HEAD_DIM, HEAD_DIM), lambda b, h: (b, h, 0, 0))),
        scratch_shapes=[tbuf] * 7 + [pltpu.VMEM((hps, t_len // CHUNK * HEAD_DIM, HEAD_DIM), F32)] * 2,
        compiler_params=_cparams("arbitrary", "arbitrary"), name="gdn_prompt")(
            z, z, z, conv_w, conv_w, conv_w, p1, p2, z, norm_w.reshape(1, HEAD_DIM))


_GH = SSD_HEADS // SSD_GROUPS
_GW = _GH * SSD_HEAD_DIM
_SSD_UNROLL = 4


def _ssd_prompt_kernel(zx_ref, zb_ref, zc_ref, wx_ref, wb_ref, wc_ref, bx_ref, bb_ref, bc_ref, p1_ref, p2_ref,
                       zg_ref, d_ref, nw_ref, o_ref, st_ref, x_s, b_s, c_s, y_s, state_s, *, t_len):
    grp = pl.program_id(1)
    c = CHUNK
    x_s[...] = _silu(_causal_conv_rows(zx_ref[...], wx_ref, CONV_W) + bx_ref[...])
    b_s[...] = _silu(_causal_conv_rows(zb_ref[...], wb_ref, CONV_W) + bb_ref[...])
    c_s[...] = _silu(_causal_conv_rows(zc_ref[...], wc_ref, CONV_W) + bc_ref[...])
    state_s[...] = jnp.zeros_like(state_s)

    def chunk_group(gi, carry):
        n = _SSD_UNROLL
        rows = [pl.ds(pl.multiple_of((gi * n + u) * c, c), c) for u in range(n)]
        cm = [c_s[r, :] for r in rows]
        bm = [b_s[r, :] for r in rows]
        cb = [_dot_nt(cm[u], bm[u]) for u in range(n)]
        eac, dec, xs, xv, gam = [], [], [], [], []
        for u in range(n):
            x, p1, p2 = x_s[rows[u], :], p1_ref[rows[u], :], p2_ref[rows[u], :]
            eac_h, dec_h, xs_h, xv_h, gam_h = [], [], [], [], []
            for hh in range(_GH):
                lane = SSD_LANE + grp * _GH + hh
                ac = _lane_col(p2, lane)
                a_last = ac[c - 1:c, :]
                xv_hh = x[:, hh * SSD_HEAD_DIM:(hh + 1) * SSD_HEAD_DIM] * _lane_col(p1, lane)
                gam_h.append(_decay_matrix(ac, c)[0])
                xv_h.append(xv_hh)
                xs_h.append(xv_hh * jnp.exp(a_last - ac))
                eac_h.append(jnp.broadcast_to(jnp.exp(ac), (c, SSD_HEAD_DIM)))
                dec_h.append(jnp.broadcast_to(jnp.exp(a_last), (1, SSD_HEAD_DIM)))
            gam.append(gam_h)
            xv.append(xv_h)
            xs.append(jnp.concatenate(xs_h, axis=1))
            eac.append(jnp.concatenate(eac_h, axis=1))
            dec.append(jnp.concatenate(dec_h, axis=1))
        y_intra = [jnp.concatenate([_dot(cb[u] * gam[u][hh], xv[u][hh]) for hh in range(_GH)], axis=1)
                   for u in range(n)]
        new_states = [_dot(bm[u].T, xs[u]) for u in range(n)]
        s_in = [state_s[...]]
        for u in range(n):
            s_in.append(s_in[u] * dec[u] + new_states[u])
        state_s[...] = s_in[n]
        for u in range(n):
            y_s[rows[u], :] = y_intra[u] + eac[u] * _dot(cm[u], s_in[u])
        return carry

    lax.fori_loop(0, t_len // (c * _SSD_UNROLL), chunk_group, 0)
    for hh in range(_GH):
        st_ref[hh] = state_s[:, hh * SSD_HEAD_DIM:(hh + 1) * SSD_HEAD_DIM]
    y = (y_s[...] + d_ref[...] * x_s[...]) * _silu(zg_ref[...])
    o_ref[...] = (y * lax.rsqrt(jnp.mean(y * y, axis=-1, keepdims=True) + EPS) * nw_ref[...]).astype(o_ref.dtype)


def _ssd_prompt(z, p1, p2, conv_w, conv_b, d_vec, norm_w, n_seq, t_len):
    m = n_seq * t_len
    xb, bb, cbk = SSDX_OFF // _GW, (SSDX_OFF + SSD_INNER) // LANE, (SSDX_OFF + SSD_INNER + 2 * SSD_STATE) // LANE
    wb0, wc0 = SSD_INNER // LANE, (SSD_INNER + 2 * SSD_STATE) // LANE

    def spec(rows, width, off):
        return pl.BlockSpec((rows, width), functools.partial(lambda b, g, off: (0, off + g), off=off))

    def zspec(width, off):
        return pl.BlockSpec((t_len, width), functools.partial(lambda b, g, off: (b, off + g), off=off))

    aux = pl.BlockSpec((t_len, LANE), lambda b, g: (b, 0))
    return pl.pallas_call(
        functools.partial(_ssd_prompt_kernel, t_len=t_len),
        out_shape=(jax.ShapeDtypeStruct((m, SSD_INNER), BF16),
                   jax.ShapeDtypeStruct((n_seq, SSD_HEADS, SSD_STATE, SSD_HEAD_DIM), F32)),
        grid=(n_seq, SSD_GROUPS),
        in_specs=[zspec(_GW, xb), zspec(LANE, bb), zspec(LANE, cbk),
                  spec(CONV_W, _GW, 0), spec(CONV_W, LANE, wb0), spec(CONV_W, LANE, wc0),
                  spec(1, _GW, 0), spec(1, LANE, wb0), spec(1, LANE, wc0),
                  aux, aux, zspec(_GW, SSDZ_OFF // _GW), spec(1, _GW, 0), spec(1, _GW, 0)],
        out_specs=(pl.BlockSpec((t_len, _GW), lambda b, g: (b, g)),
                   pl.BlockSpec((None, _GH, SSD_STATE, SSD_HEAD_DIM), lambda b, g: (b, g, 0, 0))),
        scratch_shapes=[pltpu.VMEM((t_len, _GW), F32), pltpu.VMEM((t_len, LANE), F32), pltpu.VMEM((t_len, LANE), F32),
                        pltpu.VMEM((t_len, _GW), F32), pltpu.VMEM((SSD_STATE, _GW), F32)],
        compiler_params=_cparams("arbitrary", "arbitrary"), name="ssd_prompt")(
            z, z, z, conv_w, conv_w, conv_w, conv_b, conv_b, conv_b, p1, p2, z, d_vec, norm_w.reshape(1, SSD_INNER))


_FFN_TN = 256
_FFN_NJ = FFN_DIM // _FFN_TN
_TAIL = 8


def _ffn_up_kernel(a_ref, as_ref, bg_ref, bu_ref, wg_ref, wu_ref, cg_ref, cu_ref, s0g_ref, s1g_ref, s0u_ref, s1u_ref,
                   g_ref, sg_ref, su_ref, gs_ref, ygs_ref, yus_ref, b_s, y_s, *, tiles_per_seq, n_sub):
    i = pl.program_id(1)
    tn = _FFN_TN
    tm = a_ref.shape[0]

    @pl.when(i == 0)
    def _():
        b_s[:, 0:tn] = bg_ref[...].astype(BF16)
        b_s[:, tn:2 * tn] = bu_ref[...].astype(BF16)
        ys = _dot(as_ref[...], b_s[...])
        yg, yu = ys[:, 0:tn], ys[:, tn:2 * tn]
        ug = wg_ref[0:1, :] * s0g_ref[...] + wg_ref[1:2, :] * s1g_ref[...] + wg_ref[2:3, :] * yg + cg_ref[...]
        uu = wu_ref[0:1, :] * s0u_ref[...] + wu_ref[1:2, :] * s1u_ref[...] + wu_ref[2:3, :] * yu + cu_ref[...]
        gs_ref[...] = (_silu(ug) * uu).astype(gs_ref.dtype)
        ygs_ref[...] = yg
        yus_ref[...] = yu

    @pl.when(i % tiles_per_seq == 0)
    def _():
        y_s[0:_TAIL, :] = jnp.zeros((_TAIL, 2 * tn), F32)

    @pl.when(i % tiles_per_seq != 0)
    def _():
        y_s[0:_TAIL, :] = y_s[tm:tm + _TAIL, :]

    w = jnp.concatenate([wg_ref[...], wu_ref[...]], axis=1)
    c = jnp.concatenate([cg_ref[...], cu_ref[...]], axis=1)
    ts = tm // n_sub
    for s in range(n_sub):
        r0 = _TAIL + s * ts
        y = _dot(a_ref[s * ts:(s + 1) * ts, :], b_s[...])
        y_s[r0:r0 + ts, :] = y
        u = w[0:1, :] * y_s[r0 - 2:r0 - 2 + ts, :] + w[1:2, :] * y_s[r0 - 1:r0 - 1 + ts, :] + w[2:3, :] * y + c
        g_ref[s * ts:(s + 1) * ts, :] = (_silu(u[:, 0:tn]) * u[:, tn:2 * tn]).astype(g_ref.dtype)
    sg_ref[...] = y_s[tm:tm + _TAIL, 0:tn]
    su_ref[...] = y_s[tm:tm + _TAIL, tn:2 * tn]


def _ffn_up(h2, h2s, w_up, layer, conv_w, conv_b, s0, s1, n_seq, t_len, tm):
    m = n_seq * t_len
    k = h2.shape[1]
    ms = h2s.shape[0]
    nj = _FFN_NJ
    tiles_per_seq = t_len // tm
    lo = lambda j, i: (0, j)
    hi = lambda j, i: (0, j + nj)
    srow = lambda f: pl.BlockSpec((ms, _FFN_TN), f)
    tail = jax.ShapeDtypeStruct((m // tm, _TAIL, FFN_DIM), F32)
    tail_spec = pl.BlockSpec((None, _TAIL, _FFN_TN), lambda j, i: (i, 0, j))
    ysd = jax.ShapeDtypeStruct((ms, FFN_DIM), F32)
    return pl.pallas_call(
        functools.partial(_ffn_up_kernel, tiles_per_seq=tiles_per_seq, n_sub=2),
        out_shape=(jax.ShapeDtypeStruct((m, FFN_DIM), BF16), tail, tail,
                   jax.ShapeDtypeStruct((ms, FFN_DIM), BF16), ysd, ysd),
        grid=(nj, m // tm),
        in_specs=[pl.BlockSpec((tm, k), lambda j, i: (i, 0)), pl.BlockSpec((ms, k), lambda j, i: (0, 0)),
                  pl.BlockSpec((None, k, _FFN_TN), lambda j, i: (layer, 0, j)),
                  pl.BlockSpec((None, k, _FFN_TN), lambda j, i: (layer, 0, j + nj)),
                  pl.BlockSpec((FFN_CONV_W, _FFN_TN), lo), pl.BlockSpec((FFN_CONV_W, _FFN_TN), hi),
                  pl.BlockSpec((1, _FFN_TN), lo), pl.BlockSpec((1, _FFN_TN), hi),
                  srow(lo), srow(lo), srow(hi), srow(hi)],
        out_specs=(pl.BlockSpec((tm, _FFN_TN), lambda j, i: (i, j)), tail_spec, tail_spec,
                   srow(lo), srow(lo), srow(lo)),
        scratch_shapes=[pltpu.VMEM((k, 2 * _FFN_TN), BF16), pltpu.VMEM((_TAIL + tm, 2 * _FFN_TN), F32)],
        compiler_params=_cparams("arbitrary", "arbitrary"), name="ffn_up")(
            h2, h2s, w_up, w_up, conv_w, conv_w, conv_b, conv_b, s0, s1, s0, s1)


_PAGES_PER_STEP = 16


def _kmean_kernel(pt_ref, *refs):
    o_ref = refs[-1]
    page = refs[0].shape[0]
    per_block = MOBA_BLOCK // page
    for n in range(_PAGES_PER_STEP // per_block):
        acc = jnp.sum(refs[n * per_block][...], axis=0)
        for r in range(1, per_block):
            acc = acc + jnp.sum(refs[n * per_block + r][...], axis=0)
        o_ref[n] = acc * (1.0 / MOBA_BLOCK)


def _kmean(cache_k, page_table, layer):
    n_b, n_pages = page_table.shape
    page, n_kv, d = cache_k.shape[2:]
    per_block = MOBA_BLOCK // page
    n_blocks = n_pages // per_block
    steps = n_pages // _PAGES_PER_STEP
    in_specs = [pl.BlockSpec((None, None, page, n_kv, d),
                             functools.partial(lambda b, s, pt, r: (layer, pt[b, s * _PAGES_PER_STEP + r], 0, 0, 0), r=r))
                for r in range(_PAGES_PER_STEP)]
    grid_spec = pltpu.PrefetchScalarGridSpec(
        num_scalar_prefetch=1, grid=(n_b, steps), in_specs=in_specs,
        out_specs=pl.BlockSpec((None, _PAGES_PER_STEP // per_block, n_kv, d), lambda b, s, pt: (b, s, 0, 0)))
    return pl.pallas_call(
        _kmean_kernel, out_shape=jax.ShapeDtypeStruct((n_b, n_blocks, n_kv, d), F32), grid_spec=grid_spec,
        compiler_params=_cparams("arbitrary", "arbitrary"), name="kmean")(page_table, *([cache_k] * _PAGES_PER_STEP))


def _select_kernel(q_ref, k_ref, kmean_ref, wq_ref, wk_ref, qn_ref, kn_ref, idx_ref):
    n_b, n_h, _ = q_ref.shape
    n_blocks = kmean_ref.shape[1]
    q = q_ref[...]
    qn = q * lax.rsqrt(jnp.mean(q * q, axis=-1, keepdims=True) + EPS) * wq_ref[...]
    qn_ref[...] = qn
    k = k_ref[...]
    kn_ref[...] = k * lax.rsqrt(jnp.mean(k * k, axis=-1, keepdims=True) + EPS) * wk_ref[...]
    head = lax.broadcasted_iota(jnp.int32, (n_h, n_blocks), 0)
    lane = lax.broadcasted_iota(jnp.int32, (n_h, n_blocks), 1).astype(F32)
    lane_out = lax.broadcasted_iota(jnp.int32, (n_h, LANE), 1)
    for b in range(n_b):
        gate = jnp.zeros((n_h, n_blocks), F32)
        for kv in range(ATT_KV_HEADS):
            gk = _dot_nt(qn[b], kmean_ref[b, :, kv * HEAD_DIM:(kv + 1) * HEAD_DIM], HIGHEST)
            gate = jnp.where(head // ATT_GROUP == kv, gk, gate)
        out = jnp.zeros((n_h, LANE), F32)
        for s in range(MOBA_TOPK):
            mx = jnp.max(gate, axis=-1, keepdims=True)
            pick = jnp.min(jnp.where(gate == mx, lane, float(n_blocks)), axis=-1, keepdims=True)
            out = jnp.where(lane_out == s, pick, out)
            gate = jnp.where(lane == pick, -jnp.inf, gate)
        idx_ref[b] = out.astype(jnp.int32)


def _select(q3, k3, kmean, wq, wk):
    n_b, n_h, d = q3.shape
    full = lambda shape: pl.BlockSpec(shape, lambda i: (0,) * len(shape))
    return pl.pallas_call(
        _select_kernel,
        out_shape=(jax.ShapeDtypeStruct(q3.shape, F32), jax.ShapeDtypeStruct(k3.shape, F32),
                   jax.ShapeDtypeStruct((n_b, n_h, LANE), jnp.int32)),
        grid=(1,),
        in_specs=[full(q3.shape), full(k3.shape), full(kmean.shape), full((1, 1, d)), full((1, 1, d))],
        out_specs=(full(q3.shape), full(k3.shape), full((n_b, n_h, LANE))),
        compiler_params=_cparams("arbitrary"), name="select")(q3, k3, kmean, wq.reshape(1, 1, d), wk.reshape(1, 1, d))


def _attn_sample_kernel(pt_ref, idx_ref, qn_ref, kn_ref, vn_ref, *refs, n_sel, per_block):
    o_ref = refs[-1]
    pages = refs[:-1]
    n_pg = n_sel * per_block
    k_refs, v_refs = pages[:n_pg], pages[n_pg:]
    h = pl.program_id(1)
    kv = h // ATT_GROUP
    q = qn_ref[pl.ds(h, 1), :] * (HEAD_DIM ** -0.5)
    k_new = kn_ref[pl.ds(kv, 1), :]
    v_new = vn_ref[pl.ds(kv, 1), :]
    s_own = jnp.sum(q * k_new, axis=-1, keepdims=True)
    mine = pl.ds(kv, k_refs[0].shape[0] // ATT_KV_HEADS, stride=ATT_KV_HEADS)
    logits = [jnp.sum(r[mine, :] * q, axis=-1, keepdims=True) for r in k_refs]
    mx = s_own
    for s in logits:
        mx = jnp.maximum(mx, jnp.max(s, axis=0, keepdims=True))
    p_own = jnp.exp(s_own - mx)
    den = p_own
    acc = p_own * v_new
    for s, v_ref in zip(logits, v_refs):
        p = jnp.exp(s - mx)
        den = den + jnp.sum(p, axis=0, keepdims=True)
        acc = acc + jnp.sum(p * v_ref[mine, :], axis=0, keepdims=True)
    o_ref[pl.ds(h, 1), :] = acc / den


def _attn_sample(cache_k, cache_v, page_table, idx, qn, kn, vn, layer):
    n_b, n_h, d = qn.shape
    depth, n_pool, page, n_kv = cache_k.shape[:4]
    per_block = MOBA_BLOCK // page
    n_sel = idx.shape[2]
    cache_k = cache_k.reshape(depth, n_pool, page * n_kv, d)
    cache_v = cache_v.reshape(depth, n_pool, page * n_kv, d)

    def page_spec(s, r):
        def imap(b, h, pt, ix):
            return (layer, pt[b, ix[b, h * n_sel + s] * per_block + r], 0, 0)
        return pl.BlockSpec((None, None, page * n_kv, d), imap)

    pspecs = [page_spec(s, r) for s in range(n_sel) for r in range(per_block)]
    slab = lambda n: pl.BlockSpec((None, n, d), lambda b, h, pt, ix: (b, 0, 0))
    grid_spec = pltpu.PrefetchScalarGridSpec(
        num_scalar_prefetch=2, grid=(n_b, n_h),
        in_specs=[slab(n_h), slab(ATT_KV_HEADS), slab(ATT_KV_HEADS)] + pspecs + pspecs,
        out_specs=slab(n_h))
    n_pg = len(pspecs)
    return pl.pallas_call(
        functools.partial(_attn_sample_kernel, n_sel=n_sel, per_block=per_block),
        out_shape=jax.ShapeDtypeStruct((n_b, n_h, d), F32), grid_spec=grid_spec,
        compiler_params=_cparams("arbitrary", "arbitrary"), name="attn_sample")(
            page_table, idx.reshape(n_b, n_h * n_sel), qn, kn, vn, *([cache_k] * n_pg), *([cache_v] * n_pg))


def _rows_to_cols(x):
    r, n = x.shape
    return jnp.concatenate([x, jnp.zeros((n - r, n), x.dtype)], axis=0).T


def _mix_sample_kernel(dx_ref, dbuf_ref, dw_ref, dbeta_ref, dg_ref, dz_ref, dnw_ref, dst_ref,
                       sx_ref, sxbuf_ref, sxw_ref, sxb_ref, bc_ref, bcbuf_ref, bcw_ref, bcb_ref,
                       sdt_ref, sa_ref, sz_ref, sd_ref, snw_ref, sst_ref,
                       dno_ref, dnst_ref, so_ref, sso_ref, y_s):
    x = dx_ref[...]
    conv = x * dw_ref[CONV_W - 1]
    for i in range(CONV_W - 1):
        conv = conv + dbuf_ref[i] * dw_ref[i]
    act = _silu(conv)
    nh = DN_HEADS
    q, k, v = act[0:nh], act[nh:2 * nh], act[2 * nh:3 * nh]
    q = q * lax.rsqrt(jnp.sum(q * q, axis=-1, keepdims=True) + EPS) * (HEAD_DIM ** -0.5)
    k = k * lax.rsqrt(jnp.sum(k * k, axis=-1, keepdims=True) + EPS)
    beta = dbeta_ref[...]
    eg = jnp.exp(dg_ref[...])
    qk = jnp.sum(q * k, axis=-1, keepdims=True)
    qt, kt = _rows_to_cols(q), _rows_to_cols(k)
    o_rows = []
    for h in range(nh):
        s0 = dst_ref[h]
        kcol, qcol = kt[:, h:h + 1], qt[:, h:h + 1]
        ks = jnp.sum(kcol * s0, axis=0, keepdims=True)
        qs = jnp.sum(qcol * s0, axis=0, keepdims=True)
        b_h, e_h = beta[h:h + 1, :], eg[h:h + 1, :]
        v_new = v[h:h + 1, :] * b_h - (b_h * e_h) * ks
        o_rows.append(e_h * qs + qk[h:h + 1, :] * v_new)
        dnst_ref[h] = s0 * e_h + kcol * v_new
    o = jnp.concatenate(o_rows, axis=0)
    on = o * lax.rsqrt(jnp.mean(o * o, axis=-1, keepdims=True) + EPS) * dnw_ref[...]
    dno_ref[...] = on * _silu(dz_ref[...])
    xs = sx_ref[...] * sxw_ref[CONV_W - 1] + sxb_ref[...]
    bc = bc_ref[...] * bcw_ref[CONV_W - 1] + bcb_ref[...]
    for i in range(CONV_W - 1):
        xs = xs + sxbuf_ref[i] * sxw_ref[i]
        bc = bc + bcbuf_ref[i] * bcw_ref[i]
    xs = _silu(xs)
    bc = _silu(bc)
    bct = _rows_to_cols(bc)
    cb = jnp.sum(bc[0:SSD_GROUPS] * bc[SSD_GROUPS:2 * SSD_GROUPS], axis=-1, keepdims=True)
    dt = sdt_ref[...]
    ea = jnp.exp(sa_ref[...])
    for h in range(SSD_HEADS):
        grp = h // _GH
        s0 = sst_ref[h]
        bcol, ccol = bct[:, grp:grp + 1], bct[:, SSD_GROUPS + grp:SSD_GROUPS + grp + 1]
        xv = xs[h:h + 1, :] * dt[h:h + 1, :]
        e_h = ea[h:h + 1, :]
        y_s[h:h + 1, :] = cb[grp:grp + 1, :] * xv + e_h * jnp.sum(ccol * s0, axis=0, keepdims=True)
        sso_ref[h] = s0 * e_h + bcol * xv
    y = (y_s[...] + sd_ref[...] * xs) * _silu(sz_ref[...])
    norm_rows = []
    for grp in range(SSD_GROUPS):
        yg = y[grp * _GH:(grp + 1) * _GH]
        ms = jnp.sum(jnp.sum(yg * yg, axis=-1, keepdims=True), axis=0, keepdims=True) * (1.0 / _GW)
        norm_rows.append(yg * lax.rsqrt(ms + EPS))
    so_ref[...] = jnp.concatenate(norm_rows, axis=0) * snw_ref[...]


def _mix_sample(args, n_b):
    def per_b(shape):
        nd = len(shape) - 1
        return pl.BlockSpec((None,) + tuple(shape[1:]), lambda b: (b,) + (0,) * nd)

    def shared(shape):
        nd = len(shape)
        return pl.BlockSpec(tuple(shape), lambda b: (0,) * nd)

    names_per_b = {"dx", "dbuf", "dbeta", "dg", "dz", "dst", "sx", "sxbuf", "bc", "bcbuf", "sdt", "sa", "sz", "sst"}
    order = ["dx", "dbuf", "dw", "dbeta", "dg", "dz", "dnw", "dst", "sx", "sxbuf", "sxw", "sxb", "bc", "bcbuf", "bcw",
             "bcb", "sdt", "sa", "sz", "sd", "snw", "sst"]
    in_specs = [per_b(args[n].shape) if n in names_per_b else shared(args[n].shape) for n in order]
    outs = (jax.ShapeDtypeStruct((n_b, DN_HEADS, HEAD_DIM), F32),
            jax.ShapeDtypeStruct((n_b, DN_HEADS, HEAD_DIM, HEAD_DIM), F32),
            jax.ShapeDtypeStruct((n_b, SSD_HEADS, SSD_HEAD_DIM), F32),
            jax.ShapeDtypeStruct((n_b, SSD_HEADS, SSD_STATE, SSD_HEAD_DIM), F32))
    return pl.pallas_call(
        _mix_sample_kernel, out_shape=outs, grid=(n_b,), in_specs=in_specs,
        out_specs=tuple(per_b(o.shape) for o in outs),
        scratch_shapes=[pltpu.VMEM((SSD_HEADS, SSD_HEAD_DIM), F32)],
        compiler_params=_cparams("arbitrary"), name="mix_sample")(*[args[n] for n in order])


def _split_w_in(w_in):
    depth, k, _ = w_in.shape
    wt = jnp.swapaxes(w_in, 1, 2)
    wt_b = wt[:, ZB_SRC:ZB_SRC + ZB_COLS]
    wt_s = jnp.concatenate([wt[:, ZS_SRC[0]:ZS_SRC[0] + 16], wt[:, ZS_SRC[1]:ZS_SRC[1] + 16],
                            jnp.zeros((depth, LANE - 32, k), w_in.dtype)], axis=1)
    return wt, wt_b, wt_s


def _in_proj_all(h, hs, lw, layer, tm):
    return [_in_proj(h, hs, lw["wt_a"], layer, ZA_COLS, tm=tm, tn=512, name="in_proj_a"),
            _in_proj(h, hs, lw["wt_b"], layer, ZB_COLS, tm=tm, tn=512, name="in_proj_b"),
            _in_proj(h, hs, lw["wt_s"], layer, LANE, tm=tm, tn=LANE, name="in_proj_s")]


def _lane_vec(dn_vals, ssd_vals):
    v = jnp.zeros((LANE,), F32)
    v = v.at[DNG_LANE:DNG_LANE + DN_HEADS].set(dn_vals.astype(F32))
    v = v.at[SSD_LANE:SSD_LANE + SSD_HEADS].set(ssd_vals.astype(F32))
    return v.reshape(1, LANE)


_PROMPT_TM = 1024


def _prompt_mix(za, zb, zs, lw, n_seq, t_len):
    p1, p2 = _aux(zs, lw["bias_vec"], lw["alog_vec"], n_seq, t_len, CHUNK)
    o_att, k_norm = _attn_prompt(za, lw["attn_q_norm_w"], lw["attn_k_norm_w"], n_seq, t_len)
    o_dn, dn_state = _gdn_prompt(za, p1, p2, lw["dn_conv_w"], lw["dn_norm_w"], n_seq, t_len)
    o_ssd, ssd_state = _ssd_prompt(zb, p1, p2, lw["ssd_conv_w"], lw["ssd_conv_b"].reshape(1, -1), lw["ssd_d_vec"],
                                   lw["ssd_norm_w"], n_seq, t_len)
    za3 = za.reshape(n_seq, t_len, ZA_COLS)
    zb3 = zb.reshape(n_seq, t_len, ZB_COLS)
    new = [k_norm.reshape(n_seq, t_len, ATT_KV_HEADS, HEAD_DIM),
           za3[:, :, V_OFF:V_OFF + 512].reshape(n_seq, t_len, ATT_KV_HEADS, HEAD_DIM),
           za3[:, t_len - (CONV_W - 1):, DNQKV_OFF:DNQKV_OFF + DN_CONV_CH],
           dn_state,
           zb3[:, t_len - (CONV_W - 1):, SSDX_OFF:SSDX_OFF + SSD_CONV_CH],
           ssd_state]
    return [o_att, o_dn, o_ssd], new


_SAMPLE_ROWS = 16


def _sample_mix(za, zb2, zs, lw, layer, cache_k, cache_v, page_table, dn_conv_buf, dn_state, ssd_conv_buf,
                ssd_state):
    n_b = page_table.shape[0]
    rows = za.shape[0]
    p1, p2 = _aux(zs, lw["bias_vec"], lw["alog_vec"], 1, rows, 1)
    zb = za[:n_b]
    zbs = zb2[:n_b]
    kmean = _kmean(cache_k, page_table, layer)
    kmean = kmean.reshape(n_b, kmean.shape[1], ATT_KV_HEADS * HEAD_DIM)
    q3 = zb[:, Q_OFF:Q_OFF + 2048].reshape(n_b, ATT_HEADS, HEAD_DIM)
    k3 = zb[:, K_OFF:K_OFF + 512].reshape(n_b, ATT_KV_HEADS, HEAD_DIM)
    v3 = zb[:, V_OFF:V_OFF + 512].reshape(n_b, ATT_KV_HEADS, HEAD_DIM)
    qn, kn, idx = _select(q3, k3, kmean, lw["attn_q_norm_w"], lw["attn_k_norm_w"])
    o_att = _attn_sample(cache_k, cache_v, page_table, idx[:, :, :MOBA_TOPK], qn, kn, v3, layer)
    dn_x = zb[:, DNQKV_OFF:DNQKV_OFF + DN_CONV_CH]
    ssd_x = zbs[:, SSDX_OFF:SSDX_OFF + SSD_CONV_CH]
    nh3 = 3 * DN_HEADS
    ng2 = 2 * SSD_GROUPS
    args = {
        "dx": dn_x.reshape(n_b, nh3, HEAD_DIM),
        "dbuf": dn_conv_buf.reshape(n_b, CONV_W - 1, nh3, HEAD_DIM),
        "dw": lw["dn_conv_w"].reshape(CONV_W, nh3, HEAD_DIM),
        "dbeta": p1[:n_b, BETA_LANE:BETA_LANE + DN_HEADS].reshape(n_b, DN_HEADS, 1),
        "dg": p2[:n_b, DNG_LANE:DNG_LANE + DN_HEADS].reshape(n_b, DN_HEADS, 1),
        "dz": zb[:, DNZ_OFF:DNZ_OFF + DN_WIDTH].reshape(n_b, DN_HEADS, HEAD_DIM),
        "dnw": lw["dn_norm_w"].reshape(1, HEAD_DIM),
        "dst": dn_state,
        "sx": ssd_x[:, :SSD_INNER].reshape(n_b, SSD_HEADS, SSD_HEAD_DIM),
        "sxbuf": ssd_conv_buf[:, :, :SSD_INNER].reshape(n_b, CONV_W - 1, SSD_HEADS, SSD_HEAD_DIM),
        "sxw": lw["ssd_conv_w"][:, :SSD_INNER].reshape(CONV_W, SSD_HEADS, SSD_HEAD_DIM),
        "sxb": lw["ssd_conv_b"][:SSD_INNER].reshape(SSD_HEADS, SSD_HEAD_DIM),
        "bc": ssd_x[:, SSD_INNER:].reshape(n_b, ng2, SSD_STATE),
        "bcbuf": ssd_conv_buf[:, :, SSD_INNER:].reshape(n_b, CONV_W - 1, ng2, SSD_STATE),
        "bcw": lw["ssd_conv_w"][:, SSD_INNER:].reshape(CONV_W, ng2, SSD_STATE),
        "bcb": lw["ssd_conv_b"][SSD_INNER:].reshape(ng2, SSD_STATE),
        "sdt": p1[:n_b, SSD_LANE:SSD_LANE + SSD_HEADS].reshape(n_b, SSD_HEADS, 1),
        "sa": p2[:n_b, SSD_LANE:SSD_LANE + SSD_HEADS].reshape(n_b, SSD_HEADS, 1),
        "sz": zbs[:, SSDZ_OFF:SSDZ_OFF + SSD_INNER].reshape(n_b, SSD_HEADS, SSD_HEAD_DIM),
        "sd": lw["ssd_D"].astype(F32).reshape(SSD_HEADS, 1),
        "snw": lw["ssd_norm_w"].reshape(SSD_HEADS, SSD_HEAD_DIM),
        "sst": ssd_state,
    }
    o_dn, dn_state_new, o_ssd, ssd_state_new = _mix_sample(args, n_b)
    pad = lambda a: jnp.pad(a.reshape(n_b, -1), ((0, rows - n_b), (0, 0))).astype(BF16)
    new = [kn.reshape(n_b, 1, ATT_KV_HEADS, HEAD_DIM), v3.reshape(n_b, 1, ATT_KV_HEADS, HEAD_DIM),
           jnp.concatenate([dn_conv_buf[:, 1:], dn_x[:, None]], axis=1), dn_state_new,
           jnp.concatenate([ssd_conv_buf[:, 1:], ssd_x[:, None]], axis=1), ssd_state_new]
    return [pad(o_att), pad(o_dn), pad(o_ssd)], new


def _layer(xp, xs, lw, layer, n_seq, t_len, cache_k, cache_v, page_table, dn_conv_buf, dn_state, ssd_conv_buf,
           ssd_state, ffn_conv_buf):
    tm = _PROMPT_TM
    n_b = page_table.shape[0]
    rows = xs.shape[0]
    h = _rmsnorm_cast(xp, lw["norm1_w"], 512)
    hs = _rmsnorm_cast(xs, lw["norm1_w"], rows)
    (za, za_s), (zb, zb_s), (zs, zs_s) = _in_proj_all(h, hs, lw, layer, tm)
    mix_p, new_p = _prompt_mix(za, zb, zs, lw, n_seq, t_len)
    mix_s, new_s = _sample_mix(za_s, zb_s, zs_s, lw, layer, cache_k, cache_v, page_table, dn_conv_buf, dn_state,
                               ssd_conv_buf, ssd_state)
    xp, xs = _out_proj(mix_p, mix_s, lw["w_out"], layer, xp, xs, tm=tm, tn=512)
    h2 = _rmsnorm_cast(xp, lw["norm2_w"], 512)
    h2s = _rmsnorm_cast(xs, lw["norm2_w"], rows)
    padf = lambda a: jnp.pad(a, ((0, rows - n_b), (0, 0)))
    g, tail_g, tail_u, gs, y_g, y_u = _ffn_up(h2, h2s, lw["ffn_w_up"], layer, lw["ffn_conv_w"],
                                              lw["ffn_conv_b"].reshape(1, -1), padf(ffn_conv_buf[:, 0]),
                                              padf(ffn_conv_buf[:, 1]), n_seq, t_len, tm)
    xp = _ffn_down(g, lw["ffn_w_down"], layer, xp, tm)
    xs = _ffn_down(gs, lw["ffn_w_down"], layer, xs, rows)
    tps = t_len // tm
    new_p.append(jnp.concatenate([tail_g[tps - 1::tps, _TAIL - 2:], tail_u[tps - 1::tps, _TAIL - 2:]], axis=-1))
    y_new = jnp.concatenate([y_g[:n_b], y_u[:n_b]], axis=-1)
    new_s.append(jnp.stack([ffn_conv_buf[:, 1], y_new], axis=1))
    return xp, xs, new_p, new_s


def kernel(x_prompt, x_sample, cache_k, cache_v, page_table, state_dn_conv, state_dn, state_ssd_conv, state_ssd, state_ffn_conv, norm1_w, w_in, attn_q_norm_w, attn_k_norm_w, dn_conv_w, dn_A_log, dn_dt_bias, dn_norm_w, ssd_conv_w, ssd_conv_b, ssd_dt_bias, ssd_A_log, ssd_D, ssd_norm_w, w_out, norm2_w, ffn_w_up, ffn_conv_w, ffn_conv_b, ffn_w_down):
    depth = w_in.shape[0]
    n_seq, t_len, d_model = x_prompt.shape
    n_b = x_sample.shape[0]
    yp = x_prompt.reshape(n_seq * t_len, d_model)
    ys = jnp.pad(x_sample.reshape(n_b, d_model), ((0, _SAMPLE_ROWS - n_b), (0, 0)))
    wt_a, wt_b, wt_s = _split_w_in(w_in)
    outs_p, outs_s = [], []
    for l in range(depth):
        lw = {"norm1_w": norm1_w[l], "wt_a": wt_a, "wt_b": wt_b, "wt_s": wt_s, "attn_q_norm_w": attn_q_norm_w[l],
              "attn_k_norm_w": attn_k_norm_w[l], "dn_conv_w": dn_conv_w[l], "dn_norm_w": dn_norm_w[l],
              "ssd_conv_w": ssd_conv_w[l], "ssd_conv_b": ssd_conv_b[l], "ssd_D": ssd_D[l],
              "ssd_norm_w": ssd_norm_w[l], "w_out": w_out, "norm2_w": norm2_w[l], "ffn_w_up": ffn_w_up,
              "ffn_conv_w": ffn_conv_w[l], "ffn_conv_b": ffn_conv_b[l], "ffn_w_down": ffn_w_down,
              "bias_vec": _lane_vec(dn_dt_bias[l], ssd_dt_bias[l]),
              "alog_vec": _lane_vec(dn_A_log[l], ssd_A_log[l]),
              "ssd_d_vec": jnp.repeat(ssd_D[l].astype(F32), SSD_HEAD_DIM).reshape(1, SSD_INNER)}
        yp, ys, new_p, new_s = _layer(yp, ys, lw, l, n_seq, t_len, cache_k, cache_v, page_table, state_dn_conv[l],
                                      state_dn[l], state_ssd_conv[l], state_ssd[l], state_ffn_conv[l])
        outs_p.append(new_p)
        outs_s.append(new_s)
    st = lambda outs, i: jnp.stack([o[i] for o in outs])
    return (yp.reshape(n_seq, t_len, d_model), ys[:n_b].reshape(n_b, 1, d_model),
            st(outs_p, 0), st(outs_p, 1), st(outs_s, 0), st(outs_s, 1),
            st(outs_p, 2), st(outs_s, 2), st(outs_p, 3), st(outs_s, 3),
            st(outs_p, 4), st(outs_s, 4), st(outs_p, 5), st(outs_s, 5),
            st(outs_p, 6), st(outs_s, 6))
```

```python
import functools
import math

import jax
import jax.numpy as jnp
from jax import lax
from jax.experimental import pallas as pl
from jax.experimental.pallas import tpu as pltpu

F32 = jnp.float32
BF16 = jnp.bfloat16
HIGHEST = lax.Precision.HIGHEST

D_MODEL = 4096
HEAD_DIM = 128
ATT_HEADS = 16
ATT_KV_HEADS = 4
ATT_GROUP = 4
MOBA_BLOCK = 256
MOBA_TOPK = 3
DN_HEADS = 8
DN_WIDTH = 1024
DN_CONV_CH = 3072
SSD_INNER = 1024
SSD_HEAD_DIM = 64
SSD_HEADS = 16
SSD_GROUPS = 2
SSD_STATE = 128
SSD_CONV_CH = 1536
CONV_W = 4
CHUNK = 64
FFN_DIM = 11008
FFN_CONV_W = 3
EPS = 1e-6

Q_OFF, K_OFF, V_OFF = 0, 2048, 2560
DNQKV_OFF, DNZ_OFF = 3072, 6144
ZA_COLS = 7168
SSDX_OFF, SSDZ_OFF = 0, 1536
ZB_SRC, ZB_COLS = 7184, 2560
ZS_SRC = (7168, 9744)
LANE = 128
BETA_LANE, DNG_LANE, SSD_LANE = 0, 8, 16

VMEM_LIMIT_BYTES = 56 * 1024 * 1024
NEG_BIG = -1e30


def _cparams(*sem):
    return pltpu.CompilerParams(dimension_semantics=sem, vmem_limit_bytes=VMEM_LIMIT_BYTES)


def _silu(x):
    return x / (1.0 + jnp.exp(-x))


def _sigmoid(x):
    return 1.0 / (1.0 + jnp.exp(-x))


def _softplus(x):
    return jnp.maximum(x, 0.0) + jnp.log1p(jnp.exp(-jnp.abs(x)))


def _dot(a, b, precision=None):
    return jnp.dot(a, b, preferred_element_type=F32, precision=precision)


def _dot_nt(a, b, precision=None):
    return lax.dot_general(a, b, (((1,), (1,)), ((), ())), preferred_element_type=F32, precision=precision)


def _shift_rows(x, s):
    y = pltpu.roll(x, s, axis=0)
    head_rows = lax.broadcasted_iota(jnp.int32, (8, x.shape[1]), 0)
    return jnp.concatenate([jnp.where(head_rows < s, 0.0, y[0:8]), y[8:]], axis=0)


def _causal_conv_rows(x, w_ref, width):
    y = x * w_ref[width - 1:width, :]
    for i in range(width - 1):
        y = y + _shift_rows(x, width - 1 - i) * w_ref[i:i + 1, :]
    return y


def _lane_col(x, lane):
    lanes = lax.broadcasted_iota(jnp.int32, x.shape, 1)
    return jnp.sum(jnp.where(lanes == lane, x, 0.0), axis=-1, keepdims=True)


def _decay_matrix(gc_col, c):
    ii = lax.broadcasted_iota(jnp.int32, (c, c), 0)
    jj = lax.broadcasted_iota(jnp.int32, (c, c), 1)
    gcb = jnp.broadcast_to(gc_col, (c, c))
    gc_row = jnp.sum(jnp.where(ii == jj, gcb, 0.0), axis=0, keepdims=True)
    low = ii >= jj
    gam = jnp.where(low, jnp.exp(jnp.where(low, gcb - gc_row, 0.0)), 0.0)
    return gam, ii, jj


def _rmsnorm_kernel(x_ref, w_ref, o_ref):
    x = x_ref[...]
    ms = jnp.mean(x * x, axis=-1, keepdims=True)
    o_ref[...] = (x * lax.rsqrt(ms + EPS) * w_ref[...]).astype(o_ref.dtype)


def _rmsnorm_cast(x, w, tm):
    m, d = x.shape
    return pl.pallas_call(
        _rmsnorm_kernel, out_shape=jax.ShapeDtypeStruct((m, d), BF16), grid=(m // tm,),
        in_specs=[pl.BlockSpec((tm, d), lambda i: (i, 0)), pl.BlockSpec((1, d), lambda i: (0, 0))],
        out_specs=pl.BlockSpec((tm, d), lambda i: (i, 0)),
        compiler_params=_cparams("arbitrary"), name="rmsnorm")(x, w.reshape(1, d))


def _ffn_down_kernel(g_ref, w_ref, res_ref, o_ref):
    o_ref[...] = res_ref[...] + _dot(g_ref[...], w_ref[...].astype(BF16))


def _ffn_down(g, w_down, layer, res, tm):
    m, k = g.shape
    n = w_down.shape[2]
    tn = 256
    g_mode = dict(pipeline_mode=pl.Buffered(1)) if tm >= 512 else {}
    return pl.pallas_call(
        _ffn_down_kernel, out_shape=jax.ShapeDtypeStruct((m, n), F32), grid=(m // tm, n // tn),
        in_specs=[pl.BlockSpec((tm, k), lambda i, j: (i, 0), **g_mode),
                  pl.BlockSpec((None, k, tn), lambda i, j: (layer, 0, j)),
                  pl.BlockSpec((tm, tn), lambda i, j: (i, j))],
        out_specs=pl.BlockSpec((tm, tn), lambda i, j: (i, j)),
        compiler_params=_cparams("arbitrary", "arbitrary"), name="ffn_down")(g, w_down, res)


def _out_proj_kernel(*refs, widths):
    n = len(widths)
    a_refs, as_refs = refs[:n], refs[n:2 * n]
    w_ref, res_ref, ress_ref, o_ref, os_ref, b_s = refs[2 * n:]

    def project(parts, res):
        acc, row = res[...], 0
        for part, kp in zip(parts, widths):
            acc = acc + _dot(part[...], b_s[row:row + kp, :])
            row += kp
        return acc

    @pl.when(pl.program_id(1) == 0)
    def _():
        b_s[...] = w_ref[...].astype(BF16)
        os_ref[...] = project(as_refs, ress_ref)

    o_ref[...] = project(a_refs, res_ref)


def _out_proj(parts, parts_s, w, layer, res, res_s, *, tm, tn):
    widths = tuple(p.shape[1] for p in parts)
    m, ms = parts[0].shape[0], parts_s[0].shape[0]
    k, n = w.shape[1], w.shape[2]
    assert sum(widths) == k
    in_specs = [pl.BlockSpec((tm, kp), lambda j, i: (i, 0)) for kp in widths]
    in_specs += [pl.BlockSpec((ms, kp), lambda j, i: (0, 0)) for kp in widths]
    in_specs += [pl.BlockSpec((None, k, tn), lambda j, i: (layer, 0, j)),
                 pl.BlockSpec((tm, tn), lambda j, i: (i, j)), pl.BlockSpec((ms, tn), lambda j, i: (0, j))]
    return pl.pallas_call(
        functools.partial(_out_proj_kernel, widths=widths),
        out_shape=(jax.ShapeDtypeStruct((m, n), F32), jax.ShapeDtypeStruct((ms, n), F32)), grid=(n // tn, m // tm),
        in_specs=in_specs,
        out_specs=(pl.BlockSpec((tm, tn), lambda j, i: (i, j)), pl.BlockSpec((ms, tn), lambda j, i: (0, j))),
        scratch_shapes=[pltpu.VMEM((k, tn), BF16)],
        compiler_params=_cparams("arbitrary", "arbitrary"), name="out_proj")(*parts, *parts_s, w, res, res_s)


def _in_proj_kernel(a_ref, as_ref, w_ref, o_ref, os_ref, b_s):
    @pl.when(pl.program_id(1) == 0)
    def _():
        b_s[...] = w_ref[...].T.astype(BF16)
        os_ref[...] = _dot(as_ref[...], b_s[...])

    o_ref[...] = _dot(a_ref[...], b_s[...])


def _in_proj(h, hs, wt, layer, n, *, tm, tn, name):
    m, k = h.shape
    ms = hs.shape[0]
    return pl.pallas_call(
        _in_proj_kernel,
        out_shape=(jax.ShapeDtypeStruct((m, n), F32), jax.ShapeDtypeStruct((ms, n), F32)), grid=(n // tn, m // tm),
        in_specs=[pl.BlockSpec((tm, k), lambda j, i: (i, 0)), pl.BlockSpec((ms, k), lambda j, i: (0, 0)),
                  pl.BlockSpec((None, tn, k), lambda j, i: (layer, j, 0))],
        out_specs=(pl.BlockSpec((tm, tn), lambda j, i: (i, j)), pl.BlockSpec((ms, tn), lambda j, i: (0, j))),
        scratch_shapes=[pltpu.VMEM((k, tn), BF16)],
        compiler_params=_cparams("arbitrary", "arbitrary"), name=name)(h, hs, wt)


def _aux_kernel(s_ref, bias_ref, alog_ref, p1_ref, p2_ref, *, t_len, chunk):
    x = s_ref[...]
    lanes = lax.broadcasted_iota(jnp.int32, x.shape, 1)
    sp = _softplus(x + bias_ref[...])
    p1_ref[...] = jnp.where(lanes < DNG_LANE, _sigmoid(x), sp)
    g = -jnp.exp(alog_ref[...]) * sp
    if chunk == 1:
        p2_ref[...] = g
    else:
        ii = lax.broadcasted_iota(jnp.int32, (chunk, chunk), 0)
        jj = lax.broadcasted_iota(jnp.int32, (chunk, chunk), 1)
        tril = jnp.where(ii >= jj, 1.0, 0.0).astype(F32)
        for c in range(t_len // chunk):
            p2_ref[c * chunk:(c + 1) * chunk, :] = _dot(tril, g[c * chunk:(c + 1) * chunk, :], HIGHEST)


def _aux(z, bias_vec, alog_vec, n_seq, t_len, chunk):
    m = n_seq * t_len
    blk = pl.BlockSpec((t_len, LANE), lambda b: (b, 0))
    vec = pl.BlockSpec((1, LANE), lambda b: (0, 0))
    out = pl.BlockSpec((t_len, LANE), lambda b: (b, 0))
    return pl.pallas_call(
        functools.partial(_aux_kernel, t_len=t_len, chunk=chunk),
        out_shape=(jax.ShapeDtypeStruct((m, LANE), F32), jax.ShapeDtypeStruct((m, LANE), F32)),
        grid=(n_seq,), in_specs=[blk, vec, vec], out_specs=(out, out),
        compiler_params=_cparams("arbitrary"), name="aux")(z, bias_vec, alog_vec)


_ONES_ROWS = 16


def _attn_prompt_kernel(q_ref, k_ref, v_ref, wq_ref, wk_ref, o_ref, kn_ref, kb_ref, vt_ref, kmean_ref,
                        *, t_len):
    nb = t_len // MOBA_BLOCK
    g = pl.program_id(2)

    @pl.when(g == 0)
    def _():
        k = k_ref[...]
        kn = k * lax.rsqrt(jnp.mean(k * k, axis=-1, keepdims=True) + EPS) * wk_ref[...]
        kn_ref[...] = kn
        kb_ref[...] = kn.astype(BF16)
        for n in range(nb):
            kmean_ref[n:n + 1, :] = jnp.mean(kn[n * MOBA_BLOCK:(n + 1) * MOBA_BLOCK, :], axis=0, keepdims=True)
        vt_ref[0:HEAD_DIM, :] = v_ref[...].T.astype(BF16)
        vt_ref[HEAD_DIM:HEAD_DIM + _ONES_ROWS, :] = jnp.ones((_ONES_ROWS, t_len), BF16)

    q = q_ref[...]
    qn = q * lax.rsqrt(jnp.mean(q * q, axis=-1, keepdims=True) + EPS) * wq_ref[...]
    gate = _dot_nt(kmean_ref[...], qn, HIGHEST)
    blk = lax.broadcasted_iota(jnp.int32, (nb, t_len), 0)
    own = lax.broadcasted_iota(jnp.int32, (nb, t_len), 1) // MOBA_BLOCK
    valid = blk < own
    gm = jnp.where(valid, gate, -jnp.inf)
    cnt = jnp.zeros((nb, t_len), F32)
    for m in range(nb):
        row = gm[m:m + 1, :]
        beats = jnp.where(row > gm, 1.0, jnp.where(row == gm, jnp.where(blk > m, 1.0, 0.0), 0.0))
        cnt = cnt + beats
    bias = jnp.where(valid, jnp.where(cnt < MOBA_TOPK, 0.0, NEG_BIG), NEG_BIG)
    qs = (qn * (HEAD_DIM ** -0.5 * math.log2(math.e))).astype(BF16)
    kk = lax.broadcasted_iota(jnp.int32, (MOBA_BLOCK, MOBA_BLOCK), 0)
    qq = lax.broadcasted_iota(jnp.int32, (MOBA_BLOCK, MOBA_BLOCK), 1)
    causal = jnp.where(kk <= qq, 0.0, NEG_BIG)
    for qi in range(nb):
        n_keys = (qi + 1) * MOBA_BLOCK
        cols = slice(qi * MOBA_BLOCK, (qi + 1) * MOBA_BLOCK)
        st = _dot_nt(kb_ref[0:n_keys, :], qs[cols, :])
        blocks = [st[n * MOBA_BLOCK:(n + 1) * MOBA_BLOCK, :] + bias[n:n + 1, cols] for n in range(qi)]
        blocks.append(st[qi * MOBA_BLOCK:n_keys, :] + causal)
        mx = jnp.max(blocks[0], axis=0, keepdims=True)
        for blk_s in blocks[1:]:
            mx = jnp.maximum(mx, jnp.max(blk_s, axis=0, keepdims=True))
        p = [jnp.exp2(blk_s - mx).astype(BF16) for blk_s in blocks]
        p = jnp.concatenate(p, axis=0) if qi else p[0]
        ot = _dot(vt_ref[:, 0:n_keys], p)
        o = ot[0:HEAD_DIM, :] / ot[HEAD_DIM:HEAD_DIM + 1, :]
        o_ref[cols, :] = o.T.astype(o_ref.dtype)


def _attn_prompt(z, wq, wk, n_seq, t_len):
    m = n_seq * t_len
    qblk = pl.BlockSpec((t_len, HEAD_DIM), lambda b, k, g: (b, Q_OFF // HEAD_DIM + k * ATT_GROUP + g))
    kblk = pl.BlockSpec((t_len, HEAD_DIM), lambda b, k, g: (b, K_OFF // HEAD_DIM + k))
    vblk = pl.BlockSpec((t_len, HEAD_DIM), lambda b, k, g: (b, V_OFF // HEAD_DIM + k))
    wspec = pl.BlockSpec((1, HEAD_DIM), lambda b, k, g: (0, 0))
    return pl.pallas_call(
        functools.partial(_attn_prompt_kernel, t_len=t_len),
        out_shape=(jax.ShapeDtypeStruct((m, ATT_HEADS * HEAD_DIM), BF16),
                   jax.ShapeDtypeStruct((m, ATT_KV_HEADS * HEAD_DIM), F32)),
        grid=(n_seq, ATT_KV_HEADS, ATT_GROUP),
        in_specs=[qblk, kblk, vblk, wspec, wspec],
        out_specs=(pl.BlockSpec((t_len, HEAD_DIM), lambda b, k, g: (b, k * ATT_GROUP + g)),
                   pl.BlockSpec((t_len, HEAD_DIM), lambda b, k, g: (b, k))),
        scratch_shapes=[pltpu.VMEM((t_len, HEAD_DIM), BF16), pltpu.VMEM((HEAD_DIM + _ONES_ROWS, t_len), BF16),
                        pltpu.VMEM((t_len // MOBA_BLOCK, HEAD_DIM), F32)],
        compiler_params=_cparams("arbitrary", "arbitrary", "arbitrary"), name="attn_prompt")(
            z, z, z, wq.reshape(1, HEAD_DIM), wk.reshape(1, HEAD_DIM))


_GDN_UNROLL = 16
_GDN_HEADS_PER_STEP = 2


def _gdn_prompt_kernel(zq_ref, zk_ref, zv_ref, wq_ref, wk_ref, wv_ref, p1_ref, p2_ref, zg_ref, nw_ref,
                       o_ref, s_ref, q_s, k_s, v_s, beta_s, gc_s, qp_s, op_s, mn_s, nn_s, *, t_len):
    hp = pl.program_id(1)
    c = CHUNK
    d = HEAD_DIM

    def l2n(x):
        return x * lax.rsqrt(jnp.sum(x * x, axis=-1, keepdims=True) + EPS)

    ii = lax.broadcasted_iota(jnp.int32, (c, c), 0)
    jj = lax.broadcasted_iota(jnp.int32, (c, c), 1)
    eye = jnp.where(ii == jj, 1.0, 0.0).astype(F32)

    for hh in range(_GDN_HEADS_PER_STEP):
        lanes = slice(hh * d, (hh + 1) * d)
        h = hp * _GDN_HEADS_PER_STEP + hh
        q_s[hh] = l2n(_silu(_causal_conv_rows(zq_ref[:, lanes], wq_ref.at[:, lanes], CONV_W))) * (HEAD_DIM ** -0.5)
        k_s[hh] = l2n(_silu(_causal_conv_rows(zk_ref[:, lanes], wk_ref.at[:, lanes], CONV_W)))
        v_s[hh] = _silu(_causal_conv_rows(zv_ref[:, lanes], wv_ref.at[:, lanes], CONV_W))
        beta_s[hh] = jnp.broadcast_to(_lane_col(p1_ref[...], BETA_LANE + h), (t_len, HEAD_DIM))
        gc_s[hh] = jnp.broadcast_to(_lane_col(p2_ref[...], DNG_LANE + h), (t_len, HEAD_DIM))
        lax.fori_loop(0, t_len // (c * _GDN_UNROLL),
                      functools.partial(_gdn_prepare_group, refs=(q_s.at[hh], k_s.at[hh], v_s.at[hh], beta_s.at[hh],
                                                                  gc_s.at[hh], qp_s.at[hh], op_s.at[hh], mn_s.at[hh],
                                                                  nn_s.at[hh]), consts=(ii, jj, eye)), 0)

    def chunk_step(ci, states):
        r = pl.ds(pl.multiple_of(ci * c, c), c)
        rd = pl.ds(pl.multiple_of(ci * d, d), d)
        ms = [_dot(jnp.concatenate([mn_s[hh, rd, :], qp_s[hh, r, :]], axis=0), states[hh])
              for hh in range(_GDN_HEADS_PER_STEP)]
        new = []
        for hh in range(_GDN_HEADS_PER_STEP):
            op_s[hh, r, :] = ms[hh][d:d + c] + op_s[hh, r, :]
            e = jnp.exp(gc_s[hh, pl.ds(ci * c + c - 1, 1), :])
            new.append(states[hh] * e[:, 0:1] + ms[hh][0:d] + nn_s[hh, rd, :])
        return tuple(new)

    zero = jnp.zeros((HEAD_DIM, HEAD_DIM), F32)
    s_fin = lax.fori_loop(0, t_len // c, chunk_step, (zero,) * _GDN_HEADS_PER_STEP)
    for hh in range(_GDN_HEADS_PER_STEP):
        lanes = slice(hh * d, (hh + 1) * d)
        s_ref[hh] = s_fin[hh]
        o = op_s[hh]
        on = o * lax.rsqrt(jnp.mean(o * o, axis=-1, keepdims=True) + EPS) * nw_ref[...]
        o_ref[:, lanes] = (on * _silu(zg_ref[:, lanes])).astype(o_ref.dtype)


def _gdn_prepare_group(gi, carry, *, refs, consts):
    q_s, k_s, v_s, beta_s, gc_s, qp_s, op_s, mn_s, nn_s = refs
    ii, jj, eye = consts
    c = CHUNK
    d = HEAD_DIM
    n = _GDN_UNROLL
    chunks = [gi * n + u for u in range(n)]
    rows = [pl.ds(pl.multiple_of(ci * c, c), c) for ci in chunks]
    q = [q_s[r, :] for r in rows]
    k = [k_s[r, :] for r in rows]
    gcb = [gc_s[r, :] for r in rows]
    kb = [k[u] * beta_s[rows[u], :] for u in range(n)]
    gam = [_decay_matrix(gcb[u][:, 0:1], c)[0] for u in range(n)]
    kq = [_dot_nt(jnp.concatenate([kb[u], q[u]], axis=0), k[u]) for u in range(n)]
    attn = [kq[u][c:2 * c] * gam[u] for u in range(n)]
    p = [-jnp.where(ii > jj, kq[u][0:c] * gam[u], 0.0) for u in range(n)]
    inv = [eye + p[u] for u in range(n)]
    for _ in range(int(math.log2(c)) - 1):
        p = [_dot(p[u], p[u]) for u in range(n)]
        inv = [inv[u] + _dot(inv[u], p[u]) for u in range(n)]
    eg = [jnp.exp(gcb[u]) for u in range(n)]
    rhs = [jnp.concatenate([v_s[rows[u], :] * beta_s[rows[u], :], kb[u] * eg[u]], axis=1) for u in range(n)]
    uw = [_dot(inv[u], rhs[u]) for u in range(n)]
    auw = [_dot(attn[u], uw[u]) for u in range(n)]
    k_dec = [k[u] * jnp.exp(gcb[u][c - 1:c, :] - gcb[u]) for u in range(n)]
    kuw = [_dot(k_dec[u].T, uw[u]) for u in range(n)]
    for u in range(n):
        rd = pl.ds(pl.multiple_of(chunks[u] * d, d), d)
        op_s[rows[u], :] = auw[u][:, 0:d]
        qp_s[rows[u], :] = eg[u] * q[u] - auw[u][:, d:2 * d]
        nn_s[rd, :] = kuw[u][:, 0:d]
        mn_s[rd, :] = -kuw[u][:, d:2 * d]
    return carry


def _gdn_prompt(z, p1, p2, conv_w, norm_w, n_seq, t_len):
    m = n_seq * t_len
    hps = _GDN_HEADS_PER_STEP
    width = hps * HEAD_DIM
    base = DNQKV_OFF // width

    def zcol(off):
        return pl.BlockSpec((t_len, width), functools.partial(lambda b, h, off: (b, off + h), off=off))

    def wcol(off):
        return pl.BlockSpec((CONV_W, width), functools.partial(lambda b, h, off: (0, off + h), off=off))

    aux = pl.BlockSpec((t_len, LANE), lambda b, h: (b, 0))
    tbuf = pltpu.VMEM((hps, t_len, HEAD_DIM), F32)
    groups = DN_HEADS // hps
    return pl.pallas_call(
        functools.partial(_gdn_prompt_kernel, t_len=t_len),
        out_shape=(jax.ShapeDtypeStruct((m, DN_WIDTH), BF16),
                   jax.ShapeDtypeStruct((n_seq, DN_HEADS, HEAD_DIM, HEAD_DIM), F32)),
        grid=(n_seq, groups),
        in_specs=[zcol(base), zcol(base + groups), zcol(base + 2 * groups),
                  wcol(0), wcol(groups), wcol(2 * groups), aux, aux,
                  zcol(DNZ_OFF // width), pl.BlockSpec((1, HEAD_DIM), lambda b, h: (0, 0))],
        out_specs=(pl.BlockSpec((t_len, width), lambda b, h: (b, h)),
                   pl.BlockSpec((None, hps, HEAD_DIM, HEAD_DIM), lambda b, h: (b, h, 0, 0))),
        scratch_shapes=[tbuf] * 7 + [pltpu.VMEM((hps, t_len // CHUNK * HEAD_DIM, HEAD_DIM), F32)] * 2,
        compiler_params=_cparams("arbitrary", "arbitrary"), name="gdn_prompt")(
            z, z, z, conv_w, conv_w, conv_w, p1, p2, z, norm_w.reshape(1, HEAD_DIM))


_GH = SSD_HEADS // SSD_GROUPS
_GW = _GH * SSD_HEAD_DIM
_SSD_UNROLL = 8


def _ssd_prompt_kernel(zx_ref, zb_ref, zc_ref, wx_ref, wb_ref, wc_ref, bx_ref, bb_ref, bc_ref, p1_ref, p2_ref,
                       zg_ref, d_ref, nw_ref, o_ref, st_ref, x_s, b_s, c_s, y_s, state_s, *, t_len):
    grp = pl.program_id(1)
    c = CHUNK
    x_s[...] = _silu(_causal_conv_rows(zx_ref[...], wx_ref, CONV_W) + bx_ref[...])
    b_s[...] = _silu(_causal_conv_rows(zb_ref[...], wb_ref, CONV_W) + bb_ref[...])
    c_s[...] = _silu(_causal_conv_rows(zc_ref[...], wc_ref, CONV_W) + bc_ref[...])
    state_s[...] = jnp.zeros_like(state_s)

    def chunk_group(gi, carry):
        n = _SSD_UNROLL
        rows = [pl.ds(pl.multiple_of((gi * n + u) * c, c), c) for u in range(n)]
        cm = [c_s[r, :] for r in rows]
        bm = [b_s[r, :] for r in rows]
        cb = [_dot_nt(cm[u], bm[u]) for u in range(n)]
        eac, dec, xs, xv, gam = [], [], [], [], []
        for u in range(n):
            x, p1, p2 = x_s[rows[u], :], p1_ref[rows[u], :], p2_ref[rows[u], :]
            eac_h, dec_h, xs_h, xv_h, gam_h = [], [], [], [], []
            for hh in range(_GH):
                lane = SSD_LANE + grp * _GH + hh
                ac = _lane_col(p2, lane)
                a_last = ac[c - 1:c, :]
                xv_hh = x[:, hh * SSD_HEAD_DIM:(hh + 1) * SSD_HEAD_DIM] * _lane_col(p1, lane)
                gam_h.append(_decay_matrix(ac, c)[0])
                xv_h.append(xv_hh)
                xs_h.append(xv_hh * jnp.exp(a_last - ac))
                eac_h.append(jnp.broadcast_to(jnp.exp(ac), (c, SSD_HEAD_DIM)))
                dec_h.append(jnp.broadcast_to(jnp.exp(a_last), (1, SSD_HEAD_DIM)))
            gam.append(gam_h)
            xv.append(xv_h)
            xs.append(jnp.concatenate(xs_h, axis=1))
            eac.append(jnp.concatenate(eac_h, axis=1))
            dec.append(jnp.concatenate(dec_h, axis=1))
        y_intra = [jnp.concatenate([_dot(cb[u] * gam[u][hh], xv[u][hh]) for hh in range(_GH)], axis=1)
                   for u in range(n)]
        new_states = [_dot(bm[u].T, xs[u]) for u in range(n)]
        s_in = [state_s[...]]
        for u in range(n):
            s_in.append(s_in[u] * dec[u] + new_states[u])
        state_s[...] = s_in[n]
        for u in range(n):
            y_s[rows[u], :] = y_intra[u] + eac[u] * _dot(cm[u], s_in[u])
        return carry

    lax.fori_loop(0, t_len // (c * _SSD_UNROLL), chunk_group, 0)
    for hh in range(_GH):
        st_ref[hh] = state_s[:, hh * SSD_HEAD_DIM:(hh + 1) * SSD_HEAD_DIM]
    y = (y_s[...] + d_ref[...] * x_s[...]) * _silu(zg_ref[...])
    o_ref[...] = (y * lax.rsqrt(jnp.mean(y * y, axis=-1, keepdims=True) + EPS) * nw_ref[...]).astype(o_ref.dtype)


def _ssd_prompt(z, p1, p2, conv_w, conv_b, d_vec, norm_w, n_seq, t_len):
    m = n_seq * t_len
    xb, bb, cbk = SSDX_OFF // _GW, (SSDX_OFF + SSD_INNER) // LANE, (SSDX_OFF + SSD_INNER + 2 * SSD_STATE) // LANE
    wb0, wc0 = SSD_INNER // LANE, (SSD_INNER + 2 * SSD_STATE) // LANE

    def spec(rows, width, off):
        return pl.BlockSpec((rows, width), functools.partial(lambda b, g, off: (0, off + g), off=off))

    def zspec(width, off):
        return pl.BlockSpec((t_len, width), functools.partial(lambda b, g, off: (b, off + g), off=off))

    aux = pl.BlockSpec((t_len, LANE), lambda b, g: (b, 0))
    return pl.pallas_call(
        functools.partial(_ssd_prompt_kernel, t_len=t_len),
        out_shape=(jax.ShapeDtypeStruct((m, SSD_INNER), BF16),
                   jax.ShapeDtypeStruct((n_seq, SSD_HEADS, SSD_STATE, SSD_HEAD_DIM), F32)),
        grid=(n_seq, SSD_GROUPS),
        in_specs=[zspec(_GW, xb), zspec(LANE, bb), zspec(LANE, cbk),
                  spec(CONV_W, _GW, 0), spec(CONV_W, LANE, wb0), spec(CONV_W, LANE, wc0),
                  spec(1, _GW, 0), spec(1, LANE, wb0), spec(1, LANE, wc0),
                  aux, aux, zspec(_GW, SSDZ_OFF // _GW), spec(1, _GW, 0), spec(1, _GW, 0)],
        out_specs=(pl.BlockSpec((t_len, _GW), lambda b, g: (b, g)),
                   pl.BlockSpec((None, _GH, SSD_STATE, SSD_HEAD_DIM), lambda b, g: (b, g, 0, 0))),
        scratch_shapes=[pltpu.VMEM((t_len, _GW), F32), pltpu.VMEM((t_len, LANE), F32), pltpu.VMEM((t_len, LANE), F32),
                        pltpu.VMEM((t_len, _GW), F32), pltpu.VMEM((SSD_STATE, _GW), F32)],
        compiler_params=_cparams("arbitrary", "arbitrary"), name="ssd_prompt")(
            z, z, z, conv_w, conv_w, conv_w, conv_b, conv_b, conv_b, p1, p2, z, d_vec, norm_w.reshape(1, SSD_INNER))


_FFN_TN = 256
_FFN_NJ = FFN_DIM // _FFN_TN
_TAIL = 8


def _ffn_up_kernel(a_ref, as_ref, bg_ref, bu_ref, wg_ref, wu_ref, cg_ref, cu_ref, s0g_ref, s1g_ref, s0u_ref, s1u_ref,
                   g_ref, sg_ref, su_ref, gs_ref, ygs_ref, yus_ref, b_s, y_s, *, tiles_per_seq, n_sub):
    i = pl.program_id(1)
    tn = _FFN_TN
    tm = a_ref.shape[0]

    @pl.when(i == 0)
    def _():
        b_s[:, 0:tn] = bg_ref[...].astype(BF16)
        b_s[:, tn:2 * tn] = bu_ref[...].astype(BF16)
        ys = _dot(as_ref[...], b_s[...])
        yg, yu = ys[:, 0:tn], ys[:, tn:2 * tn]
        ug = wg_ref[0:1, :] * s0g_ref[...] + wg_ref[1:2, :] * s1g_ref[...] + wg_ref[2:3, :] * yg + cg_ref[...]
        uu = wu_ref[0:1, :] * s0u_ref[...] + wu_ref[1:2, :] * s1u_ref[...] + wu_ref[2:3, :] * yu + cu_ref[...]
        gs_ref[...] = (_silu(ug) * uu).astype(gs_ref.dtype)
        ygs_ref[...] = yg
        yus_ref[...] = yu

    @pl.when(i % tiles_per_seq == 0)
    def _():
        y_s[0:_TAIL, :] = jnp.zeros((_TAIL, 2 * tn), F32)

    @pl.when(i % tiles_per_seq != 0)
    def _():
        y_s[0:_TAIL, :] = y_s[tm:tm + _TAIL, :]

    w = jnp.concatenate([wg_ref[...], wu_ref[...]], axis=1)
    c = jnp.concatenate([cg_ref[...], cu_ref[...]], axis=1)
    ts = tm // n_sub
    for s in range(n_sub):
        r0 = _TAIL + s * ts
        y = _dot(a_ref[s * ts:(s + 1) * ts, :], b_s[...])
        y_s[r0:r0 + ts, :] = y
        u = w[0:1, :] * y_s[r0 - 2:r0 - 2 + ts, :] + w[1:2, :] * y_s[r0 - 1:r0 - 1 + ts, :] + w[2:3, :] * y + c
        g_ref[s * ts:(s + 1) * ts, :] = (_silu(u[:, 0:tn]) * u[:, tn:2 * tn]).astype(g_ref.dtype)
    sg_ref[...] = y_s[tm:tm + _TAIL, 0:tn]
    su_ref[...] = y_s[tm:tm + _TAIL, tn:2 * tn]


def _ffn_up(h2, h2s, w_up, layer, conv_w, conv_b, s0, s1, n_seq, t_len, tm):
    m = n_seq * t_len
    k = h2.shape[1]
    ms = h2s.shape[0]
    nj = _FFN_NJ
    tiles_per_seq = t_len // tm
    lo = lambda j, i: (0, j)
    hi = lambda j, i: (0, j + nj)
    srow = lambda f: pl.BlockSpec((ms, _FFN_TN), f)
    tail = jax.ShapeDtypeStruct((m // tm, _TAIL, FFN_DIM), F32)
    tail_spec = pl.BlockSpec((None, _TAIL, _FFN_TN), lambda j, i: (i, 0, j))
    ysd = jax.ShapeDtypeStruct((ms, FFN_DIM), F32)
    return pl.pallas_call(
        functools.partial(_ffn_up_kernel, tiles_per_seq=tiles_per_seq, n_sub=2),
        out_shape=(jax.ShapeDtypeStruct((m, FFN_DIM), BF16), tail, tail,
                   jax.ShapeDtypeStruct((ms, FFN_DIM), BF16), ysd, ysd),
        grid=(nj, m // tm),
        in_specs=[pl.BlockSpec((tm, k), lambda j, i: (i, 0)), pl.BlockSpec((ms, k), lambda j, i: (0, 0)),
                  pl.BlockSpec((None, k, _FFN_TN), lambda j, i: (layer, 0, j)),
                  pl.BlockSpec((None, k, _FFN_TN), lambda j, i: (layer, 0, j + nj)),
                  pl.BlockSpec((FFN_CONV_W, _FFN_TN), lo), pl.BlockSpec((FFN_CONV_W, _FFN_TN), hi),
                  pl.BlockSpec((1, _FFN_TN), lo), pl.BlockSpec((1, _FFN_TN), hi),
                  srow(lo), srow(lo), srow(hi), srow(hi)],
        out_specs=(pl.BlockSpec((tm, _FFN_TN), lambda j, i: (i, j)), tail_spec, tail_spec,
                   srow(lo), srow(lo), srow(lo)),
        scratch_shapes=[pltpu.VMEM((k, 2 * _FFN_TN), BF16), pltpu.VMEM((_TAIL + tm, 2 * _FFN_TN), F32)],
        compiler_params=_cparams("arbitrary", "arbitrary"), name="ffn_up")(
            h2, h2s, w_up, w_up, conv_w, conv_w, conv_b, conv_b, s0, s1, s0, s1)


_PAGES_PER_STEP = 16


def _kmean_kernel(pt_ref, *refs):
    o_ref = refs[-1]
    page = refs[0].shape[0]
    per_block = MOBA_BLOCK // page
    for n in range(_PAGES_PER_STEP // per_block):
        acc = jnp.sum(refs[n * per_block][...], axis=0)
        for r in range(1, per_block):
            acc = acc + jnp.sum(refs[n * per_block + r][...], axis=0)
        o_ref[n] = acc * (1.0 / MOBA_BLOCK)


def _kmean(cache_k, page_table, layer):
    n_b, n_pages = page_table.shape
    page, n_kv, d = cache_k.shape[2:]
    per_block = MOBA_BLOCK // page
    n_blocks = n_pages // per_block
    steps = n_pages // _PAGES_PER_STEP
    in_specs = [pl.BlockSpec((None, None, page, n_kv, d),
                             functools.partial(lambda b, s, pt, r: (layer, pt[b, s * _PAGES_PER_STEP + r], 0, 0, 0), r=r))
                for r in range(_PAGES_PER_STEP)]
    grid_spec = pltpu.PrefetchScalarGridSpec(
        num_scalar_prefetch=1, grid=(n_b, steps), in_specs=in_specs,
        out_specs=pl.BlockSpec((None, _PAGES_PER_STEP // per_block, n_kv, d), lambda b, s, pt: (b, s, 0, 0)))
    return pl.pallas_call(
        _kmean_kernel, out_shape=jax.ShapeDtypeStruct((n_b, n_blocks, n_kv, d), F32), grid_spec=grid_spec,
        compiler_params=_cparams("arbitrary", "arbitrary"), name="kmean")(page_table, *([cache_k] * _PAGES_PER_STEP))


def _select_kernel(q_ref, k_ref, kmean_ref, wq_ref, wk_ref, qn_ref, kn_ref, idx_ref):
    n_b, n_h, _ = q_ref.shape
    n_blocks = kmean_ref.shape[1]
    q = q_ref[...]
    qn = q * lax.rsqrt(jnp.mean(q * q, axis=-1, keepdims=True) + EPS) * wq_ref[...]
    qn_ref[...] = qn
    k = k_ref[...]
    kn_ref[...] = k * lax.rsqrt(jnp.mean(k * k, axis=-1, keepdims=True) + EPS) * wk_ref[...]
    head = lax.broadcasted_iota(jnp.int32, (n_h, n_blocks), 0)
    lane = lax.broadcasted_iota(jnp.int32, (n_h, n_blocks), 1).astype(F32)
    lane_out = lax.broadcasted_iota(jnp.int32, (n_h, LANE), 1)
    for b in range(n_b):
        gate = jnp.zeros((n_h, n_blocks), F32)
        for kv in range(ATT_KV_HEADS):
            gk = _dot_nt(qn[b], kmean_ref[b, :, kv * HEAD_DIM:(kv + 1) * HEAD_DIM], HIGHEST)
            gate = jnp.where(head // ATT_GROUP == kv, gk, gate)
        out = jnp.zeros((n_h, LANE), F32)
        for s in range(MOBA_TOPK):
            mx = jnp.max(gate, axis=-1, keepdims=True)
            pick = jnp.min(jnp.where(gate == mx, lane, float(n_blocks)), axis=-1, keepdims=True)
            out = jnp.where(lane_out == s, pick, out)
            gate = jnp.where(lane == pick, -jnp.inf, gate)
        idx_ref[b] = out.astype(jnp.int32)


def _select(q3, k3, kmean, wq, wk):
    n_b, n_h, d = q3.shape
    full = lambda shape: pl.BlockSpec(shape, lambda i: (0,) * len(shape))
    return pl.pallas_call(
        _select_kernel,
        out_shape=(jax.ShapeDtypeStruct(q3.shape, F32), jax.ShapeDtypeStruct(k3.shape, F32),
                   jax.ShapeDtypeStruct((n_b, n_h, LANE), jnp.int32)),
        grid=(1,),
        in_specs=[full(q3.shape), full(k3.shape), full(kmean.shape), full((1, 1, d)), full((1, 1, d))],
        out_specs=(full(q3.shape), full(k3.shape), full((n_b, n_h, LANE))),
        compiler_params=_cparams("arbitrary"), name="select")(q3, k3, kmean, wq.reshape(1, 1, d), wk.reshape(1, 1, d))


def _attn_sample_kernel(pt_ref, idx_ref, qn_ref, kn_ref, vn_ref, *refs, n_sel, per_block):
    o_ref = refs[-1]
    pages = refs[:-1]
    n_pg = n_sel * per_block
    k_refs, v_refs = pages[:n_pg], pages[n_pg:]
    h = pl.program_id(1)
    kv = h // ATT_GROUP
    q = qn_ref[pl.ds(h, 1), :] * (HEAD_DIM ** -0.5)
    k_new = kn_ref[pl.ds(kv, 1), :]
    v_new = vn_ref[pl.ds(kv, 1), :]
    s_own = jnp.sum(q * k_new, axis=-1, keepdims=True)
    mine = pl.ds(kv, k_refs[0].shape[0] // ATT_KV_HEADS, stride=ATT_KV_HEADS)
    logits = [jnp.sum(r[mine, :] * q, axis=-1, keepdims=True) for r in k_refs]
    mx = s_own
    for s in logits:
        mx = jnp.maximum(mx, jnp.max(s, axis=0, keepdims=True))
    p_own = jnp.exp(s_own - mx)
    den = p_own
    acc = p_own * v_new
    for s, v_ref in zip(logits, v_refs):
        p = jnp.exp(s - mx)
        den = den + jnp.sum(p, axis=0, keepdims=True)
        acc = acc + jnp.sum(p * v_ref[mine, :], axis=0, keepdims=True)
    o_ref[pl.ds(h, 1), :] = acc / den


def _attn_sample(cache_k, cache_v, page_table, idx, qn, kn, vn, layer):
    n_b, n_h, d = qn.shape
    depth, n_pool, page, n_kv = cache_k.shape[:4]
    per_block = MOBA_BLOCK // page
    n_sel = idx.shape[2]
    cache_k = cache_k.reshape(depth, n_pool, page * n_kv, d)
    cache_v = cache_v.reshape(depth, n_pool, page * n_kv, d)

    def page_spec(s, r):
        def imap(b, h, pt, ix):
            return (layer, pt[b, ix[b, h * n_sel + s] * per_block + r], 0, 0)
        return pl.BlockSpec((None, None, page * n_kv, d), imap)

    pspecs = [page_spec(s, r) for s in range(n_sel) for r in range(per_block)]
    slab = lambda n: pl.BlockSpec((None, n, d), lambda b, h, pt, ix: (b, 0, 0))
    grid_spec = pltpu.PrefetchScalarGridSpec(
        num_scalar_prefetch=2, grid=(n_b, n_h),
        in_specs=[slab(n_h), slab(ATT_KV_HEADS), slab(ATT_KV_HEADS)] + pspecs + pspecs,
        out_specs=slab(n_h))
    n_pg = len(pspecs)
    return pl.pallas_call(
        functools.partial(_attn_sample_kernel, n_sel=n_sel, per_block=per_block),
        out_shape=jax.ShapeDtypeStruct((n_b, n_h, d), F32), grid_spec=grid_spec,
        compiler_params=_cparams("arbitrary", "arbitrary"), name="attn_sample")(
            page_table, idx.reshape(n_b, n_h * n_sel), qn, kn, vn, *([cache_k] * n_pg), *([cache_v] * n_pg))


def _rows_to_cols(x):
    r, n = x.shape
    return jnp.concatenate([x, jnp.zeros((n - r, n), x.dtype)], axis=0).T


def _mix_sample_kernel(dx_ref, dbuf_ref, dw_ref, dbeta_ref, dg_ref, dz_ref, dnw_ref, dst_ref,
                       sx_ref, sxbuf_ref, sxw_ref, sxb_ref, bc_ref, bcbuf_ref, bcw_ref, bcb_ref,
                       sdt_ref, sa_ref, sz_ref, sd_ref, snw_ref, sst_ref,
                       dno_ref, dnst_ref, so_ref, sso_ref, y_s):
    x = dx_ref[...]
    conv = x * dw_ref[CONV_W - 1]
    for i in range(CONV_W - 1):
        conv = conv + dbuf_ref[i] * dw_ref[i]
    act = _silu(conv)
    nh = DN_HEADS
    q, k, v = act[0:nh], act[nh:2 * nh], act[2 * nh:3 * nh]
    q = q * lax.rsqrt(jnp.sum(q * q, axis=-1, keepdims=True) + EPS) * (HEAD_DIM ** -0.5)
    k = k * lax.rsqrt(jnp.sum(k * k, axis=-1, keepdims=True) + EPS)
    beta = dbeta_ref[...]
    eg = jnp.exp(dg_ref[...])
    qk = jnp.sum(q * k, axis=-1, keepdims=True)
    qt, kt = _rows_to_cols(q), _rows_to_cols(k)
    o_rows = []
    for h in range(nh):
        s0 = dst_ref[h]
        kcol, qcol = kt[:, h:h + 1], qt[:, h:h + 1]
        ks = jnp.sum(kcol * s0, axis=0, keepdims=True)
        qs = jnp.sum(qcol * s0, axis=0, keepdims=True)
        b_h, e_h = beta[h:h + 1, :], eg[h:h + 1, :]
        v_new = v[h:h + 1, :] * b_h - (b_h * e_h) * ks
        o_rows.append(e_h * qs + qk[h:h + 1, :] * v_new)
        dnst_ref[h] = s0 * e_h + kcol * v_new
    o = jnp.concatenate(o_rows, axis=0)
    on = o * lax.rsqrt(jnp.mean(o * o, axis=-1, keepdims=True) + EPS) * dnw_ref[...]
    dno_ref[...] = on * _silu(dz_ref[...])
    xs = sx_ref[...] * sxw_ref[CONV_W - 1] + sxb_ref[...]
    bc = bc_ref[...] * bcw_ref[CONV_W - 1] + bcb_ref[...]
    for i in range(CONV_W - 1):
        xs = xs + sxbuf_ref[i] * sxw_ref[i]
        bc = bc + bcbuf_ref[i] * bcw_ref[i]
    xs = _silu(xs)
    bc = _silu(bc)
    bct = _rows_to_cols(bc)
    cb = jnp.sum(bc[0:SSD_GROUPS] * bc[SSD_GROUPS:2 * SSD_GROUPS], axis=-1, keepdims=True)
    dt = sdt_ref[...]
    ea = jnp.exp(sa_ref[...])
    for h in range(SSD_HEADS):
        grp = h // _GH
        s0 = sst_ref[h]
        bcol, ccol = bct[:, grp:grp + 1], bct[:, SSD_GROUPS + grp:SSD_GROUPS + grp + 1]
        xv = xs[h:h + 1, :] * dt[h:h + 1, :]
        e_h = ea[h:h + 1, :]
        y_s[h:h + 1, :] = cb[grp:grp + 1, :] * xv + e_h * jnp.sum(ccol * s0, axis=0, keepdims=True)
        sso_ref[h] = s0 * e_h + bcol * xv
    y = (y_s[...] + sd_ref[...] * xs) * _silu(sz_ref[...])
    norm_rows = []
    for grp in range(SSD_GROUPS):
        yg = y[grp * _GH:(grp + 1) * _GH]
        ms = jnp.sum(jnp.sum(yg * yg, axis=-1, keepdims=True), axis=0, keepdims=True) * (1.0 / _GW)
        norm_rows.append(yg * lax.rsqrt(ms + EPS))
    so_ref[...] = jnp.concatenate(norm_rows, axis=0) * snw_ref[...]


def _mix_sample(args, n_b):
    def per_b(shape):
        nd = len(shape) - 1
        return pl.BlockSpec((None,) + tuple(shape[1:]), lambda b: (b,) + (0,) * nd)

    def shared(shape):
        nd = len(shape)
        return pl.BlockSpec(tuple(shape), lambda b: (0,) * nd)

    names_per_b = {"dx", "dbuf", "dbeta", "dg", "dz", "dst", "sx", "sxbuf", "bc", "bcbuf", "sdt", "sa", "sz", "sst"}
    order = ["dx", "dbuf", "dw", "dbeta", "dg", "dz", "dnw", "dst", "sx", "sxbuf", "sxw", "sxb", "bc", "bcbuf", "bcw",
             "bcb", "sdt", "sa", "sz", "sd", "snw", "sst"]
    in_specs = [per_b(args[n].shape) if n in names_per_b else shared(args[n].shape) for n in order]
    outs = (jax.ShapeDtypeStruct((n_b, DN_HEADS, HEAD_DIM), F32),
            jax.ShapeDtypeStruct((n_b, DN_HEADS, HEAD_DIM, HEAD_DIM), F32),
            jax.ShapeDtypeStruct((n_b, SSD_HEADS, SSD_HEAD_DIM), F32),
            jax.ShapeDtypeStruct((n_b, SSD_HEADS, SSD_STATE, SSD_HEAD_DIM), F32))
    return pl.pallas_call(
        _mix_sample_kernel, out_shape=outs, grid=(n_b,), in_specs=in_specs,
        out_specs=tuple(per_b(o.shape) for o in outs),
        scratch_shapes=[pltpu.VMEM((SSD_HEADS, SSD_HEAD_DIM), F32)],
        compiler_params=_cparams("arbitrary"), name="mix_sample")(*[args[n] for n in order])


def _split_w_in(w_in):
    depth, k, _ = w_in.shape
    wt = jnp.swapaxes(w_in, 1, 2)
    wt_b = wt[:, ZB_SRC:ZB_SRC + ZB_COLS]
    wt_s = jnp.concatenate([wt[:, ZS_SRC[0]:ZS_SRC[0] + 16], wt[:, ZS_SRC[1]:ZS_SRC[1] + 16],
                            jnp.zeros((depth, LANE - 32, k), w_in.dtype)], axis=1)
    return wt, wt_b, wt_s


def _in_proj_all(h, hs, lw, layer, tm):
    return [_in_proj(h, hs, lw["wt_a"], layer, ZA_COLS, tm=tm, tn=512, name="in_proj_a"),
            _in_proj(h, hs, lw["wt_b"], layer, ZB_COLS, tm=tm, tn=512, name="in_proj_b"),
            _in_proj(h, hs, lw["wt_s"], layer, LANE, tm=tm, tn=LANE, name="in_proj_s")]


def _lane_vec(dn_vals, ssd_vals):
    v = jnp.zeros((LANE,), F32)
    v = v.at[DNG_LANE:DNG_LANE + DN_HEADS].set(dn_vals.astype(F32))
    v = v.at[SSD_LANE:SSD_LANE + SSD_HEADS].set(ssd_vals.astype(F32))
    return v.reshape(1, LANE)


_PROMPT_TM = 1024


def _prompt_mix(za, zb, zs, lw, n_seq, t_len):
    p1, p2 = _aux(zs, lw["bias_vec"], lw["alog_vec"], n_seq, t_len, CHUNK)
    o_att, k_norm = _attn_prompt(za, lw["attn_q_norm_w"], lw["attn_k_norm_w"], n_seq, t_len)
    o_dn, dn_state = _gdn_prompt(za, p1, p2, lw["dn_conv_w"], lw["dn_norm_w"], n_seq, t_len)
    o_ssd, ssd_state = _ssd_prompt(zb, p1, p2, lw["ssd_conv_w"], lw["ssd_conv_b"].reshape(1, -1), lw["ssd_d_vec"],
                                   lw["ssd_norm_w"], n_seq, t_len)
    za3 = za.reshape(n_seq, t_len, ZA_COLS)
    zb3 = zb.reshape(n_seq, t_len, ZB_COLS)
    new = [k_norm.reshape(n_seq, t_len, ATT_KV_HEADS, HEAD_DIM),
           za3[:, :, V_OFF:V_OFF + 512].reshape(n_seq, t_len, ATT_KV_HEADS, HEAD_DIM),
           za3[:, t_len - (CONV_W - 1):, DNQKV_OFF:DNQKV_OFF + DN_CONV_CH],
           dn_state,
           zb3[:, t_len - (CONV_W - 1):, SSDX_OFF:SSDX_OFF + SSD_CONV_CH],
           ssd_state]
    return [o_att, o_dn, o_ssd], new


_SAMPLE_ROWS = 16


def _sample_mix(za, zb2, zs, lw, layer, cache_k, cache_v, page_table, dn_conv_buf, dn_state, ssd_conv_buf,
                ssd_state):
    n_b = page_table.shape[0]
    rows = za.shape[0]
    p1, p2 = _aux(zs, lw["bias_vec"], lw["alog_vec"], 1, rows, 1)
    zb = za[:n_b]
    zbs = zb2[:n_b]
    kmean = _kmean(cache_k, page_table, layer)
    kmean = kmean.reshape(n_b, kmean.shape[1], ATT_KV_HEADS * HEAD_DIM)
    q3 = zb[:, Q_OFF:Q_OFF + 2048].reshape(n_b, ATT_HEADS, HEAD_DIM)
    k3 = zb[:, K_OFF:K_OFF + 512].reshape(n_b, ATT_KV_HEADS, HEAD_DIM)
    v3 = zb[:, V_OFF:V_OFF + 512].reshape(n_b, ATT_KV_HEADS, HEAD_DIM)
    qn, kn, idx = _select(q3, k3, kmean, lw["attn_q_norm_w"], lw["attn_k_norm_w"])
    o_att = _attn_sample(cache_k, cache_v, page_table, idx[:, :, :MOBA_TOPK], qn, kn, v3, layer)
    dn_x = zb[:, DNQKV_OFF:DNQKV_OFF + DN_CONV_CH]
    ssd_x = zbs[:, SSDX_OFF:SSDX_OFF + SSD_CONV_CH]
    nh3 = 3 * DN_HEADS
    ng2 = 2 * SSD_GROUPS
    args = {
        "dx": dn_x.reshape(n_b, nh3, HEAD_DIM),
        "dbuf": dn_conv_buf.reshape(n_b, CONV_W - 1, nh3, HEAD_DIM),
        "dw": lw["dn_conv_w"].reshape(CONV_W, nh3, HEAD_DIM),
        "dbeta": p1[:n_b, BETA_LANE:BETA_LANE + DN_HEADS].reshape(n_b, DN_HEADS, 1),
        "dg": p2[:n_b, DNG_LANE:DNG_LANE + DN_HEADS].reshape(n_b, DN_HEADS, 1),
        "dz": zb[:, DNZ_OFF:DNZ_OFF + DN_WIDTH].reshape(n_b, DN_HEADS, HEAD_DIM),
        "dnw": lw["dn_norm_w"].reshape(1, HEAD_DIM),
        "dst": dn_state,
        "sx": ssd_x[:, :SSD_INNER].reshape(n_b, SSD_HEADS, SSD_HEAD_DIM),
        "sxbuf": ssd_conv_buf[:, :, :SSD_INNER].reshape(n_b, CONV_W - 1, SSD_HEADS, SSD_HEAD_DIM),
        "sxw": lw["ssd_conv_w"][:, :SSD_INNER].reshape(CONV_W, SSD_HEADS, SSD_HEAD_DIM),
        "sxb": lw["ssd_conv_b"][:SSD_INNER].reshape(SSD_HEADS, SSD_HEAD_DIM),
        "bc": ssd_x[:, SSD_INNER:].reshape(n_b, ng2, SSD_STATE),
        "bcbuf": ssd_conv_buf[:, :, SSD_INNER:].reshape(n_b, CONV_W - 1, ng2, SSD_STATE),
        "bcw": lw["ssd_conv_w"][:, SSD_INNER:].reshape(CONV_W, ng2, SSD_STATE),
        "bcb": lw["ssd_conv_b"][SSD_INNER:].reshape(ng2, SSD_STATE),
        "sdt": p1[:n_b, SSD_LANE:SSD_LANE + SSD_HEADS].reshape(n_b, SSD_HEADS, 1),
        "sa": p2[:n_b, SSD_LANE:SSD_LANE + SSD_HEADS].reshape(n_b, SSD_HEADS, 1),
        "sz": zbs[:, SSDZ_OFF:SSDZ_OFF + SSD_INNER].reshape(n_b, SSD_HEADS, SSD_HEAD_DIM),
        "sd": lw["ssd_D"].astype(F32).reshape(SSD_HEADS, 1),
        "snw": lw["ssd_norm_w"].reshape(SSD_HEADS, SSD_HEAD_DIM),
        "sst": ssd_state,
    }
    o_dn, dn_state_new, o_ssd, ssd_state_new = _mix_sample(args, n_b)
    pad = lambda a: jnp.pad(a.reshape(n_b, -1), ((0, rows - n_b), (0, 0))).astype(BF16)
    new = [kn.reshape(n_b, 1, ATT_KV_HEADS, HEAD_DIM), v3.reshape(n_b, 1, ATT_KV_HEADS, HEAD_DIM),
           jnp.concatenate([dn_conv_buf[:, 1:], dn_x[:, None]], axis=1), dn_state_new,
           jnp.concatenate([ssd_conv_buf[:, 1:], ssd_x[:, None]], axis=1), ssd_state_new]
    return [pad(o_att), pad(o_dn), pad(o_ssd)], new


def _layer(xp, xs, lw, layer, n_seq, t_len, cache_k, cache_v, page_table, dn_conv_buf, dn_state, ssd_conv_buf,
           ssd_state, ffn_conv_buf):
    tm = _PROMPT_TM
    n_b = page_table.shape[0]
    rows = xs.shape[0]
    h = _rmsnorm_cast(xp, lw["norm1_w"], 512)
    hs = _rmsnorm_cast(xs, lw["norm1_w"], rows)
    (za, za_s), (zb, zb_s), (zs, zs_s) = _in_proj_all(h, hs, lw, layer, tm)
    mix_p, new_p = _prompt_mix(za, zb, zs, lw, n_seq, t_len)
    mix_s, new_s = _sample_mix(za_s, zb_s, zs_s, lw, layer, cache_k, cache_v, page_table, dn_conv_buf, dn_state,
                               ssd_conv_buf, ssd_state)
    xp, xs = _out_proj(mix_p, mix_s, lw["w_out"], layer, xp, xs, tm=tm, tn=512)
    h2 = _rmsnorm_cast(xp, lw["norm2_w"], 512)
    h2s = _rmsnorm_cast(xs, lw["norm2_w"], rows)
    padf = lambda a: jnp.pad(a, ((0, rows - n_b), (0, 0)))
    g, tail_g, tail_u, gs, y_g, y_u = _ffn_up(h2, h2s, lw["ffn_w_up"], layer, lw["ffn_conv_w"],
                                              lw["ffn_conv_b"].reshape(1, -1), padf(ffn_conv_buf[:, 0]),
                                              padf(ffn_conv_buf[:, 1]), n_seq, t_len, tm)
    xp = _ffn_down(g, lw["ffn_w_down"], layer, xp, tm)
    xs = _ffn_down(gs, lw["ffn_w_down"], layer, xs, rows)
    tps = t_len // tm
    new_p.append(jnp.concatenate([tail_g[tps - 1::tps, _TAIL - 2:], tail_u[tps - 1::tps, _TAIL - 2:]], axis=-1))
    y_new = jnp.concatenate([y_g[:n_b], y_u[:n_b]], axis=-1)
    new_s.append(jnp.stack([ffn_conv_buf[:, 1], y_new], axis=1))
    return xp, xs, new_p, new_s


def kernel(x_prompt, x_sample, cache_k, cache_v, page_table, state_dn_conv, state_dn, state_ssd_conv, state_ssd, state_ffn_conv, norm1_w, w_in, attn_q_norm_w, attn_k_norm_w, dn_conv_w, dn_A_log, dn_dt_bias, dn_norm_w, ssd_conv_w, ssd_conv_b, ssd_dt_bias, ssd_A_log, ssd_D, ssd_norm_w, w_out, norm2_w, ffn_w_up, ffn_conv_w, ffn_conv_b, ffn_w_down):
    depth = w_in.shape[0]
    n_seq, t_len, d_model = x_prompt.shape
    n_b = x_sample.shape[0]
    yp = x_prompt.reshape(n_seq * t_len, d_model)
    ys = jnp.pad(x_sample.reshape(n_b, d_model), ((0, _SAMPLE_ROWS - n_b), (0, 0)))
    wt_a, wt_b, wt_s = _split_w_in(w_in)
    outs_p, outs_s = [], []
    for l in range(depth):
        lw = {"norm1_w": norm1_w[l], "wt_a": wt_a, "wt_b": wt_b, "wt_s": wt_s, "attn_q_norm_w": attn_q_norm_w[l],
              "attn_k_norm_w": attn_k_norm_w[l], "dn_conv_w": dn_conv_w[l], "dn_norm_w": dn_norm_w[l],
              "ssd_conv_w": ssd_conv_w[l], "ssd_conv_b": ssd_conv_b[l], "ssd_D": ssd_D[l],
              "ssd_norm_w": ssd_norm_w[l], "w_out": w_out, "norm2_w": norm2_w[l], "ffn_w_up": ffn_w_up,
              "ffn_conv_w": ffn_conv_w[l], "ffn_conv_b": ffn_conv_b[l], "ffn_w_down": ffn_w_down,
              "bias_vec": _lane_vec(dn_dt_bias[l], ssd_dt_bias[l]),
              "alog_vec": _lane_vec(dn_A_log[l], ssd_A_log[l]),
              "ssd_d_vec": jnp.repeat(ssd_D[l].astype(F32), SSD_HEAD_DIM).reshape(1, SSD_INNER)}
        yp, ys, new_p, new_s = _layer(yp, ys, lw, l, n_seq, t_len, cache_k, cache_v, page_table, state_dn_conv[l],
                                      state_dn[l], state_ssd_conv[l], state_ssd[l], state_ffn_conv[l])
        outs_p.append(new_p)
        outs_s.append(new_s)
    st = lambda outs, i: jnp.stack([o[i] for o in outs])
    return (yp.reshape(n_seq, t_len, d_model), ys[:n_b].reshape(n_b, 1, d_model),
            st(outs_p, 0), st(outs_p, 1), st(outs_s, 0), st(outs_s, 1),
            st(outs_p, 2), st(outs_s, 2), st(outs_p, 3), st(outs_s, 3),
            st(outs_p, 4), st(outs_s, 4), st(outs_p, 5), st(outs_s, 5),
            st(outs_p, 6), st(outs_s, 6))
```
